```python
import jax, jax.numpy as jnp
from jax import lax
import numpy as np

D_MODEL = 2048
BATCH = 2
SEQ = 4096
DEPTH = 1

CHUNK = 64
Q_BLOCK = 128
PLE_DIM = 256
EPS = 1e-6
MASK_VALUE = -1e30

MLA_HEADS = 8
QK_NOPE = 128
QK_ROPE = 64
QK_HEAD = QK_NOPE + QK_ROPE
V_HEAD = 128
Q_RANK = 512
KV_RANK = 256
ROPE_BASE = 10000.0
D_ATTN = MLA_HEADS * V_HEAD

D_CONV = D_MODEL // 2
CONV_WIDTH = 3

D_MIX = D_ATTN + D_CONV
D_IN = Q_RANK + KV_RANK + QK_ROPE + 3 * D_CONV

N_GROUPS = 4
EXPERTS_PER_GROUP = 8
N_EXPERTS = N_GROUPS * EXPERTS_PER_GROUP
TOP_K = 2
D_EXPERT = 512

kernel_name = "hybrid_mla_shortconv_hmoe_ple"


def rms_norm(x, g):
    xf = x.astype(jnp.float32)
    y = xf * lax.rsqrt(jnp.mean(xf * xf, axis=-1, keepdims=True) + EPS)
    return (y * g.astype(jnp.float32)).astype(x.dtype)


def rope_tables(positions, dtype):
    freq = ROPE_BASE ** (-jnp.arange(0, QK_ROPE, 2, dtype=jnp.float32) / QK_ROPE)
    ang = positions.astype(jnp.float32)[..., None] * freq
    return jnp.cos(ang).astype(dtype), jnp.sin(ang).astype(dtype)


def apply_rope(x, cos, sin):
    half = x.shape[-1] // 2
    x1, x2 = x[..., :half], x[..., half:]
    return jnp.concatenate([x1 * cos - x2 * sin, x2 * cos + x1 * sin], axis=-1)


def mla_mixer(q_lat, kv_lat, k_rope, positions, g_q_lat, g_kv_lat, w_uq, w_ukv, g_q_head, g_k_head):
    B, S, _ = q_lat.shape
    H = MLA_HEADS
    q = jnp.einsum('bsr,rhd->bshd', rms_norm(q_lat, g_q_lat), w_uq)
    kv = jnp.einsum('bsr,rhd->bshd', rms_norm(kv_lat, g_kv_lat), w_ukv)
    k_nope, v = kv[..., :QK_NOPE], kv[..., QK_NOPE:]
    k = jnp.concatenate([k_nope, jnp.broadcast_to(k_rope[:, :, None, :], (B, S, H, QK_ROPE))], axis=-1)
    q = rms_norm(q, g_q_head)
    k = rms_norm(k, g_k_head)
    cos, sin = rope_tables(positions, q.dtype)
    cos, sin = cos[:, :, None, :], sin[:, :, None, :]
    q = jnp.concatenate([q[..., :QK_NOPE], apply_rope(q[..., QK_NOPE:], cos, sin)], axis=-1)
    k = jnp.concatenate([k[..., :QK_NOPE], apply_rope(k[..., QK_NOPE:], cos, sin)], axis=-1)

    nb = S // Q_BLOCK
    qb = q.reshape(B, nb, Q_BLOCK, H, QK_HEAD).transpose(1, 0, 3, 2, 4)
    kT = k.transpose(0, 2, 1, 3)
    vT = v.transpose(0, 2, 1, 3)
    key_chunk = jnp.arange(S) // CHUNK
    scale = QK_HEAD ** -0.5

    def attend_block(args):
        qi, i = args
        s = jnp.einsum('bhqd,bhkd->bhqk', qi, kT).astype(jnp.float32) * scale
        q_chunk = (i * Q_BLOCK + jnp.arange(Q_BLOCK)) // CHUNK
        mask = key_chunk[None, :] <= q_chunk[:, None]
        s = jnp.where(mask, s, MASK_VALUE)
        pr = jax.nn.softmax(s, axis=-1).astype(vT.dtype)
        return jnp.einsum('bhqk,bhkd->bhqd', pr, vT)

    out = lax.map(attend_block, (qb, jnp.arange(nb)))
    return out.transpose(1, 0, 3, 2, 4).reshape(B, S, D_ATTN)


def short_conv_mixer(b_gate, c_gate, h, w_conv):
    S = h.shape[1]
    u = c_gate * h
    u_pad = jnp.pad(u, ((0, 0), (CONV_WIDTH - 1, 0), (0, 0)))
    conv = sum(u_pad[:, j:j + S] * w_conv[j] for j in range(CONV_WIDTH))
    return b_gate * conv


def hier_moe(x, w_group, b_group, w_router, b_router, w1, w3, w2):
    B, S, D = x.shape
    t = x.reshape(-1, D)
    T = t.shape[0]
    g_logits = (t @ w_group + b_group).astype(jnp.float32)
    g_prob = jax.nn.softmax(g_logits, axis=-1)
    g_sel = jnp.argmax(g_logits, axis=-1)
    p_g = jnp.take_along_axis(g_prob, g_sel[:, None], axis=-1)
    e_logits = (t @ w_router + b_router).astype(jnp.float32).reshape(T, N_GROUPS, EXPERTS_PER_GROUP)
    idx = jnp.broadcast_to(g_sel[:, None, None], (T, 1, EXPERTS_PER_GROUP))
    e_in_group = jnp.take_along_axis(e_logits, idx, axis=1)[:, 0]
    top_vals, top_idx = lax.top_k(e_in_group, TOP_K)
    w_top = jax.nn.softmax(top_vals, axis=-1) * p_g
    expert_id = g_sel[:, None] * EXPERTS_PER_GROUP + top_idx
    combine = jnp.sum(jax.nn.one_hot(expert_id, N_EXPERTS, dtype=jnp.float32) * w_top[..., None], axis=1)

    def expert_step(acc, params):
        w1e, w3e, w2e, ce = params
        hid = jax.nn.silu(t @ w1e) * (t @ w3e)
        return acc + ce[:, None].astype(t.dtype) * (hid @ w2e), None

    y, _ = lax.scan(expert_step, jnp.zeros_like(t), (w1, w3, w2, combine.T))
    return y.reshape(B, S, D)


def setup_inputs(seed: int = 0) -> dict:
    key = jax.random.key(seed)
    ks = jax.random.split(key, 32)
    f32 = jnp.float32

    def nrm(k, shape, fan_in):
        return jax.random.normal(k, shape, f32) * (fan_in ** -0.5)

    def gain(k, shape):
        return 1.0 + 0.01 * jax.random.normal(k, shape, f32)

    L = DEPTH
    x = jax.random.normal(ks[0], (BATCH, SEQ, D_MODEL), f32)
    p = jax.random.normal(ks[1], (DEPTH, BATCH, SEQ, PLE_DIM), f32)
    offsets = jax.random.randint(ks[2], (BATCH, 1), 0, 65536, dtype=jnp.int32)
    positions = offsets + jnp.arange(SEQ, dtype=jnp.int32)[None, :]
    return {
        "x": x,
        "p": p,
        "positions": positions,
        "norm_mix": gain(ks[3], (L, D_MODEL)),
        "w_in": nrm(ks[4], (L, D_MODEL, D_IN), D_MODEL),
        "g_q_lat": gain(ks[5], (L, Q_RANK)),
        "g_kv_lat": gain(ks[6], (L, KV_RANK)),
        "w_uq": nrm(ks[7], (L, Q_RANK, MLA_HEADS, QK_HEAD), Q_RANK),
        "w_ukv": nrm(ks[8], (L, KV_RANK, MLA_HEADS, QK_NOPE + V_HEAD), KV_RANK),
        "g_q_head": gain(ks[9], (L, QK_HEAD)),
        "g_k_head": gain(ks[10], (L, QK_HEAD)),
        "w_conv": nrm(ks[11], (L, CONV_WIDTH, D_CONV), CONV_WIDTH),
        "g_out_attn": gain(ks[12], (L, D_ATTN)),
        "g_out_conv": gain(ks[13], (L, D_CONV)),
        "w_out": nrm(ks[14], (L, D_MIX, D_MODEL), D_MIX),
        "norm_moe": gain(ks[15], (L, D_MODEL)),
        "w_group": nrm(ks[16], (L, D_MODEL, N_GROUPS), D_MODEL),
        "b_group": 0.01 * jax.random.normal(ks[17], (L, N_GROUPS), f32),
        "w_router": nrm(ks[18], (L, D_MODEL, N_EXPERTS), D_MODEL),
        "b_router": 0.01 * jax.random.normal(ks[19], (L, N_EXPERTS), f32),
        "w1": nrm(ks[20], (L, N_EXPERTS, D_MODEL, D_EXPERT), D_MODEL),
        "w3": nrm(ks[21], (L, N_EXPERTS, D_MODEL, D_EXPERT), D_MODEL),
        "w2": nrm(ks[22], (L, N_EXPERTS, D_EXPERT, D_MODEL), D_EXPERT),
        "norm_ple": gain(ks[23], (L, D_MODEL)),
        "w_ple": nrm(ks[24], (L, PLE_DIM, D_MODEL), PLE_DIM),
        "w_ple_gate": nrm(ks[25], (L, D_MODEL, D_MODEL), D_MODEL),
        "b_ple_gate": 0.01 * jax.random.normal(ks[26], (L, D_MODEL), f32),
    }


def reference(x, p, positions, norm_mix, w_in, g_q_lat, g_kv_lat, w_uq, w_ukv, g_q_head, g_k_head,
              w_conv, g_out_attn, g_out_conv, w_out, norm_moe, w_group, b_group, w_router, b_router,
              w1, w3, w2, norm_ple, w_ple, w_ple_gate, b_ple_gate):
    h = x
    splits = np.cumsum([Q_RANK, KV_RANK, QK_ROPE, D_CONV, D_CONV]).tolist()
    for i in range(DEPTH):
        xn = rms_norm(h, norm_mix[i])
        proj = xn @ w_in[i]
        q_lat, kv_lat, k_rope, b_gate, c_gate, h_conv = jnp.split(proj, splits, axis=-1)
        y_attn = mla_mixer(q_lat, kv_lat, k_rope, positions, g_q_lat[i], g_kv_lat[i],
                           w_uq[i], w_ukv[i], g_q_head[i], g_k_head[i])
        y_conv = short_conv_mixer(b_gate, c_gate, h_conv, w_conv[i])
        y_mix = jnp.concatenate([rms_norm(y_attn, g_out_attn[i]), rms_norm(y_conv, g_out_conv[i])], axis=-1)
        h = h + y_mix @ w_out[i]
        h = h + hier_moe(rms_norm(h, norm_moe[i]), w_group[i], b_group[i], w_router[i], b_router[i],
                         w1[i], w3[i], w2[i])
        gate = jax.nn.sigmoid(rms_norm(h, norm_ple[i]) @ w_ple_gate[i] + b_ple_gate[i])
        h = h + gate * (p[i] @ w_ple[i])
    return h
```

```python
import functools

import jax
import jax.numpy as jnp
from jax import lax
from jax.experimental import pallas as pl
from jax.experimental.pallas import tpu as pltpu

F32 = jnp.float32
BF16 = jnp.bfloat16

CHUNK = 64
EPS = 1e-6
MASK_VALUE = -1e30
ROPE_BASE = 10000.0

MLA_HEADS = 8
QK_NOPE = 128
QK_ROPE = 64
QK_HEAD = QK_NOPE + QK_ROPE
V_HEAD = 128
HEAD_PAD = 256
N_GROUPS = 4
EXPERTS_PER_GROUP = 8
N_EXPERTS = N_GROUPS * EXPERTS_PER_GROUP
TOP_K = 2
LANES = 128

TM_FRONT = 512
TQ_ATTN = 512
TM_OUT = 512
TM_MOE = 256
TM_PLE = 256
VMEM_LIMIT = 56 * 1024 * 1024


def _rms(x, g, n):
    ms = jnp.sum(x * x, axis=-1, keepdims=True) * (1.0 / n)
    return x * lax.rsqrt(ms + EPS) * g


def _front_kernel(x_ref, pos_ref, freq_ref, gmix_ref, win_ref, gql_ref, gkvl_ref,
                  wuq_ref, wukv_ref, gq_ref, gk_ref, wconv_ref, gconv_ref,
                  q_ref, k_ref, v_ref, yc_ref, prev_ref, *, tiles_per_seq, q_rank, kv_rank, d_conv):
    i = pl.program_id(0)
    is_start = (i % tiles_per_seq) == 0
    tm = x_ref.shape[0]
    d_model = x_ref.shape[1]
    lat_w = q_rank + kv_rank + LANES

    xn = _rms(x_ref[...], gmix_ref[...], d_model).astype(BF16)
    lat = jnp.dot(xn, win_ref[:, :lat_w], preferred_element_type=F32)
    q_lat = lat[:, :q_rank]
    kv_lat = lat[:, q_rank:q_rank + kv_rank]
    kr = lat[:, q_rank + kv_rank:]

    qn = _rms(q_lat, gql_ref[...], q_rank).astype(BF16)
    kvn = _rms(kv_lat, gkvl_ref[...], kv_rank).astype(BF16)
    q_all = jnp.dot(qn, wuq_ref[...], preferred_element_type=F32)
    kv_all = jnp.dot(kvn, wukv_ref[...], preferred_element_type=F32)

    lane = lax.broadcasted_iota(jnp.int32, (tm, LANES), 1)
    ang = pos_ref[...].astype(F32) * freq_ref[...]
    cos = jnp.cos(ang)
    sin = jnp.sin(ang)
    rope_c = jnp.where(lane < QK_ROPE, cos, 0.0)
    rope_s = jnp.where(lane < QK_ROPE // 2, -sin, jnp.where(lane < QK_ROPE, sin, 0.0))
    rope_lanes = lane < QK_ROPE

    def rope(t):
        return t * rope_c + pltpu.roll(t, QK_ROPE, 1) * rope_s

    gq = gq_ref[...]
    gk = gk_ref[...]
    inv_n = 1.0 / QK_HEAD
    kr_ssq = jnp.sum(jnp.where(rope_lanes, kr * kr, 0.0), axis=-1, keepdims=True)
    kr_rot = rope(kr * gk[:, QK_NOPE:])
    for h in range(MLA_HEADS):
        qa = q_all[:, h * HEAD_PAD:h * HEAD_PAD + QK_NOPE]
        qb = q_all[:, h * HEAD_PAD + QK_NOPE:(h + 1) * HEAD_PAD]
        ssq = (jnp.sum(qa * qa, axis=-1, keepdims=True)
               + jnp.sum(jnp.where(rope_lanes, qb * qb, 0.0), axis=-1, keepdims=True))
        inv = lax.rsqrt(ssq * inv_n + EPS)
        q_ref[:, h * HEAD_PAD:h * HEAD_PAD + QK_NOPE] = (qa * inv * gq[:, :QK_NOPE]).astype(BF16)
        q_ref[:, h * HEAD_PAD + QK_NOPE:(h + 1) * HEAD_PAD] = rope(qb * inv * gq[:, QK_NOPE:]).astype(BF16)

        ka = kv_all[:, h * QK_NOPE:(h + 1) * QK_NOPE]
        kinv = lax.rsqrt((jnp.sum(ka * ka, axis=-1, keepdims=True) + kr_ssq) * inv_n + EPS)
        k_ref[:, h * HEAD_PAD:h * HEAD_PAD + QK_NOPE] = (ka * kinv * gk[:, :QK_NOPE]).astype(BF16)
        k_ref[:, h * HEAD_PAD + QK_NOPE:(h + 1) * HEAD_PAD] = (kr_rot * kinv).astype(BF16)
    v_ref[...] = kv_all[:, MLA_HEADS * QK_NOPE:].astype(BF16)

    cv = jnp.dot(xn, win_ref[:, lat_w:], preferred_element_type=F32)
    b_gate = cv[:, :d_conv]
    u = cv[:, d_conv:2 * d_conv] * cv[:, 2 * d_conv:]

    @pl.when(is_start)
    def _():
        prev_ref[...] = jnp.zeros_like(prev_ref)

    prev = prev_ref[...]
    hm1 = prev[7:8, :]
    hm2 = prev[6:7, :]
    row = lax.broadcasted_iota(jnp.int32, u.shape, 0)
    u1 = jnp.where(row == 0, hm1, pltpu.roll(u, 1, 0))
    u2 = jnp.where(row == 0, hm2, jnp.where(row == 1, hm1, pltpu.roll(u, 2, 0)))
    prev_ref[...] = u[tm - 8:, :]
    wc = wconv_ref[...]
    y = b_gate * (u2 * wc[0:1, :] + u1 * wc[1:2, :] + u * wc[2:3, :])
    yc_ref[...] = _rms(y, gconv_ref[...], d_conv).astype(BF16)


def _front(x2, pos2, freq, gmix, w_in_p, gql, gkvl, wuq_p, wukv_p, gq_p, gk_p, wconv, gconv, *, seq):
    t, d_model = x2.shape
    tm = TM_FRONT
    q_rank = gql.shape[1]
    kv_rank = gkvl.shape[1]
    d_conv = gconv.shape[1]
    const = lambda i: (0, 0)
    row = lambda i: (i, 0)
    full = lambda a: pl.BlockSpec(a.shape, const)
    kern = functools.partial(_front_kernel, tiles_per_seq=seq // tm, q_rank=q_rank, kv_rank=kv_rank, d_conv=d_conv)
    return pl.pallas_call(
        kern,
        grid=(t // tm,),
        in_specs=[pl.BlockSpec((tm, d_model), row), pl.BlockSpec((tm, 1), row), full(freq), full(gmix),
                  pl.BlockSpec(w_in_p.shape, const, pipeline_mode=pl.Buffered(1)),
                  full(gql), full(gkvl), full(wuq_p), full(wukv_p), full(gq_p), full(gk_p), full(wconv), full(gconv)],
        out_specs=[pl.BlockSpec((tm, MLA_HEADS * HEAD_PAD), row), pl.BlockSpec((tm, MLA_HEADS * HEAD_PAD), row),
                   pl.BlockSpec((tm, MLA_HEADS * V_HEAD), row), pl.BlockSpec((tm, d_conv), row)],
        out_shape=[jax.ShapeDtypeStruct((t, MLA_HEADS * HEAD_PAD), BF16), jax.ShapeDtypeStruct((t, MLA_HEADS * HEAD_PAD), BF16),
                   jax.ShapeDtypeStruct((t, MLA_HEADS * V_HEAD), BF16), jax.ShapeDtypeStruct((t, d_conv), BF16)],
        scratch_shapes=[pltpu.VMEM((8, d_conv), F32)],
        compiler_params=pltpu.CompilerParams(dimension_semantics=("arbitrary",), vmem_limit_bytes=VMEM_LIMIT),
        name="front",
    )(x2, pos2, freq, gmix, w_in_p, gql, gkvl, wuq_p, wukv_p, gq_p, gk_p, wconv, gconv)


def _attn_kernel(q_ref, k_ref, v_ref, o_ref):
    i = pl.program_id(2)
    tq = q_ref.shape[0]
    q = q_ref[...]
    scale = QK_HEAD ** -0.5
    r = lax.broadcasted_iota(jnp.int32, (tq, tq), 0)
    c = lax.broadcasted_iota(jnp.int32, (tq, tq), 1)
    diag_mask = (c // CHUNK) <= (r // CHUNK)

    def step(j, carry, masked):
        m, l, acc = carry
        start = pl.multiple_of(j * tq, tq)
        kb = k_ref[pl.ds(start, tq), :]
        vb = v_ref[pl.ds(start, tq), :]
        s = lax.dot_general(q, kb, (((1,), (1,)), ((), ())), preferred_element_type=F32) * scale
        if masked:
            s = jnp.where(diag_mask, s, MASK_VALUE)
        m_new = jnp.maximum(m, jnp.max(s, axis=-1, keepdims=True))
        alpha = jnp.exp(m - m_new)
        p = jnp.exp(s - m_new)
        l = alpha * l + jnp.sum(p, axis=-1, keepdims=True)
        acc = alpha * acc + jnp.dot(p.astype(BF16), vb, preferred_element_type=F32)
        return m_new, l, acc

    init = (jnp.full((tq, 1), MASK_VALUE, F32), jnp.zeros((tq, 1), F32), jnp.zeros((tq, V_HEAD), F32))
    carry = lax.fori_loop(0, i, lambda j, cr: step(j, cr, False), init)
    _, l, acc = step(i, carry, True)
    o_ref[...] = acc / l


def _attention(q, k, v, *, batch, seq):
    tq = TQ_ATTN
    nq = seq // tq
    return pl.pallas_call(
        _attn_kernel,
        grid=(batch, MLA_HEADS, nq),
        in_specs=[pl.BlockSpec((tq, HEAD_PAD), lambda b, h, i: (b * nq + i, h)),
                  pl.BlockSpec((seq, HEAD_PAD), lambda b, h, i: (b, h)),
                  pl.BlockSpec((seq, V_HEAD), lambda b, h, i: (b, h))],
        out_specs=pl.BlockSpec((tq, V_HEAD), lambda b, h, i: (b * nq + i, h)),
        out_shape=jax.ShapeDtypeStruct((batch * seq, MLA_HEADS * V_HEAD), F32),
        compiler_params=pltpu.CompilerParams(dimension_semantics=("arbitrary", "arbitrary", "arbitrary"),
                                             vmem_limit_bytes=VMEM_LIMIT),
        name="attn",
    )(q, k, v)


def _outproj_kernel(ya_ref, yc_ref, x_ref, gattn_ref, wout_ref, gmoe_ref, wgr_ref, bgr_ref,
                    h1_ref, xm_ref, route_ref):
    tm, d_attn = ya_ref.shape
    d_model = x_ref.shape[1]
    ya = _rms(ya_ref[...], gattn_ref[...], d_attn).astype(BF16)
    h1 = (x_ref[...]
          + jnp.dot(ya, wout_ref[:d_attn, :], preferred_element_type=F32)
          + jnp.dot(yc_ref[...], wout_ref[d_attn:, :], preferred_element_type=F32))
    h1_ref[...] = h1
    xm = _rms(h1, gmoe_ref[...], d_model)
    xm_ref[...] = xm

    logits = jnp.dot(xm, wgr_ref[...], precision=lax.Precision.HIGHEST, preferred_element_type=F32) + bgr_ref[...]
    lane = lax.broadcasted_iota(jnp.int32, (tm, LANES), 1).astype(F32)
    neg = -jnp.inf
    big = 1e9
    gl = jnp.where(lane < N_GROUPS, logits, neg)
    gmax = jnp.max(gl, axis=-1, keepdims=True)
    gsel = jnp.min(jnp.where(gl == gmax, lane, big), axis=-1, keepdims=True)
    p_g = 1.0 / jnp.sum(jnp.exp(gl - gmax), axis=-1, keepdims=True)
    lo = N_GROUPS + EXPERTS_PER_GROUP * gsel
    el = jnp.where((lane >= lo) & (lane < lo + EXPERTS_PER_GROUP), logits, neg)
    v1 = jnp.max(el, axis=-1, keepdims=True)
    i1 = jnp.min(jnp.where(el == v1, lane, big), axis=-1, keepdims=True)
    el2 = jnp.where(lane == i1, neg, el)
    v2 = jnp.max(el2, axis=-1, keepdims=True)
    i2 = jnp.min(jnp.where(el2 == v2, lane, big), axis=-1, keepdims=True)
    e2 = jnp.exp(v2 - v1)
    den = 1.0 + e2
    w1 = p_g / den
    w2 = p_g * e2 / den
    route_ref[...] = jnp.where(lane == 0, i1 - N_GROUPS,
                               jnp.where(lane == 1, i2 - N_GROUPS,
                                         jnp.where(lane == 2, w1, jnp.where(lane == 3, w2, 0.0))))


def _outproj(ya, yc, x2, gattn, w_out, gmoe, w_gr, b_gr):
    t, d_model = x2.shape
    d_attn = ya.shape[1]
    tm = TM_OUT
    const = lambda i: (0, 0)
    row = lambda i: (i, 0)
    full = lambda a: pl.BlockSpec(a.shape, const)
    return pl.pallas_call(
        _outproj_kernel,
        grid=(t // tm,),
        in_specs=[pl.BlockSpec((tm, d_attn), row), pl.BlockSpec((tm, yc.shape[1]), row), pl.BlockSpec((tm, d_model), row),
                  full(gattn), pl.BlockSpec(w_out.shape, const, pipeline_mode=pl.Buffered(1)), full(gmoe), full(w_gr), full(b_gr)],
        out_specs=[pl.BlockSpec((tm, d_model), row), pl.BlockSpec((tm, d_model), row), pl.BlockSpec((tm, LANES), row)],
        out_shape=[jax.ShapeDtypeStruct((t, d_model), F32), jax.ShapeDtypeStruct((t, d_model), F32),
                   jax.ShapeDtypeStruct((t, LANES), F32)],
        compiler_params=pltpu.CompilerParams(dimension_semantics=("arbitrary",), vmem_limit_bytes=VMEM_LIMIT),
        name="outproj",
    )(ya, yc, x2, gattn, w_out, gmoe, w_gr, b_gr)


def _row_gather_start(idx_ref, src_hbm, dst_ref, sem, n_rows):
    def body(r, carry):
        pltpu.make_async_copy(src_hbm.at[pl.ds(idx_ref[0, 0, r], 1), :], dst_ref.at[pl.ds(r, 1), :], sem).start()
        return carry
    lax.fori_loop(0, n_rows, body, 0)


def _row_gather_wait(src_hbm, dst_ref, sem, n_rows):
    pltpu.make_async_copy(src_hbm.at[pl.ds(0, n_rows), :], dst_ref, sem).wait()


def _moe_kernel(tile_ref, expert_ref, lo_ref, hi_ref, n_items_ref, tok_cur_ref, tok_next_ref, xm_hbm,
                w1_ref, w3_ref, w2_ref, y_ref, xbuf_ref, sem_ref, *, n_row_tiles):
    w = pl.program_id(0)
    tm = y_ref.shape[0]
    tile = tile_ref[w]
    first = (w == 0) | (tile != tile_ref[jnp.maximum(w - 1, 0)])
    active = w < n_items_ref[0]
    slot = tile % 2

    @pl.when(w == 0)
    def _():
        _row_gather_start(tok_cur_ref, xm_hbm, xbuf_ref.at[0], sem_ref.at[0], tm)

    @pl.when(active & first & (tile + 1 < n_row_tiles))
    def _():
        _row_gather_start(tok_next_ref, xm_hbm, xbuf_ref.at[1 - slot], sem_ref.at[1 - slot], tm)

    @pl.when(active)
    def _():
        @pl.when(first)
        def _():
            _row_gather_wait(xm_hbm, xbuf_ref.at[slot], sem_ref.at[slot], tm)

        x = xbuf_ref[slot].astype(BF16)
        a = jnp.dot(x, w1_ref[0].astype(BF16), preferred_element_type=F32)
        g = jnp.dot(x, w3_ref[0].astype(BF16), preferred_element_type=F32)
        hid = (a * jax.nn.sigmoid(a) * g).astype(BF16)
        y = jnp.dot(hid, w2_ref[0].astype(BF16), preferred_element_type=F32)

        @pl.when(first)
        def _():
            y_ref[...] = y

        @pl.when(jnp.logical_not(first))
        def _():
            row = lax.broadcasted_iota(jnp.int32, (tm, 1), 0)
            mine = (row >= lo_ref[w]) & (row < hi_ref[w])
            y_ref[...] = jnp.where(mine, y, y_ref[...])


def _moe(plan, xm, w1, w3, w2):
    tile_id, expert_id, row_lo, row_hi, n_items, row_token3 = plan
    n_row_tiles = row_token3.shape[0]
    tm = TM_MOE
    d_model = xm.shape[1]
    d_exp = w1.shape[2]
    last = n_row_tiles - 1
    grid_spec = pltpu.PrefetchScalarGridSpec(
        num_scalar_prefetch=5,
        grid=(tile_id.shape[0],),
        in_specs=[pl.BlockSpec((1, 1, tm), lambda w, t, e, lo, hi, n: (t[w], 0, 0), memory_space=pltpu.SMEM),
                  pl.BlockSpec((1, 1, tm), lambda w, t, e, lo, hi, n: (jnp.minimum(t[w] + 1, last), 0, 0),
                               memory_space=pltpu.SMEM),
                  pl.BlockSpec(memory_space=pl.ANY),
                  pl.BlockSpec((1, d_model, d_exp), lambda w, t, e, lo, hi, n: (e[w], 0, 0)),
                  pl.BlockSpec((1, d_model, d_exp), lambda w, t, e, lo, hi, n: (e[w], 0, 0)),
                  pl.BlockSpec((1, d_exp, d_model), lambda w, t, e, lo, hi, n: (e[w], 0, 0))],
        out_specs=pl.BlockSpec((tm, d_model), lambda w, t, e, lo, hi, n: (t[w], 0)),
        scratch_shapes=[pltpu.VMEM((2, tm, d_model), F32), pltpu.SemaphoreType.DMA((2,))],
    )
    return pl.pallas_call(
        functools.partial(_moe_kernel, n_row_tiles=n_row_tiles),
        grid_spec=grid_spec,
        out_shape=jax.ShapeDtypeStruct((n_row_tiles * tm, d_model), F32),
        compiler_params=pltpu.CompilerParams(dimension_semantics=("arbitrary",), vmem_limit_bytes=VMEM_LIMIT),
        name="moe",
    )(tile_id, expert_id, row_lo, row_hi, n_items, row_token3, row_token3, xm, w1, w3, w2)


def _ple_kernel(dest_cur_ref, dest_next_ref, y_hbm, h1_ref, route_ref, p_ref, wple_ref, gple_ref, wpg_ref, bpg_ref,
                o_ref, ybuf_ref, sem_ref):
    i = pl.program_id(0)
    n = pl.num_programs(0)
    tm, d_model = h1_ref.shape
    slot = i % 2

    @pl.when(i == 0)
    def _():
        _row_gather_start(dest_cur_ref, y_hbm, ybuf_ref.at[0], sem_ref.at[0], TOP_K * tm)

    @pl.when(i + 1 < n)
    def _():
        _row_gather_start(dest_next_ref, y_hbm, ybuf_ref.at[1 - slot], sem_ref.at[1 - slot], TOP_K * tm)

    _row_gather_wait(y_hbm, ybuf_ref.at[slot], sem_ref.at[slot], TOP_K * tm)
    route = route_ref[...]
    moe = route[:, 2:3] * ybuf_ref[slot, :tm, :] + route[:, 3:4] * ybuf_ref[slot, tm:, :]
    h2 = h1_ref[...] + moe
    xn = _rms(h2, gple_ref[...], d_model).astype(BF16)
    gate = jax.nn.sigmoid(jnp.dot(xn, wpg_ref[...], preferred_element_type=F32) + bpg_ref[...])
    pe = jnp.dot(p_ref[...].astype(BF16), wple_ref[...], preferred_element_type=F32)
    o_ref[...] = h2 + gate * pe


def _ple(dest3, y, h1, route, p2, w_ple, gple, w_pg, b_pg):
    t, d_model = h1.shape
    tm = TM_PLE
    n_tiles = t // tm
    last = n_tiles - 1
    const = lambda i: (0, 0)
    row = lambda i: (i, 0)
    full = lambda a: pl.BlockSpec(a.shape, const)
    return pl.pallas_call(
        _ple_kernel,
        grid=(n_tiles,),
        in_specs=[pl.BlockSpec((1, 1, TOP_K * tm), lambda i: (i, 0, 0), memory_space=pltpu.SMEM),
                  pl.BlockSpec((1, 1, TOP_K * tm), lambda i: (jnp.minimum(i + 1, last), 0, 0), memory_space=pltpu.SMEM),
                  pl.BlockSpec(memory_space=pl.ANY),
                  pl.BlockSpec((tm, d_model), row), pl.BlockSpec((tm, LANES), row), pl.BlockSpec((tm, p2.shape[1]), row),
                  full(w_ple), full(gple), pl.BlockSpec(w_pg.shape, const, pipeline_mode=pl.Buffered(1)), full(b_pg)],
        out_specs=pl.BlockSpec((tm, d_model), row),
        out_shape=jax.ShapeDtypeStruct((t, d_model), F32),
        scratch_shapes=[pltpu.VMEM((2, TOP_K * tm, d_model), F32), pltpu.SemaphoreType.DMA((2,))],
        compiler_params=pltpu.CompilerParams(dimension_semantics=("arbitrary",), vmem_limit_bytes=VMEM_LIMIT),
        name="ple",
    )(dest3, dest3, y, h1, route, p2, w_ple, gple, w_pg, b_pg)


def _dispatch_plan(route):
    t = route.shape[0]
    tm = TM_MOE
    total = TOP_K * t
    n_row_tiles = total // tm
    n_items_max = n_row_tiles + N_EXPERTS
    expert = route[:, :TOP_K].astype(jnp.int32).reshape(-1)
    order = jnp.argsort(expert, stable=True).astype(jnp.int32)
    sorted_e = expert[order]
    pos = jnp.arange(total, dtype=jnp.int32)
    dest = jnp.zeros((total,), jnp.int32).at[order].set(pos)
    group_start = jnp.searchsorted(sorted_e, jnp.arange(N_EXPERTS, dtype=jnp.int32), side="left").astype(jnp.int32)
    cuts = jnp.sort(jnp.concatenate([jnp.arange(0, total, tm, dtype=jnp.int32), group_start]))
    seg_len = jnp.concatenate([cuts[1:], jnp.full((1,), total, jnp.int32)]) - cuts
    valid = seg_len > 0
    n_items = jnp.sum(valid.astype(jnp.int32))
    keep = jnp.argsort(jnp.logical_not(valid), stable=True)
    idx = keep[jnp.minimum(jnp.arange(n_items_max, dtype=jnp.int32), n_items - 1)]
    start = cuts[idx]
    tile_id = start // tm
    expert_id = sorted_e[start]
    row_lo = start - tile_id * tm
    row_hi = row_lo + seg_len[idx]
    row_token3 = (order // TOP_K).reshape(n_row_tiles, 1, tm)
    dest3 = dest.reshape(t // TM_PLE, TM_PLE, TOP_K).transpose(0, 2, 1).reshape(t // TM_PLE, 1, TOP_K * TM_PLE)
    plan = (tile_id, expert_id, row_lo, row_hi, n_items.reshape(1), row_token3)
    return plan, dest3


def _rope_swap_pad(w):
    half = QK_ROPE // 2
    x1 = w[..., QK_NOPE:QK_NOPE + half]
    x2 = w[..., QK_NOPE + half:]
    return jnp.concatenate([w[..., :QK_NOPE], x1, x2, x2, x1], axis=-1)


def kernel(x, p, positions, norm_mix, w_in, g_q_lat, g_kv_lat, w_uq, w_ukv, g_q_head, g_k_head, w_conv, g_out_attn, g_out_conv, w_out, norm_moe, w_group, b_group, w_router, b_router, w1, w3, w2, norm_ple, w_ple, w_ple_gate, b_ple_gate):
    batch, seq, d_model = x.shape
    depth = w_in.shape[0]
    t = batch * seq
    q_rank = g_q_lat.shape[1]
    kv_rank = g_kv_lat.shape[1]
    half = QK_ROPE // 2
    rope_lo = q_rank + kv_rank

    pos2 = positions.reshape(t, 1)
    inv_freq = ROPE_BASE ** (-jnp.arange(0, QK_ROPE, 2, dtype=F32) / QK_ROPE)
    freq = jnp.tile(inv_freq, LANES // half).reshape(1, LANES)

    h = x.reshape(t, d_model)
    for i in range(depth):
        wi = w_in[i]
        w_in_p = jnp.concatenate([wi[:, :rope_lo + QK_ROPE], wi[:, rope_lo + half:rope_lo + QK_ROPE],
                                  wi[:, rope_lo:rope_lo + half], wi[:, rope_lo + QK_ROPE:]], axis=1).astype(BF16)
        wuq_p = _rope_swap_pad(w_uq[i]).reshape(q_rank, MLA_HEADS * HEAD_PAD).astype(BF16)
        wukv_p = jnp.concatenate([w_ukv[i][:, :, :QK_NOPE].reshape(kv_rank, -1),
                                  w_ukv[i][:, :, QK_NOPE:].reshape(kv_rank, -1)], axis=1).astype(BF16)
        gq_p = _rope_swap_pad(g_q_head[i]).reshape(1, HEAD_PAD)
        gk_p = _rope_swap_pad(g_k_head[i]).reshape(1, HEAD_PAD)
        q, k, v, yc = _front(h, pos2, freq, norm_mix[i].reshape(1, -1), w_in_p, g_q_lat[i].reshape(1, -1),
                             g_kv_lat[i].reshape(1, -1), wuq_p, wukv_p, gq_p, gk_p, w_conv[i],
                             g_out_conv[i].reshape(1, -1), seq=seq)
        ya = _attention(q, k, v, batch=batch, seq=seq)

        w_gr = jnp.zeros((d_model, LANES), F32).at[:, :N_GROUPS].set(w_group[i]).at[:, N_GROUPS:N_GROUPS + N_EXPERTS].set(w_router[i])
        b_gr = jnp.zeros((1, LANES), F32).at[0, :N_GROUPS].set(b_group[i]).at[0, N_GROUPS:N_GROUPS + N_EXPERTS].set(b_router[i])
        h1, xm, route = _outproj(ya, yc, h, g_out_attn[i].reshape(1, -1), w_out[i].astype(BF16),
                                 norm_moe[i].reshape(1, -1), w_gr, b_gr)

        plan, dest3 = _dispatch_plan(route)
        y = _moe(plan, xm, w1[i], w3[i], w2[i])
        h = _ple(dest3, y, h1, route, p[i].reshape(t, -1), w_ple[i].astype(BF16), norm_ple[i].reshape(1, -1),
                 w_ple_gate[i].astype(BF16), b_ple_gate[i].reshape(1, -1))
    return h.reshape(batch, seq, d_model)
```

```python
import functools

import jax
import jax.numpy as jnp
from jax import lax
from jax.experimental import pallas as pl
from jax.experimental.pallas import tpu as pltpu

F32 = jnp.float32
BF16 = jnp.bfloat16

CHUNK = 64
EPS = 1e-6
MASK_VALUE = -1e30
ROPE_BASE = 10000.0

MLA_HEADS = 8
QK_NOPE = 128
QK_ROPE = 64
QK_HEAD = QK_NOPE + QK_ROPE
V_HEAD = 128
HEAD_PAD = 256
N_GROUPS = 4
EXPERTS_PER_GROUP = 8
N_EXPERTS = N_GROUPS * EXPERTS_PER_GROUP
TOP_K = 2
LANES = 128
SLAB = 16
SLAB_PITCH = 20

TM_FRONT = 512
TQ_ATTN = 512
TM_OUT = 512
TM_MOE = 256
TM_PLE = 256
VMEM_LIMIT = 56 * 1024 * 1024


def _rms(x, g, n):
    ms = jnp.sum(x * x, axis=-1, keepdims=True) * (1.0 / n)
    return x * lax.rsqrt(ms + EPS) * g


def _store_slabs(ref, val, pitch, mask=None):
    n, width = val.shape
    for c in range(width // LANES):
        piece = val[:, c * LANES:(c + 1) * LANES]
        rows = pl.ds(c, n, stride=pitch)
        if mask is not None:
            piece = jnp.where(mask, piece, ref[rows, :])
        ref[rows, :] = piece


def _load_slabs(ref, start, n, slab, pitch):
    return jnp.concatenate([ref[pl.ds(start + c, n, stride=pitch), :] for c in range(slab)], axis=1)


def _front_kernel(x_ref, pos_ref, freq_ref, gmix_ref, win_ref, gql_ref, gkvl_ref,
                  wuq_ref, wukv_ref, gq_ref, gk_ref, wconv_ref, gconv_ref,
                  q_ref, k_ref, v_ref, yc_ref, prev_ref, *, tiles_per_seq, q_rank, kv_rank, d_conv):
    i = pl.program_id(0)
    is_start = (i % tiles_per_seq) == 0
    tm = x_ref.shape[0]
    d_model = x_ref.shape[1]
    lat_w = q_rank + kv_rank + LANES

    xn = _rms(x_ref[...], gmix_ref[...], d_model).astype(BF16)
    lat = jnp.dot(xn, win_ref[:, :lat_w], preferred_element_type=F32)
    q_lat = lat[:, :q_rank]
    kv_lat = lat[:, q_rank:q_rank + kv_rank]
    kr = lat[:, q_rank + kv_rank:]

    qn = _rms(q_lat, gql_ref[...], q_rank).astype(BF16)
    kvn = _rms(kv_lat, gkvl_ref[...], kv_rank).astype(BF16)
    q_all = jnp.dot(qn, wuq_ref[...], preferred_element_type=F32)
    kv_all = jnp.dot(kvn, wukv_ref[...], preferred_element_type=F32)

    lane = lax.broadcasted_iota(jnp.int32, (tm, LANES), 1)
    ang = pos_ref[...].astype(F32) * freq_ref[...]
    cos = jnp.cos(ang)
    sin = jnp.sin(ang)
    rope_c = jnp.where(lane < QK_ROPE, cos, 0.0)
    rope_s = jnp.where(lane < QK_ROPE // 2, -sin, jnp.where(lane < QK_ROPE, sin, 0.0))
    rope_lanes = lane < QK_ROPE

    def rope(t):
        return t * rope_c + pltpu.roll(t, QK_ROPE, 1) * rope_s

    gq = gq_ref[...]
    gk = gk_ref[...]
    inv_n = 1.0 / QK_HEAD
    kr_ssq = jnp.sum(jnp.where(rope_lanes, kr * kr, 0.0), axis=-1, keepdims=True)
    kr_rot = rope(kr * gk[:, QK_NOPE:])
    for h in range(MLA_HEADS):
        qa = q_all[:, h * HEAD_PAD:h * HEAD_PAD + QK_NOPE]
        qb = q_all[:, h * HEAD_PAD + QK_NOPE:(h + 1) * HEAD_PAD]
        ssq = (jnp.sum(qa * qa, axis=-1, keepdims=True)
               + jnp.sum(jnp.where(rope_lanes, qb * qb, 0.0), axis=-1, keepdims=True))
        inv = lax.rsqrt(ssq * inv_n + EPS)
        q_ref[:, h * HEAD_PAD:h * HEAD_PAD + QK_NOPE] = (qa * inv * gq[:, :QK_NOPE]).astype(BF16)
        q_ref[:, h * HEAD_PAD + QK_NOPE:(h + 1) * HEAD_PAD] = rope(qb * inv * gq[:, QK_NOPE:]).astype(BF16)

        ka = kv_all[:, h * QK_NOPE:(h + 1) * QK_NOPE]
        kinv = lax.rsqrt((jnp.sum(ka * ka, axis=-1, keepdims=True) + kr_ssq) * inv_n + EPS)
        k_ref[:, h * HEAD_PAD:h * HEAD_PAD + QK_NOPE] = (ka * kinv * gk[:, :QK_NOPE]).astype(BF16)
        k_ref[:, h * HEAD_PAD + QK_NOPE:(h + 1) * HEAD_PAD] = (kr_rot * kinv).astype(BF16)
    v_ref[...] = kv_all[:, MLA_HEADS * QK_NOPE:].astype(BF16)

    cv = jnp.dot(xn, win_ref[:, lat_w:], preferred_element_type=F32)
    b_gate = cv[:, :d_conv]
    u = cv[:, d_conv:2 * d_conv] * cv[:, 2 * d_conv:]

    @pl.when(is_start)
    def _():
        prev_ref[...] = jnp.zeros_like(prev_ref)

    prev = prev_ref[...]
    hm1 = prev[7:8, :]
    hm2 = prev[6:7, :]
    row = lax.broadcasted_iota(jnp.int32, u.shape, 0)
    u1 = jnp.where(row == 0, hm1, pltpu.roll(u, 1, 0))
    u2 = jnp.where(row == 0, hm2, jnp.where(row == 1, hm1, pltpu.roll(u, 2, 0)))
    prev_ref[...] = u[tm - 8:, :]
    wc = wconv_ref[...]
    y = b_gate * (u2 * wc[0:1, :] + u1 * wc[1:2, :] + u * wc[2:3, :])
    yc_ref[...] = _rms(y, gconv_ref[...], d_conv).astype(BF16)


def _front(x2, pos2, freq, gmix, w_in_p, gql, gkvl, wuq_p, wukv_p, gq_p, gk_p, wconv, gconv, *, seq):
    t, d_model = x2.shape
    tm = TM_FRONT
    q_rank = gql.shape[1]
    kv_rank = gkvl.shape[1]
    d_conv = gconv.shape[1]
    const = lambda i: (0, 0)
    row = lambda i: (i, 0)
    full = lambda a: pl.BlockSpec(a.shape, const)
    kern = functools.partial(_front_kernel, tiles_per_seq=seq // tm, q_rank=q_rank, kv_rank=kv_rank, d_conv=d_conv)
    return pl.pallas_call(
        kern,
        grid=(t // tm,),
        in_specs=[pl.BlockSpec((tm, d_model), row), pl.BlockSpec((tm, 1), row), full(freq), full(gmix),
                  pl.BlockSpec(w_in_p.shape, const, pipeline_mode=pl.Buffered(1)),
                  full(gql), full(gkvl), full(wuq_p), full(wukv_p), full(gq_p), full(gk_p), full(wconv), full(gconv)],
        out_specs=[pl.BlockSpec((tm, MLA_HEADS * HEAD_PAD), row), pl.BlockSpec((tm, MLA_HEADS * HEAD_PAD), row),
                   pl.BlockSpec((tm, MLA_HEADS * V_HEAD), row), pl.BlockSpec((tm, d_conv), row)],
        out_shape=[jax.ShapeDtypeStruct((t, MLA_HEADS * HEAD_PAD), BF16), jax.ShapeDtypeStruct((t, MLA_HEADS * HEAD_PAD), BF16),
                   jax.ShapeDtypeStruct((t, MLA_HEADS * V_HEAD), BF16), jax.ShapeDtypeStruct((t, d_conv), BF16)],
        scratch_shapes=[pltpu.VMEM((8, d_conv), F32)],
        compiler_params=pltpu.CompilerParams(dimension_semantics=("arbitrary",), vmem_limit_bytes=VMEM_LIMIT),
        name="front",
    )(x2, pos2, freq, gmix, w_in_p, gql, gkvl, wuq_p, wukv_p, gq_p, gk_p, wconv, gconv)


def _attn_kernel(q_ref, k_ref, v_ref, o_ref):
    i = pl.program_id(2)
    tq = q_ref.shape[0]
    q = q_ref[...]
    scale = QK_HEAD ** -0.5
    r = lax.broadcasted_iota(jnp.int32, (tq, tq), 0)
    c = lax.broadcasted_iota(jnp.int32, (tq, tq), 1)
    diag_mask = (c // CHUNK) <= (r // CHUNK)

    def step(j, carry, masked):
        m, l, acc = carry
        start = pl.multiple_of(j * tq, tq)
        kb = k_ref[pl.ds(start, tq), :]
        vb = v_ref[pl.ds(start, tq), :]
        s = lax.dot_general(q, kb, (((1,), (1,)), ((), ())), preferred_element_type=F32) * scale
        if masked:
            s = jnp.where(diag_mask, s, MASK_VALUE)
        m_new = jnp.maximum(m, jnp.max(s, axis=-1, keepdims=True))
        alpha = jnp.exp(m - m_new)
        p = jnp.exp(s - m_new)
        l = alpha * l + jnp.sum(p, axis=-1, keepdims=True)
        acc = alpha * acc + jnp.dot(p.astype(BF16), vb, preferred_element_type=F32)
        return m_new, l, acc

    init = (jnp.full((tq, 1), MASK_VALUE, F32), jnp.zeros((tq, 1), F32), jnp.zeros((tq, V_HEAD), F32))
    carry = lax.fori_loop(0, i, lambda j, cr: step(j, cr, False), init)
    _, l, acc = step(i, carry, True)
    o_ref[...] = acc / l


def _attention(q, k, v, *, batch, seq):
    tq = TQ_ATTN
    nq = seq // tq
    return pl.pallas_call(
        _attn_kernel,
        grid=(batch, MLA_HEADS, nq),
        in_specs=[pl.BlockSpec((tq, HEAD_PAD), lambda b, h, i: (b * nq + i, h)),
                  pl.BlockSpec((seq, HEAD_PAD), lambda b, h, i: (b, h)),
                  pl.BlockSpec((seq, V_HEAD), lambda b, h, i: (b, h))],
        out_specs=pl.BlockSpec((tq, V_HEAD), lambda b, h, i: (b * nq + i, h)),
        out_shape=jax.ShapeDtypeStruct((batch * seq, MLA_HEADS * V_HEAD), F32),
        compiler_params=pltpu.CompilerParams(dimension_semantics=("arbitrary", "arbitrary", "arbitrary"),
                                             vmem_limit_bytes=VMEM_LIMIT),
        name="attn",
    )(q, k, v)


def _outproj_kernel(ya_ref, yc_ref, x_ref, gattn_ref, wout_ref, gmoe_ref, wgr_ref, bgr_ref,
                    h1_ref, xm_ref, route_ref):
    tm, d_attn = ya_ref.shape
    d_model = x_ref.shape[1]
    ya = _rms(ya_ref[...], gattn_ref[...], d_attn).astype(BF16)
    h1 = (x_ref[...]
          + jnp.dot(ya, wout_ref[:d_attn, :], preferred_element_type=F32)
          + jnp.dot(yc_ref[...], wout_ref[d_attn:, :], preferred_element_type=F32))
    h1_ref[...] = h1
    xm = _rms(h1, gmoe_ref[...], d_model)
    _store_slabs(xm_ref, xm, SLAB)

    x_hi = xm.astype(BF16)
    x_lo = (xm - x_hi.astype(F32)).astype(BF16)
    hi_part = jnp.dot(x_hi, wgr_ref[...], preferred_element_type=F32)
    lo_part = jnp.dot(x_lo, wgr_ref[:, :LANES], preferred_element_type=F32)
    logits = hi_part[:, :LANES] + hi_part[:, LANES:] + lo_part + bgr_ref[...]
    lane = lax.broadcasted_iota(jnp.int32, (tm, LANES), 1).astype(F32)
    neg = -jnp.inf
    big = 1e9
    gl = jnp.where(lane < N_GROUPS, logits, neg)
    gmax = jnp.max(gl, axis=-1, keepdims=True)
    gsel = jnp.min(jnp.where(gl == gmax, lane, big), axis=-1, keepdims=True)
    p_g = 1.0 / jnp.sum(jnp.exp(gl - gmax), axis=-1, keepdims=True)
    lo = N_GROUPS + EXPERTS_PER_GROUP * gsel
    el = jnp.where((lane >= lo) & (lane < lo + EXPERTS_PER_GROUP), logits, neg)
    v1 = jnp.max(el, axis=-1, keepdims=True)
    i1 = jnp.min(jnp.where(el == v1, lane, big), axis=-1, keepdims=True)
    el2 = jnp.where(lane == i1, neg, el)
    v2 = jnp.max(el2, axis=-1, keepdims=True)
    i2 = jnp.min(jnp.where(el2 == v2, lane, big), axis=-1, keepdims=True)
    e2 = jnp.exp(v2 - v1)
    den = 1.0 + e2
    w1 = p_g / den
    w2 = p_g * e2 / den
    route_ref[...] = jnp.where(lane == 0, i1 - N_GROUPS,
                               jnp.where(lane == 1, i2 - N_GROUPS,
                                         jnp.where(lane == 2, w1, jnp.where(lane == 3, w2, 0.0))))


def _outproj(ya, yc, x2, gattn, w_out, gmoe, w_gr, b_gr):
    t, d_model = x2.shape
    d_attn = ya.shape[1]
    tm = TM_OUT
    const = lambda i: (0, 0)
    row = lambda i: (i, 0)
    full = lambda a: pl.BlockSpec(a.shape, const)
    return pl.pallas_call(
        _outproj_kernel,
        grid=(t // tm,),
        in_specs=[pl.BlockSpec((tm, d_attn), row), pl.BlockSpec((tm, yc.shape[1]), row), pl.BlockSpec((tm, d_model), row),
                  full(gattn), pl.BlockSpec(w_out.shape, const, pipeline_mode=pl.Buffered(1)), full(gmoe), full(w_gr), full(b_gr)],
        out_specs=[pl.BlockSpec((tm, d_model), row), pl.BlockSpec((tm * SLAB, LANES), row), pl.BlockSpec((tm, LANES), row)],
        out_shape=[jax.ShapeDtypeStruct((t, d_model), F32), jax.ShapeDtypeStruct((t * SLAB, LANES), F32),
                   jax.ShapeDtypeStruct((t, LANES), F32)],
        compiler_params=pltpu.CompilerParams(dimension_semantics=("arbitrary",), vmem_limit_bytes=VMEM_LIMIT),
        name="outproj",
    )(ya, yc, x2, gattn, w_out, gmoe, w_gr, b_gr)


def _row_gather_start(idx_ref, src_hbm, dst_ref, sem, n_rows):
    def body(r, carry):
        src = src_hbm.at[pl.ds(idx_ref[0, 0, r] * SLAB, SLAB), :]
        pltpu.make_async_copy(src, dst_ref.at[pl.ds(r * SLAB_PITCH, SLAB), :], sem).start()
        return carry
    lax.fori_loop(0, n_rows, body, 0)


def _row_gather_wait(src_hbm, dst_ref, sem, n_rows):
    pltpu.make_async_copy(src_hbm.at[pl.ds(0, n_rows * SLAB), :], dst_ref.at[pl.ds(0, n_rows * SLAB), :], sem).wait()


def _moe_kernel(tile_ref, expert_ref, lo_ref, hi_ref, n_items_ref, tok_cur_ref, tok_next_ref, xm_hbm,
                w1_ref, w3_ref, w2_ref, y_ref, xbuf_ref, sem_ref, *, n_row_tiles, tm):
    w = pl.program_id(0)
    tile = tile_ref[w]
    first = (w == 0) | (tile != tile_ref[jnp.maximum(w - 1, 0)])
    active = w < n_items_ref[0]
    slot = tile % 2
    buf_rows = tm * SLAB_PITCH

    def buf(s):
        return xbuf_ref.at[pl.ds(s * buf_rows, buf_rows), :]

    @pl.when(w == 0)
    def _():
        _row_gather_start(tok_cur_ref, xm_hbm, buf(0), sem_ref.at[0], tm)

    @pl.when(active & first & (tile + 1 < n_row_tiles))
    def _():
        _row_gather_start(tok_next_ref, xm_hbm, buf(1 - slot), sem_ref.at[1 - slot], tm)

    @pl.when(active)
    def _():
        @pl.when(first)
        def _():
            _row_gather_wait(xm_hbm, buf(slot), sem_ref.at[slot], tm)

        x = _load_slabs(xbuf_ref, slot * buf_rows, tm, SLAB, SLAB_PITCH).astype(BF16)
        a = jnp.dot(x, w1_ref[0].astype(BF16), preferred_element_type=F32)
        g = jnp.dot(x, w3_ref[0].astype(BF16), preferred_element_type=F32)
        hid = (a * jax.nn.sigmoid(a) * g).astype(BF16)
        y = jnp.dot(hid, w2_ref[0].astype(BF16), preferred_element_type=F32)

        @pl.when(first)
        def _():
            _store_slabs(y_ref, y, SLAB)

        @pl.when(jnp.logical_not(first))
        def _():
            row = lax.broadcasted_iota(jnp.int32, (tm, 1), 0)
            mine = (row >= lo_ref[w]) & (row < hi_ref[w])
            _store_slabs(y_ref, y, SLAB, mask=mine)


def _moe(plan, xm, w1, w3, w2):
    tile_id, expert_id, row_lo, row_hi, n_items, row_token3 = plan
    n_row_tiles = row_token3.shape[0]
    tm = TM_MOE
    d_model = w1.shape[1]
    d_exp = w1.shape[2]
    last = n_row_tiles - 1
    grid_spec = pltpu.PrefetchScalarGridSpec(
        num_scalar_prefetch=5,
        grid=(tile_id.shape[0],),
        in_specs=[pl.BlockSpec((1, 1, tm), lambda w, t, e, lo, hi, n: (t[w], 0, 0), memory_space=pltpu.SMEM),
                  pl.BlockSpec((1, 1, tm), lambda w, t, e, lo, hi, n: (jnp.minimum(t[w] + 1, last), 0, 0),
                               memory_space=pltpu.SMEM),
                  pl.BlockSpec(memory_space=pl.ANY),
                  pl.BlockSpec((1, d_model, d_exp), lambda w, t, e, lo, hi, n: (e[w], 0, 0)),
                  pl.BlockSpec((1, d_model, d_exp), lambda w, t, e, lo, hi, n: (e[w], 0, 0)),
                  pl.BlockSpec((1, d_exp, d_model), lambda w, t, e, lo, hi, n: (e[w], 0, 0))],
        out_specs=pl.BlockSpec((tm * SLAB, LANES), lambda w, t, e, lo, hi, n: (t[w], 0)),
        scratch_shapes=[pltpu.VMEM((2 * tm * SLAB_PITCH, LANES), F32), pltpu.SemaphoreType.DMA((2,))],
    )
    return pl.pallas_call(
        functools.partial(_moe_kernel, n_row_tiles=n_row_tiles, tm=tm),
        grid_spec=grid_spec,
        out_shape=jax.ShapeDtypeStruct((n_row_tiles * tm * SLAB, LANES), F32),
        compiler_params=pltpu.CompilerParams(dimension_semantics=("arbitrary",), vmem_limit_bytes=VMEM_LIMIT),
        name="moe",
    )(tile_id, expert_id, row_lo, row_hi, n_items, row_token3, row_token3, xm, w1, w3, w2)


def _ple_kernel(dest_cur_ref, dest_next_ref, y_hbm, h1_ref, route_ref, p_ref, wple_ref, gple_ref, wpg_ref, bpg_ref,
                o_ref, ybuf_ref, sem_ref):
    i = pl.program_id(0)
    n = pl.num_programs(0)
    tm, d_model = h1_ref.shape
    slot = i % 2
    buf_rows = TOP_K * tm * SLAB_PITCH

    def buf(s):
        return ybuf_ref.at[pl.ds(s * buf_rows, buf_rows), :]

    @pl.when(i == 0)
    def _():
        _row_gather_start(dest_cur_ref, y_hbm, buf(0), sem_ref.at[0], TOP_K * tm)

    @pl.when(i + 1 < n)
    def _():
        _row_gather_start(dest_next_ref, y_hbm, buf(1 - slot), sem_ref.at[1 - slot], TOP_K * tm)

    _row_gather_wait(y_hbm, buf(slot), sem_ref.at[slot], TOP_K * tm)
    route = route_ref[...]
    y0 = _load_slabs(ybuf_ref, slot * buf_rows, tm, SLAB, SLAB_PITCH)
    y1 = _load_slabs(ybuf_ref, slot * buf_rows + tm * SLAB_PITCH, tm, SLAB, SLAB_PITCH)
    h2 = h1_ref[...] + (route[:, 2:3] * y0 + route[:, 3:4] * y1)
    xn = _rms(h2, gple_ref[...], d_model).astype(BF16)
    gate = jax.nn.sigmoid(jnp.dot(xn, wpg_ref[...], preferred_element_type=F32) + bpg_ref[...])
    pe = jnp.dot(p_ref[...].astype(BF16), wple_ref[...], preferred_element_type=F32)
    o_ref[...] = h2 + gate * pe


def _ple(dest3, y, h1, route, p2, w_ple, gple, w_pg, b_pg):
    t, d_model = h1.shape
    tm = TM_PLE
    n_tiles = t // tm
    last = n_tiles - 1
    const = lambda i: (0, 0)
    row = lambda i: (i, 0)
    full = lambda a: pl.BlockSpec(a.shape, const)
    return pl.pallas_call(
        _ple_kernel,
        grid=(n_tiles,),
        in_specs=[pl.BlockSpec((1, 1, TOP_K * tm), lambda i: (i, 0, 0), memory_space=pltpu.SMEM),
                  pl.BlockSpec((1, 1, TOP_K * tm), lambda i: (jnp.minimum(i + 1, last), 0, 0), memory_space=pltpu.SMEM),
                  pl.BlockSpec(memory_space=pl.ANY),
                  pl.BlockSpec((tm, d_model), row), pl.BlockSpec((tm, LANES), row), pl.BlockSpec((tm, p2.shape[1]), row),
                  full(w_ple), full(gple), pl.BlockSpec(w_pg.shape, const, pipeline_mode=pl.Buffered(1)), full(b_pg)],
        out_specs=pl.BlockSpec((tm, d_model), row),
        out_shape=jax.ShapeDtypeStruct((t, d_model), F32),
        scratch_shapes=[pltpu.VMEM((2 * TOP_K * tm * SLAB_PITCH, LANES), F32), pltpu.SemaphoreType.DMA((2,))],
        compiler_params=pltpu.CompilerParams(dimension_semantics=("arbitrary",), vmem_limit_bytes=VMEM_LIMIT),
        name="ple",
    )(dest3, dest3, y, h1, route, p2, w_ple, gple, w_pg, b_pg)


def _dispatch_plan(route):
    t = route.shape[0]
    tm = TM_MOE
    total = TOP_K * t
    n_row_tiles = total // tm
    expert = route[:, :TOP_K].astype(jnp.int32).reshape(-1)
    pos = jnp.arange(total, dtype=jnp.int32)
    order = jnp.sort(expert * total + pos) % total
    dest = jnp.sort(order * total + pos) % total
    experts = jnp.arange(N_EXPERTS, dtype=jnp.int32)
    counts = jnp.sum((expert[:, None] == experts[None, :]).astype(jnp.int32), axis=0)
    group_end = jnp.cumsum(counts)
    group_start = group_end - counts
    cuts = jnp.sort(jnp.concatenate([jnp.arange(0, total, tm, dtype=jnp.int32), group_start]))
    nxt = jnp.concatenate([cuts[1:], jnp.full((1,), total, jnp.int32)])
    valid = nxt > cuts
    n_items = jnp.sum(valid.astype(jnp.int32))
    last_start = jnp.max(jnp.where(valid, cuts, 0))
    start = jnp.minimum(jnp.sort(jnp.where(valid, cuts, total)), last_start)
    end = jnp.concatenate([start[1:], jnp.full((1,), total, jnp.int32)])
    end = jnp.where(end > start, end, total)
    tile_id = start // tm
    expert_id = jnp.sum((group_end[None, :] <= start[:, None]).astype(jnp.int32), axis=1)
    row_lo = start - tile_id * tm
    row_hi = row_lo + (end - start)
    row_token3 = (order // TOP_K).reshape(n_row_tiles, 1, tm)
    dest3 = dest.reshape(t // TM_PLE, TM_PLE, TOP_K).transpose(0, 2, 1).reshape(t // TM_PLE, 1, TOP_K * TM_PLE)
    plan = (tile_id, expert_id, row_lo, row_hi, n_items.reshape(1), row_token3)
    return plan, dest3


def _rope_swap_pad(w):
    half = QK_ROPE // 2
    x1 = w[..., QK_NOPE:QK_NOPE + half]
    x2 = w[..., QK_NOPE + half:]
    return jnp.concatenate([w[..., :QK_NOPE], x1, x2, x2, x1], axis=-1)


def kernel(x, p, positions, norm_mix, w_in, g_q_lat, g_kv_lat, w_uq, w_ukv, g_q_head, g_k_head, w_conv, g_out_attn, g_out_conv, w_out, norm_moe, w_group, b_group, w_router, b_router, w1, w3, w2, norm_ple, w_ple, w_ple_gate, b_ple_gate):
    batch, seq, d_model = x.shape
    depth = w_in.shape[0]
    t = batch * seq
    q_rank = g_q_lat.shape[1]
    kv_rank = g_kv_lat.shape[1]
    half = QK_ROPE // 2
    rope_lo = q_rank + kv_rank

    pos2 = positions.reshape(t, 1)
    inv_freq = ROPE_BASE ** (-jnp.arange(0, QK_ROPE, 2, dtype=F32) / QK_ROPE)
    freq = jnp.tile(inv_freq, LANES // half).reshape(1, LANES)

    h = x.reshape(t, d_model)
    for i in range(depth):
        wi = w_in[i]
        w_in_p = jnp.concatenate([wi[:, :rope_lo + QK_ROPE], wi[:, rope_lo + half:rope_lo + QK_ROPE],
                                  wi[:, rope_lo:rope_lo + half], wi[:, rope_lo + QK_ROPE:]], axis=1).astype(BF16)
        wuq_p = _rope_swap_pad(w_uq[i]).reshape(q_rank, MLA_HEADS * HEAD_PAD).astype(BF16)
        wukv_p = jnp.concatenate([w_ukv[i][:, :, :QK_NOPE].reshape(kv_rank, -1),
                                  w_ukv[i][:, :, QK_NOPE:].reshape(kv_rank, -1)], axis=1).astype(BF16)
        gq_p = _rope_swap_pad(g_q_head[i]).reshape(1, HEAD_PAD)
        gk_p = _rope_swap_pad(g_k_head[i]).reshape(1, HEAD_PAD)
        q, k, v, yc = _front(h, pos2, freq, norm_mix[i].reshape(1, -1), w_in_p, g_q_lat[i].reshape(1, -1),
                             g_kv_lat[i].reshape(1, -1), wuq_p, wukv_p, gq_p, gk_p, w_conv[i],
                             g_out_conv[i].reshape(1, -1), seq=seq)
        ya = _attention(q, k, v, batch=batch, seq=seq)

        pad = LANES - N_GROUPS - N_EXPERTS
        w_gr32 = jnp.concatenate([w_group[i], w_router[i], jnp.zeros((d_model, pad), F32)], axis=1)
        w_gr_hi = w_gr32.astype(BF16)
        w_gr = jnp.concatenate([w_gr_hi, (w_gr32 - w_gr_hi.astype(F32)).astype(BF16)], axis=1)
        b_gr = jnp.concatenate([b_group[i], b_router[i], jnp.zeros((pad,), F32)]).reshape(1, LANES)
        h1, xm, route = _outproj(ya, yc, h, g_out_attn[i].reshape(1, -1), w_out[i].astype(BF16),
                                 norm_moe[i].reshape(1, -1), w_gr, b_gr)

        plan, dest3 = _dispatch_plan(route)
        y = _moe(plan, xm, w1[i], w3[i], w2[i])
        h = _ple(dest3, y, h1, route, p[i].reshape(t, -1), w_ple[i].astype(BF16), norm_ple[i].reshape(1, -1),
                 w_ple_gate[i].astype(BF16), b_ple_gate[i].reshape(1, -1))
    return h.reshape(batch, seq, d_model)
```

```python
import functools

import jax
import jax.numpy as jnp
from jax import lax
from jax.experimental import pallas as pl
from jax.experimental.pallas import tpu as pltpu

F32 = jnp.float32
BF16 = jnp.bfloat16

CHUNK = 64
EPS = 1e-6
MASK_VALUE = -1e30
ROPE_BASE = 10000.0

MLA_HEADS = 8
QK_NOPE = 128
QK_ROPE = 64
QK_HEAD = QK_NOPE + QK_ROPE
V_HEAD = 128
HEAD_PAD = 256
N_GROUPS = 4
EXPERTS_PER_GROUP = 8
N_EXPERTS = N_GROUPS * EXPERTS_PER_GROUP
TOP_K = 2
LANES = 128
SLAB = 16
SLAB_PITCH = 20

TM_FRONT = 512
TQ_ATTN = 512
TM_OUT = 512
TM_MOE = 256
TM_PLE = 256
VMEM_LIMIT = 56 * 1024 * 1024


def _rms(x, g, n):
    ms = jnp.sum(x * x, axis=-1, keepdims=True) * (1.0 / n)
    return x * lax.rsqrt(ms + EPS) * g


def _store_slabs(ref, val, pitch, mask=None):
    n, width = val.shape
    for c in range(width // LANES):
        piece = val[:, c * LANES:(c + 1) * LANES]
        rows = pl.ds(c, n, stride=pitch)
        if mask is not None:
            piece = jnp.where(mask, piece, ref[rows, :])
        ref[rows, :] = piece


def _load_slabs(ref, start, n, slab, pitch):
    return jnp.concatenate([ref[pl.ds(start + c, n, stride=pitch), :] for c in range(slab)], axis=1)


def _front_kernel(x_ref, pos_ref, freq_ref, gmix_ref, win_ref, gql_ref, gkvl_ref,
                  wuq_ref, wukv_ref, gq_ref, gk_ref, wconv_ref, gconv_ref,
                  q_ref, k_ref, v_ref, yc_ref, prev_ref, *, tiles_per_seq, q_rank, kv_rank, d_conv):
    i = pl.program_id(0)
    is_start = (i % tiles_per_seq) == 0
    tm = x_ref.shape[0]
    d_model = x_ref.shape[1]
    lat_w = q_rank + kv_rank + LANES

    xn = _rms(x_ref[...], gmix_ref[...], d_model).astype(BF16)
    lat = jnp.dot(xn, win_ref[:, :lat_w], preferred_element_type=F32)
    q_lat = lat[:, :q_rank]
    kv_lat = lat[:, q_rank:q_rank + kv_rank]
    kr = lat[:, q_rank + kv_rank:]

    qn = _rms(q_lat, gql_ref[...], q_rank).astype(BF16)
    kvn = _rms(kv_lat, gkvl_ref[...], kv_rank).astype(BF16)
    q_all = jnp.dot(qn, wuq_ref[...], preferred_element_type=F32)
    kv_all = jnp.dot(kvn, wukv_ref[...], preferred_element_type=F32)

    lane = lax.broadcasted_iota(jnp.int32, (tm, LANES), 1)
    ang = pos_ref[...].astype(F32) * freq_ref[...]
    cos = jnp.cos(ang)
    sin = jnp.sin(ang)
    rope_c = jnp.where(lane < QK_ROPE, cos, 0.0)
    rope_s = jnp.where(lane < QK_ROPE // 2, -sin, jnp.where(lane < QK_ROPE, sin, 0.0))
    rope_lanes = lane < QK_ROPE

    def rope(t):
        return t * rope_c + pltpu.roll(t, QK_ROPE, 1) * rope_s

    gq = gq_ref[...]
    gk = gk_ref[...]
    inv_n = 1.0 / QK_HEAD
    kr_ssq = jnp.sum(jnp.where(rope_lanes, kr * kr, 0.0), axis=-1, keepdims=True)
    kr_rot = rope(kr * gk[:, QK_NOPE:])
    for h in range(MLA_HEADS):
        qa = q_all[:, h * HEAD_PAD:h * HEAD_PAD + QK_NOPE]
        qb = q_all[:, h * HEAD_PAD + QK_NOPE:(h + 1) * HEAD_PAD]
        ssq = (jnp.sum(qa * qa, axis=-1, keepdims=True)
               + jnp.sum(jnp.where(rope_lanes, qb * qb, 0.0), axis=-1, keepdims=True))
        inv = lax.rsqrt(ssq * inv_n + EPS)
        q_ref[:, h * HEAD_PAD:h * HEAD_PAD + QK_NOPE] = (qa * inv * gq[:, :QK_NOPE]).astype(BF16)
        q_ref[:, h * HEAD_PAD + QK_NOPE:(h + 1) * HEAD_PAD] = rope(qb * inv * gq[:, QK_NOPE:]).astype(BF16)

        ka = kv_all[:, h * QK_NOPE:(h + 1) * QK_NOPE]
        kinv = lax.rsqrt((jnp.sum(ka * ka, axis=-1, keepdims=True) + kr_ssq) * inv_n + EPS)
        k_ref[:, h * HEAD_PAD:h * HEAD_PAD + QK_NOPE] = (ka * kinv * gk[:, :QK_NOPE]).astype(BF16)
        k_ref[:, h * HEAD_PAD + QK_NOPE:(h + 1) * HEAD_PAD] = (kr_rot * kinv).astype(BF16)
    v_ref[...] = kv_all[:, MLA_HEADS * QK_NOPE:].astype(BF16)

    cv = jnp.dot(xn, win_ref[:, lat_w:], preferred_element_type=F32)
    b_gate = cv[:, :d_conv]
    u = cv[:, d_conv:2 * d_conv] * cv[:, 2 * d_conv:]

    @pl.when(is_start)
    def _():
        prev_ref[...] = jnp.zeros_like(prev_ref)

    prev = prev_ref[...]
    hm1 = prev[7:8, :]
    hm2 = prev[6:7, :]
    row = lax.broadcasted_iota(jnp.int32, u.shape, 0)
    u1 = jnp.where(row == 0, hm1, pltpu.roll(u, 1, 0))
    u2 = jnp.where(row == 0, hm2, jnp.where(row == 1, hm1, pltpu.roll(u, 2, 0)))
    prev_ref[...] = u[tm - 8:, :]
    wc = wconv_ref[...]
    y = b_gate * (u2 * wc[0:1, :] + u1 * wc[1:2, :] + u * wc[2:3, :])
    yc_ref[...] = _rms(y, gconv_ref[...], d_conv).astype(BF16)


def _front(x2, pos2, freq, gmix, w_in_p, gql, gkvl, wuq_p, wukv_p, gq_p, gk_p, wconv, gconv, *, seq):
    t, d_model = x2.shape
    tm = TM_FRONT
    q_rank = gql.shape[1]
    kv_rank = gkvl.shape[1]
    d_conv = gconv.shape[1]
    const = lambda i: (0, 0)
    row = lambda i: (i, 0)
    full = lambda a: pl.BlockSpec(a.shape, const)
    kern = functools.partial(_front_kernel, tiles_per_seq=seq // tm, q_rank=q_rank, kv_rank=kv_rank, d_conv=d_conv)
    return pl.pallas_call(
        kern,
        grid=(t // tm,),
        in_specs=[pl.BlockSpec((tm, d_model), row), pl.BlockSpec((tm, 1), row), full(freq), full(gmix),
                  pl.BlockSpec(w_in_p.shape, const, pipeline_mode=pl.Buffered(1)),
                  full(gql), full(gkvl), full(wuq_p), full(wukv_p), full(gq_p), full(gk_p), full(wconv), full(gconv)],
        out_specs=[pl.BlockSpec((tm, MLA_HEADS * HEAD_PAD), row), pl.BlockSpec((tm, MLA_HEADS * HEAD_PAD), row),
                   pl.BlockSpec((tm, MLA_HEADS * V_HEAD), row), pl.BlockSpec((tm, d_conv), row)],
        out_shape=[jax.ShapeDtypeStruct((t, MLA_HEADS * HEAD_PAD), BF16), jax.ShapeDtypeStruct((t, MLA_HEADS * HEAD_PAD), BF16),
                   jax.ShapeDtypeStruct((t, MLA_HEADS * V_HEAD), BF16), jax.ShapeDtypeStruct((t, d_conv), BF16)],
        scratch_shapes=[pltpu.VMEM((8, d_conv), F32)],
        compiler_params=pltpu.CompilerParams(dimension_semantics=("arbitrary",), vmem_limit_bytes=VMEM_LIMIT),
        name="front",
    )(x2, pos2, freq, gmix, w_in_p, gql, gkvl, wuq_p, wukv_p, gq_p, gk_p, wconv, gconv)


def _attn_kernel(q_ref, k_ref, v_ref, o_ref):
    i = pl.program_id(2)
    tq = q_ref.shape[0]
    q = q_ref[...]
    scale = QK_HEAD ** -0.5
    r = lax.broadcasted_iota(jnp.int32, (tq, tq), 0)
    c = lax.broadcasted_iota(jnp.int32, (tq, tq), 1)
    diag_mask = (c // CHUNK) <= (r // CHUNK)

    def step(j, carry, masked):
        m, l, acc = carry
        start = pl.multiple_of(j * tq, tq)
        kb = k_ref[pl.ds(start, tq), :]
        vb = v_ref[pl.ds(start, tq), :]
        s = lax.dot_general(q, kb, (((1,), (1,)), ((), ())), preferred_element_type=F32) * scale
        if masked:
            s = jnp.where(diag_mask, s, MASK_VALUE)
        m_new = jnp.maximum(m, jnp.max(s, axis=-1, keepdims=True))
        alpha = jnp.exp(m - m_new)
        p = jnp.exp(s - m_new)
        l = alpha * l + jnp.sum(p, axis=-1, keepdims=True)
        acc = alpha * acc + jnp.dot(p.astype(BF16), vb, preferred_element_type=F32)
        return m_new, l, acc

    init = (jnp.full((tq, 1), MASK_VALUE, F32), jnp.zeros((tq, 1), F32), jnp.zeros((tq, V_HEAD), F32))
    carry = lax.fori_loop(0, i, lambda j, cr: step(j, cr, False), init)
    _, l, acc = step(i, carry, True)
    o_ref[...] = acc / l


def _attention(q, k, v, *, batch, seq):
    tq = TQ_ATTN
    nq = seq // tq
    return pl.pallas_call(
        _attn_kernel,
        grid=(batch, MLA_HEADS, nq),
        in_specs=[pl.BlockSpec((tq, HEAD_PAD), lambda b, h, i: (b * nq + i, h)),
                  pl.BlockSpec((seq, HEAD_PAD), lambda b, h, i: (b, h)),
                  pl.BlockSpec((seq, V_HEAD), lambda b, h, i: (b, h))],
        out_specs=pl.BlockSpec((tq, V_HEAD), lambda b, h, i: (b * nq + i, h)),
        out_shape=jax.ShapeDtypeStruct((batch * seq, MLA_HEADS * V_HEAD), F32),
        compiler_params=pltpu.CompilerParams(dimension_semantics=("arbitrary", "arbitrary", "arbitrary"),
                                             vmem_limit_bytes=VMEM_LIMIT),
        name="attn",
    )(q, k, v)


def _outproj_kernel(ya_ref, yc_ref, x_ref, gattn_ref, wout_ref, gmoe_ref, wgr_ref, bgr_ref,
                    h1_ref, xm_ref, route_ref):
    tm, d_attn = ya_ref.shape
    d_model = x_ref.shape[1]
    ya = _rms(ya_ref[...], gattn_ref[...], d_attn).astype(BF16)
    h1 = (x_ref[...]
          + jnp.dot(ya, wout_ref[:d_attn, :], preferred_element_type=F32)
          + jnp.dot(yc_ref[...], wout_ref[d_attn:, :], preferred_element_type=F32))
    h1_ref[...] = h1
    xm = _rms(h1, gmoe_ref[...], d_model)
    _store_slabs(xm_ref, xm, SLAB)

    x_hi = xm.astype(BF16)
    x_lo = (xm - x_hi.astype(F32)).astype(BF16)
    hi_part = jnp.dot(x_hi, wgr_ref[...], preferred_element_type=F32)
    lo_part = jnp.dot(x_lo, wgr_ref[:, :LANES], preferred_element_type=F32)
    logits = hi_part[:, :LANES] + hi_part[:, LANES:] + lo_part + bgr_ref[...]
    lane = lax.broadcasted_iota(jnp.int32, (tm, LANES), 1).astype(F32)
    neg = -jnp.inf
    big = 1e9
    gl = jnp.where(lane < N_GROUPS, logits, neg)
    gmax = jnp.max(gl, axis=-1, keepdims=True)
    gsel = jnp.min(jnp.where(gl == gmax, lane, big), axis=-1, keepdims=True)
    p_g = 1.0 / jnp.sum(jnp.exp(gl - gmax), axis=-1, keepdims=True)
    lo = N_GROUPS + EXPERTS_PER_GROUP * gsel
    el = jnp.where((lane >= lo) & (lane < lo + EXPERTS_PER_GROUP), logits, neg)
    v1 = jnp.max(el, axis=-1, keepdims=True)
    i1 = jnp.min(jnp.where(el == v1, lane, big), axis=-1, keepdims=True)
    el2 = jnp.where(lane == i1, neg, el)
    v2 = jnp.max(el2, axis=-1, keepdims=True)
    i2 = jnp.min(jnp.where(el2 == v2, lane, big), axis=-1, keepdims=True)
    e2 = jnp.exp(v2 - v1)
    den = 1.0 + e2
    w1 = p_g / den
    w2 = p_g * e2 / den
    route_ref[...] = jnp.where(lane == 0, i1 - N_GROUPS,
                               jnp.where(lane == 1, i2 - N_GROUPS,
                                         jnp.where(lane == 2, w1, jnp.where(lane == 3, w2, 0.0))))


def _outproj(ya, yc, x2, gattn, w_out, gmoe, w_gr, b_gr):
    t, d_model = x2.shape
    d_attn = ya.shape[1]
    tm = TM_OUT
    const = lambda i: (0, 0)
    row = lambda i: (i, 0)
    full = lambda a: pl.BlockSpec(a.shape, const)
    return pl.pallas_call(
        _outproj_kernel,
        grid=(t // tm,),
        in_specs=[pl.BlockSpec((tm, d_attn), row), pl.BlockSpec((tm, yc.shape[1]), row), pl.BlockSpec((tm, d_model), row),
                  full(gattn), pl.BlockSpec(w_out.shape, const, pipeline_mode=pl.Buffered(1)), full(gmoe), full(w_gr), full(b_gr)],
        out_specs=[pl.BlockSpec((tm, d_model), row), pl.BlockSpec((tm * SLAB, LANES), row), pl.BlockSpec((tm, LANES), row)],
        out_shape=[jax.ShapeDtypeStruct((t, d_model), F32), jax.ShapeDtypeStruct((t * SLAB, LANES), F32),
                   jax.ShapeDtypeStruct((t, LANES), F32)],
        compiler_params=pltpu.CompilerParams(dimension_semantics=("arbitrary",), vmem_limit_bytes=VMEM_LIMIT),
        name="outproj",
    )(ya, yc, x2, gattn, w_out, gmoe, w_gr, b_gr)


def _row_gather_start(idx_ref, src_hbm, dst_ref, sem, n_rows):
    def body(r, carry):
        src = src_hbm.at[pl.ds(idx_ref[0, 0, r] * SLAB, SLAB), :]
        pltpu.make_async_copy(src, dst_ref.at[pl.ds(r * SLAB_PITCH, SLAB), :], sem).start()
        return carry
    lax.fori_loop(0, n_rows, body, 0, unroll=8)


def _row_gather_wait(src_hbm, dst_ref, sem, n_rows):
    pltpu.make_async_copy(src_hbm.at[pl.ds(0, n_rows * SLAB), :], dst_ref.at[pl.ds(0, n_rows * SLAB), :], sem).wait()


def _moe_kernel(tile_expert_ref, n_active_ref, tok_cur_ref, tok_next_ref, xm_hbm, w1_ref, w3_ref, w2_ref,
                y_ref, xbuf_ref, sem_ref, *, tm):
    i = pl.program_id(0)
    n_active = n_active_ref[0]
    slot = i % 2
    buf_rows = tm * SLAB_PITCH

    def buf(s):
        return xbuf_ref.at[pl.ds(s * buf_rows, buf_rows), :]

    @pl.when(i == 0)
    def _():
        _row_gather_start(tok_cur_ref, xm_hbm, buf(0), sem_ref.at[0], tm)

    @pl.when(i + 1 < n_active)
    def _():
        _row_gather_start(tok_next_ref, xm_hbm, buf(1 - slot), sem_ref.at[1 - slot], tm)

    @pl.when(i < n_active)
    def _():
        _row_gather_wait(xm_hbm, buf(slot), sem_ref.at[slot], tm)
        x = _load_slabs(xbuf_ref, slot * buf_rows, tm, SLAB, SLAB_PITCH).astype(BF16)
        a = jnp.dot(x, w1_ref[0].astype(BF16), preferred_element_type=F32)
        g = jnp.dot(x, w3_ref[0].astype(BF16), preferred_element_type=F32)
        hid = (a * jax.nn.sigmoid(a) * g).astype(BF16)
        _store_slabs(y_ref, jnp.dot(hid, w2_ref[0].astype(BF16), preferred_element_type=F32), SLAB)

    @pl.when(i >= n_active)
    def _():
        y_ref[...] = jnp.zeros_like(y_ref)


def _moe(plan, xm, w1, w3, w2):
    tile_expert, n_active, row_token3 = plan
    n_tiles = row_token3.shape[0]
    tm = TM_MOE
    d_model = w1.shape[1]
    d_exp = w1.shape[2]
    last = n_tiles - 1
    grid_spec = pltpu.PrefetchScalarGridSpec(
        num_scalar_prefetch=2,
        grid=(n_tiles,),
        in_specs=[pl.BlockSpec((1, 1, tm), lambda i, te, na: (i, 0, 0), memory_space=pltpu.SMEM),
                  pl.BlockSpec((1, 1, tm), lambda i, te, na: (jnp.minimum(i + 1, last), 0, 0), memory_space=pltpu.SMEM),
                  pl.BlockSpec(memory_space=pl.ANY),
                  pl.BlockSpec((1, d_model, d_exp), lambda i, te, na: (te[i], 0, 0)),
                  pl.BlockSpec((1, d_model, d_exp), lambda i, te, na: (te[i], 0, 0)),
                  pl.BlockSpec((1, d_exp, d_model), lambda i, te, na: (te[i], 0, 0))],
        out_specs=pl.BlockSpec((tm * SLAB, LANES), lambda i, te, na: (i, 0)),
        scratch_shapes=[pltpu.VMEM((2 * tm * SLAB_PITCH, LANES), F32), pltpu.SemaphoreType.DMA((2,))],
    )
    return pl.pallas_call(
        functools.partial(_moe_kernel, tm=tm),
        grid_spec=grid_spec,
        out_shape=jax.ShapeDtypeStruct((n_tiles * tm * SLAB, LANES), F32),
        compiler_params=pltpu.CompilerParams(dimension_semantics=("arbitrary",), vmem_limit_bytes=VMEM_LIMIT),
        name="moe",
    )(tile_expert, n_active, row_token3, row_token3, xm, w1, w3, w2)


def _ple_kernel(dest_cur_ref, dest_next_ref, y_hbm, h1_ref, route_ref, p_ref, wple_ref, gple_ref, wpg_ref, bpg_ref,
                o_ref, ybuf_ref, sem_ref):
    i = pl.program_id(0)
    n = pl.num_programs(0)
    tm, d_model = h1_ref.shape
    slot = i % 2
    buf_rows = TOP_K * tm * SLAB_PITCH

    def buf(s):
        return ybuf_ref.at[pl.ds(s * buf_rows, buf_rows), :]

    @pl.when(i == 0)
    def _():
        _row_gather_start(dest_cur_ref, y_hbm, buf(0), sem_ref.at[0], TOP_K * tm)

    @pl.when(i + 1 < n)
    def _():
        _row_gather_start(dest_next_ref, y_hbm, buf(1 - slot), sem_ref.at[1 - slot], TOP_K * tm)

    _row_gather_wait(y_hbm, buf(slot), sem_ref.at[slot], TOP_K * tm)
    route = route_ref[...]
    y0 = _load_slabs(ybuf_ref, slot * buf_rows, tm, SLAB, SLAB_PITCH)
    y1 = _load_slabs(ybuf_ref, slot * buf_rows + tm * SLAB_PITCH, tm, SLAB, SLAB_PITCH)
    h2 = h1_ref[...] + (route[:, 2:3] * y0 + route[:, 3:4] * y1)
    xn = _rms(h2, gple_ref[...], d_model).astype(BF16)
    gate = jax.nn.sigmoid(jnp.dot(xn, wpg_ref[...], preferred_element_type=F32) + bpg_ref[...])
    pe = jnp.dot(p_ref[...].astype(BF16), wple_ref[...], preferred_element_type=F32)
    o_ref[...] = h2 + gate * pe


def _ple(dest3, y, h1, route, p2, w_ple, gple, w_pg, b_pg):
    t, d_model = h1.shape
    tm = TM_PLE
    n_tiles = t // tm
    last = n_tiles - 1
    const = lambda i: (0, 0)
    row = lambda i: (i, 0)
    full = lambda a: pl.BlockSpec(a.shape, const)
    return pl.pallas_call(
        _ple_kernel,
        grid=(n_tiles,),
        in_specs=[pl.BlockSpec((1, 1, TOP_K * tm), lambda i: (i, 0, 0), memory_space=pltpu.SMEM),
                  pl.BlockSpec((1, 1, TOP_K * tm), lambda i: (jnp.minimum(i + 1, last), 0, 0), memory_space=pltpu.SMEM),
                  pl.BlockSpec(memory_space=pl.ANY),
                  pl.BlockSpec((tm, d_model), row), pl.BlockSpec((tm, LANES), row), pl.BlockSpec((tm, p2.shape[1]), row),
                  full(w_ple), full(gple), pl.BlockSpec(w_pg.shape, const, pipeline_mode=pl.Buffered(1)), full(b_pg)],
        out_specs=pl.BlockSpec((tm, d_model), row),
        out_shape=jax.ShapeDtypeStruct((t, d_model), F32),
        scratch_shapes=[pltpu.VMEM((2 * TOP_K * tm * SLAB_PITCH, LANES), F32), pltpu.SemaphoreType.DMA((2,))],
        compiler_params=pltpu.CompilerParams(dimension_semantics=("arbitrary",), vmem_limit_bytes=VMEM_LIMIT),
        name="ple",
    )(dest3, dest3, y, h1, route, p2, w_ple, gple, w_pg, b_pg)


def _dispatch_plan(route):
    t = route.shape[0]
    tm = TM_MOE
    total = TOP_K * t
    n_tiles = total // tm + N_EXPERTS
    expert = route[:, :TOP_K].astype(jnp.int32).reshape(-1)
    pos = jnp.arange(total, dtype=jnp.int32)
    order = jnp.sort(expert * total + pos) % total
    rank = jnp.sort(order * total + pos) % total
    experts = jnp.arange(N_EXPERTS, dtype=jnp.int32)
    onehot = (expert[:, None] == experts[None, :]).astype(jnp.int32)
    counts = jnp.sum(onehot, axis=0)
    group_start = jnp.cumsum(counts) - counts
    tile_end = jnp.cumsum((counts + tm - 1) // tm)
    tile_start = tile_end - (counts + tm - 1) // tm
    n_active = tile_end[-1]
    shift = tile_start * tm - group_start
    dest = rank + jnp.sum(onehot * shift[None, :], axis=1)
    tiles = jnp.minimum(jnp.arange(n_tiles, dtype=jnp.int32), n_active - 1)
    tile_expert = jnp.sum((tile_end[None, :] <= tiles[:, None]).astype(jnp.int32), axis=1)
    is_e = tile_expert[:, None] == experts[None, :]
    src0 = tiles * tm - jnp.sum(jnp.where(is_e, shift[None, :], 0), axis=1)
    token_sorted = jnp.concatenate([order // TOP_K, jnp.zeros((tm,), jnp.int32)])
    row_token3 = jax.vmap(lambda s0: lax.dynamic_slice(token_sorted, (s0,), (tm,)))(src0).reshape(n_tiles, 1, tm)
    dest3 = dest.reshape(t // TM_PLE, TM_PLE, TOP_K).transpose(0, 2, 1).reshape(t // TM_PLE, 1, TOP_K * TM_PLE)
    plan = (tile_expert, n_active.reshape(1), row_token3)
    return plan, dest3


def _rope_swap_pad(w):
    half = QK_ROPE // 2
    x1 = w[..., QK_NOPE:QK_NOPE + half]
    x2 = w[..., QK_NOPE + half:]
    return jnp.concatenate([w[..., :QK_NOPE], x1, x2, x2, x1], axis=-1)


def kernel(x, p, positions, norm_mix, w_in, g_q_lat, g_kv_lat, w_uq, w_ukv, g_q_head, g_k_head, w_conv, g_out_attn, g_out_conv, w_out, norm_moe, w_group, b_group, w_router, b_router, w1, w3, w2, norm_ple, w_ple, w_ple_gate, b_ple_gate):
    batch, seq, d_model = x.shape
    depth = w_in.shape[0]
    t = batch * seq
    q_rank = g_q_lat.shape[1]
    kv_rank = g_kv_lat.shape[1]
    half = QK_ROPE // 2
    rope_lo = q_rank + kv_rank

    pos2 = positions.reshape(t, 1)
    inv_freq = ROPE_BASE ** (-jnp.arange(0, QK_ROPE, 2, dtype=F32) / QK_ROPE)
    freq = jnp.tile(inv_freq, LANES // half).reshape(1, LANES)

    h = x.reshape(t, d_model)
    for i in range(depth):
        wi = w_in[i]
        w_in_p = jnp.concatenate([wi[:, :rope_lo + QK_ROPE], wi[:, rope_lo + half:rope_lo + QK_ROPE],
                                  wi[:, rope_lo:rope_lo + half], wi[:, rope_lo + QK_ROPE:]], axis=1).astype(BF16)
        wuq_p = _rope_swap_pad(w_uq[i]).reshape(q_rank, MLA_HEADS * HEAD_PAD).astype(BF16)
        wukv_p = jnp.concatenate([w_ukv[i][:, :, :QK_NOPE].reshape(kv_rank, -1),
                                  w_ukv[i][:, :, QK_NOPE:].reshape(kv_rank, -1)], axis=1).astype(BF16)
        gq_p = _rope_swap_pad(g_q_head[i]).reshape(1, HEAD_PAD)
        gk_p = _rope_swap_pad(g_k_head[i]).reshape(1, HEAD_PAD)
        q, k, v, yc = _front(h, pos2, freq, norm_mix[i].reshape(1, -1), w_in_p, g_q_lat[i].reshape(1, -1),
                             g_kv_lat[i].reshape(1, -1), wuq_p, wukv_p, gq_p, gk_p, w_conv[i],
                             g_out_conv[i].reshape(1, -1), seq=seq)
        ya = _attention(q, k, v, batch=batch, seq=seq)

        pad = LANES - N_GROUPS - N_EXPERTS
        w_gr32 = jnp.concatenate([w_group[i], w_router[i], jnp.zeros((d_model, pad), F32)], axis=1)
        w_gr_hi = w_gr32.astype(BF16)
        w_gr = jnp.concatenate([w_gr_hi, (w_gr32 - w_gr_hi.astype(F32)).astype(BF16)], axis=1)
        b_gr = jnp.concatenate([b_group[i], b_router[i], jnp.zeros((pad,), F32)]).reshape(1, LANES)
        h1, xm, route = _outproj(ya, yc, h, g_out_attn[i].reshape(1, -1), w_out[i].astype(BF16),
                                 norm_moe[i].reshape(1, -1), w_gr, b_gr)

        plan, dest3 = _dispatch_plan(route)
        y = _moe(plan, xm, w1[i], w3[i], w2[i])
        h = _ple(dest3, y, h1, route, p[i].reshape(t, -1), w_ple[i].astype(BF16), norm_ple[i].reshape(1, -1),
                 w_ple_gate[i].astype(BF16), b_ple_gate[i].reshape(1, -1))
    return h.reshape(batch, seq, d_model)
```

```python
import functools

import jax
import jax.numpy as jnp
from jax import lax
from jax.experimental import pallas as pl
from jax.experimental.pallas import tpu as pltpu

F32 = jnp.float32
BF16 = jnp.bfloat16

CHUNK = 64
EPS = 1e-6
MASK_VALUE = -1e30
ROPE_BASE = 10000.0

MLA_HEADS = 8
QK_NOPE = 128
QK_ROPE = 64
QK_HEAD = QK_NOPE + QK_ROPE
V_HEAD = 128
HEAD_PAD = 256
Q_SCALE = QK_HEAD ** -0.5 * 1.4426950408889634
N_GROUPS = 4
EXPERTS_PER_GROUP = 8
N_EXPERTS = N_GROUPS * EXPERTS_PER_GROUP
TOP_K = 2
LANES = 128
SLAB = 16
SLAB_PITCH = 20

TM_FRONT = 512
TQ_ATTN = 1024
TM_OUT = 512
TM_MOE = 256
TM_PLE = 256
VMEM_LIMIT = 56 * 1024 * 1024


def _rms(x, g, n):
    ms = jnp.sum(x * x, axis=-1, keepdims=True) * (1.0 / n)
    return x * lax.rsqrt(ms + EPS) * g


def _store_slabs(ref, val, pitch, mask=None):
    n, width = val.shape
    for c in range(width // LANES):
        piece = val[:, c * LANES:(c + 1) * LANES]
        rows = pl.ds(c, n, stride=pitch)
        if mask is not None:
            piece = jnp.where(mask, piece, ref[rows, :])
        ref[rows, :] = piece


def _load_slabs(ref, start, n, slab, pitch):
    return jnp.concatenate([ref[pl.ds(start + c, n, stride=pitch), :] for c in range(slab)], axis=1)


def _front_kernel(x_ref, pos_ref, freq_ref, gmix_ref, win_ref, gql_ref, gkvl_ref,
                  wuq_ref, wukv_ref, gq_ref, gk_ref, wconv_ref, gconv_ref,
                  q_ref, k_ref, v_ref, yc_ref, prev_ref, *, tiles_per_seq, q_rank, kv_rank, d_conv):
    i = pl.program_id(0)
    is_start = (i % tiles_per_seq) == 0
    tm = x_ref.shape[0]
    d_model = x_ref.shape[1]
    lat_w = q_rank + kv_rank + LANES

    xn = _rms(x_ref[...], gmix_ref[...], d_model).astype(BF16)
    lat = jnp.dot(xn, win_ref[:, :lat_w], preferred_element_type=F32)
    q_lat = lat[:, :q_rank]
    kv_lat = lat[:, q_rank:q_rank + kv_rank]
    kr = lat[:, q_rank + kv_rank:]

    qn = _rms(q_lat, gql_ref[...], q_rank).astype(BF16)
    kvn = _rms(kv_lat, gkvl_ref[...], kv_rank).astype(BF16)
    q_all = jnp.dot(qn, wuq_ref[...], preferred_element_type=F32)
    kv_all = jnp.dot(kvn, wukv_ref[...], preferred_element_type=F32)

    lane = lax.broadcasted_iota(jnp.int32, (tm, LANES), 1)
    ang = pos_ref[...].astype(F32) * freq_ref[...]
    cos = jnp.cos(ang)
    sin = jnp.sin(ang)
    rope_c = jnp.where(lane < QK_ROPE, cos, 0.0)
    rope_s = jnp.where(lane < QK_ROPE // 2, -sin, jnp.where(lane < QK_ROPE, sin, 0.0))
    rope_lanes = lane < QK_ROPE

    def rope(t):
        return t * rope_c + pltpu.roll(t, QK_ROPE, 1) * rope_s

    gq = gq_ref[...]
    gk = gk_ref[...]
    inv_n = 1.0 / QK_HEAD
    kr_ssq = jnp.sum(jnp.where(rope_lanes, kr * kr, 0.0), axis=-1, keepdims=True)
    kr_rot = rope(kr * gk[:, QK_NOPE:])
    for h in range(MLA_HEADS):
        qa = q_all[:, h * HEAD_PAD:h * HEAD_PAD + QK_NOPE]
        qb = q_all[:, h * HEAD_PAD + QK_NOPE:(h + 1) * HEAD_PAD]
        ssq = (jnp.sum(qa * qa, axis=-1, keepdims=True)
               + jnp.sum(jnp.where(rope_lanes, qb * qb, 0.0), axis=-1, keepdims=True))
        inv = lax.rsqrt(ssq * inv_n + EPS)
        q_ref[:, h * HEAD_PAD:h * HEAD_PAD + QK_NOPE] = (qa * inv * gq[:, :QK_NOPE] * Q_SCALE).astype(BF16)
        q_ref[:, h * HEAD_PAD + QK_NOPE:(h + 1) * HEAD_PAD] = (rope(qb * inv * gq[:, QK_NOPE:]) * Q_SCALE).astype(BF16)

        ka = kv_all[:, h * QK_NOPE:(h + 1) * QK_NOPE]
        kinv = lax.rsqrt((jnp.sum(ka * ka, axis=-1, keepdims=True) + kr_ssq) * inv_n + EPS)
        k_ref[:, h * HEAD_PAD:h * HEAD_PAD + QK_NOPE] = (ka * kinv * gk[:, :QK_NOPE]).astype(BF16)
        k_ref[:, h * HEAD_PAD + QK_NOPE:(h + 1) * HEAD_PAD] = (kr_rot * kinv).astype(BF16)
        v_lo = MLA_HEADS * QK_NOPE + h * V_HEAD
        v_ref[:, h * HEAD_PAD:h * HEAD_PAD + V_HEAD] = kv_all[:, v_lo:v_lo + V_HEAD].astype(BF16)
        v_ref[:, h * HEAD_PAD + V_HEAD:(h + 1) * HEAD_PAD] = jnp.ones((tm, HEAD_PAD - V_HEAD), BF16)

    cv = jnp.dot(xn, win_ref[:, lat_w:], preferred_element_type=F32)
    b_gate = cv[:, :d_conv]
    u = cv[:, d_conv:2 * d_conv] * cv[:, 2 * d_conv:]

    @pl.when(is_start)
    def _():
        prev_ref[...] = jnp.zeros_like(prev_ref)

    prev = prev_ref[...]
    hm1 = prev[7:8, :]
    hm2 = prev[6:7, :]
    row = lax.broadcasted_iota(jnp.int32, u.shape, 0)
    u1 = jnp.where(row == 0, hm1, pltpu.roll(u, 1, 0))
    u2 = jnp.where(row == 0, hm2, jnp.where(row == 1, hm1, pltpu.roll(u, 2, 0)))
    prev_ref[...] = u[tm - 8:, :]
    wc = wconv_ref[...]
    y = b_gate * (u2 * wc[0:1, :] + u1 * wc[1:2, :] + u * wc[2:3, :])
    yc_ref[...] = _rms(y, gconv_ref[...], d_conv).astype(BF16)


def _front(x2, pos2, freq, gmix, w_in_p, gql, gkvl, wuq_p, wukv_p, gq_p, gk_p, wconv, gconv, *, seq):
    t, d_model = x2.shape
    tm = TM_FRONT
    q_rank = gql.shape[1]
    kv_rank = gkvl.shape[1]
    d_conv = gconv.shape[1]
    const = lambda i: (0, 0)
    row = lambda i: (i, 0)
    full = lambda a: pl.BlockSpec(a.shape, const)
    kern = functools.partial(_front_kernel, tiles_per_seq=seq // tm, q_rank=q_rank, kv_rank=kv_rank, d_conv=d_conv)
    return pl.pallas_call(
        kern,
        grid=(t // tm,),
        in_specs=[pl.BlockSpec((tm, d_model), row), pl.BlockSpec((tm, 1), row), full(freq), full(gmix),
                  pl.BlockSpec(w_in_p.shape, const, pipeline_mode=pl.Buffered(1)),
                  full(gql), full(gkvl), full(wuq_p), full(wukv_p), full(gq_p), full(gk_p), full(wconv), full(gconv)],
        out_specs=[pl.BlockSpec((tm, MLA_HEADS * HEAD_PAD), row), pl.BlockSpec((tm, MLA_HEADS * HEAD_PAD), row),
                   pl.BlockSpec((tm, MLA_HEADS * HEAD_PAD), row), pl.BlockSpec((tm, d_conv), row)],
        out_shape=[jax.ShapeDtypeStruct((t, MLA_HEADS * HEAD_PAD), BF16), jax.ShapeDtypeStruct((t, MLA_HEADS * HEAD_PAD), BF16),
                   jax.ShapeDtypeStruct((t, MLA_HEADS * HEAD_PAD), BF16), jax.ShapeDtypeStruct((t, d_conv), BF16)],
        scratch_shapes=[pltpu.VMEM((8, d_conv), F32)],
        compiler_params=pltpu.CompilerParams(dimension_semantics=("arbitrary",), vmem_limit_bytes=VMEM_LIMIT),
        name="front",
    )(x2, pos2, freq, gmix, w_in_p, gql, gkvl, wuq_p, wukv_p, gq_p, gk_p, wconv, gconv)


def _attn_kernel(q_ref, k_ref, v_ref, o_ref):
    i = pl.program_id(2)
    tq = q_ref.shape[0]
    half = tq // 2
    q_halves = (q_ref[:half, :], q_ref[half:, :])

    def step(start, n_keys, masks, carry):
        kvs = [(k_ref[pl.ds(start, n), :], v_ref[pl.ds(start, n), :]) for n in n_keys]
        scores = [lax.dot_general(qh, kb, (((1,), (1,)), ((), ())), preferred_element_type=F32)
                  for qh, (kb, _) in zip(q_halves, kvs)]
        scores = [s if mask is None else jnp.where(mask, s, MASK_VALUE) for s, mask in zip(scores, masks)]
        m_new = [jnp.maximum(m, jnp.max(s, axis=-1, keepdims=True)) for s, (m, _) in zip(scores, carry)]
        probs = [jnp.exp2(s - mn).astype(BF16) for s, mn in zip(scores, m_new)]
        return tuple((mn, jnp.exp2(m - mn) * acc + jnp.dot(p, vb, preferred_element_type=F32))
                     for p, mn, (m, acc), (_, vb) in zip(probs, m_new, carry, kvs))

    init = tuple((jnp.full((half, 1), MASK_VALUE, F32), jnp.zeros((half, HEAD_PAD), F32)) for _ in range(2))
    carry = lax.fori_loop(0, i, lambda j, cr: step(pl.multiple_of(j * tq, tq), (tq, tq), (None, None), cr), init)

    def chunk_mask(n, first_row):
        r = (lax.broadcasted_iota(jnp.int32, (half, n), 0) + first_row) // CHUNK
        c = lax.broadcasted_iota(jnp.int32, (half, n), 1) // CHUNK
        return c <= r

    (_, acc0), (_, acc1) = step(pl.multiple_of(i * tq, tq), (half, tq), (chunk_mask(half, 0), chunk_mask(tq, half)), carry)
    o_ref[:half, :] = acc0[:, :V_HEAD] / acc0[:, V_HEAD:]
    o_ref[half:, :] = acc1[:, :V_HEAD] / acc1[:, V_HEAD:]


def _attention(q, k, v, *, batch, seq):
    tq = TQ_ATTN
    nq = seq // tq
    return pl.pallas_call(
        _attn_kernel,
        grid=(batch, MLA_HEADS, nq),
        in_specs=[pl.BlockSpec((tq, HEAD_PAD), lambda b, h, i: (b * nq + i, h)),
                  pl.BlockSpec((seq, HEAD_PAD), lambda b, h, i: (b, h)),
                  pl.BlockSpec((seq, HEAD_PAD), lambda b, h, i: (b, h))],
        out_specs=pl.BlockSpec((tq, V_HEAD), lambda b, h, i: (b * nq + i, h)),
        out_shape=jax.ShapeDtypeStruct((batch * seq, MLA_HEADS * V_HEAD), F32),
        compiler_params=pltpu.CompilerParams(dimension_semantics=("arbitrary", "arbitrary", "arbitrary"),
                                             vmem_limit_bytes=VMEM_LIMIT),
        name="attn",
    )(q, k, v)


def _outproj_kernel(ya_ref, yc_ref, x_ref, gattn_ref, wout_ref, gmoe_ref, wgr_ref, bgr_ref,
                    h1_ref, xm_ref, route_ref):
    tm, d_attn = ya_ref.shape
    d_model = x_ref.shape[1]
    ya = _rms(ya_ref[...], gattn_ref[...], d_attn).astype(BF16)
    h1 = (x_ref[...]
          + jnp.dot(ya, wout_ref[:d_attn, :], preferred_element_type=F32)
          + jnp.dot(yc_ref[...], wout_ref[d_attn:, :], preferred_element_type=F32))
    h1_ref[...] = h1
    xm = _rms(h1, gmoe_ref[...], d_model)
    _store_slabs(xm_ref, xm, SLAB)

    x_hi = xm.astype(BF16)
    x_lo = (xm - x_hi.astype(F32)).astype(BF16)
    hi_part = jnp.dot(x_hi, wgr_ref[...], preferred_element_type=F32)
    lo_part = jnp.dot(x_lo, wgr_ref[:, :LANES], preferred_element_type=F32)
    logits = hi_part[:, :LANES] + hi_part[:, LANES:] + lo_part + bgr_ref[...]
    lane = lax.broadcasted_iota(jnp.int32, (tm, LANES), 1).astype(F32)
    neg = -jnp.inf
    big = 1e9
    gl = jnp.where(lane < N_GROUPS, logits, neg)
    gmax = jnp.max(gl, axis=-1, keepdims=True)
    gsel = jnp.min(jnp.where(gl == gmax, lane, big), axis=-1, keepdims=True)
    p_g = 1.0 / jnp.sum(jnp.exp(gl - gmax), axis=-1, keepdims=True)
    lo = N_GROUPS + EXPERTS_PER_GROUP * gsel
    el = jnp.where((lane >= lo) & (lane < lo + EXPERTS_PER_GROUP), logits, neg)
    v1 = jnp.max(el, axis=-1, keepdims=True)
    i1 = jnp.min(jnp.where(el == v1, lane, big), axis=-1, keepdims=True)
    el2 = jnp.where(lane == i1, neg, el)
    v2 = jnp.max(el2, axis=-1, keepdims=True)
    i2 = jnp.min(jnp.where(el2 == v2, lane, big), axis=-1, keepdims=True)
    e2 = jnp.exp(v2 - v1)
    den = 1.0 + e2
    w1 = p_g / den
    w2 = p_g * e2 / den
    route_ref[...] = jnp.where(lane == 0, i1 - N_GROUPS,
                               jnp.where(lane == 1, i2 - N_GROUPS,
                                         jnp.where(lane == 2, w1, jnp.where(lane == 3, w2, 0.0))))


def _outproj(ya, yc, x2, gattn, w_out, gmoe, w_gr, b_gr):
    t, d_model = x2.shape
    d_attn = ya.shape[1]
    tm = TM_OUT
    const = lambda i: (0, 0)
    row = lambda i: (i, 0)
    full = lambda a: pl.BlockSpec(a.shape, const)
    return pl.pallas_call(
        _outproj_kernel,
        grid=(t // tm,),
        in_specs=[pl.BlockSpec((tm, d_attn), row), pl.BlockSpec((tm, yc.shape[1]), row), pl.BlockSpec((tm, d_model), row),
                  full(gattn), pl.BlockSpec(w_out.shape, const, pipeline_mode=pl.Buffered(1)), full(gmoe), full(w_gr), full(b_gr)],
        out_specs=[pl.BlockSpec((tm, d_model), row), pl.BlockSpec((tm * SLAB, LANES), row), pl.BlockSpec((tm, LANES), row)],
        out_shape=[jax.ShapeDtypeStruct((t, d_model), F32), jax.ShapeDtypeStruct((t * SLAB, LANES), F32),
                   jax.ShapeDtypeStruct((t, LANES), F32)],
        compiler_params=pltpu.CompilerParams(dimension_semantics=("arbitrary",), vmem_limit_bytes=VMEM_LIMIT),
        name="outproj",
    )(ya, yc, x2, gattn, w_out, gmoe, w_gr, b_gr)


def _row_gather_start(idx_ref, src_hbm, dst_ref, sem, n_rows):
    def body(r, carry):
        src = src_hbm.at[pl.ds(idx_ref[0, 0, r] * SLAB, SLAB), :]
        pltpu.make_async_copy(src, dst_ref.at[pl.ds(r * SLAB_PITCH, SLAB), :], sem).start(priority=1)
        return carry
    lax.fori_loop(0, n_rows, body, 0, unroll=8)


def _row_gather_wait(src_hbm, dst_ref, sem, n_rows):
    pltpu.make_async_copy(src_hbm.at[pl.ds(0, n_rows * SLAB), :], dst_ref.at[pl.ds(0, n_rows * SLAB), :], sem).wait()


def _moe_kernel(tile_expert_ref, n_active_ref, first_ref, wslot_ref, next_expert_ref, tok_cur_ref, tok_next_ref,
                xm_hbm, w1_hbm, w3_hbm, w2_hbm, y_ref, xbuf_ref, w1_buf, w3_buf, w2_buf, sem_ref, wsem_ref, *, tm):
    i = pl.program_id(0)
    n_active = n_active_ref[0]
    slot = i % 2
    buf_rows = tm * SLAB_PITCH
    wslot = wslot_ref[i]

    def buf(s):
        return xbuf_ref.at[pl.ds(s * buf_rows, buf_rows), :]

    def weight_copies(e, s):
        return (pltpu.make_async_copy(w1_hbm.at[e], w1_buf.at[s], wsem_ref.at[0, s]),
                pltpu.make_async_copy(w3_hbm.at[e], w3_buf.at[s], wsem_ref.at[1, s]),
                pltpu.make_async_copy(w2_hbm.at[e], w2_buf.at[s], wsem_ref.at[2, s]))

    @pl.when(i == 0)
    def _():
        for cp in weight_copies(tile_expert_ref[0], 0):
            cp.start()
        _row_gather_start(tok_cur_ref, xm_hbm, buf(0), sem_ref.at[0], tm)

    active = i < n_active
    new_expert = active & (first_ref[i] == 1)

    @pl.when(new_expert & (next_expert_ref[i] >= 0))
    def _():
        for cp in weight_copies(next_expert_ref[i], 1 - wslot):
            cp.start()

    @pl.when(i + 1 < n_active)
    def _():
        _row_gather_start(tok_next_ref, xm_hbm, buf(1 - slot), sem_ref.at[1 - slot], tm)

    @pl.when(new_expert)
    def _():
        for cp in weight_copies(tile_expert_ref[i], wslot):
            cp.wait()

    @pl.when(active)
    def _():
        _row_gather_wait(xm_hbm, buf(slot), sem_ref.at[slot], tm)
        x = _load_slabs(xbuf_ref, slot * buf_rows, tm, SLAB, SLAB_PITCH).astype(BF16)
        a = jnp.dot(x, w1_buf[wslot].astype(BF16), preferred_element_type=F32)
        g = jnp.dot(x, w3_buf[wslot].astype(BF16), preferred_element_type=F32)
        hid = (a * jax.nn.sigmoid(a) * g).astype(BF16)
        _store_slabs(y_ref, jnp.dot(hid, w2_buf[wslot].astype(BF16), preferred_element_type=F32), SLAB)

    @pl.when(i >= n_active)
    def _():
        y_ref[...] = jnp.zeros_like(y_ref)


def _moe(plan, xm, w1, w3, w2):
    tile_expert, n_active, first, wslot, next_expert, row_token3 = plan
    n_tiles = row_token3.shape[0]
    tm = TM_MOE
    _, d_model, d_exp = w1.shape
    last = n_tiles - 1
    grid_spec = pltpu.PrefetchScalarGridSpec(
        num_scalar_prefetch=5,
        grid=(n_tiles,),
        in_specs=[pl.BlockSpec((1, 1, tm), lambda i, *_: (i, 0, 0), memory_space=pltpu.SMEM),
                  pl.BlockSpec((1, 1, tm), lambda i, *_: (jnp.minimum(i + 1, last), 0, 0), memory_space=pltpu.SMEM),
                  pl.BlockSpec(memory_space=pl.ANY), pl.BlockSpec(memory_space=pl.ANY),
                  pl.BlockSpec(memory_space=pl.ANY), pl.BlockSpec(memory_space=pl.ANY)],
        out_specs=pl.BlockSpec((tm * SLAB, LANES), lambda i, *_: (i, 0)),
        scratch_shapes=[pltpu.VMEM((2 * tm * SLAB_PITCH, LANES), F32),
                        pltpu.VMEM((2, d_model, d_exp), F32), pltpu.VMEM((2, d_model, d_exp), F32),
                        pltpu.VMEM((2, d_exp, d_model), F32),
                        pltpu.SemaphoreType.DMA((2,)), pltpu.SemaphoreType.DMA((3, 2))],
    )
    return pl.pallas_call(
        functools.partial(_moe_kernel, tm=tm),
        grid_spec=grid_spec,
        out_shape=jax.ShapeDtypeStruct((n_tiles * tm * SLAB, LANES), F32),
        compiler_params=pltpu.CompilerParams(dimension_semantics=("arbitrary",), vmem_limit_bytes=VMEM_LIMIT),
        name="moe",
    )(tile_expert, n_active, first, wslot, next_expert, row_token3, row_token3, xm, w1, w3, w2)


def _ple_kernel(dest_cur_ref, dest_next_ref, y_hbm, h1_ref, route_ref, p_ref, wple_ref, gple_ref, wpg_ref, bpg_ref,
                o_ref, ybuf_ref, sem_ref):
    i = pl.program_id(0)
    n = pl.num_programs(0)
    tm, d_model = h1_ref.shape
    slot = i % 2
    buf_rows = TOP_K * tm * SLAB_PITCH

    def buf(s):
        return ybuf_ref.at[pl.ds(s * buf_rows, buf_rows), :]

    @pl.when(i == 0)
    def _():
        _row_gather_start(dest_cur_ref, y_hbm, buf(0), sem_ref.at[0], TOP_K * tm)

    @pl.when(i + 1 < n)
    def _():
        _row_gather_start(dest_next_ref, y_hbm, buf(1 - slot), sem_ref.at[1 - slot], TOP_K * tm)

    _row_gather_wait(y_hbm, buf(slot), sem_ref.at[slot], TOP_K * tm)
    route = route_ref[...]
    y0 = _load_slabs(ybuf_ref, slot * buf_rows, tm, SLAB, SLAB_PITCH)
    y1 = _load_slabs(ybuf_ref, slot * buf_rows + tm * SLAB_PITCH, tm, SLAB, SLAB_PITCH)
    h2 = h1_ref[...] + (route[:, 2:3] * y0 + route[:, 3:4] * y1)
    xn = _rms(h2, gple_ref[...], d_model).astype(BF16)
    gate = jax.nn.sigmoid(jnp.dot(xn, wpg_ref[...], preferred_element_type=F32) + bpg_ref[...])
    pe = jnp.dot(p_ref[...].astype(BF16), wple_ref[...], preferred_element_type=F32)
    o_ref[...] = h2 + gate * pe


def _ple(dest3, y, h1, route, p2, w_ple, gple, w_pg, b_pg):
    t, d_model = h1.shape
    tm = TM_PLE
    n_tiles = t // tm
    last = n_tiles - 1
    const = lambda i: (0, 0)
    row = lambda i: (i, 0)
    full = lambda a: pl.BlockSpec(a.shape, const)
    return pl.pallas_call(
        _ple_kernel,
        grid=(n_tiles,),
        in_specs=[pl.BlockSpec((1, 1, TOP_K * tm), lambda i: (i, 0, 0), memory_space=pltpu.SMEM),
                  pl.BlockSpec((1, 1, TOP_K * tm), lambda i: (jnp.minimum(i + 1, last), 0, 0), memory_space=pltpu.SMEM),
                  pl.BlockSpec(memory_space=pl.ANY),
                  pl.BlockSpec((tm, d_model), row), pl.BlockSpec((tm, LANES), row), pl.BlockSpec((tm, p2.shape[1]), row),
                  full(w_ple), full(gple), pl.BlockSpec(w_pg.shape, const, pipeline_mode=pl.Buffered(1)), full(b_pg)],
        out_specs=pl.BlockSpec((tm, d_model), row),
        out_shape=jax.ShapeDtypeStruct((t, d_model), F32),
        scratch_shapes=[pltpu.VMEM((2 * TOP_K * tm * SLAB_PITCH, LANES), F32), pltpu.SemaphoreType.DMA((2,))],
        compiler_params=pltpu.CompilerParams(dimension_semantics=("arbitrary",), vmem_limit_bytes=VMEM_LIMIT),
        name="ple",
    )(dest3, dest3, y, h1, route, p2, w_ple, gple, w_pg, b_pg)


def _dispatch_plan(route):
    t = route.shape[0]
    tm = TM_MOE
    total = TOP_K * t
    n_tiles = total // tm + N_EXPERTS
    expert = route[:, :TOP_K].astype(jnp.int32).reshape(-1)
    pos = jnp.arange(total, dtype=jnp.int32)
    order = jnp.sort(expert * total + pos) % total
    rank = jnp.sort(order * total + pos) % total
    experts = jnp.arange(N_EXPERTS, dtype=jnp.int32)
    onehot = (expert[:, None] == experts[None, :]).astype(jnp.int32)
    counts = jnp.sum(onehot, axis=0)
    group_start = jnp.cumsum(counts) - counts
    tile_end = jnp.cumsum((counts + tm - 1) // tm)
    tile_start = tile_end - (counts + tm - 1) // tm
    n_active = tile_end[-1]
    shift = tile_start * tm - group_start
    dest = rank + jnp.sum(onehot * shift[None, :], axis=1)
    tiles = jnp.minimum(jnp.arange(n_tiles, dtype=jnp.int32), n_active - 1)
    tile_expert = jnp.sum((tile_end[None, :] <= tiles[:, None]).astype(jnp.int32), axis=1)
    j = jnp.arange(tm, dtype=jnp.int32)
    fill_row = (tile_start * tm + counts)[:, None] + j[None, :]
    fill_row = jnp.where(fill_row < (tile_end * tm)[:, None], fill_row, n_tiles * tm)
    fill_tok = (experts[:, None] * tm + j[None, :]) % t
    keys = jnp.concatenate([dest * t + pos // TOP_K, (fill_row * t + fill_tok).reshape(-1)])
    row_token3 = (jnp.sort(keys) % t).reshape(n_tiles, 1, tm)
    dest3 = dest.reshape(t // TM_PLE, TM_PLE, TOP_K).transpose(0, 2, 1).reshape(t // TM_PLE, 1, TOP_K * TM_PLE)
    prev_expert = jnp.concatenate([jnp.full((1,), -1, jnp.int32), tile_expert[:-1]])
    first = (tile_expert != prev_expert).astype(jnp.int32)
    nonempty = counts > 0
    wslot = (jnp.sum((nonempty[None, :] & (experts[None, :] < tile_expert[:, None])).astype(jnp.int32), axis=1)) % 2
    later = nonempty[None, :] & (experts[None, :] > tile_expert[:, None])
    next_expert = jnp.min(jnp.where(later, experts[None, :], N_EXPERTS), axis=1)
    next_expert = jnp.where(next_expert < N_EXPERTS, next_expert, -1)
    plan = (tile_expert, n_active.reshape(1), first, wslot, next_expert, row_token3)
    return plan, dest3


def _rope_swap_pad(w):
    half = QK_ROPE // 2
    x1 = w[..., QK_NOPE:QK_NOPE + half]
    x2 = w[..., QK_NOPE + half:]
    return jnp.concatenate([w[..., :QK_NOPE], x1, x2, x2, x1], axis=-1)


def kernel(x, p, positions, norm_mix, w_in, g_q_lat, g_kv_lat, w_uq, w_ukv, g_q_head, g_k_head, w_conv, g_out_attn, g_out_conv, w_out, norm_moe, w_group, b_group, w_router, b_router, w1, w3, w2, norm_ple, w_ple, w_ple_gate, b_ple_gate):
    batch, seq, d_model = x.shape
    depth = w_in.shape[0]
    t = batch * seq
    q_rank = g_q_lat.shape[1]
    kv_rank = g_kv_lat.shape[1]
    half = QK_ROPE // 2
    rope_lo = q_rank + kv_rank

    pos2 = positions.reshape(t, 1)
    inv_freq = ROPE_BASE ** (-jnp.arange(0, QK_ROPE, 2, dtype=F32) / QK_ROPE)
    freq = jnp.tile(inv_freq, LANES // half).reshape(1, LANES)

    h = x.reshape(t, d_model)
    for i in range(depth):
        wi = w_in[i]
        w_in_p = jnp.concatenate([wi[:, :rope_lo + QK_ROPE], wi[:, rope_lo + half:rope_lo + QK_ROPE],
                                  wi[:, rope_lo:rope_lo + half], wi[:, rope_lo + QK_ROPE:]], axis=1).astype(BF16)
        wuq_p = _rope_swap_pad(w_uq[i]).reshape(q_rank, MLA_HEADS * HEAD_PAD).astype(BF16)
        wukv_p = jnp.concatenate([w_ukv[i][:, :, :QK_NOPE].reshape(kv_rank, -1),
                                  w_ukv[i][:, :, QK_NOPE:].reshape(kv_rank, -1)], axis=1).astype(BF16)
        gq_p = _rope_swap_pad(g_q_head[i]).reshape(1, HEAD_PAD)
        gk_p = _rope_swap_pad(g_k_head[i]).reshape(1, HEAD_PAD)
        q, k, v, yc = _front(h, pos2, freq, norm_mix[i].reshape(1, -1), w_in_p, g_q_lat[i].reshape(1, -1),
                             g_kv_lat[i].reshape(1, -1), wuq_p, wukv_p, gq_p, gk_p, w_conv[i],
                             g_out_conv[i].reshape(1, -1), seq=seq)
        ya = _attention(q, k, v, batch=batch, seq=seq)

        pad = LANES - N_GROUPS - N_EXPERTS
        w_gr32 = jnp.concatenate([w_group[i], w_router[i], jnp.zeros((d_model, pad), F32)], axis=1)
        w_gr_hi = w_gr32.astype(BF16)
        w_gr = jnp.concatenate([w_gr_hi, (w_gr32 - w_gr_hi.astype(F32)).astype(BF16)], axis=1)
        b_gr = jnp.concatenate([b_group[i], b_router[i], jnp.zeros((pad,), F32)]).reshape(1, LANES)
        h1, xm, route = _outproj(ya, yc, h, g_out_attn[i].reshape(1, -1), w_out[i].astype(BF16),
                                 norm_moe[i].reshape(1, -1), w_gr, b_gr)

        plan, dest3 = _dispatch_plan(route)
        y = _moe(plan, xm, w1[i], w3[i], w2[i])
        h = _ple(dest3, y, h1, route, p[i].reshape(t, -1), w_ple[i].astype(BF16), norm_ple[i].reshape(1, -1),
                 w_ple_gate[i].astype(BF16), b_ple_gate[i].reshape(1, -1))
    return h.reshape(batch, seq, d_model)
```

```python
import functools

import jax
import jax.numpy as jnp
from jax import lax
from jax.experimental import pallas as pl
from jax.experimental.pallas import tpu as pltpu

F32 = jnp.float32
BF16 = jnp.bfloat16

CHUNK = 64
EPS = 1e-6
MASK_VALUE = -1e30
ROPE_BASE = 10000.0

MLA_HEADS = 8
QK_NOPE = 128
QK_ROPE = 64
QK_HEAD = QK_NOPE + QK_ROPE
V_HEAD = 128
HEAD_PAD = 256
Q_SCALE = QK_HEAD ** -0.5 * 1.4426950408889634
N_GROUPS = 4
EXPERTS_PER_GROUP = 8
N_EXPERTS = N_GROUPS * EXPERTS_PER_GROUP
TOP_K = 2
LANES = 128
SLAB = 16
SLAB_PITCH = 20
DMA_PRIORITIES = 2

TM_FRONT = 512
TQ_ATTN = 1024
TM_OUT = 512
TM_MOE = 256
TM_PLE = 256
VMEM_LIMIT = 56 * 1024 * 1024


def _rms(x, g, n):
    ms = jnp.sum(x * x, axis=-1, keepdims=True) * (1.0 / n)
    return x * lax.rsqrt(ms + EPS) * g


def _store_slabs(ref, val, pitch, mask=None):
    n, width = val.shape
    for c in range(width // LANES):
        piece = val[:, c * LANES:(c + 1) * LANES]
        rows = pl.ds(c, n, stride=pitch)
        if mask is not None:
            piece = jnp.where(mask, piece, ref[rows, :])
        ref[rows, :] = piece


def _load_slabs(ref, start, n, slab, pitch):
    return jnp.concatenate([ref[pl.ds(start + c, n, stride=pitch), :] for c in range(slab)], axis=1)


def _front_kernel(x_ref, pos_ref, freq_ref, gmix_ref, win_ref, gql_ref, gkvl_ref,
                  wuq_ref, wukv_ref, gq_ref, gk_ref, wconv_ref, gconv_ref,
                  q_ref, k_ref, v_ref, yc_ref, prev_ref, *, tiles_per_seq, q_rank, kv_rank, d_conv):
    i = pl.program_id(0)
    is_start = (i % tiles_per_seq) == 0
    tm = x_ref.shape[0]
    d_model = x_ref.shape[1]
    lat_w = q_rank + kv_rank + LANES

    xn = _rms(x_ref[...], gmix_ref[...], d_model).astype(BF16)
    lat = jnp.dot(xn, win_ref[:, :lat_w], preferred_element_type=F32)
    q_lat = lat[:, :q_rank]
    kv_lat = lat[:, q_rank:q_rank + kv_rank]
    kr = lat[:, q_rank + kv_rank:]

    qn = _rms(q_lat, gql_ref[...], q_rank).astype(BF16)
    kvn = _rms(kv_lat, gkvl_ref[...], kv_rank).astype(BF16)
    q_all = jnp.dot(qn, wuq_ref[...], preferred_element_type=F32)
    kv_all = jnp.dot(kvn, wukv_ref[...], preferred_element_type=F32)

    lane = lax.broadcasted_iota(jnp.int32, (tm, LANES), 1)
    ang = pos_ref[...].astype(F32) * freq_ref[...]
    cos = jnp.cos(ang)
    sin = jnp.sin(ang)
    rope_c = jnp.where(lane < QK_ROPE, cos, 0.0)
    rope_s = jnp.where(lane < QK_ROPE // 2, -sin, jnp.where(lane < QK_ROPE, sin, 0.0))
    rope_lanes = lane < QK_ROPE

    def rope(t):
        return t * rope_c + pltpu.roll(t, QK_ROPE, 1) * rope_s

    gq = gq_ref[...]
    gk = gk_ref[...]
    inv_n = 1.0 / QK_HEAD
    kr_ssq = jnp.sum(jnp.where(rope_lanes, kr * kr, 0.0), axis=-1, keepdims=True)
    kr_rot = rope(kr * gk[:, QK_NOPE:])
    for h in range(MLA_HEADS):
        qa = q_all[:, h * HEAD_PAD:h * HEAD_PAD + QK_NOPE]
        qb = q_all[:, h * HEAD_PAD + QK_NOPE:(h + 1) * HEAD_PAD]
        ssq = (jnp.sum(qa * qa, axis=-1, keepdims=True)
               + jnp.sum(jnp.where(rope_lanes, qb * qb, 0.0), axis=-1, keepdims=True))
        inv = lax.rsqrt(ssq * inv_n + EPS)
        q_ref[:, h * HEAD_PAD:h * HEAD_PAD + QK_NOPE] = (qa * inv * gq[:, :QK_NOPE] * Q_SCALE).astype(BF16)
        q_ref[:, h * HEAD_PAD + QK_NOPE:(h + 1) * HEAD_PAD] = (rope(qb * inv * gq[:, QK_NOPE:]) * Q_SCALE).astype(BF16)

        ka = kv_all[:, h * QK_NOPE:(h + 1) * QK_NOPE]
        kinv = lax.rsqrt((jnp.sum(ka * ka, axis=-1, keepdims=True) + kr_ssq) * inv_n + EPS)
        k_ref[:, h * HEAD_PAD:h * HEAD_PAD + QK_NOPE] = (ka * kinv * gk[:, :QK_NOPE]).astype(BF16)
        k_ref[:, h * HEAD_PAD + QK_NOPE:(h + 1) * HEAD_PAD] = (kr_rot * kinv).astype(BF16)
        v_lo = MLA_HEADS * QK_NOPE + h * V_HEAD
        v_ref[:, h * HEAD_PAD:h * HEAD_PAD + V_HEAD] = kv_all[:, v_lo:v_lo + V_HEAD].astype(BF16)
        v_ref[:, h * HEAD_PAD + V_HEAD:(h + 1) * HEAD_PAD] = jnp.ones((tm, HEAD_PAD - V_HEAD), BF16)

    cv = jnp.dot(xn, win_ref[:, lat_w:], preferred_element_type=F32)
    b_gate = cv[:, :d_conv]
    u = cv[:, d_conv:2 * d_conv] * cv[:, 2 * d_conv:]

    @pl.when(is_start)
    def _():
        prev_ref[...] = jnp.zeros_like(prev_ref)

    prev = prev_ref[...]
    hm1 = prev[7:8, :]
    hm2 = prev[6:7, :]
    row = lax.broadcasted_iota(jnp.int32, u.shape, 0)
    u1 = jnp.where(row == 0, hm1, pltpu.roll(u, 1, 0))
    u2 = jnp.where(row == 0, hm2, jnp.where(row == 1, hm1, pltpu.roll(u, 2, 0)))
    prev_ref[...] = u[tm - 8:, :]
    wc = wconv_ref[...]
    y = b_gate * (u2 * wc[0:1, :] + u1 * wc[1:2, :] + u * wc[2:3, :])
    yc_ref[...] = _rms(y, gconv_ref[...], d_conv).astype(BF16)


def _front(x2, pos2, freq, gmix, w_in_p, gql, gkvl, wuq_p, wukv_p, gq_p, gk_p, wconv, gconv, *, seq):
    t, d_model = x2.shape
    tm = TM_FRONT
    q_rank = gql.shape[1]
    kv_rank = gkvl.shape[1]
    d_conv = gconv.shape[1]
    const = lambda i: (0, 0)
    row = lambda i: (i, 0)
    full = lambda a: pl.BlockSpec(a.shape, const)
    kern = functools.partial(_front_kernel, tiles_per_seq=seq // tm, q_rank=q_rank, kv_rank=kv_rank, d_conv=d_conv)
    return pl.pallas_call(
        kern,
        grid=(t // tm,),
        in_specs=[pl.BlockSpec((tm, d_model), row), pl.BlockSpec((tm, 1), row), full(freq), full(gmix),
                  pl.BlockSpec(w_in_p.shape, const, pipeline_mode=pl.Buffered(1)),
                  full(gql), full(gkvl), full(wuq_p), full(wukv_p), full(gq_p), full(gk_p), full(wconv), full(gconv)],
        out_specs=[pl.BlockSpec((tm, MLA_HEADS * HEAD_PAD), row), pl.BlockSpec((tm, MLA_HEADS * HEAD_PAD), row),
                   pl.BlockSpec((tm, MLA_HEADS * HEAD_PAD), row), pl.BlockSpec((tm, d_conv), row)],
        out_shape=[jax.ShapeDtypeStruct((t, MLA_HEADS * HEAD_PAD), BF16), jax.ShapeDtypeStruct((t, MLA_HEADS * HEAD_PAD), BF16),
                   jax.ShapeDtypeStruct((t, MLA_HEADS * HEAD_PAD), BF16), jax.ShapeDtypeStruct((t, d_conv), BF16)],
        scratch_shapes=[pltpu.VMEM((8, d_conv), F32)],
        compiler_params=pltpu.CompilerParams(dimension_semantics=("arbitrary",), vmem_limit_bytes=VMEM_LIMIT),
        name="front",
    )(x2, pos2, freq, gmix, w_in_p, gql, gkvl, wuq_p, wukv_p, gq_p, gk_p, wconv, gconv)


def _attn_kernel(q_ref, k_ref, v_ref, o_ref):
    i = pl.program_id(2)
    tq = q_ref.shape[0]
    half = tq // 2
    q_halves = (q_ref[:half, :], q_ref[half:, :])

    def step(start, n_keys, masks, carry):
        kvs = [(k_ref[pl.ds(start, n), :], v_ref[pl.ds(start, n), :]) for n in n_keys]
        scores = [lax.dot_general(qh, kb, (((1,), (1,)), ((), ())), preferred_element_type=F32)
                  for qh, (kb, _) in zip(q_halves, kvs)]
        scores = [s if mask is None else jnp.where(mask, s, MASK_VALUE) for s, mask in zip(scores, masks)]
        m_new = [jnp.maximum(m, jnp.max(s, axis=-1, keepdims=True)) for s, (m, _) in zip(scores, carry)]
        probs = [jnp.exp2(s - mn).astype(BF16) for s, mn in zip(scores, m_new)]
        return tuple((mn, jnp.exp2(m - mn) * acc + jnp.dot(p, vb, preferred_element_type=F32))
                     for p, mn, (m, acc), (_, vb) in zip(probs, m_new, carry, kvs))

    def chunk_mask(n, first_row):
        r = (lax.broadcasted_iota(jnp.int32, (half, n), 0) + first_row) // CHUNK
        c = lax.broadcasted_iota(jnp.int32, (half, n), 1) // CHUNK
        return c <= r

    for qi in range(k_ref.shape[0] // tq):
        @pl.when(i == qi)
        def _(qi=qi):
            carry = tuple((jnp.full((half, 1), MASK_VALUE, F32), jnp.zeros((half, HEAD_PAD), F32)) for _ in range(2))
            for j in range(qi):
                carry = step(j * tq, (tq, tq), (None, None), carry)
            (_, acc0), (_, acc1) = step(qi * tq, (half, tq), (chunk_mask(half, 0), chunk_mask(tq, half)), carry)
            o_ref[:half, :] = acc0[:, :V_HEAD] / acc0[:, V_HEAD:]
            o_ref[half:, :] = acc1[:, :V_HEAD] / acc1[:, V_HEAD:]


def _attention(q, k, v, *, batch, seq):
    tq = TQ_ATTN
    nq = seq // tq
    return pl.pallas_call(
        _attn_kernel,
        grid=(batch, MLA_HEADS, nq),
        in_specs=[pl.BlockSpec((tq, HEAD_PAD), lambda b, h, i: (b * nq + i, h)),
                  pl.BlockSpec((seq, HEAD_PAD), lambda b, h, i: (b, h)),
                  pl.BlockSpec((seq, HEAD_PAD), lambda b, h, i: (b, h))],
        out_specs=pl.BlockSpec((tq, V_HEAD), lambda b, h, i: (b * nq + i, h)),
        out_shape=jax.ShapeDtypeStruct((batch * seq, MLA_HEADS * V_HEAD), F32),
        compiler_params=pltpu.CompilerParams(dimension_semantics=("arbitrary", "arbitrary", "arbitrary"),
                                             vmem_limit_bytes=VMEM_LIMIT),
        name="attn",
    )(q, k, v)


def _outproj_kernel(ya_ref, yc_ref, x_ref, gattn_ref, wout_ref, gmoe_ref, wgr_ref, bgr_ref,
                    h1_ref, xm_ref, route_ref):
    tm, d_attn = ya_ref.shape
    d_model = x_ref.shape[1]
    ya = _rms(ya_ref[...], gattn_ref[...], d_attn).astype(BF16)
    h1 = (x_ref[...]
          + jnp.dot(ya, wout_ref[:d_attn, :], preferred_element_type=F32)
          + jnp.dot(yc_ref[...], wout_ref[d_attn:, :], preferred_element_type=F32))
    h1_ref[...] = h1
    xm = _rms(h1, gmoe_ref[...], d_model)
    _store_slabs(xm_ref, xm, SLAB)

    x_hi = xm.astype(BF16)
    x_lo = (xm - x_hi.astype(F32)).astype(BF16)
    hi_part = jnp.dot(x_hi, wgr_ref[...], preferred_element_type=F32)
    lo_part = jnp.dot(x_lo, wgr_ref[:, :LANES], preferred_element_type=F32)
    logits = hi_part[:, :LANES] + hi_part[:, LANES:] + lo_part + bgr_ref[...]
    lane = lax.broadcasted_iota(jnp.int32, (tm, LANES), 1).astype(F32)
    neg = -jnp.inf
    big = 1e9
    gl = jnp.where(lane < N_GROUPS, logits, neg)
    gmax = jnp.max(gl, axis=-1, keepdims=True)
    gsel = jnp.min(jnp.where(gl == gmax, lane, big), axis=-1, keepdims=True)
    p_g = 1.0 / jnp.sum(jnp.exp(gl - gmax), axis=-1, keepdims=True)
    lo = N_GROUPS + EXPERTS_PER_GROUP * gsel
    el = jnp.where((lane >= lo) & (lane < lo + EXPERTS_PER_GROUP), logits, neg)
    v1 = jnp.max(el, axis=-1, keepdims=True)
    i1 = jnp.min(jnp.where(el == v1, lane, big), axis=-1, keepdims=True)
    el2 = jnp.where(lane == i1, neg, el)
    v2 = jnp.max(el2, axis=-1, keepdims=True)
    i2 = jnp.min(jnp.where(el2 == v2, lane, big), axis=-1, keepdims=True)
    e2 = jnp.exp(v2 - v1)
    den = 1.0 + e2
    w1 = p_g / den
    w2 = p_g * e2 / den
    route_ref[...] = jnp.where(lane == 0, i1 - N_GROUPS,
                               jnp.where(lane == 1, i2 - N_GROUPS,
                                         jnp.where(lane == 2, w1, jnp.where(lane == 3, w2, 0.0))))


def _outproj(ya, yc, x2, gattn, w_out, gmoe, w_gr, b_gr):
    t, d_model = x2.shape
    d_attn = ya.shape[1]
    tm = TM_OUT
    const = lambda i: (0, 0)
    row = lambda i: (i, 0)
    full = lambda a: pl.BlockSpec(a.shape, const)
    return pl.pallas_call(
        _outproj_kernel,
        grid=(t // tm,),
        in_specs=[pl.BlockSpec((tm, d_attn), row), pl.BlockSpec((tm, yc.shape[1]), row), pl.BlockSpec((tm, d_model), row),
                  full(gattn), pl.BlockSpec(w_out.shape, const, pipeline_mode=pl.Buffered(1)), full(gmoe), full(w_gr), full(b_gr)],
        out_specs=[pl.BlockSpec((tm, d_model), row), pl.BlockSpec((tm * SLAB, LANES), row), pl.BlockSpec((tm, LANES), row)],
        out_shape=[jax.ShapeDtypeStruct((t, d_model), F32), jax.ShapeDtypeStruct((t * SLAB, LANES), F32),
                   jax.ShapeDtypeStruct((t, LANES), F32)],
        compiler_params=pltpu.CompilerParams(dimension_semantics=("arbitrary",), vmem_limit_bytes=VMEM_LIMIT),
        name="outproj",
    )(ya, yc, x2, gattn, w_out, gmoe, w_gr, b_gr)


def _row_gather_start(idx_ref, src_hbm, dst_ref, sem, n_rows):
    def body(pair, carry):
        for prio in range(DMA_PRIORITIES):
            r = pair * DMA_PRIORITIES + prio
            src = src_hbm.at[pl.ds(idx_ref[0, 0, r] * SLAB, SLAB), :]
            pltpu.make_async_copy(src, dst_ref.at[pl.ds(r * SLAB_PITCH, SLAB), :], sem).start(priority=prio)
        return carry
    lax.fori_loop(0, n_rows // DMA_PRIORITIES, body, 0, unroll=4)


def _row_gather_wait(src_hbm, dst_ref, sem, n_rows):
    pltpu.make_async_copy(src_hbm.at[pl.ds(0, n_rows * SLAB), :], dst_ref.at[pl.ds(0, n_rows * SLAB), :], sem).wait()


def _moe_kernel(tile_expert_ref, n_active_ref, first_ref, wslot_ref, next_expert_ref, tok_cur_ref, tok_next_ref,
                xm_hbm, w1_hbm, w3_hbm, w2_hbm, y_ref, xbuf_ref, w1_buf, w3_buf, w2_buf, sem_ref, wsem_ref, *, tm):
    i = pl.program_id(0)
    n_active = n_active_ref[0]
    slot = i % 2
    buf_rows = tm * SLAB_PITCH
    wslot = wslot_ref[i]

    def buf(s):
        return xbuf_ref.at[pl.ds(s * buf_rows, buf_rows), :]

    def weight_copies(e, s):
        return (pltpu.make_async_copy(w1_hbm.at[e], w1_buf.at[s], wsem_ref.at[0, s]),
                pltpu.make_async_copy(w3_hbm.at[e], w3_buf.at[s], wsem_ref.at[1, s]),
                pltpu.make_async_copy(w2_hbm.at[e], w2_buf.at[s], wsem_ref.at[2, s]))

    @pl.when(i == 0)
    def _():
        for cp in weight_copies(tile_expert_ref[0], 0):
            cp.start()
        _row_gather_start(tok_cur_ref, xm_hbm, buf(0), sem_ref.at[0], tm)

    active = i < n_active
    new_expert = active & (first_ref[i] == 1)

    @pl.when(new_expert & (next_expert_ref[i] >= 0))
    def _():
        for cp in weight_copies(next_expert_ref[i], 1 - wslot):
            cp.start()

    @pl.when(i + 1 < n_active)
    def _():
        _row_gather_start(tok_next_ref, xm_hbm, buf(1 - slot), sem_ref.at[1 - slot], tm)

    @pl.when(new_expert)
    def _():
        for cp in weight_copies(tile_expert_ref[i], wslot):
            cp.wait()

    @pl.when(active)
    def _():
        _row_gather_wait(xm_hbm, buf(slot), sem_ref.at[slot], tm)
        x = _load_slabs(xbuf_ref, slot * buf_rows, tm, SLAB, SLAB_PITCH).astype(BF16)
        a = jnp.dot(x, w1_buf[wslot].astype(BF16), preferred_element_type=F32)
        g = jnp.dot(x, w3_buf[wslot].astype(BF16), preferred_element_type=F32)
        hid = (a * jax.nn.sigmoid(a) * g).astype(BF16)
        _store_slabs(y_ref, jnp.dot(hid, w2_buf[wslot].astype(BF16), preferred_element_type=F32), SLAB)

    @pl.when(i >= n_active)
    def _():
        y_ref[...] = jnp.zeros_like(y_ref)


def _moe(plan, xm, w1, w3, w2):
    tile_expert, n_active, first, wslot, next_expert, row_token3 = plan
    n_tiles = row_token3.shape[0]
    tm = TM_MOE
    _, d_model, d_exp = w1.shape
    last = n_tiles - 1
    grid_spec = pltpu.PrefetchScalarGridSpec(
        num_scalar_prefetch=5,
        grid=(n_tiles,),
        in_specs=[pl.BlockSpec((1, 1, tm), lambda i, *_: (i, 0, 0), memory_space=pltpu.SMEM),
                  pl.BlockSpec((1, 1, tm), lambda i, *_: (jnp.minimum(i + 1, last), 0, 0), memory_space=pltpu.SMEM),
                  pl.BlockSpec(memory_space=pl.ANY), pl.BlockSpec(memory_space=pl.ANY),
                  pl.BlockSpec(memory_space=pl.ANY), pl.BlockSpec(memory_space=pl.ANY)],
        out_specs=pl.BlockSpec((tm * SLAB, LANES), lambda i, *_: (i, 0)),
        scratch_shapes=[pltpu.VMEM((2 * tm * SLAB_PITCH, LANES), F32),
                        pltpu.VMEM((2, d_model, d_exp), F32), pltpu.VMEM((2, d_model, d_exp), F32),
                        pltpu.VMEM((2, d_exp, d_model), F32),
                        pltpu.SemaphoreType.DMA((2,)), pltpu.SemaphoreType.DMA((3, 2))],
    )
    return pl.pallas_call(
        functools.partial(_moe_kernel, tm=tm),
        grid_spec=grid_spec,
        out_shape=jax.ShapeDtypeStruct((n_tiles * tm * SLAB, LANES), F32),
        compiler_params=pltpu.CompilerParams(dimension_semantics=("arbitrary",), vmem_limit_bytes=VMEM_LIMIT),
        name="moe",
    )(tile_expert, n_active, first, wslot, next_expert, row_token3, row_token3, xm, w1, w3, w2)


def _ple_kernel(dest_cur_ref, dest_next_ref, y_hbm, h1_ref, route_ref, p_ref, wple_ref, gple_ref, wpg_ref, bpg_ref,
                o_ref, ybuf_ref, sem_ref):
    i = pl.program_id(0)
    n = pl.num_programs(0)
    tm, d_model = h1_ref.shape
    slot = i % 2
    buf_rows = TOP_K * tm * SLAB_PITCH

    def buf(s):
        return ybuf_ref.at[pl.ds(s * buf_rows, buf_rows), :]

    @pl.when(i == 0)
    def _():
        _row_gather_start(dest_cur_ref, y_hbm, buf(0), sem_ref.at[0], TOP_K * tm)

    @pl.when(i + 1 < n)
    def _():
        _row_gather_start(dest_next_ref, y_hbm, buf(1 - slot), sem_ref.at[1 - slot], TOP_K * tm)

    _row_gather_wait(y_hbm, buf(slot), sem_ref.at[slot], TOP_K * tm)
    route = route_ref[...]
    y0 = _load_slabs(ybuf_ref, slot * buf_rows, tm, SLAB, SLAB_PITCH)
    y1 = _load_slabs(ybuf_ref, slot * buf_rows + tm * SLAB_PITCH, tm, SLAB, SLAB_PITCH)
    h2 = h1_ref[...] + (route[:, 2:3] * y0 + route[:, 3:4] * y1)
    xn = _rms(h2, gple_ref[...], d_model).astype(BF16)
    gate = jax.nn.sigmoid(jnp.dot(xn, wpg_ref[...], preferred_element_type=F32) + bpg_ref[...])
    pe = jnp.dot(p_ref[...].astype(BF16), wple_ref[...], preferred_element_type=F32)
    o_ref[...] = h2 + gate * pe


def _ple(dest3, y, h1, route, p2, w_ple, gple, w_pg, b_pg):
    t, d_model = h1.shape
    tm = TM_PLE
    n_tiles = t // tm
    last = n_tiles - 1
    const = lambda i: (0, 0)
    row = lambda i: (i, 0)
    full = lambda a: pl.BlockSpec(a.shape, const)
    return pl.pallas_call(
        _ple_kernel,
        grid=(n_tiles,),
        in_specs=[pl.BlockSpec((1, 1, TOP_K * tm), lambda i: (i, 0, 0), memory_space=pltpu.SMEM),
                  pl.BlockSpec((1, 1, TOP_K * tm), lambda i: (jnp.minimum(i + 1, last), 0, 0), memory_space=pltpu.SMEM),
                  pl.BlockSpec(memory_space=pl.ANY),
                  pl.BlockSpec((tm, d_model), row), pl.BlockSpec((tm, LANES), row), pl.BlockSpec((tm, p2.shape[1]), row),
                  full(w_ple), full(gple), pl.BlockSpec(w_pg.shape, const, pipeline_mode=pl.Buffered(1)), full(b_pg)],
        out_specs=pl.BlockSpec((tm, d_model), row),
        out_shape=jax.ShapeDtypeStruct((t, d_model), F32),
        scratch_shapes=[pltpu.VMEM((2 * TOP_K * tm * SLAB_PITCH, LANES), F32), pltpu.SemaphoreType.DMA((2,))],
        compiler_params=pltpu.CompilerParams(dimension_semantics=("arbitrary",), vmem_limit_bytes=VMEM_LIMIT),
        name="ple",
    )(dest3, dest3, y, h1, route, p2, w_ple, gple, w_pg, b_pg)


def _dispatch_plan(route):
    t = route.shape[0]
    tm = TM_MOE
    total = TOP_K * t
    n_tiles = total // tm + N_EXPERTS
    expert = route[:, :TOP_K].astype(jnp.int32).reshape(-1)
    pos = jnp.arange(total, dtype=jnp.int32)
    order = jnp.sort(expert * total + pos) % total
    rank = jnp.sort(order * total + pos) % total
    experts = jnp.arange(N_EXPERTS, dtype=jnp.int32)
    onehot = (expert[:, None] == experts[None, :]).astype(jnp.int32)
    counts = jnp.sum(onehot, axis=0)
    group_start = jnp.cumsum(counts) - counts
    tile_end = jnp.cumsum((counts + tm - 1) // tm)
    tile_start = tile_end - (counts + tm - 1) // tm
    n_active = tile_end[-1]
    shift = tile_start * tm - group_start
    dest = rank + jnp.sum(onehot * shift[None, :], axis=1)
    tiles = jnp.minimum(jnp.arange(n_tiles, dtype=jnp.int32), n_active - 1)
    tile_expert = jnp.sum((tile_end[None, :] <= tiles[:, None]).astype(jnp.int32), axis=1)
    j = jnp.arange(tm, dtype=jnp.int32)
    fill_row = (tile_start * tm + counts)[:, None] + j[None, :]
    fill_row = jnp.where(fill_row < (tile_end * tm)[:, None], fill_row, n_tiles * tm)
    fill_tok = (experts[:, None] * tm + j[None, :]) % t
    keys = jnp.concatenate([dest * t + pos // TOP_K, (fill_row * t + fill_tok).reshape(-1)])
    row_token3 = (jnp.sort(keys) % t).reshape(n_tiles, 1, tm)
    dest3 = dest.reshape(t // TM_PLE, TM_PLE, TOP_K).transpose(0, 2, 1).reshape(t // TM_PLE, 1, TOP_K * TM_PLE)
    prev_expert = jnp.concatenate([jnp.full((1,), -1, jnp.int32), tile_expert[:-1]])
    first = (tile_expert != prev_expert).astype(jnp.int32)
    nonempty = counts > 0
    wslot = (jnp.sum((nonempty[None, :] & (experts[None, :] < tile_expert[:, None])).astype(jnp.int32), axis=1)) % 2
    later = nonempty[None, :] & (experts[None, :] > tile_expert[:, None])
    next_expert = jnp.min(jnp.where(later, experts[None, :], N_EXPERTS), axis=1)
    next_expert = jnp.where(next_expert < N_EXPERTS, next_expert, -1)
    plan = (tile_expert, n_active.reshape(1), first, wslot, next_expert, row_token3)
    return plan, dest3


def _rope_swap_pad(w):
    half = QK_ROPE // 2
    x1 = w[..., QK_NOPE:QK_NOPE + half]
    x2 = w[..., QK_NOPE + half:]
    return jnp.concatenate([w[..., :QK_NOPE], x1, x2, x2, x1], axis=-1)


def kernel(x, p, positions, norm_mix, w_in, g_q_lat, g_kv_lat, w_uq, w_ukv, g_q_head, g_k_head, w_conv, g_out_attn, g_out_conv, w_out, norm_moe, w_group, b_group, w_router, b_router, w1, w3, w2, norm_ple, w_ple, w_ple_gate, b_ple_gate):
    batch, seq, d_model = x.shape
    depth = w_in.shape[0]
    t = batch * seq
    q_rank = g_q_lat.shape[1]
    kv_rank = g_kv_lat.shape[1]
    half = QK_ROPE // 2
    rope_lo = q_rank + kv_rank

    pos2 = positions.reshape(t, 1)
    inv_freq = ROPE_BASE ** (-jnp.arange(0, QK_ROPE, 2, dtype=F32) / QK_ROPE)
    freq = jnp.tile(inv_freq, LANES // half).reshape(1, LANES)

    h = x.reshape(t, d_model)
    for i in range(depth):
        wi = w_in[i]
        w_in_p = jnp.concatenate([wi[:, :rope_lo + QK_ROPE], wi[:, rope_lo + half:rope_lo + QK_ROPE],
                                  wi[:, rope_lo:rope_lo + half], wi[:, rope_lo + QK_ROPE:]], axis=1).astype(BF16)
        wuq_p = _rope_swap_pad(w_uq[i]).reshape(q_rank, MLA_HEADS * HEAD_PAD).astype(BF16)
        wukv_p = jnp.concatenate([w_ukv[i][:, :, :QK_NOPE].reshape(kv_rank, -1),
                                  w_ukv[i][:, :, QK_NOPE:].reshape(kv_rank, -1)], axis=1).astype(BF16)
        gq_p = _rope_swap_pad(g_q_head[i]).reshape(1, HEAD_PAD)
        gk_p = _rope_swap_pad(g_k_head[i]).reshape(1, HEAD_PAD)
        q, k, v, yc = _front(h, pos2, freq, norm_mix[i].reshape(1, -1), w_in_p, g_q_lat[i].reshape(1, -1),
                             g_kv_lat[i].reshape(1, -1), wuq_p, wukv_p, gq_p, gk_p, w_conv[i],
                             g_out_conv[i].reshape(1, -1), seq=seq)
        ya = _attention(q, k, v, batch=batch, seq=seq)

        pad = LANES - N_GROUPS - N_EXPERTS
        w_gr32 = jnp.concatenate([w_group[i], w_router[i], jnp.zeros((d_model, pad), F32)], axis=1)
        w_gr_hi = w_gr32.astype(BF16)
        w_gr = jnp.concatenate([w_gr_hi, (w_gr32 - w_gr_hi.astype(F32)).astype(BF16)], axis=1)
        b_gr = jnp.concatenate([b_group[i], b_router[i], jnp.zeros((pad,), F32)]).reshape(1, LANES)
        h1, xm, route = _outproj(ya, yc, h, g_out_attn[i].reshape(1, -1), w_out[i].astype(BF16),
                                 norm_moe[i].reshape(1, -1), w_gr, b_gr)

        plan, dest3 = _dispatch_plan(route)
        y = _moe(plan, xm, w1[i], w3[i], w2[i])
        h = _ple(dest3, y, h1, route, p[i].reshape(t, -1), w_ple[i].astype(BF16), norm_ple[i].reshape(1, -1),
                 w_ple_gate[i].astype(BF16), b_ple_gate[i].reshape(1, -1))
    return h.reshape(batch, seq, d_model)
```

```python
import functools

import jax
import jax.numpy as jnp
from jax import lax
from jax.experimental import pallas as pl
from jax.experimental.pallas import tpu as pltpu

F32 = jnp.float32
BF16 = jnp.bfloat16

CHUNK = 64
EPS = 1e-6
MASK_VALUE = -1e30
ROPE_BASE = 10000.0

MLA_HEADS = 8
QK_NOPE = 128
QK_ROPE = 64
QK_HEAD = QK_NOPE + QK_ROPE
V_HEAD = 128
HEAD_PAD = 256
Q_SCALE = QK_HEAD ** -0.5 * 1.4426950408889634
N_GROUPS = 4
EXPERTS_PER_GROUP = 8
N_EXPERTS = N_GROUPS * EXPERTS_PER_GROUP
TOP_K = 2
LANES = 128
SLAB = 16
SLAB_PITCH = 20
DMA_PRIORITIES = 2

TM_FRONT = 512
TQ_ATTN = 1024
TM_OUT = 512
TM_MOE = 256
TM_PLE = 256
VMEM_LIMIT = 56 * 1024 * 1024


def _rms(x, g, n):
    ms = jnp.sum(x * x, axis=-1, keepdims=True) * (1.0 / n)
    return x * lax.rsqrt(ms + EPS) * g


def _store_slabs(ref, val, pitch, mask=None):
    n, width = val.shape
    for c in range(width // LANES):
        piece = val[:, c * LANES:(c + 1) * LANES]
        rows = pl.ds(c, n, stride=pitch)
        if mask is not None:
            piece = jnp.where(mask, piece, ref[rows, :])
        ref[rows, :] = piece


def _load_slabs(ref, start, n, slab, pitch):
    return jnp.concatenate([ref[pl.ds(start + c, n, stride=pitch), :] for c in range(slab)], axis=1)


def _pack_w_in_kernel(w_ref, o_ref, *, rope_lo):
    w = w_ref[...]
    half = QK_ROPE // 2
    rope_hi = rope_lo + QK_ROPE
    o_ref[...] = jnp.concatenate([w[:, :rope_hi], w[:, rope_lo + half:rope_hi], w[:, rope_lo:rope_lo + half],
                                  w[:, rope_hi:]], axis=1).astype(BF16)


def _pack_w_in(w, rope_lo):
    k, n = w.shape
    rows = 256
    return pl.pallas_call(
        functools.partial(_pack_w_in_kernel, rope_lo=rope_lo),
        grid=(k // rows,),
        in_specs=[pl.BlockSpec((rows, n), lambda i: (i, 0))],
        out_specs=pl.BlockSpec((rows, n + QK_ROPE), lambda i: (i, 0)),
        out_shape=jax.ShapeDtypeStruct((k, n + QK_ROPE), BF16),
        compiler_params=pltpu.CompilerParams(dimension_semantics=("arbitrary",), vmem_limit_bytes=VMEM_LIMIT),
        name="pack_w_in",
    )(w)


def _front_kernel(x_ref, pos_ref, freq_ref, gmix_ref, win_ref, gql_ref, gkvl_ref,
                  wuq_ref, wukv_ref, gq_ref, gk_ref, wconv_ref, gconv_ref,
                  q_ref, k_ref, v_ref, yc_ref, prev_ref, *, tiles_per_seq, q_rank, kv_rank, d_conv):
    i = pl.program_id(0)
    is_start = (i % tiles_per_seq) == 0
    tm = x_ref.shape[0]
    d_model = x_ref.shape[1]
    lat_w = q_rank + kv_rank + LANES

    xn = _rms(x_ref[...], gmix_ref[...], d_model).astype(BF16)
    lat = jnp.dot(xn, win_ref[:, :lat_w], preferred_element_type=F32)
    q_lat = lat[:, :q_rank]
    kv_lat = lat[:, q_rank:q_rank + kv_rank]
    kr = lat[:, q_rank + kv_rank:]

    qn = _rms(q_lat, gql_ref[...], q_rank).astype(BF16)
    kvn = _rms(kv_lat, gkvl_ref[...], kv_rank).astype(BF16)
    q_all = jnp.dot(qn, wuq_ref[...], preferred_element_type=F32)
    kv_all = jnp.dot(kvn, wukv_ref[...], preferred_element_type=F32)

    lane = lax.broadcasted_iota(jnp.int32, (tm, LANES), 1)
    ang = pos_ref[...].astype(F32) * freq_ref[...]
    cos = jnp.cos(ang)
    sin = jnp.sin(ang)
    rope_c = jnp.where(lane < QK_ROPE, cos, 0.0)
    rope_s = jnp.where(lane < QK_ROPE // 2, -sin, jnp.where(lane < QK_ROPE, sin, 0.0))
    rope_lanes = lane < QK_ROPE

    def rope(t):
        return t * rope_c + pltpu.roll(t, QK_ROPE, 1) * rope_s

    gq = gq_ref[...]
    gk = gk_ref[...]
    inv_n = 1.0 / QK_HEAD
    kr_ssq = jnp.sum(jnp.where(rope_lanes, kr * kr, 0.0), axis=-1, keepdims=True)
    kr_rot = rope(kr * gk[:, QK_NOPE:])
    for h in range(MLA_HEADS):
        qa = q_all[:, h * HEAD_PAD:h * HEAD_PAD + QK_NOPE]
        qb = q_all[:, h * HEAD_PAD + QK_NOPE:(h + 1) * HEAD_PAD]
        ssq = (jnp.sum(qa * qa, axis=-1, keepdims=True)
               + jnp.sum(jnp.where(rope_lanes, qb * qb, 0.0), axis=-1, keepdims=True))
        inv = lax.rsqrt(ssq * inv_n + EPS)
        q_ref[:, h * HEAD_PAD:h * HEAD_PAD + QK_NOPE] = (qa * inv * gq[:, :QK_NOPE] * Q_SCALE).astype(BF16)
        q_ref[:, h * HEAD_PAD + QK_NOPE:(h + 1) * HEAD_PAD] = (rope(qb * inv * gq[:, QK_NOPE:]) * Q_SCALE).astype(BF16)

        ka = kv_all[:, h * QK_NOPE:(h + 1) * QK_NOPE]
        kinv = lax.rsqrt((jnp.sum(ka * ka, axis=-1, keepdims=True) + kr_ssq) * inv_n + EPS)
        k_ref[:, h * HEAD_PAD:h * HEAD_PAD + QK_NOPE] = (ka * kinv * gk[:, :QK_NOPE]).astype(BF16)
        k_ref[:, h * HEAD_PAD + QK_NOPE:(h + 1) * HEAD_PAD] = (kr_rot * kinv).astype(BF16)
        v_lo = MLA_HEADS * QK_NOPE + h * V_HEAD
        v_ref[:, h * HEAD_PAD:h * HEAD_PAD + V_HEAD] = kv_all[:, v_lo:v_lo + V_HEAD].astype(BF16)
        v_ref[:, h * HEAD_PAD + V_HEAD:(h + 1) * HEAD_PAD] = jnp.ones((tm, HEAD_PAD - V_HEAD), BF16)

    cv = jnp.dot(xn, win_ref[:, lat_w:], preferred_element_type=F32)
    b_gate = cv[:, :d_conv]
    u = cv[:, d_conv:2 * d_conv] * cv[:, 2 * d_conv:]

    @pl.when(is_start)
    def _():
        prev_ref[...] = jnp.zeros_like(prev_ref)

    prev = prev_ref[...]
    hm1 = prev[7:8, :]
    hm2 = prev[6:7, :]
    row = lax.broadcasted_iota(jnp.int32, u.shape, 0)
    u1 = jnp.where(row == 0, hm1, pltpu.roll(u, 1, 0))
    u2 = jnp.where(row == 0, hm2, jnp.where(row == 1, hm1, pltpu.roll(u, 2, 0)))
    prev_ref[...] = u[tm - 8:, :]
    wc = wconv_ref[...]
    y = b_gate * (u2 * wc[0:1, :] + u1 * wc[1:2, :] + u * wc[2:3, :])
    yc_ref[...] = _rms(y, gconv_ref[...], d_conv).astype(BF16)


def _front(x2, pos2, freq, gmix, w_in_p, gql, gkvl, wuq_p, wukv_p, gq_p, gk_p, wconv, gconv, *, seq):
    t, d_model = x2.shape
    tm = TM_FRONT
    q_rank = gql.shape[1]
    kv_rank = gkvl.shape[1]
    d_conv = gconv.shape[1]
    const = lambda i: (0, 0)
    row = lambda i: (i, 0)
    full = lambda a: pl.BlockSpec(a.shape, const)
    kern = functools.partial(_front_kernel, tiles_per_seq=seq // tm, q_rank=q_rank, kv_rank=kv_rank, d_conv=d_conv)
    return pl.pallas_call(
        kern,
        grid=(t // tm,),
        in_specs=[pl.BlockSpec((tm, d_model), row), pl.BlockSpec((tm, 1), row), full(freq), full(gmix),
                  pl.BlockSpec(w_in_p.shape, const, pipeline_mode=pl.Buffered(1)),
                  full(gql), full(gkvl), full(wuq_p), full(wukv_p), full(gq_p), full(gk_p), full(wconv), full(gconv)],
        out_specs=[pl.BlockSpec((tm, MLA_HEADS * HEAD_PAD), row), pl.BlockSpec((tm, MLA_HEADS * HEAD_PAD), row),
                   pl.BlockSpec((tm, MLA_HEADS * HEAD_PAD), row), pl.BlockSpec((tm, d_conv), row)],
        out_shape=[jax.ShapeDtypeStruct((t, MLA_HEADS * HEAD_PAD), BF16), jax.ShapeDtypeStruct((t, MLA_HEADS * HEAD_PAD), BF16),
                   jax.ShapeDtypeStruct((t, MLA_HEADS * HEAD_PAD), BF16), jax.ShapeDtypeStruct((t, d_conv), BF16)],
        scratch_shapes=[pltpu.VMEM((8, d_conv), F32)],
        compiler_params=pltpu.CompilerParams(dimension_semantics=("arbitrary",), vmem_limit_bytes=VMEM_LIMIT),
        name="front",
    )(x2, pos2, freq, gmix, w_in_p, gql, gkvl, wuq_p, wukv_p, gq_p, gk_p, wconv, gconv)


def _attn_kernel(q_ref, k_ref, v_ref, o_ref):
    i = pl.program_id(2)
    tq = q_ref.shape[0]
    half = tq // 2
    q_halves = (q_ref[:half, :], q_ref[half:, :])

    def step(start, n_keys, masks, carry):
        kvs = [(k_ref[pl.ds(start, n), :], v_ref[pl.ds(start, n), :]) for n in n_keys]
        scores = [lax.dot_general(qh, kb, (((1,), (1,)), ((), ())), preferred_element_type=F32)
                  for qh, (kb, _) in zip(q_halves, kvs)]
        scores = [s if mask is None else jnp.where(mask, s, MASK_VALUE) for s, mask in zip(scores, masks)]
        m_new = [jnp.maximum(m, jnp.max(s, axis=-1, keepdims=True)) for s, (m, _) in zip(scores, carry)]
        probs = [jnp.exp2(s - mn).astype(BF16) for s, mn in zip(scores, m_new)]
        return tuple((mn, jnp.exp2(m - mn) * acc + jnp.dot(p, vb, preferred_element_type=F32))
                     for p, mn, (m, acc), (_, vb) in zip(probs, m_new, carry, kvs))

    def chunk_mask(n, first_row):
        r = (lax.broadcasted_iota(jnp.int32, (half, n), 0) + first_row) // CHUNK
        c = lax.broadcasted_iota(jnp.int32, (half, n), 1) // CHUNK
        return c <= r

    for qi in range(k_ref.shape[0] // tq):
        @pl.when(i == qi)
        def _(qi=qi):
            carry = tuple((jnp.full((half, 1), MASK_VALUE, F32), jnp.zeros((half, HEAD_PAD), F32)) for _ in range(2))
            for j in range(qi):
                carry = step(j * tq, (tq, tq), (None, None), carry)
            (_, acc0), (_, acc1) = step(qi * tq, (half, tq), (chunk_mask(half, 0), chunk_mask(tq, half)), carry)
            o_ref[:half, :] = acc0[:, :V_HEAD] / acc0[:, V_HEAD:]
            o_ref[half:, :] = acc1[:, :V_HEAD] / acc1[:, V_HEAD:]


def _attention(q, k, v, *, batch, seq):
    tq = TQ_ATTN
    nq = seq // tq
    return pl.pallas_call(
        _attn_kernel,
        grid=(batch, MLA_HEADS, nq),
        in_specs=[pl.BlockSpec((tq, HEAD_PAD), lambda b, h, i: (b * nq + i, h)),
                  pl.BlockSpec((seq, HEAD_PAD), lambda b, h, i: (b, h)),
                  pl.BlockSpec((seq, HEAD_PAD), lambda b, h, i: (b, h))],
        out_specs=pl.BlockSpec((tq, V_HEAD), lambda b, h, i: (b * nq + i, h)),
        out_shape=jax.ShapeDtypeStruct((batch * seq, MLA_HEADS * V_HEAD), F32),
        compiler_params=pltpu.CompilerParams(dimension_semantics=("arbitrary", "arbitrary", "arbitrary"),
                                             vmem_limit_bytes=VMEM_LIMIT),
        name="attn",
    )(q, k, v)


def _outproj_kernel(ya_ref, yc_ref, x_ref, gattn_ref, wout_ref, gmoe_ref, wgr_ref, bgr_ref,
                    h1_ref, xm_ref, route_ref):
    tm, d_attn = ya_ref.shape
    d_model = x_ref.shape[1]
    ya = _rms(ya_ref[...], gattn_ref[...], d_attn).astype(BF16)
    h1 = (x_ref[...]
          + jnp.dot(ya, wout_ref[:d_attn, :], preferred_element_type=F32)
          + jnp.dot(yc_ref[...], wout_ref[d_attn:, :], preferred_element_type=F32))
    h1_ref[...] = h1
    xm = _rms(h1, gmoe_ref[...], d_model)
    _store_slabs(xm_ref, xm, SLAB)

    x_hi = xm.astype(BF16)
    x_lo = (xm - x_hi.astype(F32)).astype(BF16)
    hi_part = jnp.dot(x_hi, wgr_ref[...], preferred_element_type=F32)
    lo_part = jnp.dot(x_lo, wgr_ref[:, :LANES], preferred_element_type=F32)
    logits = hi_part[:, :LANES] + hi_part[:, LANES:] + lo_part + bgr_ref[...]
    lane = lax.broadcasted_iota(jnp.int32, (tm, LANES), 1).astype(F32)
    neg = -jnp.inf
    big = 1e9
    gl = jnp.where(lane < N_GROUPS, logits, neg)
    gmax = jnp.max(gl, axis=-1, keepdims=True)
    gsel = jnp.min(jnp.where(gl == gmax, lane, big), axis=-1, keepdims=True)
    p_g = 1.0 / jnp.sum(jnp.exp(gl - gmax), axis=-1, keepdims=True)
    lo = N_GROUPS + EXPERTS_PER_GROUP * gsel
    el = jnp.where((lane >= lo) & (lane < lo + EXPERTS_PER_GROUP), logits, neg)
    v1 = jnp.max(el, axis=-1, keepdims=True)
    i1 = jnp.min(jnp.where(el == v1, lane, big), axis=-1, keepdims=True)
    el2 = jnp.where(lane == i1, neg, el)
    v2 = jnp.max(el2, axis=-1, keepdims=True)
    i2 = jnp.min(jnp.where(el2 == v2, lane, big), axis=-1, keepdims=True)
    e2 = jnp.exp(v2 - v1)
    den = 1.0 + e2
    w1 = p_g / den
    w2 = p_g * e2 / den
    route_ref[...] = jnp.where(lane == 0, i1 - N_GROUPS,
                               jnp.where(lane == 1, i2 - N_GROUPS,
                                         jnp.where(lane == 2, w1, jnp.where(lane == 3, w2, 0.0))))


def _outproj(ya, yc, x2, gattn, w_out, gmoe, w_gr, b_gr):
    t, d_model = x2.shape
    d_attn = ya.shape[1]
    tm = TM_OUT
    const = lambda i: (0, 0)
    row = lambda i: (i, 0)
    full = lambda a: pl.BlockSpec(a.shape, const)
    return pl.pallas_call(
        _outproj_kernel,
        grid=(t // tm,),
        in_specs=[pl.BlockSpec((tm, d_attn), row), pl.BlockSpec((tm, yc.shape[1]), row), pl.BlockSpec((tm, d_model), row),
                  full(gattn), pl.BlockSpec(w_out.shape, const, pipeline_mode=pl.Buffered(1)), full(gmoe), full(w_gr), full(b_gr)],
        out_specs=[pl.BlockSpec((tm, d_model), row), pl.BlockSpec((tm * SLAB, LANES), row), pl.BlockSpec((tm, LANES), row)],
        out_shape=[jax.ShapeDtypeStruct((t, d_model), F32), jax.ShapeDtypeStruct((t * SLAB, LANES), F32),
                   jax.ShapeDtypeStruct((t, LANES), F32)],
        compiler_params=pltpu.CompilerParams(dimension_semantics=("arbitrary",), vmem_limit_bytes=VMEM_LIMIT),
        name="outproj",
    )(ya, yc, x2, gattn, w_out, gmoe, w_gr, b_gr)


def _row_gather_start(idx_ref, src_hbm, dst_ref, sem, n_rows):
    def body(pair, carry):
        for prio in range(DMA_PRIORITIES):
            r = pair * DMA_PRIORITIES + prio
            src = src_hbm.at[pl.ds(idx_ref[0, 0, r] * SLAB, SLAB), :]
            pltpu.make_async_copy(src, dst_ref.at[pl.ds(r * SLAB_PITCH, SLAB), :], sem).start(priority=prio)
        return carry
    lax.fori_loop(0, n_rows // DMA_PRIORITIES, body, 0, unroll=True)


def _row_gather_wait(src_hbm, dst_ref, sem, n_rows):
    pltpu.make_async_copy(src_hbm.at[pl.ds(0, n_rows * SLAB), :], dst_ref.at[pl.ds(0, n_rows * SLAB), :], sem).wait()


def _moe_kernel(tile_expert_ref, n_active_ref, first_ref, wslot_ref, next_expert_ref, tok_cur_ref, tok_next_ref,
                xm_hbm, w1_hbm, w3_hbm, w2_hbm, y_ref, xbuf_ref, w1_buf, w3_buf, w2_buf, sem_ref, wsem_ref, *, tm):
    i = pl.program_id(0)
    n_active = n_active_ref[0]
    slot = i % 2
    buf_rows = tm * SLAB_PITCH
    wslot = wslot_ref[i]

    def buf(s):
        return xbuf_ref.at[pl.ds(s * buf_rows, buf_rows), :]

    def weight_copies(e, s):
        return (pltpu.make_async_copy(w1_hbm.at[e], w1_buf.at[s], wsem_ref.at[0, s]),
                pltpu.make_async_copy(w3_hbm.at[e], w3_buf.at[s], wsem_ref.at[1, s]),
                pltpu.make_async_copy(w2_hbm.at[e], w2_buf.at[s], wsem_ref.at[2, s]))

    @pl.when(i == 0)
    def _():
        for cp in weight_copies(tile_expert_ref[0], 0):
            cp.start()
        _row_gather_start(tok_cur_ref, xm_hbm, buf(0), sem_ref.at[0], tm)

    active = i < n_active
    new_expert = active & (first_ref[i] == 1)

    @pl.when(new_expert & (next_expert_ref[i] >= 0))
    def _():
        for cp in weight_copies(next_expert_ref[i], 1 - wslot):
            cp.start()

    @pl.when(i + 1 < n_active)
    def _():
        _row_gather_start(tok_next_ref, xm_hbm, buf(1 - slot), sem_ref.at[1 - slot], tm)

    @pl.when(new_expert)
    def _():
        for cp in weight_copies(tile_expert_ref[i], wslot):
            cp.wait()

    @pl.when(active)
    def _():
        _row_gather_wait(xm_hbm, buf(slot), sem_ref.at[slot], tm)
        x = _load_slabs(xbuf_ref, slot * buf_rows, tm, SLAB, SLAB_PITCH).astype(BF16)
        a = jnp.dot(x, w1_buf[wslot].astype(BF16), preferred_element_type=F32)
        g = jnp.dot(x, w3_buf[wslot].astype(BF16), preferred_element_type=F32)
        hid = (a * jax.nn.sigmoid(a) * g).astype(BF16)
        _store_slabs(y_ref, jnp.dot(hid, w2_buf[wslot].astype(BF16), preferred_element_type=F32), SLAB)

    @pl.when(i >= n_active)
    def _():
        y_ref[...] = jnp.zeros_like(y_ref)


def _moe(plan, xm, w1, w3, w2):
    tile_expert, n_active, first, wslot, next_expert, row_token3 = plan
    n_tiles = row_token3.shape[0]
    tm = TM_MOE
    _, d_model, d_exp = w1.shape
    last = n_tiles - 1
    grid_spec = pltpu.PrefetchScalarGridSpec(
        num_scalar_prefetch=5,
        grid=(n_tiles,),
        in_specs=[pl.BlockSpec((1, 1, tm), lambda i, *_: (i, 0, 0), memory_space=pltpu.SMEM),
                  pl.BlockSpec((1, 1, tm), lambda i, *_: (jnp.minimum(i + 1, last), 0, 0), memory_space=pltpu.SMEM),
                  pl.BlockSpec(memory_space=pl.ANY), pl.BlockSpec(memory_space=pl.ANY),
                  pl.BlockSpec(memory_space=pl.ANY), pl.BlockSpec(memory_space=pl.ANY)],
        out_specs=pl.BlockSpec((tm * SLAB, LANES), lambda i, *_: (i, 0)),
        scratch_shapes=[pltpu.VMEM((2 * tm * SLAB_PITCH, LANES), F32),
                        pltpu.VMEM((2, d_model, d_exp), F32), pltpu.VMEM((2, d_model, d_exp), F32),
                        pltpu.VMEM((2, d_exp, d_model), F32),
                        pltpu.SemaphoreType.DMA((2,)), pltpu.SemaphoreType.DMA((3, 2))],
    )
    return pl.pallas_call(
        functools.partial(_moe_kernel, tm=tm),
        grid_spec=grid_spec,
        out_shape=jax.ShapeDtypeStruct((n_tiles * tm * SLAB, LANES), F32),
        compiler_params=pltpu.CompilerParams(dimension_semantics=("arbitrary",), vmem_limit_bytes=VMEM_LIMIT),
        name="moe",
    )(tile_expert, n_active, first, wslot, next_expert, row_token3, row_token3, xm, w1, w3, w2)


def _ple_kernel(dest_cur_ref, dest_next_ref, y_hbm, h1_ref, route_ref, p_ref, wple_ref, gple_ref, wpg_ref, bpg_ref,
                o_ref, ybuf_ref, sem_ref):
    i = pl.program_id(0)
    n = pl.num_programs(0)
    tm, d_model = h1_ref.shape
    slot = i % 2
    buf_rows = TOP_K * tm * SLAB_PITCH

    def buf(s):
        return ybuf_ref.at[pl.ds(s * buf_rows, buf_rows), :]

    @pl.when(i == 0)
    def _():
        _row_gather_start(dest_cur_ref, y_hbm, buf(0), sem_ref.at[0], TOP_K * tm)

    @pl.when(i + 1 < n)
    def _():
        _row_gather_start(dest_next_ref, y_hbm, buf(1 - slot), sem_ref.at[1 - slot], TOP_K * tm)

    _row_gather_wait(y_hbm, buf(slot), sem_ref.at[slot], TOP_K * tm)
    route = route_ref[...]
    y0 = _load_slabs(ybuf_ref, slot * buf_rows, tm, SLAB, SLAB_PITCH)
    y1 = _load_slabs(ybuf_ref, slot * buf_rows + tm * SLAB_PITCH, tm, SLAB, SLAB_PITCH)
    h2 = h1_ref[...] + (route[:, 2:3] * y0 + route[:, 3:4] * y1)
    xn = _rms(h2, gple_ref[...], d_model).astype(BF16)
    gate = jax.nn.sigmoid(jnp.dot(xn, wpg_ref[...], preferred_element_type=F32) + bpg_ref[...])
    pe = jnp.dot(p_ref[...].astype(BF16), wple_ref[...], preferred_element_type=F32)
    o_ref[...] = h2 + gate * pe


def _ple(dest3, y, h1, route, p2, w_ple, gple, w_pg, b_pg):
    t, d_model = h1.shape
    tm = TM_PLE
    n_tiles = t // tm
    last = n_tiles - 1
    const = lambda i: (0, 0)
    row = lambda i: (i, 0)
    full = lambda a: pl.BlockSpec(a.shape, const)
    return pl.pallas_call(
        _ple_kernel,
        grid=(n_tiles,),
        in_specs=[pl.BlockSpec((1, 1, TOP_K * tm), lambda i: (i, 0, 0), memory_space=pltpu.SMEM),
                  pl.BlockSpec((1, 1, TOP_K * tm), lambda i: (jnp.minimum(i + 1, last), 0, 0), memory_space=pltpu.SMEM),
                  pl.BlockSpec(memory_space=pl.ANY),
                  pl.BlockSpec((tm, d_model), row), pl.BlockSpec((tm, LANES), row), pl.BlockSpec((tm, p2.shape[1]), row),
                  full(w_ple), full(gple), pl.BlockSpec(w_pg.shape, const, pipeline_mode=pl.Buffered(1)), full(b_pg)],
        out_specs=pl.BlockSpec((tm, d_model), row),
        out_shape=jax.ShapeDtypeStruct((t, d_model), F32),
        scratch_shapes=[pltpu.VMEM((2 * TOP_K * tm * SLAB_PITCH, LANES), F32), pltpu.SemaphoreType.DMA((2,))],
        compiler_params=pltpu.CompilerParams(dimension_semantics=("arbitrary",), vmem_limit_bytes=VMEM_LIMIT),
        name="ple",
    )(dest3, dest3, y, h1, route, p2, w_ple, gple, w_pg, b_pg)


def _dispatch_plan(route):
    t = route.shape[0]
    tm = TM_MOE
    total = TOP_K * t
    n_tiles = total // tm + N_EXPERTS
    expert = route[:, :TOP_K].astype(jnp.int32).reshape(-1)
    pos = jnp.arange(total, dtype=jnp.int32)
    order = jnp.sort(expert * total + pos) % total
    rank = jnp.sort(order * total + pos) % total
    experts = jnp.arange(N_EXPERTS, dtype=jnp.int32)
    onehot = (expert[:, None] == experts[None, :]).astype(jnp.int32)
    counts = jnp.sum(onehot, axis=0)
    group_start = jnp.cumsum(counts) - counts
    tile_end = jnp.cumsum((counts + tm - 1) // tm)
    tile_start = tile_end - (counts + tm - 1) // tm
    n_active = tile_end[-1]
    shift = tile_start * tm - group_start
    dest = rank + jnp.sum(onehot * shift[None, :], axis=1)
    tiles = jnp.minimum(jnp.arange(n_tiles, dtype=jnp.int32), n_active - 1)
    tile_expert = jnp.sum((tile_end[None, :] <= tiles[:, None]).astype(jnp.int32), axis=1)
    j = jnp.arange(tm, dtype=jnp.int32)
    fill_row = (tile_start * tm + counts)[:, None] + j[None, :]
    fill_row = jnp.where(fill_row < (tile_end * tm)[:, None], fill_row, n_tiles * tm)
    fill_tok = (experts[:, None] * tm + j[None, :]) % t
    keys = jnp.concatenate([dest * t + pos // TOP_K, (fill_row * t + fill_tok).reshape(-1)])
    row_token3 = (jnp.sort(keys) % t).reshape(n_tiles, 1, tm)
    dest3 = dest.reshape(t // TM_PLE, TM_PLE, TOP_K).transpose(0, 2, 1).reshape(t // TM_PLE, 1, TOP_K * TM_PLE)
    prev_expert = jnp.concatenate([jnp.full((1,), -1, jnp.int32), tile_expert[:-1]])
    first = (tile_expert != prev_expert).astype(jnp.int32)
    nonempty = counts > 0
    wslot = (jnp.sum((nonempty[None, :] & (experts[None, :] < tile_expert[:, None])).astype(jnp.int32), axis=1)) % 2
    later = nonempty[None, :] & (experts[None, :] > tile_expert[:, None])
    next_expert = jnp.min(jnp.where(later, experts[None, :], N_EXPERTS), axis=1)
    next_expert = jnp.where(next_expert < N_EXPERTS, next_expert, -1)
    plan = (tile_expert, n_active.reshape(1), first, wslot, next_expert, row_token3)
    return plan, dest3


def _rope_swap_pad(w):
    half = QK_ROPE // 2
    x1 = w[..., QK_NOPE:QK_NOPE + half]
    x2 = w[..., QK_NOPE + half:]
    return jnp.concatenate([w[..., :QK_NOPE], x1, x2, x2, x1], axis=-1)


def kernel(x, p, positions, norm_mix, w_in, g_q_lat, g_kv_lat, w_uq, w_ukv, g_q_head, g_k_head, w_conv, g_out_attn, g_out_conv, w_out, norm_moe, w_group, b_group, w_router, b_router, w1, w3, w2, norm_ple, w_ple, w_ple_gate, b_ple_gate):
    batch, seq, d_model = x.shape
    depth = w_in.shape[0]
    t = batch * seq
    q_rank = g_q_lat.shape[1]
    kv_rank = g_kv_lat.shape[1]
    half = QK_ROPE // 2
    rope_lo = q_rank + kv_rank

    pos2 = positions.reshape(t, 1)
    inv_freq = ROPE_BASE ** (-jnp.arange(0, QK_ROPE, 2, dtype=F32) / QK_ROPE)
    freq = jnp.tile(inv_freq, LANES // half).reshape(1, LANES)

    h = x.reshape(t, d_model)
    for i in range(depth):
        w_in_p = _pack_w_in(w_in[i], rope_lo)
        wuq_p = _rope_swap_pad(w_uq[i]).reshape(q_rank, MLA_HEADS * HEAD_PAD).astype(BF16)
        wukv_p = jnp.concatenate([w_ukv[i][:, :, :QK_NOPE].reshape(kv_rank, -1),
                                  w_ukv[i][:, :, QK_NOPE:].reshape(kv_rank, -1)], axis=1).astype(BF16)
        gq_p = _rope_swap_pad(g_q_head[i]).reshape(1, HEAD_PAD)
        gk_p = _rope_swap_pad(g_k_head[i]).reshape(1, HEAD_PAD)
        q, k, v, yc = _front(h, pos2, freq, norm_mix[i].reshape(1, -1), w_in_p, g_q_lat[i].reshape(1, -1),
                             g_kv_lat[i].reshape(1, -1), wuq_p, wukv_p, gq_p, gk_p, w_conv[i],
                             g_out_conv[i].reshape(1, -1), seq=seq)
        ya = _attention(q, k, v, batch=batch, seq=seq)

        pad = LANES - N_GROUPS - N_EXPERTS
        w_gr32 = jnp.concatenate([w_group[i], w_router[i], jnp.zeros((d_model, pad), F32)], axis=1)
        w_gr_hi = w_gr32.astype(BF16)
        w_gr = jnp.concatenate([w_gr_hi, (w_gr32 - w_gr_hi.astype(F32)).astype(BF16)], axis=1)
        b_gr = jnp.concatenate([b_group[i], b_router[i], jnp.zeros((pad,), F32)]).reshape(1, LANES)
        h1, xm, route = _outproj(ya, yc, h, g_out_attn[i].reshape(1, -1), w_out[i].astype(BF16),
                                 norm_moe[i].reshape(1, -1), w_gr, b_gr)

        plan, dest3 = _dispatch_plan(route)
        y = _moe(plan, xm, w1[i], w3[i], w2[i])
        h = _ple(dest3, y, h1, route, p[i].reshape(t, -1), w_ple[i].astype(BF16), norm_ple[i].reshape(1, -1),
                 w_ple_gate[i].astype(BF16), b_ple_gate[i].reshape(1, -1))
    return h.reshape(batch, seq, d_model)
```

```python
import functools

import jax
import jax.numpy as jnp
from jax import lax
from jax.experimental import pallas as pl
from jax.experimental.pallas import tpu as pltpu

F32 = jnp.float32
BF16 = jnp.bfloat16

CHUNK = 64
EPS = 1e-6
MASK_VALUE = -1e30
ROPE_BASE = 10000.0

MLA_HEADS = 8
QK_NOPE = 128
QK_ROPE = 64
QK_HEAD = QK_NOPE + QK_ROPE
V_HEAD = 128
HEAD_PAD = 256
Q_SCALE = QK_HEAD ** -0.5 * 1.4426950408889634
N_GROUPS = 4
EXPERTS_PER_GROUP = 8
N_EXPERTS = N_GROUPS * EXPERTS_PER_GROUP
TOP_K = 2
LANES = 128
SLAB = 16
SLAB_PITCH = 20
DMA_PRIORITIES = 2

TM_FRONT = 512
TQ_ATTN = 1024
TM_OUT = 512
TM_MOE = 256
TM_PLE = 256
VMEM_LIMIT = 56 * 1024 * 1024


def _rms(x, g, n):
    ms = jnp.sum(x * x, axis=-1, keepdims=True) * (1.0 / n)
    return x * lax.rsqrt(ms + EPS) * g


def _store_slabs(ref, val, pitch, mask=None):
    n, width = val.shape
    for c in range(width // LANES):
        piece = val[:, c * LANES:(c + 1) * LANES]
        rows = pl.ds(c, n, stride=pitch)
        if mask is not None:
            piece = jnp.where(mask, piece, ref[rows, :])
        ref[rows, :] = piece


def _load_slabs(ref, start, n, slab, pitch):
    return jnp.concatenate([ref[pl.ds(start + c, n, stride=pitch), :] for c in range(slab)], axis=1)


def _pack_w_in_kernel(w_ref, o_ref, *, rope_lo):
    w = w_ref[...]
    half = QK_ROPE // 2
    rope_hi = rope_lo + QK_ROPE
    o_ref[...] = jnp.concatenate([w[:, :rope_hi], w[:, rope_lo + half:rope_hi], w[:, rope_lo:rope_lo + half],
                                  w[:, rope_hi:]], axis=1).astype(BF16)


def _pack_w_in(w, layer, rope_lo):
    _, k, n = w.shape
    rows = 256
    return pl.pallas_call(
        functools.partial(_pack_w_in_kernel, rope_lo=rope_lo),
        grid=(k // rows,),
        in_specs=[pl.BlockSpec((None, rows, n), lambda i: (layer, i, 0))],
        out_specs=pl.BlockSpec((rows, n + QK_ROPE), lambda i: (i, 0)),
        out_shape=jax.ShapeDtypeStruct((k, n + QK_ROPE), BF16),
        compiler_params=pltpu.CompilerParams(dimension_semantics=("arbitrary",), vmem_limit_bytes=VMEM_LIMIT),
        name="pack_w_in",
    )(w)


def _front_kernel(x_ref, pos_ref, freq_ref, gmix_ref, win_ref, gql_ref, gkvl_ref,
                  wuq_ref, wukv_ref, gq_ref, gk_ref, wconv_ref, gconv_ref,
                  q_ref, k_ref, v_ref, yc_ref, prev_ref, *, tiles_per_seq, q_rank, kv_rank, d_conv):
    i = pl.program_id(0)
    is_start = (i % tiles_per_seq) == 0
    tm = x_ref.shape[0]
    d_model = x_ref.shape[1]
    lat_w = q_rank + kv_rank + LANES

    xn = _rms(x_ref[...], gmix_ref[...], d_model).astype(BF16)
    lat = jnp.dot(xn, win_ref[:, :lat_w], preferred_element_type=F32)
    q_lat = lat[:, :q_rank]
    kv_lat = lat[:, q_rank:q_rank + kv_rank]
    kr = lat[:, q_rank + kv_rank:]

    qn = _rms(q_lat, gql_ref[...], q_rank).astype(BF16)
    kvn = _rms(kv_lat, gkvl_ref[...], kv_rank).astype(BF16)
    q_all = jnp.dot(qn, wuq_ref[...], preferred_element_type=F32)
    kv_all = jnp.dot(kvn, wukv_ref[...], preferred_element_type=F32)

    lane = lax.broadcasted_iota(jnp.int32, (tm, LANES), 1)
    ang = pos_ref[...].astype(F32) * freq_ref[...]
    cos = jnp.cos(ang)
    sin = jnp.sin(ang)
    rope_c = jnp.where(lane < QK_ROPE, cos, 0.0)
    rope_s = jnp.where(lane < QK_ROPE // 2, -sin, jnp.where(lane < QK_ROPE, sin, 0.0))
    rope_lanes = lane < QK_ROPE

    def rope(t):
        return t * rope_c + pltpu.roll(t, QK_ROPE, 1) * rope_s

    gq = gq_ref[...]
    gk = gk_ref[...]
    inv_n = 1.0 / QK_HEAD
    kr_ssq = jnp.sum(jnp.where(rope_lanes, kr * kr, 0.0), axis=-1, keepdims=True)
    kr_rot = rope(kr * gk[:, QK_NOPE:])
    for h in range(MLA_HEADS):
        qa = q_all[:, h * HEAD_PAD:h * HEAD_PAD + QK_NOPE]
        qb = q_all[:, h * HEAD_PAD + QK_NOPE:(h + 1) * HEAD_PAD]
        ssq = (jnp.sum(qa * qa, axis=-1, keepdims=True)
               + jnp.sum(jnp.where(rope_lanes, qb * qb, 0.0), axis=-1, keepdims=True))
        inv = lax.rsqrt(ssq * inv_n + EPS)
        q_ref[:, h * HEAD_PAD:h * HEAD_PAD + QK_NOPE] = (qa * inv * gq[:, :QK_NOPE] * Q_SCALE).astype(BF16)
        q_ref[:, h * HEAD_PAD + QK_NOPE:(h + 1) * HEAD_PAD] = (rope(qb * inv * gq[:, QK_NOPE:]) * Q_SCALE).astype(BF16)

        ka = kv_all[:, h * QK_NOPE:(h + 1) * QK_NOPE]
        kinv = lax.rsqrt((jnp.sum(ka * ka, axis=-1, keepdims=True) + kr_ssq) * inv_n + EPS)
        k_ref[:, h * HEAD_PAD:h * HEAD_PAD + QK_NOPE] = (ka * kinv * gk[:, :QK_NOPE]).astype(BF16)
        k_ref[:, h * HEAD_PAD + QK_NOPE:(h + 1) * HEAD_PAD] = (kr_rot * kinv).astype(BF16)
        v_lo = MLA_HEADS * QK_NOPE + h * V_HEAD
        v_ref[:, h * HEAD_PAD:h * HEAD_PAD + V_HEAD] = kv_all[:, v_lo:v_lo + V_HEAD].astype(BF16)
        v_ref[:, h * HEAD_PAD + V_HEAD:(h + 1) * HEAD_PAD] = jnp.ones((tm, HEAD_PAD - V_HEAD), BF16)

    cv = jnp.dot(xn, win_ref[:, lat_w:], preferred_element_type=F32)
    b_gate = cv[:, :d_conv]
    u = cv[:, d_conv:2 * d_conv] * cv[:, 2 * d_conv:]

    @pl.when(is_start)
    def _():
        prev_ref[...] = jnp.zeros_like(prev_ref)

    prev = prev_ref[...]
    hm1 = prev[7:8, :]
    hm2 = prev[6:7, :]
    row = lax.broadcasted_iota(jnp.int32, u.shape, 0)
    u1 = jnp.where(row == 0, hm1, pltpu.roll(u, 1, 0))
    u2 = jnp.where(row == 0, hm2, jnp.where(row == 1, hm1, pltpu.roll(u, 2, 0)))
    prev_ref[...] = u[tm - 8:, :]
    wc = wconv_ref[...]
    y = b_gate * (u2 * wc[0:1, :] + u1 * wc[1:2, :] + u * wc[2:3, :])
    yc_ref[...] = _rms(y, gconv_ref[...], d_conv).astype(BF16)


def _front(x2, pos2, freq, gmix, w_in_p, gql, gkvl, wuq_p, wukv_p, gq_p, gk_p, wconv, gconv, *, seq):
    t, d_model = x2.shape
    tm = TM_FRONT
    q_rank = gql.shape[1]
    kv_rank = gkvl.shape[1]
    d_conv = gconv.shape[1]
    const = lambda i: (0, 0)
    row = lambda i: (i, 0)
    full = lambda a: pl.BlockSpec(a.shape, const)
    kern = functools.partial(_front_kernel, tiles_per_seq=seq // tm, q_rank=q_rank, kv_rank=kv_rank, d_conv=d_conv)
    return pl.pallas_call(
        kern,
        grid=(t // tm,),
        in_specs=[pl.BlockSpec((tm, d_model), row), pl.BlockSpec((tm, 1), row), full(freq), full(gmix),
                  pl.BlockSpec(w_in_p.shape, const, pipeline_mode=pl.Buffered(1)),
                  full(gql), full(gkvl), full(wuq_p), full(wukv_p), full(gq_p), full(gk_p), full(wconv), full(gconv)],
        out_specs=[pl.BlockSpec((tm, MLA_HEADS * HEAD_PAD), row), pl.BlockSpec((tm, MLA_HEADS * HEAD_PAD), row),
                   pl.BlockSpec((tm, MLA_HEADS * HEAD_PAD), row), pl.BlockSpec((tm, d_conv), row)],
        out_shape=[jax.ShapeDtypeStruct((t, MLA_HEADS * HEAD_PAD), BF16), jax.ShapeDtypeStruct((t, MLA_HEADS * HEAD_PAD), BF16),
                   jax.ShapeDtypeStruct((t, MLA_HEADS * HEAD_PAD), BF16), jax.ShapeDtypeStruct((t, d_conv), BF16)],
        scratch_shapes=[pltpu.VMEM((8, d_conv), F32)],
        compiler_params=pltpu.CompilerParams(dimension_semantics=("arbitrary",), vmem_limit_bytes=VMEM_LIMIT),
        name="front",
    )(x2, pos2, freq, gmix, w_in_p, gql, gkvl, wuq_p, wukv_p, gq_p, gk_p, wconv, gconv)


def _attn_kernel(q_ref, k_ref, v_ref, o_ref):
    i = pl.program_id(2)
    tq = q_ref.shape[0]
    half = tq // 2
    q_halves = (q_ref[:half, :], q_ref[half:, :])

    def step(start, n_keys, masks, carry):
        kvs = [(k_ref[pl.ds(start, n), :], v_ref[pl.ds(start, n), :]) for n in n_keys]
        scores = [lax.dot_general(qh, kb, (((1,), (1,)), ((), ())), preferred_element_type=F32)
                  for qh, (kb, _) in zip(q_halves, kvs)]
        scores = [s if mask is None else jnp.where(mask, s, MASK_VALUE) for s, mask in zip(scores, masks)]
        m_new = [jnp.maximum(m, jnp.max(s, axis=-1, keepdims=True)) for s, (m, _) in zip(scores, carry)]
        probs = [jnp.exp2(s - mn).astype(BF16) for s, mn in zip(scores, m_new)]
        return tuple((mn, jnp.exp2(m - mn) * acc + jnp.dot(p, vb, preferred_element_type=F32))
                     for p, mn, (m, acc), (_, vb) in zip(probs, m_new, carry, kvs))

    def chunk_mask(n, first_row):
        r = (lax.broadcasted_iota(jnp.int32, (half, n), 0) + first_row) // CHUNK
        c = lax.broadcasted_iota(jnp.int32, (half, n), 1) // CHUNK
        return c <= r

    for qi in range(k_ref.shape[0] // tq):
        @pl.when(i == qi)
        def _(qi=qi):
            carry = tuple((jnp.full((half, 1), MASK_VALUE, F32), jnp.zeros((half, HEAD_PAD), F32)) for _ in range(2))
            for j in range(qi):
                carry = step(j * tq, (tq, tq), (None, None), carry)
            (_, acc0), (_, acc1) = step(qi * tq, (half, tq), (chunk_mask(half, 0), chunk_mask(tq, half)), carry)
            o_ref[:half, :] = acc0[:, :V_HEAD] / acc0[:, V_HEAD:]
            o_ref[half:, :] = acc1[:, :V_HEAD] / acc1[:, V_HEAD:]


def _attention(q, k, v, *, batch, seq):
    tq = TQ_ATTN
    nq = seq // tq
    return pl.pallas_call(
        _attn_kernel,
        grid=(batch, MLA_HEADS, nq),
        in_specs=[pl.BlockSpec((tq, HEAD_PAD), lambda b, h, i: (b * nq + i, h)),
                  pl.BlockSpec((seq, HEAD_PAD), lambda b, h, i: (b, h)),
                  pl.BlockSpec((seq, HEAD_PAD), lambda b, h, i: (b, h))],
        out_specs=pl.BlockSpec((tq, V_HEAD), lambda b, h, i: (b * nq + i, h)),
        out_shape=jax.ShapeDtypeStruct((batch * seq, MLA_HEADS * V_HEAD), F32),
        compiler_params=pltpu.CompilerParams(dimension_semantics=("arbitrary", "arbitrary", "arbitrary"),
                                             vmem_limit_bytes=VMEM_LIMIT),
        name="attn",
    )(q, k, v)


def _outproj_kernel(ya_ref, yc_ref, x_ref, gattn_ref, wout_ref, gmoe_ref, wgr_ref, bgr_ref,
                    h1_ref, xm_ref, route_ref):
    tm, d_attn = ya_ref.shape
    d_model = x_ref.shape[1]
    ya = _rms(ya_ref[...], gattn_ref[...], d_attn).astype(BF16)
    h1 = (x_ref[...]
          + jnp.dot(ya, wout_ref[:d_attn, :], preferred_element_type=F32)
          + jnp.dot(yc_ref[...], wout_ref[d_attn:, :], preferred_element_type=F32))
    h1_ref[...] = h1
    xm = _rms(h1, gmoe_ref[...], d_model)
    _store_slabs(xm_ref, xm, SLAB)

    x_hi = xm.astype(BF16)
    x_lo = (xm - x_hi.astype(F32)).astype(BF16)
    hi_part = jnp.dot(x_hi, wgr_ref[...], preferred_element_type=F32)
    lo_part = jnp.dot(x_lo, wgr_ref[:, :LANES], preferred_element_type=F32)
    logits = hi_part[:, :LANES] + hi_part[:, LANES:] + lo_part + bgr_ref[...]
    lane = lax.broadcasted_iota(jnp.int32, (tm, LANES), 1).astype(F32)
    neg = -jnp.inf
    big = 1e9
    gl = jnp.where(lane < N_GROUPS, logits, neg)
    gmax = jnp.max(gl, axis=-1, keepdims=True)
    gsel = jnp.min(jnp.where(gl == gmax, lane, big), axis=-1, keepdims=True)
    p_g = 1.0 / jnp.sum(jnp.exp(gl - gmax), axis=-1, keepdims=True)
    lo = N_GROUPS + EXPERTS_PER_GROUP * gsel
    el = jnp.where((lane >= lo) & (lane < lo + EXPERTS_PER_GROUP), logits, neg)
    v1 = jnp.max(el, axis=-1, keepdims=True)
    i1 = jnp.min(jnp.where(el == v1, lane, big), axis=-1, keepdims=True)
    el2 = jnp.where(lane == i1, neg, el)
    v2 = jnp.max(el2, axis=-1, keepdims=True)
    i2 = jnp.min(jnp.where(el2 == v2, lane, big), axis=-1, keepdims=True)
    e2 = jnp.exp(v2 - v1)
    den = 1.0 + e2
    w1 = p_g / den
    w2 = p_g * e2 / den
    route_ref[...] = jnp.where(lane == 0, i1 - N_GROUPS,
                               jnp.where(lane == 1, i2 - N_GROUPS,
                                         jnp.where(lane == 2, w1, jnp.where(lane == 3, w2, 0.0))))


def _outproj(ya, yc, x2, gattn, w_out, gmoe, w_gr, b_gr):
    t, d_model = x2.shape
    d_attn = ya.shape[1]
    tm = TM_OUT
    const = lambda i: (0, 0)
    row = lambda i: (i, 0)
    full = lambda a: pl.BlockSpec(a.shape, const)
    return pl.pallas_call(
        _outproj_kernel,
        grid=(t // tm,),
        in_specs=[pl.BlockSpec((tm, d_attn), row), pl.BlockSpec((tm, yc.shape[1]), row), pl.BlockSpec((tm, d_model), row),
                  full(gattn), pl.BlockSpec(w_out.shape, const, pipeline_mode=pl.Buffered(1)), full(gmoe), full(w_gr), full(b_gr)],
        out_specs=[pl.BlockSpec((tm, d_model), row), pl.BlockSpec((tm * SLAB, LANES), row), pl.BlockSpec((tm, LANES), row)],
        out_shape=[jax.ShapeDtypeStruct((t, d_model), F32), jax.ShapeDtypeStruct((t * SLAB, LANES), F32),
                   jax.ShapeDtypeStruct((t, LANES), F32)],
        compiler_params=pltpu.CompilerParams(dimension_semantics=("arbitrary",), vmem_limit_bytes=VMEM_LIMIT),
        name="outproj",
    )(ya, yc, x2, gattn, w_out, gmoe, w_gr, b_gr)


def _row_gather_start(idx_ref, src_hbm, dst_ref, sem, n_rows):
    def body(pair, carry):
        for prio in range(DMA_PRIORITIES):
            r = pair * DMA_PRIORITIES + prio
            src = src_hbm.at[pl.ds(idx_ref[0, 0, r] * SLAB, SLAB), :]
            pltpu.make_async_copy(src, dst_ref.at[pl.ds(r * SLAB_PITCH, SLAB), :], sem).start(priority=prio)
        return carry
    lax.fori_loop(0, n_rows // DMA_PRIORITIES, body, 0, unroll=True)


def _row_gather_wait(src_hbm, dst_ref, sem, n_rows):
    pltpu.make_async_copy(src_hbm.at[pl.ds(0, n_rows * SLAB), :], dst_ref.at[pl.ds(0, n_rows * SLAB), :], sem).wait()


def _moe_kernel(tile_expert_ref, n_active_ref, first_ref, wslot_ref, next_expert_ref, tok_cur_ref, tok_next_ref,
                xm_hbm, w1_hbm, w3_hbm, w2_hbm, y_ref, xbuf_ref, w1_buf, w3_buf, w2_buf, sem_ref, wsem_ref, *, tm):
    i = pl.program_id(0)
    n_active = n_active_ref[0]
    slot = i % 2
    buf_rows = tm * SLAB_PITCH
    wslot = wslot_ref[i]

    def buf(s):
        return xbuf_ref.at[pl.ds(s * buf_rows, buf_rows), :]

    def weight_copies(e, s):
        return (pltpu.make_async_copy(w1_hbm.at[e], w1_buf.at[s], wsem_ref.at[0, s]),
                pltpu.make_async_copy(w3_hbm.at[e], w3_buf.at[s], wsem_ref.at[1, s]),
                pltpu.make_async_copy(w2_hbm.at[e], w2_buf.at[s], wsem_ref.at[2, s]))

    @pl.when(i == 0)
    def _():
        for cp in weight_copies(tile_expert_ref[0], 0):
            cp.start()
        _row_gather_start(tok_cur_ref, xm_hbm, buf(0), sem_ref.at[0], tm)

    active = i < n_active
    new_expert = active & (first_ref[i] == 1)

    @pl.when(new_expert & (next_expert_ref[i] >= 0))
    def _():
        for cp in weight_copies(next_expert_ref[i], 1 - wslot):
            cp.start()

    @pl.when(i + 1 < n_active)
    def _():
        _row_gather_start(tok_next_ref, xm_hbm, buf(1 - slot), sem_ref.at[1 - slot], tm)

    @pl.when(new_expert)
    def _():
        for cp in weight_copies(tile_expert_ref[i], wslot):
            cp.wait()

    @pl.when(active)
    def _():
        _row_gather_wait(xm_hbm, buf(slot), sem_ref.at[slot], tm)
        x = _load_slabs(xbuf_ref, slot * buf_rows, tm, SLAB, SLAB_PITCH).astype(BF16)
        a = jnp.dot(x, w1_buf[wslot].astype(BF16), preferred_element_type=F32)
        g = jnp.dot(x, w3_buf[wslot].astype(BF16), preferred_element_type=F32)
        hid = (a * jax.nn.sigmoid(a) * g).astype(BF16)
        _store_slabs(y_ref, jnp.dot(hid, w2_buf[wslot].astype(BF16), preferred_element_type=F32), SLAB)

    @pl.when(i >= n_active)
    def _():
        y_ref[...] = jnp.zeros_like(y_ref)


def _moe(plan, xm, w1, w3, w2):
    tile_expert, n_active, first, wslot, next_expert, row_token3 = plan
    n_tiles = row_token3.shape[0]
    tm = TM_MOE
    _, d_model, d_exp = w1.shape
    last = n_tiles - 1
    grid_spec = pltpu.PrefetchScalarGridSpec(
        num_scalar_prefetch=5,
        grid=(n_tiles,),
        in_specs=[pl.BlockSpec((1, 1, tm), lambda i, *_: (i, 0, 0), memory_space=pltpu.SMEM),
                  pl.BlockSpec((1, 1, tm), lambda i, *_: (jnp.minimum(i + 1, last), 0, 0), memory_space=pltpu.SMEM),
                  pl.BlockSpec(memory_space=pl.ANY), pl.BlockSpec(memory_space=pl.ANY),
                  pl.BlockSpec(memory_space=pl.ANY), pl.BlockSpec(memory_space=pl.ANY)],
        out_specs=pl.BlockSpec((tm * SLAB, LANES), lambda i, *_: (i, 0)),
        scratch_shapes=[pltpu.VMEM((2 * tm * SLAB_PITCH, LANES), F32),
                        pltpu.VMEM((2, d_model, d_exp), F32), pltpu.VMEM((2, d_model, d_exp), F32),
                        pltpu.VMEM((2, d_exp, d_model), F32),
                        pltpu.SemaphoreType.DMA((2,)), pltpu.SemaphoreType.DMA((3, 2))],
    )
    return pl.pallas_call(
        functools.partial(_moe_kernel, tm=tm),
        grid_spec=grid_spec,
        out_shape=jax.ShapeDtypeStruct((n_tiles * tm * SLAB, LANES), F32),
        compiler_params=pltpu.CompilerParams(dimension_semantics=("arbitrary",), vmem_limit_bytes=VMEM_LIMIT),
        name="moe",
    )(tile_expert, n_active, first, wslot, next_expert, row_token3, row_token3, xm, w1, w3, w2)


def _ple_kernel(dest_cur_ref, dest_next_ref, y_hbm, h1_ref, route_ref, p_ref, wple_ref, gple_ref, wpg_ref, bpg_ref,
                o_ref, ybuf_ref, sem_ref):
    i = pl.program_id(0)
    n = pl.num_programs(0)
    tm, d_model = h1_ref.shape
    slot = i % 2
    buf_rows = TOP_K * tm * SLAB_PITCH

    def buf(s):
        return ybuf_ref.at[pl.ds(s * buf_rows, buf_rows), :]

    @pl.when(i == 0)
    def _():
        _row_gather_start(dest_cur_ref, y_hbm, buf(0), sem_ref.at[0], TOP_K * tm)

    @pl.when(i + 1 < n)
    def _():
        _row_gather_start(dest_next_ref, y_hbm, buf(1 - slot), sem_ref.at[1 - slot], TOP_K * tm)

    _row_gather_wait(y_hbm, buf(slot), sem_ref.at[slot], TOP_K * tm)
    route = route_ref[...]
    y0 = _load_slabs(ybuf_ref, slot * buf_rows, tm, SLAB, SLAB_PITCH)
    y1 = _load_slabs(ybuf_ref, slot * buf_rows + tm * SLAB_PITCH, tm, SLAB, SLAB_PITCH)
    h2 = h1_ref[...] + (route[:, 2:3] * y0 + route[:, 3:4] * y1)
    xn = _rms(h2, gple_ref[...], d_model).astype(BF16)
    gate = jax.nn.sigmoid(jnp.dot(xn, wpg_ref[...], preferred_element_type=F32) + bpg_ref[...])
    pe = jnp.dot(p_ref[...].astype(BF16), wple_ref[...], preferred_element_type=F32)
    o_ref[...] = h2 + gate * pe


def _ple(dest3, y, h1, route, p2, w_ple, gple, w_pg, b_pg):
    t, d_model = h1.shape
    tm = TM_PLE
    n_tiles = t // tm
    last = n_tiles - 1
    const = lambda i: (0, 0)
    row = lambda i: (i, 0)
    full = lambda a: pl.BlockSpec(a.shape, const)
    return pl.pallas_call(
        _ple_kernel,
        grid=(n_tiles,),
        in_specs=[pl.BlockSpec((1, 1, TOP_K * tm), lambda i: (i, 0, 0), memory_space=pltpu.SMEM),
                  pl.BlockSpec((1, 1, TOP_K * tm), lambda i: (jnp.minimum(i + 1, last), 0, 0), memory_space=pltpu.SMEM),
                  pl.BlockSpec(memory_space=pl.ANY),
                  pl.BlockSpec((tm, d_model), row), pl.BlockSpec((tm, LANES), row), pl.BlockSpec((tm, p2.shape[1]), row),
                  full(w_ple), full(gple), pl.BlockSpec(w_pg.shape, const, pipeline_mode=pl.Buffered(1)), full(b_pg)],
        out_specs=pl.BlockSpec((tm, d_model), row),
        out_shape=jax.ShapeDtypeStruct((t, d_model), F32),
        scratch_shapes=[pltpu.VMEM((2 * TOP_K * tm * SLAB_PITCH, LANES), F32), pltpu.SemaphoreType.DMA((2,))],
        compiler_params=pltpu.CompilerParams(dimension_semantics=("arbitrary",), vmem_limit_bytes=VMEM_LIMIT),
        name="ple",
    )(dest3, dest3, y, h1, route, p2, w_ple, gple, w_pg, b_pg)


def _dispatch_plan(route):
    t = route.shape[0]
    tm = TM_MOE
    total = TOP_K * t
    n_tiles = total // tm + N_EXPERTS
    expert = route[:, :TOP_K].astype(jnp.int32).reshape(-1)
    pos = jnp.arange(total, dtype=jnp.int32)
    order = jnp.sort(expert * total + pos) % total
    rank = jnp.sort(order * total + pos) % total
    experts = jnp.arange(N_EXPERTS, dtype=jnp.int32)
    onehot = (expert[:, None] == experts[None, :]).astype(jnp.int32)
    counts = jnp.sum(onehot, axis=0)
    group_start = jnp.cumsum(counts) - counts
    tile_end = jnp.cumsum((counts + tm - 1) // tm)
    tile_start = tile_end - (counts + tm - 1) // tm
    n_active = tile_end[-1]
    shift = tile_start * tm - group_start
    dest = rank + jnp.sum(onehot * shift[None, :], axis=1)
    tiles = jnp.minimum(jnp.arange(n_tiles, dtype=jnp.int32), n_active - 1)
    tile_expert = jnp.sum((tile_end[None, :] <= tiles[:, None]).astype(jnp.int32), axis=1)
    j = jnp.arange(tm, dtype=jnp.int32)
    fill_row = (tile_start * tm + counts)[:, None] + j[None, :]
    fill_row = jnp.where(fill_row < (tile_end * tm)[:, None], fill_row, n_tiles * tm)
    fill_tok = (experts[:, None] * tm + j[None, :]) % t
    keys = jnp.concatenate([dest * t + pos // TOP_K, (fill_row * t + fill_tok).reshape(-1)])
    row_token3 = (jnp.sort(keys) % t).reshape(n_tiles, 1, tm)
    dest3 = dest.reshape(t // TM_PLE, TM_PLE, TOP_K).transpose(0, 2, 1).reshape(t // TM_PLE, 1, TOP_K * TM_PLE)
    prev_expert = jnp.concatenate([jnp.full((1,), -1, jnp.int32), tile_expert[:-1]])
    first = (tile_expert != prev_expert).astype(jnp.int32)
    nonempty = counts > 0
    wslot = (jnp.sum((nonempty[None, :] & (experts[None, :] < tile_expert[:, None])).astype(jnp.int32), axis=1)) % 2
    later = nonempty[None, :] & (experts[None, :] > tile_expert[:, None])
    next_expert = jnp.min(jnp.where(later, experts[None, :], N_EXPERTS), axis=1)
    next_expert = jnp.where(next_expert < N_EXPERTS, next_expert, -1)
    plan = (tile_expert, n_active.reshape(1), first, wslot, next_expert, row_token3)
    return plan, dest3


def _rope_swap_pad(w):
    half = QK_ROPE // 2
    x1 = w[..., QK_NOPE:QK_NOPE + half]
    x2 = w[..., QK_NOPE + half:]
    return jnp.concatenate([w[..., :QK_NOPE], x1, x2, x2, x1], axis=-1)


def kernel(x, p, positions, norm_mix, w_in, g_q_lat, g_kv_lat, w_uq, w_ukv, g_q_head, g_k_head, w_conv, g_out_attn, g_out_conv, w_out, norm_moe, w_group, b_group, w_router, b_router, w1, w3, w2, norm_ple, w_ple, w_ple_gate, b_ple_gate):
    batch, seq, d_model = x.shape
    depth = w_in.shape[0]
    t = batch * seq
    q_rank = g_q_lat.shape[1]
    kv_rank = g_kv_lat.shape[1]
    half = QK_ROPE // 2
    rope_lo = q_rank + kv_rank

    pos2 = positions.reshape(t, 1)
    inv_freq = ROPE_BASE ** (-jnp.arange(0, QK_ROPE, 2, dtype=F32) / QK_ROPE)
    freq = jnp.tile(inv_freq, LANES // half).reshape(1, LANES)

    h = x.reshape(t, d_model)
    for i in range(depth):
        w_in_p = _pack_w_in(w_in, i, rope_lo)
        wuq_p = _rope_swap_pad(w_uq[i]).reshape(q_rank, MLA_HEADS * HEAD_PAD).astype(BF16)
        wukv_p = jnp.concatenate([w_ukv[i][:, :, :QK_NOPE].reshape(kv_rank, -1),
                                  w_ukv[i][:, :, QK_NOPE:].reshape(kv_rank, -1)], axis=1).astype(BF16)
        gq_p = _rope_swap_pad(g_q_head[i]).reshape(1, HEAD_PAD)
        gk_p = _rope_swap_pad(g_k_head[i]).reshape(1, HEAD_PAD)
        q, k, v, yc = _front(h, pos2, freq, norm_mix[i].reshape(1, -1), w_in_p, g_q_lat[i].reshape(1, -1),
                             g_kv_lat[i].reshape(1, -1), wuq_p, wukv_p, gq_p, gk_p, w_conv[i],
                             g_out_conv[i].reshape(1, -1), seq=seq)
        ya = _attention(q, k, v, batch=batch, seq=seq)

        pad = LANES - N_GROUPS - N_EXPERTS
        w_gr32 = jnp.concatenate([w_group[i], w_router[i], jnp.zeros((d_model, pad), F32)], axis=1)
        w_gr_hi = w_gr32.astype(BF16)
        w_gr = jnp.concatenate([w_gr_hi, (w_gr32 - w_gr_hi.astype(F32)).astype(BF16)], axis=1)
        b_gr = jnp.concatenate([b_group[i], b_router[i], jnp.zeros((pad,), F32)]).reshape(1, LANES)
        h1, xm, route = _outproj(ya, yc, h, g_out_attn[i].reshape(1, -1), w_out[i].astype(BF16),
                                 norm_moe[i].reshape(1, -1), w_gr, b_gr)

        plan, dest3 = _dispatch_plan(route)
        y = _moe(plan, xm, w1[i], w3[i], w2[i])
        h = _ple(dest3, y, h1, route, p[i].reshape(t, -1), w_ple[i].astype(BF16), norm_ple[i].reshape(1, -1),
                 w_ple_gate[i].astype(BF16), b_ple_gate[i].reshape(1, -1))
    return h.reshape(batch, seq, d_model)
```

```python
import functools

import jax
import jax.numpy as jnp
from jax import lax
from jax.experimental import pallas as pl
from jax.experimental.pallas import tpu as pltpu

F32 = jnp.float32
BF16 = jnp.bfloat16

CHUNK = 64
EPS = 1e-6
MASK_VALUE = -1e30
ROPE_BASE = 10000.0

MLA_HEADS = 8
QK_NOPE = 128
QK_ROPE = 64
QK_HEAD = QK_NOPE + QK_ROPE
V_HEAD = 128
HEAD_PAD = 256
Q_SCALE = QK_HEAD ** -0.5 * 1.4426950408889634
N_GROUPS = 4
EXPERTS_PER_GROUP = 8
N_EXPERTS = N_GROUPS * EXPERTS_PER_GROUP
TOP_K = 2
LANES = 128
SLAB = 16
SLAB_PITCH = 20
DMA_PRIORITIES = 2

TM_FRONT = 512
TQ_ATTN = 1024
TM_OUT = 512
TM_MOE = 256
TM_PLE = 256
VMEM_LIMIT = 56 * 1024 * 1024


def _rms(x, g, n):
    ms = jnp.sum(x * x, axis=-1, keepdims=True) * (1.0 / n)
    return x * lax.rsqrt(ms + EPS) * g


def _store_slabs(ref, val, pitch, mask=None):
    n, width = val.shape
    for c in range(width // LANES):
        piece = val[:, c * LANES:(c + 1) * LANES]
        rows = pl.ds(c, n, stride=pitch)
        if mask is not None:
            piece = jnp.where(mask, piece, ref[rows, :])
        ref[rows, :] = piece


def _load_slabs(ref, start, n, slab, pitch):
    return jnp.concatenate([ref[pl.ds(start + c, n, stride=pitch), :] for c in range(slab)], axis=1)


def _front_kernel(x_ref, pos_ref, freq_ref, gmix_ref, win_ref, wrope_ref, gql_ref, gkvl_ref,
                  wuq_ref, wukv_ref, gq_ref, gk_ref, wconv_ref, gconv_ref,
                  q_ref, k_ref, v_ref, yc_ref, prev_ref, *, tiles_per_seq, q_rank, kv_rank, d_conv):
    i = pl.program_id(0)
    is_start = (i % tiles_per_seq) == 0
    tm = x_ref.shape[0]
    d_model = x_ref.shape[1]
    rope_lo = q_rank + kv_rank
    nt = (((1,), (1,)), ((), ()))

    xn = _rms(x_ref[...], gmix_ref[...], d_model).astype(BF16)
    lat = lax.dot_general(xn, win_ref[:rope_lo, :], nt, preferred_element_type=F32)
    q_lat = lat[:, :q_rank]
    kv_lat = lat[:, q_rank:]
    kr = lax.dot_general(xn, wrope_ref[...], nt, preferred_element_type=F32)

    qn = _rms(q_lat, gql_ref[...], q_rank).astype(BF16)
    kvn = _rms(kv_lat, gkvl_ref[...], kv_rank).astype(BF16)
    q_all = jnp.dot(qn, wuq_ref[...], preferred_element_type=F32)
    kv_all = jnp.dot(kvn, wukv_ref[...], preferred_element_type=F32)

    lane = lax.broadcasted_iota(jnp.int32, (tm, LANES), 1)
    ang = pos_ref[...].astype(F32) * freq_ref[...]
    cos = jnp.cos(ang)
    sin = jnp.sin(ang)
    rope_c = jnp.where(lane < QK_ROPE, cos, 0.0)
    rope_s = jnp.where(lane < QK_ROPE // 2, -sin, jnp.where(lane < QK_ROPE, sin, 0.0))
    rope_lanes = lane < QK_ROPE

    def rope(t):
        return t * rope_c + pltpu.roll(t, QK_ROPE, 1) * rope_s

    gq = gq_ref[...]
    gk = gk_ref[...]
    inv_n = 1.0 / QK_HEAD
    kr_ssq = jnp.sum(jnp.where(rope_lanes, kr * kr, 0.0), axis=-1, keepdims=True)
    kr_rot = rope(kr * gk[:, QK_NOPE:])
    for h in range(MLA_HEADS):
        qa = q_all[:, h * HEAD_PAD:h * HEAD_PAD + QK_NOPE]
        qb = q_all[:, h * HEAD_PAD + QK_NOPE:(h + 1) * HEAD_PAD]
        ssq = (jnp.sum(qa * qa, axis=-1, keepdims=True)
               + jnp.sum(jnp.where(rope_lanes, qb * qb, 0.0), axis=-1, keepdims=True))
        inv = lax.rsqrt(ssq * inv_n + EPS)
        q_ref[:, h * HEAD_PAD:h * HEAD_PAD + QK_NOPE] = (qa * inv * gq[:, :QK_NOPE] * Q_SCALE).astype(BF16)
        q_ref[:, h * HEAD_PAD + QK_NOPE:(h + 1) * HEAD_PAD] = (rope(qb * inv * gq[:, QK_NOPE:]) * Q_SCALE).astype(BF16)

        ka = kv_all[:, h * QK_NOPE:(h + 1) * QK_NOPE]
        kinv = lax.rsqrt((jnp.sum(ka * ka, axis=-1, keepdims=True) + kr_ssq) * inv_n + EPS)
        k_ref[:, h * HEAD_PAD:h * HEAD_PAD + QK_NOPE] = (ka * kinv * gk[:, :QK_NOPE]).astype(BF16)
        k_ref[:, h * HEAD_PAD + QK_NOPE:(h + 1) * HEAD_PAD] = (kr_rot * kinv).astype(BF16)
        v_lo = MLA_HEADS * QK_NOPE + h * V_HEAD
        v_ref[:, h * HEAD_PAD:h * HEAD_PAD + V_HEAD] = kv_all[:, v_lo:v_lo + V_HEAD].astype(BF16)
        v_ref[:, h * HEAD_PAD + V_HEAD:(h + 1) * HEAD_PAD] = jnp.ones((tm, HEAD_PAD - V_HEAD), BF16)

    cv = lax.dot_general(xn, win_ref[rope_lo + QK_ROPE:, :], nt, preferred_element_type=F32)
    b_gate = cv[:, :d_conv]
    u = cv[:, d_conv:2 * d_conv] * cv[:, 2 * d_conv:]

    @pl.when(is_start)
    def _():
        prev_ref[...] = jnp.zeros_like(prev_ref)

    prev = prev_ref[...]
    hm1 = prev[7:8, :]
    hm2 = prev[6:7, :]
    row = lax.broadcasted_iota(jnp.int32, u.shape, 0)
    u1 = jnp.where(row == 0, hm1, pltpu.roll(u, 1, 0))
    u2 = jnp.where(row == 0, hm2, jnp.where(row == 1, hm1, pltpu.roll(u, 2, 0)))
    prev_ref[...] = u[tm - 8:, :]
    wc = wconv_ref[...]
    y = b_gate * (u2 * wc[0:1, :] + u1 * wc[1:2, :] + u * wc[2:3, :])
    yc_ref[...] = _rms(y, gconv_ref[...], d_conv).astype(BF16)


def _front(x2, pos2, freq, gmix, w_in_t, w_rope, gql, gkvl, wuq_p, wukv_p, gq_p, gk_p, wconv, gconv, *, seq):
    t, d_model = x2.shape
    tm = TM_FRONT
    q_rank = gql.shape[1]
    kv_rank = gkvl.shape[1]
    d_conv = gconv.shape[1]
    const = lambda i: (0, 0)
    row = lambda i: (i, 0)
    full = lambda a: pl.BlockSpec(a.shape, const)
    kern = functools.partial(_front_kernel, tiles_per_seq=seq // tm, q_rank=q_rank, kv_rank=kv_rank, d_conv=d_conv)
    return pl.pallas_call(
        kern,
        grid=(t // tm,),
        in_specs=[pl.BlockSpec((tm, d_model), row), pl.BlockSpec((tm, 1), row), full(freq), full(gmix),
                  pl.BlockSpec(w_in_t.shape, const, pipeline_mode=pl.Buffered(1)), full(w_rope),
                  full(gql), full(gkvl), full(wuq_p), full(wukv_p), full(gq_p), full(gk_p), full(wconv), full(gconv)],
        out_specs=[pl.BlockSpec((tm, MLA_HEADS * HEAD_PAD), row), pl.BlockSpec((tm, MLA_HEADS * HEAD_PAD), row),
                   pl.BlockSpec((tm, MLA_HEADS * HEAD_PAD), row), pl.BlockSpec((tm, d_conv), row)],
        out_shape=[jax.ShapeDtypeStruct((t, MLA_HEADS * HEAD_PAD), BF16), jax.ShapeDtypeStruct((t, MLA_HEADS * HEAD_PAD), BF16),
                   jax.ShapeDtypeStruct((t, MLA_HEADS * HEAD_PAD), BF16), jax.ShapeDtypeStruct((t, d_conv), BF16)],
        scratch_shapes=[pltpu.VMEM((8, d_conv), F32)],
        compiler_params=pltpu.CompilerParams(dimension_semantics=("arbitrary",), vmem_limit_bytes=VMEM_LIMIT),
        name="front",
    )(x2, pos2, freq, gmix, w_in_t, w_rope, gql, gkvl, wuq_p, wukv_p, gq_p, gk_p, wconv, gconv)


def _attn_kernel(q_ref, k_ref, v_ref, o_ref):
    i = pl.program_id(2)
    tq = q_ref.shape[0]
    half = tq // 2
    q_halves = (q_ref[:half, :], q_ref[half:, :])

    def step(start, n_keys, masks, carry):
        kvs = [(k_ref[pl.ds(start, n), :], v_ref[pl.ds(start, n), :]) for n in n_keys]
        scores = [lax.dot_general(qh, kb, (((1,), (1,)), ((), ())), preferred_element_type=F32)
                  for qh, (kb, _) in zip(q_halves, kvs)]
        scores = [s if mask is None else jnp.where(mask, s, MASK_VALUE) for s, mask in zip(scores, masks)]
        m_new = [jnp.maximum(m, jnp.max(s, axis=-1, keepdims=True)) for s, (m, _) in zip(scores, carry)]
        probs = [jnp.exp2(s - mn).astype(BF16) for s, mn in zip(scores, m_new)]
        return tuple((mn, jnp.exp2(m - mn) * acc + jnp.dot(p, vb, preferred_element_type=F32))
                     for p, mn, (m, acc), (_, vb) in zip(probs, m_new, carry, kvs))

    def chunk_mask(n, first_row):
        r = (lax.broadcasted_iota(jnp.int32, (half, n), 0) + first_row) // CHUNK
        c = lax.broadcasted_iota(jnp.int32, (half, n), 1) // CHUNK
        return c <= r

    for qi in range(k_ref.shape[0] // tq):
        @pl.when(i == qi)
        def _(qi=qi):
            carry = tuple((jnp.full((half, 1), MASK_VALUE, F32), jnp.zeros((half, HEAD_PAD), F32)) for _ in range(2))
            for j in range(qi):
                carry = step(j * tq, (tq, tq), (None, None), carry)
            (_, acc0), (_, acc1) = step(qi * tq, (half, tq), (chunk_mask(half, 0), chunk_mask(tq, half)), carry)
            o_ref[:half, :] = acc0[:, :V_HEAD] / acc0[:, V_HEAD:]
            o_ref[half:, :] = acc1[:, :V_HEAD] / acc1[:, V_HEAD:]


def _attention(q, k, v, *, batch, seq):
    tq = TQ_ATTN
    nq = seq // tq
    return pl.pallas_call(
        _attn_kernel,
        grid=(batch, MLA_HEADS, nq),
        in_specs=[pl.BlockSpec((tq, HEAD_PAD), lambda b, h, i: (b * nq + i, h)),
                  pl.BlockSpec((seq, HEAD_PAD), lambda b, h, i: (b, h)),
                  pl.BlockSpec((seq, HEAD_PAD), lambda b, h, i: (b, h))],
        out_specs=pl.BlockSpec((tq, V_HEAD), lambda b, h, i: (b * nq + i, h)),
        out_shape=jax.ShapeDtypeStruct((batch * seq, MLA_HEADS * V_HEAD), F32),
        compiler_params=pltpu.CompilerParams(dimension_semantics=("arbitrary", "arbitrary", "arbitrary"),
                                             vmem_limit_bytes=VMEM_LIMIT),
        name="attn",
    )(q, k, v)


def _outproj_kernel(ya_ref, yc_ref, x_ref, gattn_ref, wout_ref, gmoe_ref, wgr_ref, bgr_ref,
                    h1_ref, xm_ref, route_ref):
    tm, d_attn = ya_ref.shape
    d_model = x_ref.shape[1]
    ya = _rms(ya_ref[...], gattn_ref[...], d_attn).astype(BF16)
    h1 = (x_ref[...]
          + jnp.dot(ya, wout_ref[:d_attn, :], preferred_element_type=F32)
          + jnp.dot(yc_ref[...], wout_ref[d_attn:, :], preferred_element_type=F32))
    h1_ref[...] = h1
    xm = _rms(h1, gmoe_ref[...], d_model)
    _store_slabs(xm_ref, xm, SLAB)

    x_hi = xm.astype(BF16)
    x_lo = (xm - x_hi.astype(F32)).astype(BF16)
    hi_part = jnp.dot(x_hi, wgr_ref[...], preferred_element_type=F32)
    lo_part = jnp.dot(x_lo, wgr_ref[:, :LANES], preferred_element_type=F32)
    logits = hi_part[:, :LANES] + hi_part[:, LANES:] + lo_part + bgr_ref[...]
    lane = lax.broadcasted_iota(jnp.int32, (tm, LANES), 1).astype(F32)
    neg = -jnp.inf
    big = 1e9
    gl = jnp.where(lane < N_GROUPS, logits, neg)
    gmax = jnp.max(gl, axis=-1, keepdims=True)
    gsel = jnp.min(jnp.where(gl == gmax, lane, big), axis=-1, keepdims=True)
    p_g = 1.0 / jnp.sum(jnp.exp(gl - gmax), axis=-1, keepdims=True)
    lo = N_GROUPS + EXPERTS_PER_GROUP * gsel
    el = jnp.where((lane >= lo) & (lane < lo + EXPERTS_PER_GROUP), logits, neg)
    v1 = jnp.max(el, axis=-1, keepdims=True)
    i1 = jnp.min(jnp.where(el == v1, lane, big), axis=-1, keepdims=True)
    el2 = jnp.where(lane == i1, neg, el)
    v2 = jnp.max(el2, axis=-1, keepdims=True)
    i2 = jnp.min(jnp.where(el2 == v2, lane, big), axis=-1, keepdims=True)
    e2 = jnp.exp(v2 - v1)
    den = 1.0 + e2
    w1 = p_g / den
    w2 = p_g * e2 / den
    route_ref[...] = jnp.where(lane == 0, i1 - N_GROUPS,
                               jnp.where(lane == 1, i2 - N_GROUPS,
                                         jnp.where(lane == 2, w1, jnp.where(lane == 3, w2, 0.0))))


def _outproj(ya, yc, x2, gattn, w_out, gmoe, w_gr, b_gr):
    t, d_model = x2.shape
    d_attn = ya.shape[1]
    tm = TM_OUT
    const = lambda i: (0, 0)
    row = lambda i: (i, 0)
    full = lambda a: pl.BlockSpec(a.shape, const)
    return pl.pallas_call(
        _outproj_kernel,
        grid=(t // tm,),
        in_specs=[pl.BlockSpec((tm, d_attn), row), pl.BlockSpec((tm, yc.shape[1]), row), pl.BlockSpec((tm, d_model), row),
                  full(gattn), pl.BlockSpec(w_out.shape, const, pipeline_mode=pl.Buffered(1)), full(gmoe), full(w_gr), full(b_gr)],
        out_specs=[pl.BlockSpec((tm, d_model), row), pl.BlockSpec((tm * SLAB, LANES), row), pl.BlockSpec((tm, LANES), row)],
        out_shape=[jax.ShapeDtypeStruct((t, d_model), F32), jax.ShapeDtypeStruct((t * SLAB, LANES), F32),
                   jax.ShapeDtypeStruct((t, LANES), F32)],
        compiler_params=pltpu.CompilerParams(dimension_semantics=("arbitrary",), vmem_limit_bytes=VMEM_LIMIT),
        name="outproj",
    )(ya, yc, x2, gattn, w_out, gmoe, w_gr, b_gr)


def _row_gather_start(idx_ref, src_hbm, dst_ref, sem, n_rows):
    def body(pair, carry):
        for prio in range(DMA_PRIORITIES):
            r = pair * DMA_PRIORITIES + prio
            src = src_hbm.at[pl.ds(idx_ref[0, 0, r] * SLAB, SLAB), :]
            pltpu.make_async_copy(src, dst_ref.at[pl.ds(r * SLAB_PITCH, SLAB), :], sem).start(priority=prio)
        return carry
    lax.fori_loop(0, n_rows // DMA_PRIORITIES, body, 0, unroll=True)


def _row_gather_wait(src_hbm, dst_ref, sem, n_rows):
    pltpu.make_async_copy(src_hbm.at[pl.ds(0, n_rows * SLAB), :], dst_ref.at[pl.ds(0, n_rows * SLAB), :], sem).wait()


def _moe_kernel(tile_expert_ref, n_active_ref, first_ref, wslot_ref, next_expert_ref, tok_cur_ref, tok_next_ref,
                xm_hbm, w1_hbm, w3_hbm, w2_hbm, y_ref, xbuf_ref, w1_buf, w3_buf, w2_buf, sem_ref, wsem_ref, *, tm):
    i = pl.program_id(0)
    n_active = n_active_ref[0]
    slot = i % 2
    buf_rows = tm * SLAB_PITCH
    wslot = wslot_ref[i]

    def buf(s):
        return xbuf_ref.at[pl.ds(s * buf_rows, buf_rows), :]

    def weight_copies(e, s):
        return (pltpu.make_async_copy(w1_hbm.at[e], w1_buf.at[s], wsem_ref.at[0, s]),
                pltpu.make_async_copy(w3_hbm.at[e], w3_buf.at[s], wsem_ref.at[1, s]),
                pltpu.make_async_copy(w2_hbm.at[e], w2_buf.at[s], wsem_ref.at[2, s]))

    @pl.when(i == 0)
    def _():
        for cp in weight_copies(tile_expert_ref[0], 0):
            cp.start()
        _row_gather_start(tok_cur_ref, xm_hbm, buf(0), sem_ref.at[0], tm)

    active = i < n_active
    new_expert = active & (first_ref[i] == 1)

    @pl.when(new_expert & (next_expert_ref[i] >= 0))
    def _():
        for cp in weight_copies(next_expert_ref[i], 1 - wslot):
            cp.start()

    @pl.when(i + 1 < n_active)
    def _():
        _row_gather_start(tok_next_ref, xm_hbm, buf(1 - slot), sem_ref.at[1 - slot], tm)

    @pl.when(new_expert)
    def _():
        for cp in weight_copies(tile_expert_ref[i], wslot):
            cp.wait()

    @pl.when(active)
    def _():
        _row_gather_wait(xm_hbm, buf(slot), sem_ref.at[slot], tm)
        x = _load_slabs(xbuf_ref, slot * buf_rows, tm, SLAB, SLAB_PITCH).astype(BF16)
        a = jnp.dot(x, w1_buf[wslot].astype(BF16), preferred_element_type=F32)
        g = jnp.dot(x, w3_buf[wslot].astype(BF16), preferred_element_type=F32)
        hid = (a * jax.nn.sigmoid(a) * g).astype(BF16)
        _store_slabs(y_ref, jnp.dot(hid, w2_buf[wslot].astype(BF16), preferred_element_type=F32), SLAB)

    @pl.when(i >= n_active)
    def _():
        y_ref[...] = jnp.zeros_like(y_ref)


def _moe(plan, xm, w1, w3, w2):
    tile_expert, n_active, first, wslot, next_expert, row_token3 = plan
    n_tiles = row_token3.shape[0]
    tm = TM_MOE
    _, d_model, d_exp = w1.shape
    last = n_tiles - 1
    grid_spec = pltpu.PrefetchScalarGridSpec(
        num_scalar_prefetch=5,
        grid=(n_tiles,),
        in_specs=[pl.BlockSpec((1, 1, tm), lambda i, *_: (i, 0, 0), memory_space=pltpu.SMEM),
                  pl.BlockSpec((1, 1, tm), lambda i, *_: (jnp.minimum(i + 1, last), 0, 0), memory_space=pltpu.SMEM),
                  pl.BlockSpec(memory_space=pl.ANY), pl.BlockSpec(memory_space=pl.ANY),
                  pl.BlockSpec(memory_space=pl.ANY), pl.BlockSpec(memory_space=pl.ANY)],
        out_specs=pl.BlockSpec((tm * SLAB, LANES), lambda i, *_: (i, 0)),
        scratch_shapes=[pltpu.VMEM((2 * tm * SLAB_PITCH, LANES), F32),
                        pltpu.VMEM((2, d_model, d_exp), F32), pltpu.VMEM((2, d_model, d_exp), F32),
                        pltpu.VMEM((2, d_exp, d_model), F32),
                        pltpu.SemaphoreType.DMA((2,)), pltpu.SemaphoreType.DMA((3, 2))],
    )
    return pl.pallas_call(
        functools.partial(_moe_kernel, tm=tm),
        grid_spec=grid_spec,
        out_shape=jax.ShapeDtypeStruct((n_tiles * tm * SLAB, LANES), F32),
        compiler_params=pltpu.CompilerParams(dimension_semantics=("arbitrary",), vmem_limit_bytes=VMEM_LIMIT),
        name="moe",
    )(tile_expert, n_active, first, wslot, next_expert, row_token3, row_token3, xm, w1, w3, w2)


def _ple_kernel(dest_cur_ref, dest_next_ref, y_hbm, h1_ref, route_ref, p_ref, wple_ref, gple_ref, wpg_ref, bpg_ref,
                o_ref, ybuf_ref, sem_ref):
    i = pl.program_id(0)
    n = pl.num_programs(0)
    tm, d_model = h1_ref.shape
    slot = i % 2
    buf_rows = TOP_K * tm * SLAB_PITCH

    def buf(s):
        return ybuf_ref.at[pl.ds(s * buf_rows, buf_rows), :]

    @pl.when(i == 0)
    def _():
        _row_gather_start(dest_cur_ref, y_hbm, buf(0), sem_ref.at[0], TOP_K * tm)

    @pl.when(i + 1 < n)
    def _():
        _row_gather_start(dest_next_ref, y_hbm, buf(1 - slot), sem_ref.at[1 - slot], TOP_K * tm)

    _row_gather_wait(y_hbm, buf(slot), sem_ref.at[slot], TOP_K * tm)
    route = route_ref[...]
    y0 = _load_slabs(ybuf_ref, slot * buf_rows, tm, SLAB, SLAB_PITCH)
    y1 = _load_slabs(ybuf_ref, slot * buf_rows + tm * SLAB_PITCH, tm, SLAB, SLAB_PITCH)
    h2 = h1_ref[...] + (route[:, 2:3] * y0 + route[:, 3:4] * y1)
    xn = _rms(h2, gple_ref[...], d_model).astype(BF16)
    gate = jax.nn.sigmoid(jnp.dot(xn, wpg_ref[...], preferred_element_type=F32) + bpg_ref[...])
    pe = jnp.dot(p_ref[...].astype(BF16), wple_ref[...], preferred_element_type=F32)
    o_ref[...] = h2 + gate * pe


def _ple(dest3, y, h1, route, p2, w_ple, gple, w_pg, b_pg):
    t, d_model = h1.shape
    tm = TM_PLE
    n_tiles = t // tm
    last = n_tiles - 1
    const = lambda i: (0, 0)
    row = lambda i: (i, 0)
    full = lambda a: pl.BlockSpec(a.shape, const)
    return pl.pallas_call(
        _ple_kernel,
        grid=(n_tiles,),
        in_specs=[pl.BlockSpec((1, 1, TOP_K * tm), lambda i: (i, 0, 0), memory_space=pltpu.SMEM),
                  pl.BlockSpec((1, 1, TOP_K * tm), lambda i: (jnp.minimum(i + 1, last), 0, 0), memory_space=pltpu.SMEM),
                  pl.BlockSpec(memory_space=pl.ANY),
                  pl.BlockSpec((tm, d_model), row), pl.BlockSpec((tm, LANES), row), pl.BlockSpec((tm, p2.shape[1]), row),
                  full(w_ple), full(gple), pl.BlockSpec(w_pg.shape, const, pipeline_mode=pl.Buffered(1)), full(b_pg)],
        out_specs=pl.BlockSpec((tm, d_model), row),
        out_shape=jax.ShapeDtypeStruct((t, d_model), F32),
        scratch_shapes=[pltpu.VMEM((2 * TOP_K * tm * SLAB_PITCH, LANES), F32), pltpu.SemaphoreType.DMA((2,))],
        compiler_params=pltpu.CompilerParams(dimension_semantics=("arbitrary",), vmem_limit_bytes=VMEM_LIMIT),
        name="ple",
    )(dest3, dest3, y, h1, route, p2, w_ple, gple, w_pg, b_pg)


def _dispatch_plan(route):
    t = route.shape[0]
    tm = TM_MOE
    total = TOP_K * t
    n_tiles = total // tm + N_EXPERTS
    expert = route[:, :TOP_K].astype(jnp.int32).reshape(-1)
    pos = jnp.arange(total, dtype=jnp.int32)
    order = jnp.sort(expert * total + pos) % total
    rank = jnp.sort(order * total + pos) % total
    experts = jnp.arange(N_EXPERTS, dtype=jnp.int32)
    onehot = (expert[:, None] == experts[None, :]).astype(jnp.int32)
    counts = jnp.sum(onehot, axis=0)
    group_start = jnp.cumsum(counts) - counts
    tile_end = jnp.cumsum((counts + tm - 1) // tm)
    tile_start = tile_end - (counts + tm - 1) // tm
    n_active = tile_end[-1]
    shift = tile_start * tm - group_start
    dest = rank + jnp.sum(onehot * shift[None, :], axis=1)
    tiles = jnp.minimum(jnp.arange(n_tiles, dtype=jnp.int32), n_active - 1)
    tile_expert = jnp.sum((tile_end[None, :] <= tiles[:, None]).astype(jnp.int32), axis=1)
    j = jnp.arange(tm, dtype=jnp.int32)
    fill_row = (tile_start * tm + counts)[:, None] + j[None, :]
    fill_row = jnp.where(fill_row < (tile_end * tm)[:, None], fill_row, n_tiles * tm)
    fill_tok = (experts[:, None] * tm + j[None, :]) % t
    keys = jnp.concatenate([dest * t + pos // TOP_K, (fill_row * t + fill_tok).reshape(-1)])
    row_token3 = (jnp.sort(keys) % t).reshape(n_tiles, 1, tm)
    dest3 = dest.reshape(t // TM_PLE, TM_PLE, TOP_K).transpose(0, 2, 1).reshape(t // TM_PLE, 1, TOP_K * TM_PLE)
    prev_expert = jnp.concatenate([jnp.full((1,), -1, jnp.int32), tile_expert[:-1]])
    first = (tile_expert != prev_expert).astype(jnp.int32)
    nonempty = counts > 0
    wslot = (jnp.sum((nonempty[None, :] & (experts[None, :] < tile_expert[:, None])).astype(jnp.int32), axis=1)) % 2
    later = nonempty[None, :] & (experts[None, :] > tile_expert[:, None])
    next_expert = jnp.min(jnp.where(later, experts[None, :], N_EXPERTS), axis=1)
    next_expert = jnp.where(next_expert < N_EXPERTS, next_expert, -1)
    plan = (tile_expert, n_active.reshape(1), first, wslot, next_expert, row_token3)
    return plan, dest3


def _rope_swap_pad(w):
    half = QK_ROPE // 2
    x1 = w[..., QK_NOPE:QK_NOPE + half]
    x2 = w[..., QK_NOPE + half:]
    return jnp.concatenate([w[..., :QK_NOPE], x1, x2, x2, x1], axis=-1)


def kernel(x, p, positions, norm_mix, w_in, g_q_lat, g_kv_lat, w_uq, w_ukv, g_q_head, g_k_head, w_conv, g_out_attn, g_out_conv, w_out, norm_moe, w_group, b_group, w_router, b_router, w1, w3, w2, norm_ple, w_ple, w_ple_gate, b_ple_gate):
    batch, seq, d_model = x.shape
    depth = w_in.shape[0]
    t = batch * seq
    q_rank = g_q_lat.shape[1]
    kv_rank = g_kv_lat.shape[1]
    half = QK_ROPE // 2
    rope_lo = q_rank + kv_rank

    pos2 = positions.reshape(t, 1)
    inv_freq = ROPE_BASE ** (-jnp.arange(0, QK_ROPE, 2, dtype=F32) / QK_ROPE)
    freq = jnp.tile(inv_freq, LANES // half).reshape(1, LANES)

    h = x.reshape(t, d_model)
    for i in range(depth):
        w_in_t = jnp.swapaxes(w_in[i], 0, 1).astype(BF16)
        w_rope = jnp.concatenate([w_in_t[rope_lo:rope_lo + QK_ROPE], w_in_t[rope_lo + half:rope_lo + QK_ROPE],
                                  w_in_t[rope_lo:rope_lo + half]], axis=0)
        wuq_p = _rope_swap_pad(w_uq[i]).reshape(q_rank, MLA_HEADS * HEAD_PAD).astype(BF16)
        wukv_p = jnp.concatenate([w_ukv[i][:, :, :QK_NOPE].reshape(kv_rank, -1),
                                  w_ukv[i][:, :, QK_NOPE:].reshape(kv_rank, -1)], axis=1).astype(BF16)
        gq_p = _rope_swap_pad(g_q_head[i]).reshape(1, HEAD_PAD)
        gk_p = _rope_swap_pad(g_k_head[i]).reshape(1, HEAD_PAD)
        q, k, v, yc = _front(h, pos2, freq, norm_mix[i].reshape(1, -1), w_in_t, w_rope, g_q_lat[i].reshape(1, -1),
                             g_kv_lat[i].reshape(1, -1), wuq_p, wukv_p, gq_p, gk_p, w_conv[i],
                             g_out_conv[i].reshape(1, -1), seq=seq)
        ya = _attention(q, k, v, batch=batch, seq=seq)

        pad = LANES - N_GROUPS - N_EXPERTS
        w_gr32 = jnp.concatenate([w_group[i], w_router[i], jnp.zeros((d_model, pad), F32)], axis=1)
        w_gr_hi = w_gr32.astype(BF16)
        w_gr = jnp.concatenate([w_gr_hi, (w_gr32 - w_gr_hi.astype(F32)).astype(BF16)], axis=1)
        b_gr = jnp.concatenate([b_group[i], b_router[i], jnp.zeros((pad,), F32)]).reshape(1, LANES)
        h1, xm, route = _outproj(ya, yc, h, g_out_attn[i].reshape(1, -1), w_out[i].astype(BF16),
                                 norm_moe[i].reshape(1, -1), w_gr, b_gr)

        plan, dest3 = _dispatch_plan(route)
        y = _moe(plan, xm, w1[i], w3[i], w2[i])
        h = _ple(dest3, y, h1, route, p[i].reshape(t, -1), w_ple[i].astype(BF16), norm_ple[i].reshape(1, -1),
                 w_ple_gate[i].astype(BF16), b_ple_gate[i].reshape(1, -1))
    return h.reshape(batch, seq, d_model)
```

```python
import functools

import jax
import jax.numpy as jnp
from jax import lax
from jax.experimental import pallas as pl
from jax.experimental.pallas import tpu as pltpu

F32 = jnp.float32
BF16 = jnp.bfloat16

CHUNK = 64
EPS = 1e-6
MASK_VALUE = -1e30
ROPE_BASE = 10000.0

MLA_HEADS = 8
QK_NOPE = 128
QK_ROPE = 64
QK_HEAD = QK_NOPE + QK_ROPE
V_HEAD = 128
HEAD_PAD = 256
Q_SCALE = QK_HEAD ** -0.5 * 1.4426950408889634
N_GROUPS = 4
EXPERTS_PER_GROUP = 8
N_EXPERTS = N_GROUPS * EXPERTS_PER_GROUP
TOP_K = 2
LANES = 128
SLAB = 16
SLAB_PITCH = 20
DMA_PRIORITIES = 2

TM_FRONT = 512
TQ_ATTN = 1024
TM_OUT = 512
TM_MOE = 128
TM_PLE = 256
VMEM_LIMIT = 56 * 1024 * 1024


def _rms(x, g, n):
    ms = jnp.sum(x * x, axis=-1, keepdims=True) * (1.0 / n)
    return x * lax.rsqrt(ms + EPS) * g


def _store_slabs(ref, val, pitch, mask=None):
    n, width = val.shape
    for c in range(width // LANES):
        piece = val[:, c * LANES:(c + 1) * LANES]
        rows = pl.ds(c, n, stride=pitch)
        if mask is not None:
            piece = jnp.where(mask, piece, ref[rows, :])
        ref[rows, :] = piece


def _load_slabs(ref, start, n, slab, pitch):
    return jnp.concatenate([ref[pl.ds(start + c, n, stride=pitch), :] for c in range(slab)], axis=1)


def _front_kernel(x_ref, pos_ref, freq_ref, gmix_ref, win_ref, wrope_ref, gql_ref, gkvl_ref,
                  wuq_ref, wukv_ref, gq_ref, gk_ref, wconv_ref, gconv_ref,
                  q_ref, k_ref, v_ref, yc_ref, prev_ref, *, tiles_per_seq, q_rank, kv_rank, d_conv):
    i = pl.program_id(0)
    is_start = (i % tiles_per_seq) == 0
    tm = x_ref.shape[0]
    d_model = x_ref.shape[1]
    rope_lo = q_rank + kv_rank
    nt = (((1,), (1,)), ((), ()))

    xn = _rms(x_ref[...], gmix_ref[...], d_model).astype(BF16)
    lat = lax.dot_general(xn, win_ref[:rope_lo, :], nt, preferred_element_type=F32)
    q_lat = lat[:, :q_rank]
    kv_lat = lat[:, q_rank:]
    kr = lax.dot_general(xn, wrope_ref[...], nt, preferred_element_type=F32)

    qn = _rms(q_lat, gql_ref[...], q_rank).astype(BF16)
    kvn = _rms(kv_lat, gkvl_ref[...], kv_rank).astype(BF16)
    q_all = jnp.dot(qn, wuq_ref[...], preferred_element_type=F32)
    kv_all = jnp.dot(kvn, wukv_ref[...], preferred_element_type=F32)

    lane = lax.broadcasted_iota(jnp.int32, (tm, LANES), 1)
    ang = pos_ref[...].astype(F32) * freq_ref[...]
    cos = jnp.cos(ang)
    sin = jnp.sin(ang)
    rope_c = jnp.where(lane < QK_ROPE, cos, 0.0)
    rope_s = jnp.where(lane < QK_ROPE // 2, -sin, jnp.where(lane < QK_ROPE, sin, 0.0))
    rope_lanes = lane < QK_ROPE

    def rope(t):
        return t * rope_c + pltpu.roll(t, QK_ROPE, 1) * rope_s

    gq = gq_ref[...]
    gk = gk_ref[...]
    inv_n = 1.0 / QK_HEAD
    kr_ssq = jnp.sum(jnp.where(rope_lanes, kr * kr, 0.0), axis=-1, keepdims=True)
    kr_rot = rope(kr * gk[:, QK_NOPE:])
    for h in range(MLA_HEADS):
        qa = q_all[:, h * HEAD_PAD:h * HEAD_PAD + QK_NOPE]
        qb = q_all[:, h * HEAD_PAD + QK_NOPE:(h + 1) * HEAD_PAD]
        ssq = (jnp.sum(qa * qa, axis=-1, keepdims=True)
               + jnp.sum(jnp.where(rope_lanes, qb * qb, 0.0), axis=-1, keepdims=True))
        inv = lax.rsqrt(ssq * inv_n + EPS)
        q_ref[:, h * HEAD_PAD:h * HEAD_PAD + QK_NOPE] = (qa * inv * gq[:, :QK_NOPE] * Q_SCALE).astype(BF16)
        q_ref[:, h * HEAD_PAD + QK_NOPE:(h + 1) * HEAD_PAD] = (rope(qb * inv * gq[:, QK_NOPE:]) * Q_SCALE).astype(BF16)

        ka = kv_all[:, h * QK_NOPE:(h + 1) * QK_NOPE]
        kinv = lax.rsqrt((jnp.sum(ka * ka, axis=-1, keepdims=True) + kr_ssq) * inv_n + EPS)
        k_ref[:, h * HEAD_PAD:h * HEAD_PAD + QK_NOPE] = (ka * kinv * gk[:, :QK_NOPE]).astype(BF16)
        k_ref[:, h * HEAD_PAD + QK_NOPE:(h + 1) * HEAD_PAD] = (kr_rot * kinv).astype(BF16)
        v_lo = MLA_HEADS * QK_NOPE + h * V_HEAD
        v_ref[:, h * HEAD_PAD:h * HEAD_PAD + V_HEAD] = kv_all[:, v_lo:v_lo + V_HEAD].astype(BF16)
        v_ref[:, h * HEAD_PAD + V_HEAD:(h + 1) * HEAD_PAD] = jnp.ones((tm, HEAD_PAD - V_HEAD), BF16)

    cv = lax.dot_general(xn, win_ref[rope_lo + QK_ROPE:, :], nt, preferred_element_type=F32)
    b_gate = cv[:, :d_conv]
    u = cv[:, d_conv:2 * d_conv] * cv[:, 2 * d_conv:]

    @pl.when(is_start)
    def _():
        prev_ref[...] = jnp.zeros_like(prev_ref)

    prev = prev_ref[...]
    hm1 = prev[7:8, :]
    hm2 = prev[6:7, :]
    row = lax.broadcasted_iota(jnp.int32, u.shape, 0)
    u1 = jnp.where(row == 0, hm1, pltpu.roll(u, 1, 0))
    u2 = jnp.where(row == 0, hm2, jnp.where(row == 1, hm1, pltpu.roll(u, 2, 0)))
    prev_ref[...] = u[tm - 8:, :]
    wc = wconv_ref[...]
    y = b_gate * (u2 * wc[0:1, :] + u1 * wc[1:2, :] + u * wc[2:3, :])
    yc_ref[...] = _rms(y, gconv_ref[...], d_conv).astype(BF16)


def _front(x2, pos2, freq, gmix, w_in_t, w_rope, gql, gkvl, wuq_p, wukv_p, gq_p, gk_p, wconv, gconv, *, seq):
    t, d_model = x2.shape
    tm = TM_FRONT
    q_rank = gql.shape[1]
    kv_rank = gkvl.shape[1]
    d_conv = gconv.shape[1]
    const = lambda i: (0, 0)
    row = lambda i: (i, 0)
    full = lambda a: pl.BlockSpec(a.shape, const)
    kern = functools.partial(_front_kernel, tiles_per_seq=seq // tm, q_rank=q_rank, kv_rank=kv_rank, d_conv=d_conv)
    return pl.pallas_call(
        kern,
        grid=(t // tm,),
        in_specs=[pl.BlockSpec((tm, d_model), row), pl.BlockSpec((tm, 1), row), full(freq), full(gmix),
                  pl.BlockSpec(w_in_t.shape, const, pipeline_mode=pl.Buffered(1)), full(w_rope),
                  full(gql), full(gkvl), full(wuq_p), full(wukv_p), full(gq_p), full(gk_p), full(wconv), full(gconv)],
        out_specs=[pl.BlockSpec((tm, MLA_HEADS * HEAD_PAD), row), pl.BlockSpec((tm, MLA_HEADS * HEAD_PAD), row),
                   pl.BlockSpec((tm, MLA_HEADS * HEAD_PAD), row), pl.BlockSpec((tm, d_conv), row)],
        out_shape=[jax.ShapeDtypeStruct((t, MLA_HEADS * HEAD_PAD), BF16), jax.ShapeDtypeStruct((t, MLA_HEADS * HEAD_PAD), BF16),
                   jax.ShapeDtypeStruct((t, MLA_HEADS * HEAD_PAD), BF16), jax.ShapeDtypeStruct((t, d_conv), BF16)],
        scratch_shapes=[pltpu.VMEM((8, d_conv), F32)],
        compiler_params=pltpu.CompilerParams(dimension_semantics=("arbitrary",), vmem_limit_bytes=VMEM_LIMIT),
        name="front",
    )(x2, pos2, freq, gmix, w_in_t, w_rope, gql, gkvl, wuq_p, wukv_p, gq_p, gk_p, wconv, gconv)


def _attn_kernel(q_ref, k_ref, v_ref, o_ref):
    i = pl.program_id(2)
    tq = q_ref.shape[0]
    half = tq // 2
    q_halves = (q_ref[:half, :], q_ref[half:, :])

    def step(start, n_keys, masks, carry):
        kvs = [(k_ref[pl.ds(start, n), :], v_ref[pl.ds(start, n), :]) for n in n_keys]
        scores = [lax.dot_general(qh, kb, (((1,), (1,)), ((), ())), preferred_element_type=F32)
                  for qh, (kb, _) in zip(q_halves, kvs)]
        scores = [s if mask is None else jnp.where(mask, s, MASK_VALUE) for s, mask in zip(scores, masks)]
        m_new = [jnp.maximum(m, jnp.max(s, axis=-1, keepdims=True)) for s, (m, _) in zip(scores, carry)]
        probs = [jnp.exp2(s - mn).astype(BF16) for s, mn in zip(scores, m_new)]
        return tuple((mn, jnp.exp2(m - mn) * acc + jnp.dot(p, vb, preferred_element_type=F32))
                     for p, mn, (m, acc), (_, vb) in zip(probs, m_new, carry, kvs))

    def chunk_mask(n, first_row):
        r = (lax.broadcasted_iota(jnp.int32, (half, n), 0) + first_row) // CHUNK
        c = lax.broadcasted_iota(jnp.int32, (half, n), 1) // CHUNK
        return c <= r

    for qi in range(k_ref.shape[0] // tq):
        @pl.when(i == qi)
        def _(qi=qi):
            carry = tuple((jnp.full((half, 1), MASK_VALUE, F32), jnp.zeros((half, HEAD_PAD), F32)) for _ in range(2))
            for j in range(qi):
                carry = step(j * tq, (tq, tq), (None, None), carry)
            (_, acc0), (_, acc1) = step(qi * tq, (half, tq), (chunk_mask(half, 0), chunk_mask(tq, half)), carry)
            o_ref[:half, :] = acc0[:, :V_HEAD] / acc0[:, V_HEAD:]
            o_ref[half:, :] = acc1[:, :V_HEAD] / acc1[:, V_HEAD:]


def _attention(q, k, v, *, batch, seq):
    tq = TQ_ATTN
    nq = seq // tq
    return pl.pallas_call(
        _attn_kernel,
        grid=(batch, MLA_HEADS, nq),
        in_specs=[pl.BlockSpec((tq, HEAD_PAD), lambda b, h, i: (b * nq + i, h)),
                  pl.BlockSpec((seq, HEAD_PAD), lambda b, h, i: (b, h)),
                  pl.BlockSpec((seq, HEAD_PAD), lambda b, h, i: (b, h))],
        out_specs=pl.BlockSpec((tq, V_HEAD), lambda b, h, i: (b * nq + i, h)),
        out_shape=jax.ShapeDtypeStruct((batch * seq, MLA_HEADS * V_HEAD), F32),
        compiler_params=pltpu.CompilerParams(dimension_semantics=("arbitrary", "arbitrary", "arbitrary"),
                                             vmem_limit_bytes=VMEM_LIMIT),
        name="attn",
    )(q, k, v)


def _outproj_kernel(ya_ref, yc_ref, x_ref, gattn_ref, wout_ref, gmoe_ref, wgr_ref, bgr_ref,
                    h1_ref, xm_ref, route_ref):
    tm, d_attn = ya_ref.shape
    d_model = x_ref.shape[1]
    ya = _rms(ya_ref[...], gattn_ref[...], d_attn).astype(BF16)
    h1 = (x_ref[...]
          + jnp.dot(ya, wout_ref[:d_attn, :], preferred_element_type=F32)
          + jnp.dot(yc_ref[...], wout_ref[d_attn:, :], preferred_element_type=F32))
    h1_ref[...] = h1
    xm = _rms(h1, gmoe_ref[...], d_model)
    _store_slabs(xm_ref, xm, SLAB)

    x_hi = xm.astype(BF16)
    x_lo = (xm - x_hi.astype(F32)).astype(BF16)
    hi_part = jnp.dot(x_hi, wgr_ref[...], preferred_element_type=F32)
    lo_part = jnp.dot(x_lo, wgr_ref[:, :LANES], preferred_element_type=F32)
    logits = hi_part[:, :LANES] + hi_part[:, LANES:] + lo_part + bgr_ref[...]
    lane = lax.broadcasted_iota(jnp.int32, (tm, LANES), 1).astype(F32)
    neg = -jnp.inf
    big = 1e9
    gl = jnp.where(lane < N_GROUPS, logits, neg)
    gmax = jnp.max(gl, axis=-1, keepdims=True)
    gsel = jnp.min(jnp.where(gl == gmax, lane, big), axis=-1, keepdims=True)
    p_g = 1.0 / jnp.sum(jnp.exp(gl - gmax), axis=-1, keepdims=True)
    lo = N_GROUPS + EXPERTS_PER_GROUP * gsel
    el = jnp.where((lane >= lo) & (lane < lo + EXPERTS_PER_GROUP), logits, neg)
    v1 = jnp.max(el, axis=-1, keepdims=True)
    i1 = jnp.min(jnp.where(el == v1, lane, big), axis=-1, keepdims=True)
    el2 = jnp.where(lane == i1, neg, el)
    v2 = jnp.max(el2, axis=-1, keepdims=True)
    i2 = jnp.min(jnp.where(el2 == v2, lane, big), axis=-1, keepdims=True)
    e2 = jnp.exp(v2 - v1)
    den = 1.0 + e2
    w1 = p_g / den
    w2 = p_g * e2 / den
    route_ref[...] = jnp.where(lane == 0, i1 - N_GROUPS,
                               jnp.where(lane == 1, i2 - N_GROUPS,
                                         jnp.where(lane == 2, w1, jnp.where(lane == 3, w2, 0.0))))


def _outproj(ya, yc, x2, gattn, w_out, gmoe, w_gr, b_gr):
    t, d_model = x2.shape
    d_attn = ya.shape[1]
    tm = TM_OUT
    const = lambda i: (0, 0)
    row = lambda i: (i, 0)
    full = lambda a: pl.BlockSpec(a.shape, const)
    return pl.pallas_call(
        _outproj_kernel,
        grid=(t // tm,),
        in_specs=[pl.BlockSpec((tm, d_attn), row), pl.BlockSpec((tm, yc.shape[1]), row), pl.BlockSpec((tm, d_model), row),
                  full(gattn), pl.BlockSpec(w_out.shape, const, pipeline_mode=pl.Buffered(1)), full(gmoe), full(w_gr), full(b_gr)],
        out_specs=[pl.BlockSpec((tm, d_model), row), pl.BlockSpec((tm * SLAB, LANES), row), pl.BlockSpec((tm, LANES), row)],
        out_shape=[jax.ShapeDtypeStruct((t, d_model), F32), jax.ShapeDtypeStruct((t * SLAB, LANES), F32),
                   jax.ShapeDtypeStruct((t, LANES), F32)],
        compiler_params=pltpu.CompilerParams(dimension_semantics=("arbitrary",), vmem_limit_bytes=VMEM_LIMIT),
        name="outproj",
    )(ya, yc, x2, gattn, w_out, gmoe, w_gr, b_gr)


def _row_gather_start(idx_ref, src_hbm, dst_ref, sem, n_rows):
    def body(pair, carry):
        for prio in range(DMA_PRIORITIES):
            r = pair * DMA_PRIORITIES + prio
            src = src_hbm.at[pl.ds(idx_ref[0, 0, r] * SLAB, SLAB), :]
            pltpu.make_async_copy(src, dst_ref.at[pl.ds(r * SLAB_PITCH, SLAB), :], sem).start(priority=prio)
        return carry
    lax.fori_loop(0, n_rows // DMA_PRIORITIES, body, 0, unroll=True)


def _row_gather_wait(src_hbm, dst_ref, sem, n_rows):
    pltpu.make_async_copy(src_hbm.at[pl.ds(0, n_rows * SLAB), :], dst_ref.at[pl.ds(0, n_rows * SLAB), :], sem).wait()


def _moe_kernel(tile_expert_ref, n_active_ref, first_ref, wslot_ref, next_expert_ref, tok_cur_ref, tok_next_ref,
                xm_hbm, w1_hbm, w3_hbm, w2_hbm, y_ref, xbuf_ref, w1_buf, w3_buf, w2_buf, sem_ref, wsem_ref, *, tm):
    i = pl.program_id(0)
    n_active = n_active_ref[0]
    slot = i % 2
    buf_rows = tm * SLAB_PITCH
    wslot = wslot_ref[i]

    def buf(s):
        return xbuf_ref.at[pl.ds(s * buf_rows, buf_rows), :]

    def weight_copies(e, s):
        return (pltpu.make_async_copy(w1_hbm.at[e], w1_buf.at[s], wsem_ref.at[0, s]),
                pltpu.make_async_copy(w3_hbm.at[e], w3_buf.at[s], wsem_ref.at[1, s]),
                pltpu.make_async_copy(w2_hbm.at[e], w2_buf.at[s], wsem_ref.at[2, s]))

    @pl.when(i == 0)
    def _():
        for cp in weight_copies(tile_expert_ref[0], 0):
            cp.start()
        _row_gather_start(tok_cur_ref, xm_hbm, buf(0), sem_ref.at[0], tm)

    active = i < n_active
    new_expert = active & (first_ref[i] == 1)

    @pl.when(new_expert & (next_expert_ref[i] >= 0))
    def _():
        for cp in weight_copies(next_expert_ref[i], 1 - wslot):
            cp.start()

    @pl.when(i + 1 < n_active)
    def _():
        _row_gather_start(tok_next_ref, xm_hbm, buf(1 - slot), sem_ref.at[1 - slot], tm)

    @pl.when(new_expert)
    def _():
        for cp in weight_copies(tile_expert_ref[i], wslot):
            cp.wait()

    @pl.when(active)
    def _():
        _row_gather_wait(xm_hbm, buf(slot), sem_ref.at[slot], tm)
        x = _load_slabs(xbuf_ref, slot * buf_rows, tm, SLAB, SLAB_PITCH).astype(BF16)
        a = jnp.dot(x, w1_buf[wslot].astype(BF16), preferred_element_type=F32)
        g = jnp.dot(x, w3_buf[wslot].astype(BF16), preferred_element_type=F32)
        hid = (a * jax.nn.sigmoid(a) * g).astype(BF16)
        _store_slabs(y_ref, jnp.dot(hid, w2_buf[wslot].astype(BF16), preferred_element_type=F32), SLAB)

    @pl.when(i >= n_active)
    def _():
        y_ref[...] = jnp.zeros_like(y_ref)


def _moe(plan, xm, w1, w3, w2):
    tile_expert, n_active, first, wslot, next_expert, row_token3 = plan
    n_tiles = row_token3.shape[0]
    tm = TM_MOE
    _, d_model, d_exp = w1.shape
    last = n_tiles - 1
    grid_spec = pltpu.PrefetchScalarGridSpec(
        num_scalar_prefetch=5,
        grid=(n_tiles,),
        in_specs=[pl.BlockSpec((1, 1, tm), lambda i, *_: (i, 0, 0), memory_space=pltpu.SMEM),
                  pl.BlockSpec((1, 1, tm), lambda i, *_: (jnp.minimum(i + 1, last), 0, 0), memory_space=pltpu.SMEM),
                  pl.BlockSpec(memory_space=pl.ANY), pl.BlockSpec(memory_space=pl.ANY),
                  pl.BlockSpec(memory_space=pl.ANY), pl.BlockSpec(memory_space=pl.ANY)],
        out_specs=pl.BlockSpec((tm * SLAB, LANES), lambda i, *_: (i, 0)),
        scratch_shapes=[pltpu.VMEM((2 * tm * SLAB_PITCH, LANES), F32),
                        pltpu.VMEM((2, d_model, d_exp), F32), pltpu.VMEM((2, d_model, d_exp), F32),
                        pltpu.VMEM((2, d_exp, d_model), F32),
                        pltpu.SemaphoreType.DMA((2,)), pltpu.SemaphoreType.DMA((3, 2))],
    )
    return pl.pallas_call(
        functools.partial(_moe_kernel, tm=tm),
        grid_spec=grid_spec,
        out_shape=jax.ShapeDtypeStruct((n_tiles * tm * SLAB, LANES), F32),
        compiler_params=pltpu.CompilerParams(dimension_semantics=("arbitrary",), vmem_limit_bytes=VMEM_LIMIT),
        name="moe",
    )(tile_expert, n_active, first, wslot, next_expert, row_token3, row_token3, xm, w1, w3, w2)


def _ple_kernel(dest_cur_ref, dest_next_ref, y_hbm, h1_ref, route_ref, p_ref, wple_ref, gple_ref, wpg_ref, bpg_ref,
                o_ref, ybuf_ref, sem_ref):
    i = pl.program_id(0)
    n = pl.num_programs(0)
    tm, d_model = h1_ref.shape
    slot = i % 2
    buf_rows = TOP_K * tm * SLAB_PITCH

    def buf(s):
        return ybuf_ref.at[pl.ds(s * buf_rows, buf_rows), :]

    @pl.when(i == 0)
    def _():
        _row_gather_start(dest_cur_ref, y_hbm, buf(0), sem_ref.at[0], TOP_K * tm)

    @pl.when(i + 1 < n)
    def _():
        _row_gather_start(dest_next_ref, y_hbm, buf(1 - slot), sem_ref.at[1 - slot], TOP_K * tm)

    _row_gather_wait(y_hbm, buf(slot), sem_ref.at[slot], TOP_K * tm)
    route = route_ref[...]
    y0 = _load_slabs(ybuf_ref, slot * buf_rows, tm, SLAB, SLAB_PITCH)
    y1 = _load_slabs(ybuf_ref, slot * buf_rows + tm * SLAB_PITCH, tm, SLAB, SLAB_PITCH)
    h2 = h1_ref[...] + (route[:, 2:3] * y0 + route[:, 3:4] * y1)
    xn = _rms(h2, gple_ref[...], d_model).astype(BF16)
    gate = jax.nn.sigmoid(jnp.dot(xn, wpg_ref[...], preferred_element_type=F32) + bpg_ref[...])
    pe = jnp.dot(p_ref[...].astype(BF16), wple_ref[...], preferred_element_type=F32)
    o_ref[...] = h2 + gate * pe


def _ple(dest3, y, h1, route, p2, w_ple, gple, w_pg, b_pg):
    t, d_model = h1.shape
    tm = TM_PLE
    n_tiles = t // tm
    last = n_tiles - 1
    const = lambda i: (0, 0)
    row = lambda i: (i, 0)
    full = lambda a: pl.BlockSpec(a.shape, const)
    return pl.pallas_call(
        _ple_kernel,
        grid=(n_tiles,),
        in_specs=[pl.BlockSpec((1, 1, TOP_K * tm), lambda i: (i, 0, 0), memory_space=pltpu.SMEM),
                  pl.BlockSpec((1, 1, TOP_K * tm), lambda i: (jnp.minimum(i + 1, last), 0, 0), memory_space=pltpu.SMEM),
                  pl.BlockSpec(memory_space=pl.ANY),
                  pl.BlockSpec((tm, d_model), row), pl.BlockSpec((tm, LANES), row), pl.BlockSpec((tm, p2.shape[1]), row),
                  full(w_ple), full(gple), pl.BlockSpec(w_pg.shape, const, pipeline_mode=pl.Buffered(1)), full(b_pg)],
        out_specs=pl.BlockSpec((tm, d_model), row),
        out_shape=jax.ShapeDtypeStruct((t, d_model), F32),
        scratch_shapes=[pltpu.VMEM((2 * TOP_K * tm * SLAB_PITCH, LANES), F32), pltpu.SemaphoreType.DMA((2,))],
        compiler_params=pltpu.CompilerParams(dimension_semantics=("arbitrary",), vmem_limit_bytes=VMEM_LIMIT),
        name="ple",
    )(dest3, dest3, y, h1, route, p2, w_ple, gple, w_pg, b_pg)


def _dispatch_plan(route):
    t = route.shape[0]
    tm = TM_MOE
    total = TOP_K * t
    n_tiles = total // tm + N_EXPERTS
    expert = route[:, :TOP_K].astype(jnp.int32).reshape(-1)
    pos = jnp.arange(total, dtype=jnp.int32)
    order = jnp.sort(expert * total + pos) % total
    rank = jnp.sort(order * total + pos) % total
    experts = jnp.arange(N_EXPERTS, dtype=jnp.int32)
    onehot = (expert[:, None] == experts[None, :]).astype(jnp.int32)
    counts = jnp.sum(onehot, axis=0)
    group_start = jnp.cumsum(counts) - counts
    tile_end = jnp.cumsum((counts + tm - 1) // tm)
    tile_start = tile_end - (counts + tm - 1) // tm
    n_active = tile_end[-1]
    shift = tile_start * tm - group_start
    dest = rank + jnp.sum(onehot * shift[None, :], axis=1)
    tiles = jnp.minimum(jnp.arange(n_tiles, dtype=jnp.int32), n_active - 1)
    tile_expert = jnp.sum((tile_end[None, :] <= tiles[:, None]).astype(jnp.int32), axis=1)
    j = jnp.arange(tm, dtype=jnp.int32)
    fill_row = (tile_start * tm + counts)[:, None] + j[None, :]
    fill_row = jnp.where(fill_row < (tile_end * tm)[:, None], fill_row, n_tiles * tm)
    fill_tok = (experts[:, None] * tm + j[None, :]) % t
    keys = jnp.concatenate([dest * t + pos // TOP_K, (fill_row * t + fill_tok).reshape(-1)])
    row_token3 = (jnp.sort(keys) % t).reshape(n_tiles, 1, tm)
    dest3 = dest.reshape(t // TM_PLE, TM_PLE, TOP_K).transpose(0, 2, 1).reshape(t // TM_PLE, 1, TOP_K * TM_PLE)
    prev_expert = jnp.concatenate([jnp.full((1,), -1, jnp.int32), tile_expert[:-1]])
    first = (tile_expert != prev_expert).astype(jnp.int32)
    nonempty = counts > 0
    wslot = (jnp.sum((nonempty[None, :] & (experts[None, :] < tile_expert[:, None])).astype(jnp.int32), axis=1)) % 2
    later = nonempty[None, :] & (experts[None, :] > tile_expert[:, None])
    next_expert = jnp.min(jnp.where(later, experts[None, :], N_EXPERTS), axis=1)
    next_expert = jnp.where(next_expert < N_EXPERTS, next_expert, -1)
    plan = (tile_expert, n_active.reshape(1), first, wslot, next_expert, row_token3)
    return plan, dest3


def _rope_swap_pad(w):
    half = QK_ROPE // 2
    x1 = w[..., QK_NOPE:QK_NOPE + half]
    x2 = w[..., QK_NOPE + half:]
    return jnp.concatenate([w[..., :QK_NOPE], x1, x2, x2, x1], axis=-1)


def kernel(x, p, positions, norm_mix, w_in, g_q_lat, g_kv_lat, w_uq, w_ukv, g_q_head, g_k_head, w_conv, g_out_attn, g_out_conv, w_out, norm_moe, w_group, b_group, w_router, b_router, w1, w3, w2, norm_ple, w_ple, w_ple_gate, b_ple_gate):
    batch, seq, d_model = x.shape
    depth = w_in.shape[0]
    t = batch * seq
    q_rank = g_q_lat.shape[1]
    kv_rank = g_kv_lat.shape[1]
    half = QK_ROPE // 2
    rope_lo = q_rank + kv_rank

    pos2 = positions.reshape(t, 1)
    inv_freq = ROPE_BASE ** (-jnp.arange(0, QK_ROPE, 2, dtype=F32) / QK_ROPE)
    freq = jnp.tile(inv_freq, LANES // half).reshape(1, LANES)

    h = x.reshape(t, d_model)
    for i in range(depth):
        w_in_t = jnp.swapaxes(w_in[i], 0, 1).astype(BF16)
        w_rope = jnp.concatenate([w_in_t[rope_lo:rope_lo + QK_ROPE], w_in_t[rope_lo + half:rope_lo + QK_ROPE],
                                  w_in_t[rope_lo:rope_lo + half]], axis=0)
        wuq_p = _rope_swap_pad(w_uq[i]).reshape(q_rank, MLA_HEADS * HEAD_PAD).astype(BF16)
        wukv_p = jnp.concatenate([w_ukv[i][:, :, :QK_NOPE].reshape(kv_rank, -1),
                                  w_ukv[i][:, :, QK_NOPE:].reshape(kv_rank, -1)], axis=1).astype(BF16)
        gq_p = _rope_swap_pad(g_q_head[i]).reshape(1, HEAD_PAD)
        gk_p = _rope_swap_pad(g_k_head[i]).reshape(1, HEAD_PAD)
        q, k, v, yc = _front(h, pos2, freq, norm_mix[i].reshape(1, -1), w_in_t, w_rope, g_q_lat[i].reshape(1, -1),
                             g_kv_lat[i].reshape(1, -1), wuq_p, wukv_p, gq_p, gk_p, w_conv[i],
                             g_out_conv[i].reshape(1, -1), seq=seq)
        ya = _attention(q, k, v, batch=batch, seq=seq)

        pad = LANES - N_GROUPS - N_EXPERTS
        w_gr32 = jnp.concatenate([w_group[i], w_router[i], jnp.zeros((d_model, pad), F32)], axis=1)
        w_gr_hi = w_gr32.astype(BF16)
        w_gr = jnp.concatenate([w_gr_hi, (w_gr32 - w_gr_hi.astype(F32)).astype(BF16)], axis=1)
        b_gr = jnp.concatenate([b_group[i], b_router[i], jnp.zeros((pad,), F32)]).reshape(1, LANES)
        h1, xm, route = _outproj(ya, yc, h, g_out_attn[i].reshape(1, -1), w_out[i].astype(BF16),
                                 norm_moe[i].reshape(1, -1), w_gr, b_gr)

        plan, dest3 = _dispatch_plan(route)
        y = _moe(plan, xm, w1[i], w3[i], w2[i])
        h = _ple(dest3, y, h1, route, p[i].reshape(t, -1), w_ple[i].astype(BF16), norm_ple[i].reshape(1, -1),
                 w_ple_gate[i].astype(BF16), b_ple_gate[i].reshape(1, -1))
    return h.reshape(batch, seq, d_model)
```

```python
import functools

import jax
import jax.numpy as jnp
from jax import lax
from jax.experimental import pallas as pl
from jax.experimental.pallas import tpu as pltpu

F32 = jnp.float32
BF16 = jnp.bfloat16

CHUNK = 64
EPS = 1e-6
MASK_VALUE = -1e30
ROPE_BASE = 10000.0

MLA_HEADS = 8
QK_NOPE = 128
QK_ROPE = 64
QK_HEAD = QK_NOPE + QK_ROPE
V_HEAD = 128
HEAD_PAD = 256
Q_SCALE = QK_HEAD ** -0.5 * 1.4426950408889634
N_GROUPS = 4
EXPERTS_PER_GROUP = 8
N_EXPERTS = N_GROUPS * EXPERTS_PER_GROUP
TOP_K = 2
LANES = 128
SLAB = 16
SLAB_PITCH = 20
DMA_PRIORITIES = 2

TM_FRONT = 512
TQ_ATTN = 1024
TM_OUT = 512
TM_MOE = 256
TM_PLE = 256
VMEM_LIMIT = 56 * 1024 * 1024


def _rms(x, g, n):
    ms = jnp.sum(x * x, axis=-1, keepdims=True) * (1.0 / n)
    return x * lax.rsqrt(ms + EPS) * g


def _store_slabs(ref, val, pitch, mask=None):
    n, width = val.shape
    for c in range(width // LANES):
        piece = val[:, c * LANES:(c + 1) * LANES]
        rows = pl.ds(c, n, stride=pitch)
        if mask is not None:
            piece = jnp.where(mask, piece, ref[rows, :])
        ref[rows, :] = piece


def _load_slabs(ref, start, n, slab, pitch):
    return jnp.concatenate([ref[pl.ds(start + c, n, stride=pitch), :] for c in range(slab)], axis=1)


def _front_kernel(x_ref, pos_ref, freq_ref, gmix_ref, win_ref, wrope_ref, gql_ref, gkvl_ref,
                  wuq_ref, wukv_ref, gq_ref, gk_ref, wconv_ref, gconv_ref,
                  q_ref, k_ref, v_ref, yc_ref, prev_ref, *, tiles_per_seq, q_rank, kv_rank, d_conv):
    i = pl.program_id(0)
    is_start = (i % tiles_per_seq) == 0
    tm = x_ref.shape[0]
    d_model = x_ref.shape[1]
    rope_lo = q_rank + kv_rank
    nt = (((1,), (1,)), ((), ()))

    xn = _rms(x_ref[...], gmix_ref[...], d_model).astype(BF16)
    lat = lax.dot_general(xn, win_ref[:rope_lo, :], nt, preferred_element_type=F32)
    q_lat = lat[:, :q_rank]
    kv_lat = lat[:, q_rank:]
    kr = lax.dot_general(xn, wrope_ref[...], nt, preferred_element_type=F32)

    qn = _rms(q_lat, gql_ref[...], q_rank).astype(BF16)
    kvn = _rms(kv_lat, gkvl_ref[...], kv_rank).astype(BF16)
    q_all = jnp.dot(qn, wuq_ref[...], preferred_element_type=F32)
    kv_all = jnp.dot(kvn, wukv_ref[...], preferred_element_type=F32)

    lane = lax.broadcasted_iota(jnp.int32, (tm, LANES), 1)
    ang = pos_ref[...].astype(F32) * freq_ref[...]
    cos = jnp.cos(ang)
    sin = jnp.sin(ang)
    rope_c = jnp.where(lane < QK_ROPE, cos, 0.0)
    rope_s = jnp.where(lane < QK_ROPE // 2, -sin, jnp.where(lane < QK_ROPE, sin, 0.0))
    rope_lanes = lane < QK_ROPE

    def rope(t):
        return t * rope_c + pltpu.roll(t, QK_ROPE, 1) * rope_s

    gq = gq_ref[...]
    gk = gk_ref[...]
    inv_n = 1.0 / QK_HEAD
    kr_ssq = jnp.sum(jnp.where(rope_lanes, kr * kr, 0.0), axis=-1, keepdims=True)
    kr_rot = rope(kr * gk[:, QK_NOPE:])
    for h in range(MLA_HEADS):
        qa = q_all[:, h * HEAD_PAD:h * HEAD_PAD + QK_NOPE]
        qb = q_all[:, h * HEAD_PAD + QK_NOPE:(h + 1) * HEAD_PAD]
        ssq = (jnp.sum(qa * qa, axis=-1, keepdims=True)
               + jnp.sum(jnp.where(rope_lanes, qb * qb, 0.0), axis=-1, keepdims=True))
        inv = lax.rsqrt(ssq * inv_n + EPS)
        q_ref[:, h * HEAD_PAD:h * HEAD_PAD + QK_NOPE] = (qa * inv * gq[:, :QK_NOPE] * Q_SCALE).astype(BF16)
        q_ref[:, h * HEAD_PAD + QK_NOPE:(h + 1) * HEAD_PAD] = (rope(qb * inv * gq[:, QK_NOPE:]) * Q_SCALE).astype(BF16)

        ka = kv_all[:, h * QK_NOPE:(h + 1) * QK_NOPE]
        kinv = lax.rsqrt((jnp.sum(ka * ka, axis=-1, keepdims=True) + kr_ssq) * inv_n + EPS)
        k_ref[:, h * HEAD_PAD:h * HEAD_PAD + QK_NOPE] = (ka * kinv * gk[:, :QK_NOPE]).astype(BF16)
        k_ref[:, h * HEAD_PAD + QK_NOPE:(h + 1) * HEAD_PAD] = (kr_rot * kinv).astype(BF16)
        v_lo = MLA_HEADS * QK_NOPE + h * V_HEAD
        v_ref[:, h * HEAD_PAD:h * HEAD_PAD + V_HEAD] = kv_all[:, v_lo:v_lo + V_HEAD].astype(BF16)
        v_ref[:, h * HEAD_PAD + V_HEAD:(h + 1) * HEAD_PAD] = jnp.ones((tm, HEAD_PAD - V_HEAD), BF16)

    cv = lax.dot_general(xn, win_ref[rope_lo + QK_ROPE:, :], nt, preferred_element_type=F32)
    b_gate = cv[:, :d_conv]
    u = cv[:, d_conv:2 * d_conv] * cv[:, 2 * d_conv:]

    @pl.when(is_start)
    def _():
        prev_ref[...] = jnp.zeros_like(prev_ref)

    prev = prev_ref[...]
    hm1 = prev[7:8, :]
    hm2 = prev[6:7, :]
    row = lax.broadcasted_iota(jnp.int32, u.shape, 0)
    u1 = jnp.where(row == 0, hm1, pltpu.roll(u, 1, 0))
    u2 = jnp.where(row == 0, hm2, jnp.where(row == 1, hm1, pltpu.roll(u, 2, 0)))
    prev_ref[...] = u[tm - 8:, :]
    wc = wconv_ref[...]
    y = b_gate * (u2 * wc[0:1, :] + u1 * wc[1:2, :] + u * wc[2:3, :])
    yc_ref[...] = _rms(y, gconv_ref[...], d_conv).astype(BF16)


def _front(x2, pos2, freq, gmix, w_in_t, w_rope, gql, gkvl, wuq_p, wukv_p, gq_p, gk_p, wconv, gconv, *, seq):
    t, d_model = x2.shape
    tm = TM_FRONT
    q_rank = gql.shape[1]
    kv_rank = gkvl.shape[1]
    d_conv = gconv.shape[1]
    const = lambda i: (0, 0)
    row = lambda i: (i, 0)
    full = lambda a: pl.BlockSpec(a.shape, const)
    kern = functools.partial(_front_kernel, tiles_per_seq=seq // tm, q_rank=q_rank, kv_rank=kv_rank, d_conv=d_conv)
    return pl.pallas_call(
        kern,
        grid=(t // tm,),
        in_specs=[pl.BlockSpec((tm, d_model), row), pl.BlockSpec((tm, 1), row), full(freq), full(gmix),
                  pl.BlockSpec(w_in_t.shape, const, pipeline_mode=pl.Buffered(1)), full(w_rope),
                  full(gql), full(gkvl), full(wuq_p), full(wukv_p), full(gq_p), full(gk_p), full(wconv), full(gconv)],
        out_specs=[pl.BlockSpec((tm, MLA_HEADS * HEAD_PAD), row), pl.BlockSpec((tm, MLA_HEADS * HEAD_PAD), row),
                   pl.BlockSpec((tm, MLA_HEADS * HEAD_PAD), row), pl.BlockSpec((tm, d_conv), row)],
        out_shape=[jax.ShapeDtypeStruct((t, MLA_HEADS * HEAD_PAD), BF16), jax.ShapeDtypeStruct((t, MLA_HEADS * HEAD_PAD), BF16),
                   jax.ShapeDtypeStruct((t, MLA_HEADS * HEAD_PAD), BF16), jax.ShapeDtypeStruct((t, d_conv), BF16)],
        scratch_shapes=[pltpu.VMEM((8, d_conv), F32)],
        compiler_params=pltpu.CompilerParams(dimension_semantics=("arbitrary",), vmem_limit_bytes=VMEM_LIMIT),
        name="front",
    )(x2, pos2, freq, gmix, w_in_t, w_rope, gql, gkvl, wuq_p, wukv_p, gq_p, gk_p, wconv, gconv)


def _attn_kernel(q_ref, k_ref, v_ref, o_ref):
    i = pl.program_id(2)
    tq = q_ref.shape[0]
    half = tq // 2
    q_halves = (q_ref[:half, :], q_ref[half:, :])

    def step(start, n_keys, masks, carry):
        kvs = [(k_ref[pl.ds(start, n), :], v_ref[pl.ds(start, n), :]) for n in n_keys]
        scores = [lax.dot_general(qh, kb, (((1,), (1,)), ((), ())), preferred_element_type=F32)
                  for qh, (kb, _) in zip(q_halves, kvs)]
        scores = [s if mask is None else jnp.where(mask, s, MASK_VALUE) for s, mask in zip(scores, masks)]
        m_new = [jnp.maximum(m, jnp.max(s, axis=-1, keepdims=True)) for s, (m, _) in zip(scores, carry)]
        probs = [jnp.exp2(s - mn).astype(BF16) for s, mn in zip(scores, m_new)]
        return tuple((mn, jnp.exp2(m - mn) * acc + jnp.dot(p, vb, preferred_element_type=F32))
                     for p, mn, (m, acc), (_, vb) in zip(probs, m_new, carry, kvs))

    def chunk_mask(n, first_row):
        r = (lax.broadcasted_iota(jnp.int32, (half, n), 0) + first_row) // CHUNK
        c = lax.broadcasted_iota(jnp.int32, (half, n), 1) // CHUNK
        return c <= r

    for qi in range(k_ref.shape[0] // tq):
        @pl.when(i == qi)
        def _(qi=qi):
            carry = tuple((jnp.full((half, 1), MASK_VALUE, F32), jnp.zeros((half, HEAD_PAD), F32)) for _ in range(2))
            for j in range(qi):
                carry = step(j * tq, (tq, tq), (None, None), carry)
            (_, acc0), (_, acc1) = step(qi * tq, (half, tq), (chunk_mask(half, 0), chunk_mask(tq, half)), carry)
            o_ref[:half, :] = acc0[:, :V_HEAD] / acc0[:, V_HEAD:]
            o_ref[half:, :] = acc1[:, :V_HEAD] / acc1[:, V_HEAD:]


def _attention(q, k, v, *, batch, seq):
    tq = TQ_ATTN
    nq = seq // tq
    return pl.pallas_call(
        _attn_kernel,
        grid=(batch, MLA_HEADS, nq),
        in_specs=[pl.BlockSpec((tq, HEAD_PAD), lambda b, h, i: (b * nq + i, h)),
                  pl.BlockSpec((seq, HEAD_PAD), lambda b, h, i: (b, h)),
                  pl.BlockSpec((seq, HEAD_PAD), lambda b, h, i: (b, h))],
        out_specs=pl.BlockSpec((tq, V_HEAD), lambda b, h, i: (b * nq + i, h)),
        out_shape=jax.ShapeDtypeStruct((batch * seq, MLA_HEADS * V_HEAD), F32),
        compiler_params=pltpu.CompilerParams(dimension_semantics=("arbitrary", "arbitrary", "arbitrary"),
                                             vmem_limit_bytes=VMEM_LIMIT),
        name="attn",
    )(q, k, v)


def _outproj_kernel(ya_ref, yc_ref, x_ref, gattn_ref, wout_ref, gmoe_ref, wgr_ref, bgr_ref,
                    h1_ref, xm_ref, route_ref):
    tm, d_attn = ya_ref.shape
    d_model = x_ref.shape[1]
    ya = _rms(ya_ref[...], gattn_ref[...], d_attn).astype(BF16)
    h1 = (x_ref[...]
          + jnp.dot(ya, wout_ref[:d_attn, :], preferred_element_type=F32)
          + jnp.dot(yc_ref[...], wout_ref[d_attn:, :], preferred_element_type=F32))
    h1_ref[...] = h1
    xm = _rms(h1, gmoe_ref[...], d_model)
    _store_slabs(xm_ref, xm, SLAB_PITCH)
    for c in range(SLAB, SLAB_PITCH):
        xm_ref[pl.ds(c, tm, stride=SLAB_PITCH), :] = jnp.zeros((tm, LANES), F32)

    x_hi = xm.astype(BF16)
    x_lo = (xm - x_hi.astype(F32)).astype(BF16)
    hi_part = jnp.dot(x_hi, wgr_ref[...], preferred_element_type=F32)
    lo_part = jnp.dot(x_lo, wgr_ref[:, :LANES], preferred_element_type=F32)
    logits = hi_part[:, :LANES] + hi_part[:, LANES:] + lo_part + bgr_ref[...]
    lane = lax.broadcasted_iota(jnp.int32, (tm, LANES), 1).astype(F32)
    neg = -jnp.inf
    big = 1e9
    gl = jnp.where(lane < N_GROUPS, logits, neg)
    gmax = jnp.max(gl, axis=-1, keepdims=True)
    gsel = jnp.min(jnp.where(gl == gmax, lane, big), axis=-1, keepdims=True)
    p_g = 1.0 / jnp.sum(jnp.exp(gl - gmax), axis=-1, keepdims=True)
    lo = N_GROUPS + EXPERTS_PER_GROUP * gsel
    el = jnp.where((lane >= lo) & (lane < lo + EXPERTS_PER_GROUP), logits, neg)
    v1 = jnp.max(el, axis=-1, keepdims=True)
    i1 = jnp.min(jnp.where(el == v1, lane, big), axis=-1, keepdims=True)
    el2 = jnp.where(lane == i1, neg, el)
    v2 = jnp.max(el2, axis=-1, keepdims=True)
    i2 = jnp.min(jnp.where(el2 == v2, lane, big), axis=-1, keepdims=True)
    e2 = jnp.exp(v2 - v1)
    den = 1.0 + e2
    w1 = p_g / den
    w2 = p_g * e2 / den
    route_ref[...] = jnp.where(lane == 0, i1 - N_GROUPS,
                               jnp.where(lane == 1, i2 - N_GROUPS,
                                         jnp.where(lane == 2, w1, jnp.where(lane == 3, w2, 0.0))))


def _outproj(ya, yc, x2, gattn, w_out, gmoe, w_gr, b_gr):
    t, d_model = x2.shape
    d_attn = ya.shape[1]
    tm = TM_OUT
    const = lambda i: (0, 0)
    row = lambda i: (i, 0)
    full = lambda a: pl.BlockSpec(a.shape, const)
    return pl.pallas_call(
        _outproj_kernel,
        grid=(t // tm,),
        in_specs=[pl.BlockSpec((tm, d_attn), row), pl.BlockSpec((tm, yc.shape[1]), row), pl.BlockSpec((tm, d_model), row),
                  full(gattn), pl.BlockSpec(w_out.shape, const, pipeline_mode=pl.Buffered(1)), full(gmoe), full(w_gr), full(b_gr)],
        out_specs=[pl.BlockSpec((tm, d_model), row), pl.BlockSpec((tm * SLAB_PITCH, LANES), row), pl.BlockSpec((tm, LANES), row)],
        out_shape=[jax.ShapeDtypeStruct((t, d_model), F32), jax.ShapeDtypeStruct((t * SLAB_PITCH, LANES), F32),
                   jax.ShapeDtypeStruct((t, LANES), F32)],
        compiler_params=pltpu.CompilerParams(dimension_semantics=("arbitrary",), vmem_limit_bytes=VMEM_LIMIT),
        name="outproj",
    )(ya, yc, x2, gattn, w_out, gmoe, w_gr, b_gr)


def _row_gather_start(idx_ref, src_hbm, dst_ref, sem, n_rows, src_pitch=SLAB):
    def body(pair, carry):
        for prio in range(DMA_PRIORITIES):
            r = pair * DMA_PRIORITIES + prio
            src = src_hbm.at[pl.ds(idx_ref[0, 0, r] * src_pitch, SLAB), :]
            pltpu.make_async_copy(src, dst_ref.at[pl.ds(r * SLAB_PITCH, SLAB), :], sem).start(priority=prio)
        return carry
    lax.fori_loop(0, n_rows // DMA_PRIORITIES, body, 0, unroll=True)


def _row_gather_wait(src_hbm, dst_ref, sem, n_rows):
    pltpu.make_async_copy(src_hbm.at[pl.ds(0, n_rows * SLAB), :], dst_ref.at[pl.ds(0, n_rows * SLAB), :], sem).wait()


def _moe_kernel(tile_expert_ref, n_active_ref, first_ref, wslot_ref, next_expert_ref, tok_cur_ref, tok_next_ref,
                xm_hbm, w1_hbm, w3_hbm, w2_hbm, y_ref, xbuf_ref, w1_buf, w3_buf, w2_buf, sem_ref, wsem_ref, *, tm):
    i = pl.program_id(0)
    n_active = n_active_ref[0]
    slot = i % 2
    buf_rows = tm * SLAB_PITCH
    wslot = wslot_ref[i]

    def buf(s):
        return xbuf_ref.at[pl.ds(s * buf_rows, buf_rows), :]

    def weight_copies(e, s):
        return (pltpu.make_async_copy(w1_hbm.at[e], w1_buf.at[s], wsem_ref.at[0, s]),
                pltpu.make_async_copy(w3_hbm.at[e], w3_buf.at[s], wsem_ref.at[1, s]),
                pltpu.make_async_copy(w2_hbm.at[e], w2_buf.at[s], wsem_ref.at[2, s]))

    @pl.when(i == 0)
    def _():
        for cp in weight_copies(tile_expert_ref[0], 0):
            cp.start()
        _row_gather_start(tok_cur_ref, xm_hbm, buf(0), sem_ref.at[0], tm, src_pitch=SLAB_PITCH)

    active = i < n_active
    new_expert = active & (first_ref[i] == 1)

    @pl.when(new_expert & (next_expert_ref[i] >= 0))
    def _():
        for cp in weight_copies(next_expert_ref[i], 1 - wslot):
            cp.start()

    @pl.when(i + 1 < n_active)
    def _():
        _row_gather_start(tok_next_ref, xm_hbm, buf(1 - slot), sem_ref.at[1 - slot], tm, src_pitch=SLAB_PITCH)

    @pl.when(new_expert)
    def _():
        for cp in weight_copies(tile_expert_ref[i], wslot):
            cp.wait()

    @pl.when(active)
    def _():
        _row_gather_wait(xm_hbm, buf(slot), sem_ref.at[slot], tm)
        x = _load_slabs(xbuf_ref, slot * buf_rows, tm, SLAB, SLAB_PITCH).astype(BF16)
        a = jnp.dot(x, w1_buf[wslot].astype(BF16), preferred_element_type=F32)
        g = jnp.dot(x, w3_buf[wslot].astype(BF16), preferred_element_type=F32)
        hid = (a * jax.nn.sigmoid(a) * g).astype(BF16)
        _store_slabs(y_ref, jnp.dot(hid, w2_buf[wslot].astype(BF16), preferred_element_type=F32), SLAB)

    @pl.when(i >= n_active)
    def _():
        y_ref[...] = jnp.zeros_like(y_ref)


def _moe(plan, xm, w1, w3, w2):
    tile_expert, n_active, first, wslot, next_expert, row_token3 = plan
    n_tiles = row_token3.shape[0]
    tm = TM_MOE
    _, d_model, d_exp = w1.shape
    last = n_tiles - 1
    grid_spec = pltpu.PrefetchScalarGridSpec(
        num_scalar_prefetch=5,
        grid=(n_tiles,),
        in_specs=[pl.BlockSpec((1, 1, tm), lambda i, *_: (i, 0, 0), memory_space=pltpu.SMEM),
                  pl.BlockSpec((1, 1, tm), lambda i, *_: (jnp.minimum(i + 1, last), 0, 0), memory_space=pltpu.SMEM),
                  pl.BlockSpec(memory_space=pl.ANY), pl.BlockSpec(memory_space=pl.ANY),
                  pl.BlockSpec(memory_space=pl.ANY), pl.BlockSpec(memory_space=pl.ANY)],
        out_specs=pl.BlockSpec((tm * SLAB, LANES), lambda i, *_: (i, 0)),
        scratch_shapes=[pltpu.VMEM((2 * tm * SLAB_PITCH, LANES), F32),
                        pltpu.VMEM((2, d_model, d_exp), F32), pltpu.VMEM((2, d_model, d_exp), F32),
                        pltpu.VMEM((2, d_exp, d_model), F32),
                        pltpu.SemaphoreType.DMA((2,)), pltpu.SemaphoreType.DMA((3, 2))],
    )
    return pl.pallas_call(
        functools.partial(_moe_kernel, tm=tm),
        grid_spec=grid_spec,
        out_shape=jax.ShapeDtypeStruct((n_tiles * tm * SLAB, LANES), F32),
        compiler_params=pltpu.CompilerParams(dimension_semantics=("arbitrary",), vmem_limit_bytes=VMEM_LIMIT),
        name="moe",
    )(tile_expert, n_active, first, wslot, next_expert, row_token3, row_token3, xm, w1, w3, w2)


def _ple_kernel(dest_cur_ref, dest_next_ref, y_hbm, h1_ref, route_ref, p_ref, wple_ref, gple_ref, wpg_ref, bpg_ref,
                o_ref, ybuf_ref, sem_ref):
    i = pl.program_id(0)
    n = pl.num_programs(0)
    tm, d_model = h1_ref.shape
    slot = i % 2
    buf_rows = TOP_K * tm * SLAB_PITCH

    def buf(s):
        return ybuf_ref.at[pl.ds(s * buf_rows, buf_rows), :]

    @pl.when(i == 0)
    def _():
        _row_gather_start(dest_cur_ref, y_hbm, buf(0), sem_ref.at[0], TOP_K * tm)

    @pl.when(i + 1 < n)
    def _():
        _row_gather_start(dest_next_ref, y_hbm, buf(1 - slot), sem_ref.at[1 - slot], TOP_K * tm)

    _row_gather_wait(y_hbm, buf(slot), sem_ref.at[slot], TOP_K * tm)
    route = route_ref[...]
    y0 = _load_slabs(ybuf_ref, slot * buf_rows, tm, SLAB, SLAB_PITCH)
    y1 = _load_slabs(ybuf_ref, slot * buf_rows + tm * SLAB_PITCH, tm, SLAB, SLAB_PITCH)
    h2 = h1_ref[...] + (route[:, 2:3] * y0 + route[:, 3:4] * y1)
    xn = _rms(h2, gple_ref[...], d_model).astype(BF16)
    gate = jax.nn.sigmoid(jnp.dot(xn, wpg_ref[...], preferred_element_type=F32) + bpg_ref[...])
    pe = jnp.dot(p_ref[...].astype(BF16), wple_ref[...], preferred_element_type=F32)
    o_ref[...] = h2 + gate * pe


def _ple(dest3, y, h1, route, p2, w_ple, gple, w_pg, b_pg):
    t, d_model = h1.shape
    tm = TM_PLE
    n_tiles = t // tm
    last = n_tiles - 1
    const = lambda i: (0, 0)
    row = lambda i: (i, 0)
    full = lambda a: pl.BlockSpec(a.shape, const)
    return pl.pallas_call(
        _ple_kernel,
        grid=(n_tiles,),
        in_specs=[pl.BlockSpec((1, 1, TOP_K * tm), lambda i: (i, 0, 0), memory_space=pltpu.SMEM),
                  pl.BlockSpec((1, 1, TOP_K * tm), lambda i: (jnp.minimum(i + 1, last), 0, 0), memory_space=pltpu.SMEM),
                  pl.BlockSpec(memory_space=pl.ANY),
                  pl.BlockSpec((tm, d_model), row), pl.BlockSpec((tm, LANES), row), pl.BlockSpec((tm, p2.shape[1]), row),
                  full(w_ple), full(gple), pl.BlockSpec(w_pg.shape, const, pipeline_mode=pl.Buffered(1)), full(b_pg)],
        out_specs=pl.BlockSpec((tm, d_model), row),
        out_shape=jax.ShapeDtypeStruct((t, d_model), F32),
        scratch_shapes=[pltpu.VMEM((2 * TOP_K * tm * SLAB_PITCH, LANES), F32), pltpu.SemaphoreType.DMA((2,))],
        compiler_params=pltpu.CompilerParams(dimension_semantics=("arbitrary",), vmem_limit_bytes=VMEM_LIMIT),
        name="ple",
    )(dest3, dest3, y, h1, route, p2, w_ple, gple, w_pg, b_pg)


def _dispatch_plan(route):
    t = route.shape[0]
    tm = TM_MOE
    total = TOP_K * t
    n_tiles = total // tm + N_EXPERTS
    expert = route[:, :TOP_K].astype(jnp.int32).reshape(-1)
    pos = jnp.arange(total, dtype=jnp.int32)
    order = jnp.sort(expert * total + pos) % total
    rank = jnp.sort(order * total + pos) % total
    experts = jnp.arange(N_EXPERTS, dtype=jnp.int32)
    onehot = (expert[:, None] == experts[None, :]).astype(jnp.int32)
    counts = jnp.sum(onehot, axis=0)
    group_start = jnp.cumsum(counts) - counts
    tile_end = jnp.cumsum((counts + tm - 1) // tm)
    tile_start = tile_end - (counts + tm - 1) // tm
    n_active = tile_end[-1]
    shift = tile_start * tm - group_start
    dest = rank + jnp.sum(onehot * shift[None, :], axis=1)
    tiles = jnp.minimum(jnp.arange(n_tiles, dtype=jnp.int32), n_active - 1)
    tile_expert = jnp.sum((tile_end[None, :] <= tiles[:, None]).astype(jnp.int32), axis=1)
    j = jnp.arange(tm, dtype=jnp.int32)
    fill_row = (tile_start * tm + counts)[:, None] + j[None, :]
    fill_row = jnp.where(fill_row < (tile_end * tm)[:, None], fill_row, n_tiles * tm)
    fill_tok = (experts[:, None] * tm + j[None, :]) % t
    keys = jnp.concatenate([dest * t + pos // TOP_K, (fill_row * t + fill_tok).reshape(-1)])
    row_token3 = (jnp.sort(keys) % t).reshape(n_tiles, 1, tm)
    dest3 = dest.reshape(t // TM_PLE, TM_PLE, TOP_K).transpose(0, 2, 1).reshape(t // TM_PLE, 1, TOP_K * TM_PLE)
    prev_expert = jnp.concatenate([jnp.full((1,), -1, jnp.int32), tile_expert[:-1]])
    first = (tile_expert != prev_expert).astype(jnp.int32)
    nonempty = counts > 0
    wslot = (jnp.sum((nonempty[None, :] & (experts[None, :] < tile_expert[:, None])).astype(jnp.int32), axis=1)) % 2
    later = nonempty[None, :] & (experts[None, :] > tile_expert[:, None])
    next_expert = jnp.min(jnp.where(later, experts[None, :], N_EXPERTS), axis=1)
    next_expert = jnp.where(next_expert < N_EXPERTS, next_expert, -1)
    plan = (tile_expert, n_active.reshape(1), first, wslot, next_expert, row_token3)
    return plan, dest3


def _rope_swap_pad(w):
    half = QK_ROPE // 2
    x1 = w[..., QK_NOPE:QK_NOPE + half]
    x2 = w[..., QK_NOPE + half:]
    return jnp.concatenate([w[..., :QK_NOPE], x1, x2, x2, x1], axis=-1)


def kernel(x, p, positions, norm_mix, w_in, g_q_lat, g_kv_lat, w_uq, w_ukv, g_q_head, g_k_head, w_conv, g_out_attn, g_out_conv, w_out, norm_moe, w_group, b_group, w_router, b_router, w1, w3, w2, norm_ple, w_ple, w_ple_gate, b_ple_gate):
    batch, seq, d_model = x.shape
    depth = w_in.shape[0]
    t = batch * seq
    q_rank = g_q_lat.shape[1]
    kv_rank = g_kv_lat.shape[1]
    half = QK_ROPE // 2
    rope_lo = q_rank + kv_rank

    pos2 = positions.reshape(t, 1)
    inv_freq = ROPE_BASE ** (-jnp.arange(0, QK_ROPE, 2, dtype=F32) / QK_ROPE)
    freq = jnp.tile(inv_freq, LANES // half).reshape(1, LANES)

    h = x.reshape(t, d_model)
    for i in range(depth):
        w_in_t = jnp.swapaxes(w_in[i], 0, 1).astype(BF16)
        w_rope = jnp.concatenate([w_in_t[rope_lo:rope_lo + QK_ROPE], w_in_t[rope_lo + half:rope_lo + QK_ROPE],
                                  w_in_t[rope_lo:rope_lo + half]], axis=0)
        wuq_p = _rope_swap_pad(w_uq[i]).reshape(q_rank, MLA_HEADS * HEAD_PAD).astype(BF16)
        wukv_p = jnp.concatenate([w_ukv[i][:, :, :QK_NOPE].reshape(kv_rank, -1),
                                  w_ukv[i][:, :, QK_NOPE:].reshape(kv_rank, -1)], axis=1).astype(BF16)
        gq_p = _rope_swap_pad(g_q_head[i]).reshape(1, HEAD_PAD)
        gk_p = _rope_swap_pad(g_k_head[i]).reshape(1, HEAD_PAD)
        q, k, v, yc = _front(h, pos2, freq, norm_mix[i].reshape(1, -1), w_in_t, w_rope, g_q_lat[i].reshape(1, -1),
                             g_kv_lat[i].reshape(1, -1), wuq_p, wukv_p, gq_p, gk_p, w_conv[i],
                             g_out_conv[i].reshape(1, -1), seq=seq)
        ya = _attention(q, k, v, batch=batch, seq=seq)

        pad = LANES - N_GROUPS - N_EXPERTS
        w_gr32 = jnp.concatenate([w_group[i], w_router[i], jnp.zeros((d_model, pad), F32)], axis=1)
        w_gr_hi = w_gr32.astype(BF16)
        w_gr = jnp.concatenate([w_gr_hi, (w_gr32 - w_gr_hi.astype(F32)).astype(BF16)], axis=1)
        b_gr = jnp.concatenate([b_group[i], b_router[i], jnp.zeros((pad,), F32)]).reshape(1, LANES)
        h1, xm, route = _outproj(ya, yc, h, g_out_attn[i].reshape(1, -1), w_out[i].astype(BF16),
                                 norm_moe[i].reshape(1, -1), w_gr, b_gr)

        plan, dest3 = _dispatch_plan(route)
        y = _moe(plan, xm, w1[i], w3[i], w2[i])
        h = _ple(dest3, y, h1, route, p[i].reshape(t, -1), w_ple[i].astype(BF16), norm_ple[i].reshape(1, -1),
                 w_ple_gate[i].astype(BF16), b_ple_gate[i].reshape(1, -1))
    return h.reshape(batch, seq, d_model)
```

```python
import functools

import jax
import jax.numpy as jnp
from jax import lax
from jax.experimental import pallas as pl
from jax.experimental.pallas import tpu as pltpu

F32 = jnp.float32
BF16 = jnp.bfloat16

CHUNK = 64
EPS = 1e-6
MASK_VALUE = -1e30
ROPE_BASE = 10000.0

MLA_HEADS = 8
QK_NOPE = 128
QK_ROPE = 64
QK_HEAD = QK_NOPE + QK_ROPE
V_HEAD = 128
HEAD_PAD = 256
Q_SCALE = QK_HEAD ** -0.5 * 1.4426950408889634
N_GROUPS = 4
EXPERTS_PER_GROUP = 8
N_EXPERTS = N_GROUPS * EXPERTS_PER_GROUP
TOP_K = 2
LANES = 128
SLAB = 16
SLAB_PITCH = 20
DMA_PRIORITIES = 2

TM_FRONT = 512
TQ_ATTN = 1024
TM_OUT = 512
TM_MOE = 256
TM_PLE = 256
VMEM_LIMIT = 56 * 1024 * 1024


def _rms(x, g, n):
    ms = jnp.sum(x * x, axis=-1, keepdims=True) * (1.0 / n)
    return x * lax.rsqrt(ms + EPS) * g


def _store_slabs(ref, val, pitch, mask=None):
    n, width = val.shape
    for c in range(width // LANES):
        piece = val[:, c * LANES:(c + 1) * LANES]
        rows = pl.ds(c, n, stride=pitch)
        if mask is not None:
            piece = jnp.where(mask, piece, ref[rows, :])
        ref[rows, :] = piece


def _load_slabs(ref, start, n, slab, pitch):
    return jnp.concatenate([ref[pl.ds(start + c, n, stride=pitch), :] for c in range(slab)], axis=1)


def _front_kernel(x_ref, pos_ref, freq_ref, gmix_ref, win_ref, wrope_ref, gql_ref, gkvl_ref,
                  wuq_ref, wukv_ref, gq_ref, gk_ref, wconv_ref, gconv_ref,
                  q_ref, k_ref, v_ref, yc_ref, prev_ref, *, tiles_per_seq, q_rank, kv_rank, d_conv):
    i = pl.program_id(0)
    is_start = (i % tiles_per_seq) == 0
    tm = x_ref.shape[0]
    d_model = x_ref.shape[1]
    rope_lo = q_rank + kv_rank
    nt = (((1,), (1,)), ((), ()))

    xn = _rms(x_ref[...], gmix_ref[...], d_model).astype(BF16)
    lat = lax.dot_general(xn, win_ref[:rope_lo, :], nt, preferred_element_type=F32)
    q_lat = lat[:, :q_rank]
    kv_lat = lat[:, q_rank:]
    kr = lax.dot_general(xn, wrope_ref[...], nt, preferred_element_type=F32)

    qn = _rms(q_lat, gql_ref[...], q_rank).astype(BF16)
    kvn = _rms(kv_lat, gkvl_ref[...], kv_rank).astype(BF16)
    q_all = jnp.dot(qn, wuq_ref[...], preferred_element_type=F32)
    kv_all = jnp.dot(kvn, wukv_ref[...], preferred_element_type=F32)

    lane = lax.broadcasted_iota(jnp.int32, (tm, LANES), 1)
    ang = pos_ref[...].astype(F32) * freq_ref[...]
    cos = jnp.cos(ang)
    sin = jnp.sin(ang)
    rope_c = jnp.where(lane < QK_ROPE, cos, 0.0)
    rope_s = jnp.where(lane < QK_ROPE // 2, -sin, jnp.where(lane < QK_ROPE, sin, 0.0))
    rope_lanes = lane < QK_ROPE

    def rope(t):
        return t * rope_c + pltpu.roll(t, QK_ROPE, 1) * rope_s

    gq = gq_ref[...]
    gk = gk_ref[...]
    inv_n = 1.0 / QK_HEAD
    kr_ssq = jnp.sum(jnp.where(rope_lanes, kr * kr, 0.0), axis=-1, keepdims=True)
    kr_rot = rope(kr * gk[:, QK_NOPE:])
    for h in range(MLA_HEADS):
        qa = q_all[:, h * HEAD_PAD:h * HEAD_PAD + QK_NOPE]
        qb = q_all[:, h * HEAD_PAD + QK_NOPE:(h + 1) * HEAD_PAD]
        ssq = (jnp.sum(qa * qa, axis=-1, keepdims=True)
               + jnp.sum(jnp.where(rope_lanes, qb * qb, 0.0), axis=-1, keepdims=True))
        inv = lax.rsqrt(ssq * inv_n + EPS)
        q_ref[:, h * HEAD_PAD:h * HEAD_PAD + QK_NOPE] = (qa * inv * gq[:, :QK_NOPE] * Q_SCALE).astype(BF16)
        q_ref[:, h * HEAD_PAD + QK_NOPE:(h + 1) * HEAD_PAD] = (rope(qb * inv * gq[:, QK_NOPE:]) * Q_SCALE).astype(BF16)

        ka = kv_all[:, h * QK_NOPE:(h + 1) * QK_NOPE]
        kinv = lax.rsqrt((jnp.sum(ka * ka, axis=-1, keepdims=True) + kr_ssq) * inv_n + EPS)
        k_ref[:, h * HEAD_PAD:h * HEAD_PAD + QK_NOPE] = (ka * kinv * gk[:, :QK_NOPE]).astype(BF16)
        k_ref[:, h * HEAD_PAD + QK_NOPE:(h + 1) * HEAD_PAD] = (kr_rot * kinv).astype(BF16)
        v_lo = MLA_HEADS * QK_NOPE + h * V_HEAD
        v_ref[:, h * HEAD_PAD:h * HEAD_PAD + V_HEAD] = kv_all[:, v_lo:v_lo + V_HEAD].astype(BF16)
        v_ref[:, h * HEAD_PAD + V_HEAD:(h + 1) * HEAD_PAD] = jnp.ones((tm, HEAD_PAD - V_HEAD), BF16)

    cv = lax.dot_general(xn, win_ref[rope_lo + QK_ROPE:, :], nt, preferred_element_type=F32)
    b_gate = cv[:, :d_conv]
    u = cv[:, d_conv:2 * d_conv] * cv[:, 2 * d_conv:]

    @pl.when(is_start)
    def _():
        prev_ref[...] = jnp.zeros_like(prev_ref)

    prev = prev_ref[...]
    hm1 = prev[7:8, :]
    hm2 = prev[6:7, :]
    row = lax.broadcasted_iota(jnp.int32, u.shape, 0)
    u1 = jnp.where(row == 0, hm1, pltpu.roll(u, 1, 0))
    u2 = jnp.where(row == 0, hm2, jnp.where(row == 1, hm1, pltpu.roll(u, 2, 0)))
    prev_ref[...] = u[tm - 8:, :]
    wc = wconv_ref[...]
    y = b_gate * (u2 * wc[0:1, :] + u1 * wc[1:2, :] + u * wc[2:3, :])
    yc_ref[...] = _rms(y, gconv_ref[...], d_conv).astype(BF16)


def _front(x2, pos2, freq, gmix, w_in_t, w_rope, gql, gkvl, wuq_p, wukv_p, gq_p, gk_p, wconv, gconv, *, seq):
    t, d_model = x2.shape
    tm = TM_FRONT
    q_rank = gql.shape[1]
    kv_rank = gkvl.shape[1]
    d_conv = gconv.shape[1]
    const = lambda i: (0, 0)
    row = lambda i: (i, 0)
    full = lambda a: pl.BlockSpec(a.shape, const)
    kern = functools.partial(_front_kernel, tiles_per_seq=seq // tm, q_rank=q_rank, kv_rank=kv_rank, d_conv=d_conv)
    return pl.pallas_call(
        kern,
        grid=(t // tm,),
        in_specs=[pl.BlockSpec((tm, d_model), row), pl.BlockSpec((tm, 1), row), full(freq), full(gmix),
                  pl.BlockSpec(w_in_t.shape, const, pipeline_mode=pl.Buffered(1)), full(w_rope),
                  full(gql), full(gkvl), full(wuq_p), full(wukv_p), full(gq_p), full(gk_p), full(wconv), full(gconv)],
        out_specs=[pl.BlockSpec((tm, MLA_HEADS * HEAD_PAD), row), pl.BlockSpec((tm, MLA_HEADS * HEAD_PAD), row),
                   pl.BlockSpec((tm, MLA_HEADS * HEAD_PAD), row), pl.BlockSpec((tm, d_conv), row)],
        out_shape=[jax.ShapeDtypeStruct((t, MLA_HEADS * HEAD_PAD), BF16), jax.ShapeDtypeStruct((t, MLA_HEADS * HEAD_PAD), BF16),
                   jax.ShapeDtypeStruct((t, MLA_HEADS * HEAD_PAD), BF16), jax.ShapeDtypeStruct((t, d_conv), BF16)],
        scratch_shapes=[pltpu.VMEM((8, d_conv), F32)],
        compiler_params=pltpu.CompilerParams(dimension_semantics=("arbitrary",), vmem_limit_bytes=VMEM_LIMIT),
        name="front",
    )(x2, pos2, freq, gmix, w_in_t, w_rope, gql, gkvl, wuq_p, wukv_p, gq_p, gk_p, wconv, gconv)


def _attn_kernel(q_ref, k_ref, v_ref, o_ref):
    i = pl.program_id(2)
    tq = q_ref.shape[0]
    half = tq // 2
    q_halves = (q_ref[:half, :], q_ref[half:, :])

    def step(start, n_keys, masks, carry):
        kvs = [(k_ref[pl.ds(start, n), :], v_ref[pl.ds(start, n), :]) for n in n_keys]
        scores = [lax.dot_general(qh, kb, (((1,), (1,)), ((), ())), preferred_element_type=F32)
                  for qh, (kb, _) in zip(q_halves, kvs)]
        scores = [s if mask is None else jnp.where(mask, s, MASK_VALUE) for s, mask in zip(scores, masks)]
        m_new = [jnp.maximum(m, jnp.max(s, axis=-1, keepdims=True)) for s, (m, _) in zip(scores, carry)]
        probs = [jnp.exp2(s - mn).astype(BF16) for s, mn in zip(scores, m_new)]
        return tuple((mn, jnp.exp2(m - mn) * acc + jnp.dot(p, vb, preferred_element_type=F32))
                     for p, mn, (m, acc), (_, vb) in zip(probs, m_new, carry, kvs))

    def chunk_mask(n, first_row):
        r = (lax.broadcasted_iota(jnp.int32, (half, n), 0) + first_row) // CHUNK
        c = lax.broadcasted_iota(jnp.int32, (half, n), 1) // CHUNK
        return c <= r

    for qi in range(k_ref.shape[0] // tq):
        @pl.when(i == qi)
        def _(qi=qi):
            carry = tuple((jnp.full((half, 1), MASK_VALUE, F32), jnp.zeros((half, HEAD_PAD), F32)) for _ in range(2))
            for j in range(qi):
                carry = step(j * tq, (tq, tq), (None, None), carry)
            (_, acc0), (_, acc1) = step(qi * tq, (half, tq), (chunk_mask(half, 0), chunk_mask(tq, half)), carry)
            o_ref[:half, :] = acc0[:, :V_HEAD] / acc0[:, V_HEAD:]
            o_ref[half:, :] = acc1[:, :V_HEAD] / acc1[:, V_HEAD:]


def _attention(q, k, v, *, batch, seq):
    tq = TQ_ATTN
    nq = seq // tq
    return pl.pallas_call(
        _attn_kernel,
        grid=(batch, MLA_HEADS, nq),
        in_specs=[pl.BlockSpec((tq, HEAD_PAD), lambda b, h, i: (b * nq + i, h)),
                  pl.BlockSpec((seq, HEAD_PAD), lambda b, h, i: (b, h)),
                  pl.BlockSpec((seq, HEAD_PAD), lambda b, h, i: (b, h))],
        out_specs=pl.BlockSpec((tq, V_HEAD), lambda b, h, i: (b * nq + i, h)),
        out_shape=jax.ShapeDtypeStruct((batch * seq, MLA_HEADS * V_HEAD), F32),
        compiler_params=pltpu.CompilerParams(dimension_semantics=("arbitrary", "arbitrary", "arbitrary"),
                                             vmem_limit_bytes=VMEM_LIMIT),
        name="attn",
    )(q, k, v)


def _outproj_kernel(ya_ref, yc_ref, x_ref, gattn_ref, wout_ref, gmoe_ref, wgr_ref, bgr_ref,
                    h1_ref, xm_ref, route_ref, counts_ref, run_ref, ltri_ref):
    tm, d_attn = ya_ref.shape
    d_model = x_ref.shape[1]
    ya = _rms(ya_ref[...], gattn_ref[...], d_attn).astype(BF16)
    h1 = (x_ref[...]
          + jnp.dot(ya, wout_ref[:d_attn, :], preferred_element_type=F32)
          + jnp.dot(yc_ref[...], wout_ref[d_attn:, :], preferred_element_type=F32))
    h1_ref[...] = h1
    xm = _rms(h1, gmoe_ref[...], d_model)
    _store_slabs(xm_ref, xm, SLAB_PITCH)
    for c in range(SLAB, SLAB_PITCH):
        xm_ref[pl.ds(c, tm, stride=SLAB_PITCH), :] = jnp.zeros((tm, LANES), F32)

    x_hi = xm.astype(BF16)
    x_lo = (xm - x_hi.astype(F32)).astype(BF16)
    hi_part = jnp.dot(x_hi, wgr_ref[...], preferred_element_type=F32)
    lo_part = jnp.dot(x_lo, wgr_ref[:, :LANES], preferred_element_type=F32)
    logits = hi_part[:, :LANES] + hi_part[:, LANES:] + lo_part + bgr_ref[...]
    lane = lax.broadcasted_iota(jnp.int32, (tm, LANES), 1).astype(F32)
    neg = -jnp.inf
    big = 1e9
    gl = jnp.where(lane < N_GROUPS, logits, neg)
    gmax = jnp.max(gl, axis=-1, keepdims=True)
    gsel = jnp.min(jnp.where(gl == gmax, lane, big), axis=-1, keepdims=True)
    p_g = 1.0 / jnp.sum(jnp.exp(gl - gmax), axis=-1, keepdims=True)
    lo = N_GROUPS + EXPERTS_PER_GROUP * gsel
    el = jnp.where((lane >= lo) & (lane < lo + EXPERTS_PER_GROUP), logits, neg)
    v1 = jnp.max(el, axis=-1, keepdims=True)
    i1 = jnp.min(jnp.where(el == v1, lane, big), axis=-1, keepdims=True)
    el2 = jnp.where(lane == i1, neg, el)
    v2 = jnp.max(el2, axis=-1, keepdims=True)
    i2 = jnp.min(jnp.where(el2 == v2, lane, big), axis=-1, keepdims=True)
    e2 = jnp.exp(v2 - v1)
    den = 1.0 + e2
    w1 = p_g / den
    w2 = p_g * e2 / den
    e1 = i1 - N_GROUPS
    e2_id = i2 - N_GROUPS

    @pl.when(pl.program_id(0) == 0)
    def _():
        run_ref[...] = jnp.zeros_like(run_ref)
        earlier = (lax.broadcasted_iota(jnp.int32, (tm, tm), 1) < lax.broadcasted_iota(jnp.int32, (tm, tm), 0))
        ltri_ref[...] = jnp.where(earlier, 1.0, 0.0).astype(BF16)

    oh1 = jnp.where(lane == e1, 1.0, 0.0)
    oh2 = jnp.where(lane == e2_id, 1.0, 0.0)
    before = jnp.dot(ltri_ref[...], jnp.concatenate([oh1, oh2], axis=1).astype(BF16),
                     preferred_element_type=F32)
    run = run_ref[...]
    tot1 = jnp.sum(oh1, axis=0, keepdims=True)
    rank1 = jnp.sum(oh1 * (before[:, :LANES] + run), axis=-1, keepdims=True)
    rank2 = jnp.sum(oh2 * (before[:, LANES:] + (run + tot1)), axis=-1, keepdims=True)
    run = run + tot1 + jnp.sum(oh2, axis=0, keepdims=True)
    run_ref[...] = run
    counts_ref[...] = jnp.broadcast_to(run, counts_ref.shape)

    route_ref[...] = jnp.where(lane == 0, e1, jnp.where(lane == 1, e2_id, jnp.where(lane == 2, w1, jnp.where(
        lane == 3, w2, jnp.where(lane == 4, rank1, jnp.where(lane == 5, rank2, 0.0))))))


def _outproj(ya, yc, x2, gattn, w_out, gmoe, w_gr, b_gr):
    t, d_model = x2.shape
    d_attn = ya.shape[1]
    tm = TM_OUT
    const = lambda i: (0, 0)
    row = lambda i: (i, 0)
    full = lambda a: pl.BlockSpec(a.shape, const)
    return pl.pallas_call(
        _outproj_kernel,
        grid=(t // tm,),
        in_specs=[pl.BlockSpec((tm, d_attn), row), pl.BlockSpec((tm, yc.shape[1]), row), pl.BlockSpec((tm, d_model), row),
                  full(gattn), pl.BlockSpec(w_out.shape, const, pipeline_mode=pl.Buffered(1)), full(gmoe), full(w_gr), full(b_gr)],
        out_specs=[pl.BlockSpec((tm, d_model), row), pl.BlockSpec((tm * SLAB_PITCH, LANES), row), pl.BlockSpec((tm, LANES), row),
                   pl.BlockSpec((8, LANES), const)],
        out_shape=[jax.ShapeDtypeStruct((t, d_model), F32), jax.ShapeDtypeStruct((t * SLAB_PITCH, LANES), F32),
                   jax.ShapeDtypeStruct((t, LANES), F32), jax.ShapeDtypeStruct((8, LANES), F32)],
        scratch_shapes=[pltpu.VMEM((1, LANES), F32), pltpu.VMEM((tm, tm), BF16)],
        compiler_params=pltpu.CompilerParams(dimension_semantics=("arbitrary",), vmem_limit_bytes=VMEM_LIMIT),
        name="outproj",
    )(ya, yc, x2, gattn, w_out, gmoe, w_gr, b_gr)


def _row_gather_start(idx_ref, src_hbm, dst_ref, sem, n_rows, src_pitch=SLAB):
    def body(pair, carry):
        for prio in range(DMA_PRIORITIES):
            r = pair * DMA_PRIORITIES + prio
            src = src_hbm.at[pl.ds(idx_ref[0, 0, r] * src_pitch, SLAB), :]
            pltpu.make_async_copy(src, dst_ref.at[pl.ds(r * SLAB_PITCH, SLAB), :], sem).start(priority=prio)
        return carry
    lax.fori_loop(0, n_rows // DMA_PRIORITIES, body, 0, unroll=True)


def _row_gather_wait(src_hbm, dst_ref, sem, n_rows):
    pltpu.make_async_copy(src_hbm.at[pl.ds(0, n_rows * SLAB), :], dst_ref.at[pl.ds(0, n_rows * SLAB), :], sem).wait()


def _moe_kernel(tile_expert_ref, n_active_ref, first_ref, wslot_ref, next_expert_ref, tok_cur_ref, tok_next_ref,
                xm_hbm, w1_hbm, w3_hbm, w2_hbm, y_ref, xbuf_ref, w1_buf, w3_buf, w2_buf, sem_ref, wsem_ref, *, tm):
    i = pl.program_id(0)
    n_active = n_active_ref[0]
    slot = i % 2
    buf_rows = tm * SLAB_PITCH
    wslot = wslot_ref[i]

    def buf(s):
        return xbuf_ref.at[pl.ds(s * buf_rows, buf_rows), :]

    def weight_copies(e, s):
        return (pltpu.make_async_copy(w1_hbm.at[e], w1_buf.at[s], wsem_ref.at[0, s]),
                pltpu.make_async_copy(w3_hbm.at[e], w3_buf.at[s], wsem_ref.at[1, s]),
                pltpu.make_async_copy(w2_hbm.at[e], w2_buf.at[s], wsem_ref.at[2, s]))

    @pl.when(i == 0)
    def _():
        for cp in weight_copies(tile_expert_ref[0], 0):
            cp.start()
        _row_gather_start(tok_cur_ref, xm_hbm, buf(0), sem_ref.at[0], tm, src_pitch=SLAB_PITCH)

    active = i < n_active
    new_expert = active & (first_ref[i] == 1)

    @pl.when(new_expert & (next_expert_ref[i] >= 0))
    def _():
        for cp in weight_copies(next_expert_ref[i], 1 - wslot):
            cp.start()

    @pl.when(i + 1 < n_active)
    def _():
        _row_gather_start(tok_next_ref, xm_hbm, buf(1 - slot), sem_ref.at[1 - slot], tm, src_pitch=SLAB_PITCH)

    @pl.when(new_expert)
    def _():
        for cp in weight_copies(tile_expert_ref[i], wslot):
            cp.wait()

    @pl.when(active)
    def _():
        _row_gather_wait(xm_hbm, buf(slot), sem_ref.at[slot], tm)
        x = _load_slabs(xbuf_ref, slot * buf_rows, tm, SLAB, SLAB_PITCH).astype(BF16)
        a = jnp.dot(x, w1_buf[wslot].astype(BF16), preferred_element_type=F32)
        g = jnp.dot(x, w3_buf[wslot].astype(BF16), preferred_element_type=F32)
        hid = (a * jax.nn.sigmoid(a) * g).astype(BF16)
        _store_slabs(y_ref, jnp.dot(hid, w2_buf[wslot].astype(BF16), preferred_element_type=F32), SLAB)

    @pl.when(i >= n_active)
    def _():
        y_ref[...] = jnp.zeros_like(y_ref)


def _moe(plan, xm, w1, w3, w2):
    tile_expert, n_active, first, wslot, next_expert, row_token3 = plan
    n_tiles = row_token3.shape[0]
    tm = TM_MOE
    _, d_model, d_exp = w1.shape
    last = n_tiles - 1
    grid_spec = pltpu.PrefetchScalarGridSpec(
        num_scalar_prefetch=5,
        grid=(n_tiles,),
        in_specs=[pl.BlockSpec((1, 1, tm), lambda i, *_: (i, 0, 0), memory_space=pltpu.SMEM),
                  pl.BlockSpec((1, 1, tm), lambda i, *_: (jnp.minimum(i + 1, last), 0, 0), memory_space=pltpu.SMEM),
                  pl.BlockSpec(memory_space=pl.ANY), pl.BlockSpec(memory_space=pl.ANY),
                  pl.BlockSpec(memory_space=pl.ANY), pl.BlockSpec(memory_space=pl.ANY)],
        out_specs=pl.BlockSpec((tm * SLAB, LANES), lambda i, *_: (i, 0)),
        scratch_shapes=[pltpu.VMEM((2 * tm * SLAB_PITCH, LANES), F32),
                        pltpu.VMEM((2, d_model, d_exp), F32), pltpu.VMEM((2, d_model, d_exp), F32),
                        pltpu.VMEM((2, d_exp, d_model), F32),
                        pltpu.SemaphoreType.DMA((2,)), pltpu.SemaphoreType.DMA((3, 2))],
    )
    return pl.pallas_call(
        functools.partial(_moe_kernel, tm=tm),
        grid_spec=grid_spec,
        out_shape=jax.ShapeDtypeStruct((n_tiles * tm * SLAB, LANES), F32),
        compiler_params=pltpu.CompilerParams(dimension_semantics=("arbitrary",), vmem_limit_bytes=VMEM_LIMIT),
        name="moe",
    )(tile_expert, n_active, first, wslot, next_expert, row_token3, row_token3, xm, w1, w3, w2)


def _ple_kernel(dest_cur_ref, dest_next_ref, y_hbm, h1_ref, route_ref, p_ref, wple_ref, gple_ref, wpg_ref, bpg_ref,
                o_ref, ybuf_ref, sem_ref):
    i = pl.program_id(0)
    n = pl.num_programs(0)
    tm, d_model = h1_ref.shape
    slot = i % 2
    buf_rows = TOP_K * tm * SLAB_PITCH

    def buf(s):
        return ybuf_ref.at[pl.ds(s * buf_rows, buf_rows), :]

    @pl.when(i == 0)
    def _():
        _row_gather_start(dest_cur_ref, y_hbm, buf(0), sem_ref.at[0], TOP_K * tm)

    @pl.when(i + 1 < n)
    def _():
        _row_gather_start(dest_next_ref, y_hbm, buf(1 - slot), sem_ref.at[1 - slot], TOP_K * tm)

    _row_gather_wait(y_hbm, buf(slot), sem_ref.at[slot], TOP_K * tm)
    route = route_ref[...]
    y0 = _load_slabs(ybuf_ref, slot * buf_rows, tm, SLAB, SLAB_PITCH)
    y1 = _load_slabs(ybuf_ref, slot * buf_rows + tm * SLAB_PITCH, tm, SLAB, SLAB_PITCH)
    h2 = h1_ref[...] + (route[:, 2:3] * y0 + route[:, 3:4] * y1)
    xn = _rms(h2, gple_ref[...], d_model).astype(BF16)
    gate = jax.nn.sigmoid(jnp.dot(xn, wpg_ref[...], preferred_element_type=F32) + bpg_ref[...])
    pe = jnp.dot(p_ref[...].astype(BF16), wple_ref[...], preferred_element_type=F32)
    o_ref[...] = h2 + gate * pe


def _ple(dest3, y, h1, route, p2, w_ple, gple, w_pg, b_pg):
    t, d_model = h1.shape
    tm = TM_PLE
    n_tiles = t // tm
    last = n_tiles - 1
    const = lambda i: (0, 0)
    row = lambda i: (i, 0)
    full = lambda a: pl.BlockSpec(a.shape, const)
    return pl.pallas_call(
        _ple_kernel,
        grid=(n_tiles,),
        in_specs=[pl.BlockSpec((1, 1, TOP_K * tm), lambda i: (i, 0, 0), memory_space=pltpu.SMEM),
                  pl.BlockSpec((1, 1, TOP_K * tm), lambda i: (jnp.minimum(i + 1, last), 0, 0), memory_space=pltpu.SMEM),
                  pl.BlockSpec(memory_space=pl.ANY),
                  pl.BlockSpec((tm, d_model), row), pl.BlockSpec((tm, LANES), row), pl.BlockSpec((tm, p2.shape[1]), row),
                  full(w_ple), full(gple), pl.BlockSpec(w_pg.shape, const, pipeline_mode=pl.Buffered(1)), full(b_pg)],
        out_specs=pl.BlockSpec((tm, d_model), row),
        out_shape=jax.ShapeDtypeStruct((t, d_model), F32),
        scratch_shapes=[pltpu.VMEM((2 * TOP_K * tm * SLAB_PITCH, LANES), F32), pltpu.SemaphoreType.DMA((2,))],
        compiler_params=pltpu.CompilerParams(dimension_semantics=("arbitrary",), vmem_limit_bytes=VMEM_LIMIT),
        name="ple",
    )(dest3, dest3, y, h1, route, p2, w_ple, gple, w_pg, b_pg)


def _dispatch_plan(route, counts_f):
    t = route.shape[0]
    tm = TM_MOE
    total = TOP_K * t
    n_tiles = total // tm + N_EXPERTS
    expert = route[:, :TOP_K].astype(jnp.int32).reshape(-1)
    rank = route[:, 4:4 + TOP_K].astype(jnp.int32).reshape(-1)
    counts = counts_f[0, :N_EXPERTS].astype(jnp.int32)
    pos = jnp.arange(total, dtype=jnp.int32)
    experts = jnp.arange(N_EXPERTS, dtype=jnp.int32)
    onehot = (expert[:, None] == experts[None, :]).astype(jnp.int32)
    tile_end = jnp.cumsum((counts + tm - 1) // tm)
    tile_start = tile_end - (counts + tm - 1) // tm
    n_active = tile_end[-1]
    dest = rank + jnp.sum(onehot * (tile_start * tm)[None, :], axis=1)
    tiles = jnp.minimum(jnp.arange(n_tiles, dtype=jnp.int32), n_active - 1)
    tile_expert = jnp.sum((tile_end[None, :] <= tiles[:, None]).astype(jnp.int32), axis=1)
    j = jnp.arange(tm, dtype=jnp.int32)
    fill_row = (tile_start * tm + counts)[:, None] + j[None, :]
    fill_row = jnp.where(fill_row < (tile_end * tm)[:, None], fill_row, n_tiles * tm)
    fill_tok = (experts[:, None] * tm + j[None, :]) % t
    keys = jnp.concatenate([dest * t + pos // TOP_K, (fill_row * t + fill_tok).reshape(-1)])
    row_token3 = (jnp.sort(keys) % t).reshape(n_tiles, 1, tm)
    dest3 = dest.reshape(t // TM_PLE, TM_PLE, TOP_K).transpose(0, 2, 1).reshape(t // TM_PLE, 1, TOP_K * TM_PLE)
    prev_expert = jnp.concatenate([jnp.full((1,), -1, jnp.int32), tile_expert[:-1]])
    first = (tile_expert != prev_expert).astype(jnp.int32)
    nonempty = counts > 0
    wslot = (jnp.sum((nonempty[None, :] & (experts[None, :] < tile_expert[:, None])).astype(jnp.int32), axis=1)) % 2
    later = nonempty[None, :] & (experts[None, :] > tile_expert[:, None])
    next_expert = jnp.min(jnp.where(later, experts[None, :], N_EXPERTS), axis=1)
    next_expert = jnp.where(next_expert < N_EXPERTS, next_expert, -1)
    plan = (tile_expert, n_active.reshape(1), first, wslot, next_expert, row_token3)
    return plan, dest3


def _rope_swap_pad(w):
    half = QK_ROPE // 2
    x1 = w[..., QK_NOPE:QK_NOPE + half]
    x2 = w[..., QK_NOPE + half:]
    return jnp.concatenate([w[..., :QK_NOPE], x1, x2, x2, x1], axis=-1)


def kernel(x, p, positions, norm_mix, w_in, g_q_lat, g_kv_lat, w_uq, w_ukv, g_q_head, g_k_head, w_conv, g_out_attn, g_out_conv, w_out, norm_moe, w_group, b_group, w_router, b_router, w1, w3, w2, norm_ple, w_ple, w_ple_gate, b_ple_gate):
    batch, seq, d_model = x.shape
    depth = w_in.shape[0]
    t = batch * seq
    q_rank = g_q_lat.shape[1]
    kv_rank = g_kv_lat.shape[1]
    half = QK_ROPE // 2
    rope_lo = q_rank + kv_rank

    pos2 = positions.reshape(t, 1)
    inv_freq = ROPE_BASE ** (-jnp.arange(0, QK_ROPE, 2, dtype=F32) / QK_ROPE)
    freq = jnp.tile(inv_freq, LANES // half).reshape(1, LANES)

    h = x.reshape(t, d_model)
    for i in range(depth):
        w_in_t = jnp.swapaxes(w_in[i], 0, 1).astype(BF16)
        w_rope = jnp.concatenate([w_in_t[rope_lo:rope_lo + QK_ROPE], w_in_t[rope_lo + half:rope_lo + QK_ROPE],
                                  w_in_t[rope_lo:rope_lo + half]], axis=0)
        wuq_p = _rope_swap_pad(w_uq[i]).reshape(q_rank, MLA_HEADS * HEAD_PAD).astype(BF16)
        wukv_p = jnp.concatenate([w_ukv[i][:, :, :QK_NOPE].reshape(kv_rank, -1),
                                  w_ukv[i][:, :, QK_NOPE:].reshape(kv_rank, -1)], axis=1).astype(BF16)
        gq_p = _rope_swap_pad(g_q_head[i]).reshape(1, HEAD_PAD)
        gk_p = _rope_swap_pad(g_k_head[i]).reshape(1, HEAD_PAD)
        q, k, v, yc = _front(h, pos2, freq, norm_mix[i].reshape(1, -1), w_in_t, w_rope, g_q_lat[i].reshape(1, -1),
                             g_kv_lat[i].reshape(1, -1), wuq_p, wukv_p, gq_p, gk_p, w_conv[i],
                             g_out_conv[i].reshape(1, -1), seq=seq)
        ya = _attention(q, k, v, batch=batch, seq=seq)

        pad = LANES - N_GROUPS - N_EXPERTS
        w_gr32 = jnp.concatenate([w_group[i], w_router[i], jnp.zeros((d_model, pad), F32)], axis=1)
        w_gr_hi = w_gr32.astype(BF16)
        w_gr = jnp.concatenate([w_gr_hi, (w_gr32 - w_gr_hi.astype(F32)).astype(BF16)], axis=1)
        b_gr = jnp.concatenate([b_group[i], b_router[i], jnp.zeros((pad,), F32)]).reshape(1, LANES)
        h1, xm, route, counts = _outproj(ya, yc, h, g_out_attn[i].reshape(1, -1), w_out[i].astype(BF16),
                                 norm_moe[i].reshape(1, -1), w_gr, b_gr)

        plan, dest3 = _dispatch_plan(route, counts)
        y = _moe(plan, xm, w1[i], w3[i], w2[i])
        h = _ple(dest3, y, h1, route, p[i].reshape(t, -1), w_ple[i].astype(BF16), norm_ple[i].reshape(1, -1),
                 w_ple_gate[i].astype(BF16), b_ple_gate[i].reshape(1, -1))
    return h.reshape(batch, seq, d_model)
```

```python
import functools

import jax
import jax.numpy as jnp
from jax import lax
from jax.experimental import pallas as pl
from jax.experimental.pallas import tpu as pltpu

F32 = jnp.float32
BF16 = jnp.bfloat16

CHUNK = 64
EPS = 1e-6
MASK_VALUE = -1e30
ROPE_BASE = 10000.0

MLA_HEADS = 8
QK_NOPE = 128
QK_ROPE = 64
QK_HEAD = QK_NOPE + QK_ROPE
V_HEAD = 128
HEAD_PAD = 256
Q_SCALE = QK_HEAD ** -0.5 * 1.4426950408889634
N_GROUPS = 4
EXPERTS_PER_GROUP = 8
N_EXPERTS = N_GROUPS * EXPERTS_PER_GROUP
TOP_K = 2
LANES = 128
SLAB = 16
SLAB_PITCH = 20
DMA_PRIORITIES = 2

TM_FRONT = 512
TQ_ATTN = 1024
TM_OUT = 512
TM_MOE = 256
TM_PLE = 256
VMEM_LIMIT = 56 * 1024 * 1024


def _rms(x, g, n):
    ms = jnp.sum(x * x, axis=-1, keepdims=True) * (1.0 / n)
    return x * lax.rsqrt(ms + EPS) * g


def _store_slabs(ref, val, pitch, mask=None):
    n, width = val.shape
    for c in range(width // LANES):
        piece = val[:, c * LANES:(c + 1) * LANES]
        rows = pl.ds(c, n, stride=pitch)
        if mask is not None:
            piece = jnp.where(mask, piece, ref[rows, :])
        ref[rows, :] = piece


def _load_slabs(ref, start, n, slab, pitch):
    return jnp.concatenate([ref[pl.ds(start + c, n, stride=pitch), :] for c in range(slab)], axis=1)


def _front_kernel(x_ref, pos_ref, freq_ref, gmix_ref, win_ref, wrope_ref, gql_ref, gkvl_ref,
                  wuq_ref, wukv_ref, gq_ref, gk_ref, wconv_ref, gconv_ref,
                  q_ref, k_ref, v_ref, yc_ref, prev_ref, *, tiles_per_seq, q_rank, kv_rank, d_conv):
    i = pl.program_id(0)
    is_start = (i % tiles_per_seq) == 0
    tm = x_ref.shape[0]
    d_model = x_ref.shape[1]
    rope_lo = q_rank + kv_rank
    nt = (((1,), (1,)), ((), ()))

    @pl.when(is_start)
    def _():
        prev_ref[...] = jnp.zeros_like(prev_ref)

    xn = _rms(x_ref[...], gmix_ref[...], d_model).astype(BF16)
    lat = lax.dot_general(xn, win_ref[:rope_lo, :], nt, preferred_element_type=F32)
    q_lat = lat[:, :q_rank]
    kv_lat = lat[:, q_rank:]
    kr = lax.dot_general(xn, wrope_ref[...], nt, preferred_element_type=F32)

    qn = _rms(q_lat, gql_ref[...], q_rank).astype(BF16)
    kvn = _rms(kv_lat, gkvl_ref[...], kv_rank).astype(BF16)
    q_all = jnp.dot(qn, wuq_ref[...], preferred_element_type=F32)
    kv_all = jnp.dot(kvn, wukv_ref[...], preferred_element_type=F32)

    lane = lax.broadcasted_iota(jnp.int32, (tm, LANES), 1)
    ang = pos_ref[...].astype(F32) * freq_ref[...]
    cos = jnp.cos(ang)
    sin = jnp.sin(ang)
    rope_c = jnp.where(lane < QK_ROPE, cos, 0.0)
    rope_s = jnp.where(lane < QK_ROPE // 2, -sin, jnp.where(lane < QK_ROPE, sin, 0.0))
    rope_lanes = lane < QK_ROPE

    def rope(t):
        return t * rope_c + pltpu.roll(t, QK_ROPE, 1) * rope_s

    gq = gq_ref[...]
    gk = gk_ref[...]
    inv_n = 1.0 / QK_HEAD
    kr_ssq = jnp.sum(jnp.where(rope_lanes, kr * kr, 0.0), axis=-1, keepdims=True)
    kr_rot = rope(kr * gk[:, QK_NOPE:])
    for h in range(MLA_HEADS):
        qa = q_all[:, h * HEAD_PAD:h * HEAD_PAD + QK_NOPE]
        qb = q_all[:, h * HEAD_PAD + QK_NOPE:(h + 1) * HEAD_PAD]
        ssq = (jnp.sum(qa * qa, axis=-1, keepdims=True)
               + jnp.sum(jnp.where(rope_lanes, qb * qb, 0.0), axis=-1, keepdims=True))
        inv = lax.rsqrt(ssq * inv_n + EPS)
        q_ref[:, h * HEAD_PAD:h * HEAD_PAD + QK_NOPE] = (qa * inv * gq[:, :QK_NOPE] * Q_SCALE).astype(BF16)
        q_ref[:, h * HEAD_PAD + QK_NOPE:(h + 1) * HEAD_PAD] = (rope(qb * inv * gq[:, QK_NOPE:]) * Q_SCALE).astype(BF16)

        ka = kv_all[:, h * QK_NOPE:(h + 1) * QK_NOPE]
        kinv = lax.rsqrt((jnp.sum(ka * ka, axis=-1, keepdims=True) + kr_ssq) * inv_n + EPS)
        k_ref[:, h * HEAD_PAD:h * HEAD_PAD + QK_NOPE] = (ka * kinv * gk[:, :QK_NOPE]).astype(BF16)
        k_ref[:, h * HEAD_PAD + QK_NOPE:(h + 1) * HEAD_PAD] = (kr_rot * kinv).astype(BF16)
        v_lo = MLA_HEADS * QK_NOPE + h * V_HEAD
        v_ref[:, h * HEAD_PAD:h * HEAD_PAD + V_HEAD] = kv_all[:, v_lo:v_lo + V_HEAD].astype(BF16)
        v_ref[:, h * HEAD_PAD + V_HEAD:(h + 1) * HEAD_PAD] = jnp.ones((tm, HEAD_PAD - V_HEAD), BF16)

    conv_lo = rope_lo + QK_ROPE
    half_c = d_conv // 2
    row = lax.broadcasted_iota(jnp.int32, (tm, half_c), 0)
    wc = wconv_ref[...]
    ys = []
    ssq = jnp.zeros((tm, 1), F32)
    for cs in (slice(0, half_c), slice(half_c, d_conv)):
        b_gate, c_gate, h_conv = (lax.dot_general(xn, win_ref[conv_lo + k * d_conv + cs.start:conv_lo + k * d_conv + cs.stop, :],
                                                  nt, preferred_element_type=F32) for k in range(3))
        u = c_gate * h_conv
        prev = prev_ref[:, cs]
        hm1 = prev[7:8, :]
        hm2 = prev[6:7, :]
        u1 = jnp.where(row == 0, hm1, pltpu.roll(u, 1, 0))
        u2 = jnp.where(row == 0, hm2, jnp.where(row == 1, hm1, pltpu.roll(u, 2, 0)))
        prev_ref[:, cs] = u[tm - 8:, :]
        y = b_gate * (u2 * wc[0:1, cs] + u1 * wc[1:2, cs] + u * wc[2:3, cs])
        ssq = ssq + jnp.sum(y * y, axis=-1, keepdims=True)
        ys.append(y)
    inv = lax.rsqrt(ssq * (1.0 / d_conv) + EPS)
    yc_ref[...] = (jnp.concatenate(ys, axis=1) * inv * gconv_ref[...]).astype(BF16)


def _front(x2, pos2, freq, gmix, w_in_t, w_rope, gql, gkvl, wuq_p, wukv_p, gq_p, gk_p, wconv, gconv, *, seq):
    t, d_model = x2.shape
    tm = TM_FRONT
    q_rank = gql.shape[1]
    kv_rank = gkvl.shape[1]
    d_conv = gconv.shape[1]
    const = lambda i: (0, 0)
    row = lambda i: (i, 0)
    full = lambda a: pl.BlockSpec(a.shape, const)
    kern = functools.partial(_front_kernel, tiles_per_seq=seq // tm, q_rank=q_rank, kv_rank=kv_rank, d_conv=d_conv)
    return pl.pallas_call(
        kern,
        grid=(t // tm,),
        in_specs=[pl.BlockSpec((tm, d_model), row), pl.BlockSpec((tm, 1), row), full(freq), full(gmix),
                  pl.BlockSpec(w_in_t.shape, const, pipeline_mode=pl.Buffered(1)), full(w_rope),
                  full(gql), full(gkvl), full(wuq_p), full(wukv_p), full(gq_p), full(gk_p), full(wconv), full(gconv)],
        out_specs=[pl.BlockSpec((tm, MLA_HEADS * HEAD_PAD), row), pl.BlockSpec((tm, MLA_HEADS * HEAD_PAD), row),
                   pl.BlockSpec((tm, MLA_HEADS * HEAD_PAD), row), pl.BlockSpec((tm, d_conv), row)],
        out_shape=[jax.ShapeDtypeStruct((t, MLA_HEADS * HEAD_PAD), BF16), jax.ShapeDtypeStruct((t, MLA_HEADS * HEAD_PAD), BF16),
                   jax.ShapeDtypeStruct((t, MLA_HEADS * HEAD_PAD), BF16), jax.ShapeDtypeStruct((t, d_conv), BF16)],
        scratch_shapes=[pltpu.VMEM((8, d_conv), F32)],
        compiler_params=pltpu.CompilerParams(dimension_semantics=("arbitrary",), vmem_limit_bytes=VMEM_LIMIT),
        name="front",
    )(x2, pos2, freq, gmix, w_in_t, w_rope, gql, gkvl, wuq_p, wukv_p, gq_p, gk_p, wconv, gconv)


def _attn_kernel(q_ref, k_ref, v_ref, o_ref):
    i = pl.program_id(2)
    tq = q_ref.shape[0]
    half = tq // 2
    q_halves = (q_ref[:half, :], q_ref[half:, :])

    def step(start, n_keys, masks, carry):
        kvs = [(k_ref[pl.ds(start, n), :], v_ref[pl.ds(start, n), :]) for n in n_keys]
        scores = [lax.dot_general(qh, kb, (((1,), (1,)), ((), ())), preferred_element_type=F32)
                  for qh, (kb, _) in zip(q_halves, kvs)]
        scores = [s if mask is None else jnp.where(mask, s, MASK_VALUE) for s, mask in zip(scores, masks)]
        m_new = [jnp.maximum(m, jnp.max(s, axis=-1, keepdims=True)) for s, (m, _) in zip(scores, carry)]
        probs = [jnp.exp2(s - mn).astype(BF16) for s, mn in zip(scores, m_new)]
        return tuple((mn, jnp.exp2(m - mn) * acc + jnp.dot(p, vb, preferred_element_type=F32))
                     for p, mn, (m, acc), (_, vb) in zip(probs, m_new, carry, kvs))

    def chunk_mask(n, first_row):
        r = (lax.broadcasted_iota(jnp.int32, (half, n), 0) + first_row) // CHUNK
        c = lax.broadcasted_iota(jnp.int32, (half, n), 1) // CHUNK
        return c <= r

    for qi in range(k_ref.shape[0] // tq):
        @pl.when(i == qi)
        def _(qi=qi):
            carry = tuple((jnp.full((half, 1), MASK_VALUE, F32), jnp.zeros((half, HEAD_PAD), F32)) for _ in range(2))
            for j in range(qi):
                carry = step(j * tq, (tq, tq), (None, None), carry)
            (_, acc0), (_, acc1) = step(qi * tq, (half, tq), (chunk_mask(half, 0), chunk_mask(tq, half)), carry)
            o_ref[:half, :] = acc0[:, :V_HEAD] / acc0[:, V_HEAD:]
            o_ref[half:, :] = acc1[:, :V_HEAD] / acc1[:, V_HEAD:]


def _attention(q, k, v, *, batch, seq):
    tq = TQ_ATTN
    nq = seq // tq
    return pl.pallas_call(
        _attn_kernel,
        grid=(batch, MLA_HEADS, nq),
        in_specs=[pl.BlockSpec((tq, HEAD_PAD), lambda b, h, i: (b * nq + i, h)),
                  pl.BlockSpec((seq, HEAD_PAD), lambda b, h, i: (b, h)),
                  pl.BlockSpec((seq, HEAD_PAD), lambda b, h, i: (b, h))],
        out_specs=pl.BlockSpec((tq, V_HEAD), lambda b, h, i: (b * nq + i, h)),
        out_shape=jax.ShapeDtypeStruct((batch * seq, MLA_HEADS * V_HEAD), F32),
        compiler_params=pltpu.CompilerParams(dimension_semantics=("arbitrary", "arbitrary", "arbitrary"),
                                             vmem_limit_bytes=VMEM_LIMIT),
        name="attn",
    )(q, k, v)


def _outproj_kernel(ya_ref, yc_ref, x_ref, gattn_ref, wout_ref, gmoe_ref, wgr_ref, bgr_ref,
                    h1_ref, xm_ref, route_ref):
    tm, d_attn = ya_ref.shape
    d_model = x_ref.shape[1]
    ya = _rms(ya_ref[...], gattn_ref[...], d_attn).astype(BF16)
    h1 = (x_ref[...]
          + jnp.dot(ya, wout_ref[:d_attn, :], preferred_element_type=F32)
          + jnp.dot(yc_ref[...], wout_ref[d_attn:, :], preferred_element_type=F32))
    h1_ref[...] = h1
    xm = _rms(h1, gmoe_ref[...], d_model)
    _store_slabs(xm_ref, xm, SLAB_PITCH)
    for c in range(SLAB, SLAB_PITCH):
        xm_ref[pl.ds(c, tm, stride=SLAB_PITCH), :] = jnp.zeros((tm, LANES), F32)

    x_hi = xm.astype(BF16)
    x_lo = (xm - x_hi.astype(F32)).astype(BF16)
    hi_part = jnp.dot(x_hi, wgr_ref[...], preferred_element_type=F32)
    lo_part = jnp.dot(x_lo, wgr_ref[:, :LANES], preferred_element_type=F32)
    logits = hi_part[:, :LANES] + hi_part[:, LANES:] + lo_part + bgr_ref[...]
    lane = lax.broadcasted_iota(jnp.int32, (tm, LANES), 1).astype(F32)
    neg = -jnp.inf
    big = 1e9
    gl = jnp.where(lane < N_GROUPS, logits, neg)
    gmax = jnp.max(gl, axis=-1, keepdims=True)
    gsel = jnp.min(jnp.where(gl == gmax, lane, big), axis=-1, keepdims=True)
    p_g = 1.0 / jnp.sum(jnp.exp(gl - gmax), axis=-1, keepdims=True)
    lo = N_GROUPS + EXPERTS_PER_GROUP * gsel
    el = jnp.where((lane >= lo) & (lane < lo + EXPERTS_PER_GROUP), logits, neg)
    v1 = jnp.max(el, axis=-1, keepdims=True)
    i1 = jnp.min(jnp.where(el == v1, lane, big), axis=-1, keepdims=True)
    el2 = jnp.where(lane == i1, neg, el)
    v2 = jnp.max(el2, axis=-1, keepdims=True)
    i2 = jnp.min(jnp.where(el2 == v2, lane, big), axis=-1, keepdims=True)
    e2 = jnp.exp(v2 - v1)
    den = 1.0 + e2
    w1 = p_g / den
    w2 = p_g * e2 / den
    route_ref[...] = jnp.where(lane == 0, i1 - N_GROUPS,
                               jnp.where(lane == 1, i2 - N_GROUPS,
                                         jnp.where(lane == 2, w1, jnp.where(lane == 3, w2, 0.0))))


def _outproj(ya, yc, x2, gattn, w_out, gmoe, w_gr, b_gr):
    t, d_model = x2.shape
    d_attn = ya.shape[1]
    tm = TM_OUT
    const = lambda i: (0, 0)
    row = lambda i: (i, 0)
    full = lambda a: pl.BlockSpec(a.shape, const)
    return pl.pallas_call(
        _outproj_kernel,
        grid=(t // tm,),
        in_specs=[pl.BlockSpec((tm, d_attn), row), pl.BlockSpec((tm, yc.shape[1]), row), pl.BlockSpec((tm, d_model), row),
                  full(gattn), pl.BlockSpec(w_out.shape, const, pipeline_mode=pl.Buffered(1)), full(gmoe), full(w_gr), full(b_gr)],
        out_specs=[pl.BlockSpec((tm, d_model), row), pl.BlockSpec((tm * SLAB_PITCH, LANES), row), pl.BlockSpec((tm, LANES), row)],
        out_shape=[jax.ShapeDtypeStruct((t, d_model), F32), jax.ShapeDtypeStruct((t * SLAB_PITCH, LANES), F32),
                   jax.ShapeDtypeStruct((t, LANES), F32)],
        compiler_params=pltpu.CompilerParams(dimension_semantics=("arbitrary",), vmem_limit_bytes=VMEM_LIMIT),
        name="outproj",
    )(ya, yc, x2, gattn, w_out, gmoe, w_gr, b_gr)


def _row_gather_start(idx_ref, src_hbm, dst_ref, sem, n_rows, src_pitch=SLAB):
    def body(pair, carry):
        for prio in range(DMA_PRIORITIES):
            r = pair * DMA_PRIORITIES + prio
            src = src_hbm.at[pl.ds(idx_ref[0, 0, r] * src_pitch, SLAB), :]
            pltpu.make_async_copy(src, dst_ref.at[pl.ds(r * SLAB_PITCH, SLAB), :], sem).start(priority=prio)
        return carry
    lax.fori_loop(0, n_rows // DMA_PRIORITIES, body, 0, unroll=True)


def _row_gather_wait(src_hbm, dst_ref, sem, n_rows):
    pltpu.make_async_copy(src_hbm.at[pl.ds(0, n_rows * SLAB), :], dst_ref.at[pl.ds(0, n_rows * SLAB), :], sem).wait()


def _moe_kernel(tile_expert_ref, n_active_ref, first_ref, wslot_ref, next_expert_ref, tok_cur_ref, tok_next_ref,
                xm_hbm, w1_hbm, w3_hbm, w2_hbm, y_ref, xbuf_ref, w1_buf, w3_buf, w2_buf, sem_ref, wsem_ref, *, tm):
    i = pl.program_id(0)
    n_active = n_active_ref[0]
    slot = i % 2
    buf_rows = tm * SLAB_PITCH
    wslot = wslot_ref[i]

    def buf(s):
        return xbuf_ref.at[pl.ds(s * buf_rows, buf_rows), :]

    def weight_copies(e, s):
        return (pltpu.make_async_copy(w1_hbm.at[e], w1_buf.at[s], wsem_ref.at[0, s]),
                pltpu.make_async_copy(w3_hbm.at[e], w3_buf.at[s], wsem_ref.at[1, s]),
                pltpu.make_async_copy(w2_hbm.at[e], w2_buf.at[s], wsem_ref.at[2, s]))

    @pl.when(i == 0)
    def _():
        for cp in weight_copies(tile_expert_ref[0], 0):
            cp.start()
        _row_gather_start(tok_cur_ref, xm_hbm, buf(0), sem_ref.at[0], tm, src_pitch=SLAB_PITCH)

    active = i < n_active
    new_expert = active & (first_ref[i] == 1)

    @pl.when(new_expert & (next_expert_ref[i] >= 0))
    def _():
        for cp in weight_copies(next_expert_ref[i], 1 - wslot):
            cp.start()

    @pl.when(i + 1 < n_active)
    def _():
        _row_gather_start(tok_next_ref, xm_hbm, buf(1 - slot), sem_ref.at[1 - slot], tm, src_pitch=SLAB_PITCH)

    @pl.when(new_expert)
    def _():
        for cp in weight_copies(tile_expert_ref[i], wslot):
            cp.wait()

    @pl.when(active)
    def _():
        _row_gather_wait(xm_hbm, buf(slot), sem_ref.at[slot], tm)
        x = _load_slabs(xbuf_ref, slot * buf_rows, tm, SLAB, SLAB_PITCH).astype(BF16)
        a = jnp.dot(x, w1_buf[wslot].astype(BF16), preferred_element_type=F32)
        g = jnp.dot(x, w3_buf[wslot].astype(BF16), preferred_element_type=F32)
        hid = (a * jax.nn.sigmoid(a) * g).astype(BF16)
        _store_slabs(y_ref, jnp.dot(hid, w2_buf[wslot].astype(BF16), preferred_element_type=F32), SLAB)

    @pl.when(i >= n_active)
    def _():
        y_ref[...] = jnp.zeros_like(y_ref)


def _moe(plan, xm, w1, w3, w2):
    tile_expert, n_active, first, wslot, next_expert, row_token3 = plan
    n_tiles = row_token3.shape[0]
    tm = TM_MOE
    _, d_model, d_exp = w1.shape
    last = n_tiles - 1
    grid_spec = pltpu.PrefetchScalarGridSpec(
        num_scalar_prefetch=5,
        grid=(n_tiles,),
        in_specs=[pl.BlockSpec((1, 1, tm), lambda i, *_: (i, 0, 0), memory_space=pltpu.SMEM),
                  pl.BlockSpec((1, 1, tm), lambda i, *_: (jnp.minimum(i + 1, last), 0, 0), memory_space=pltpu.SMEM),
                  pl.BlockSpec(memory_space=pl.ANY), pl.BlockSpec(memory_space=pl.ANY),
                  pl.BlockSpec(memory_space=pl.ANY), pl.BlockSpec(memory_space=pl.ANY)],
        out_specs=pl.BlockSpec((tm * SLAB, LANES), lambda i, *_: (i, 0)),
        scratch_shapes=[pltpu.VMEM((2 * tm * SLAB_PITCH, LANES), F32),
                        pltpu.VMEM((2, d_model, d_exp), F32), pltpu.VMEM((2, d_model, d_exp), F32),
                        pltpu.VMEM((2, d_exp, d_model), F32),
                        pltpu.SemaphoreType.DMA((2,)), pltpu.SemaphoreType.DMA((3, 2))],
    )
    return pl.pallas_call(
        functools.partial(_moe_kernel, tm=tm),
        grid_spec=grid_spec,
        out_shape=jax.ShapeDtypeStruct((n_tiles * tm * SLAB, LANES), F32),
        compiler_params=pltpu.CompilerParams(dimension_semantics=("arbitrary",), vmem_limit_bytes=VMEM_LIMIT),
        name="moe",
    )(tile_expert, n_active, first, wslot, next_expert, row_token3, row_token3, xm, w1, w3, w2)


def _ple_kernel(dest_cur_ref, dest_next_ref, y_hbm, h1_ref, route_ref, p_ref, wple_ref, gple_ref, wpg_ref, bpg_ref,
                o_ref, ybuf_ref, sem_ref):
    i = pl.program_id(0)
    n = pl.num_programs(0)
    tm, d_model = h1_ref.shape
    slot = i % 2
    buf_rows = TOP_K * tm * SLAB_PITCH

    def buf(s):
        return ybuf_ref.at[pl.ds(s * buf_rows, buf_rows), :]

    @pl.when(i == 0)
    def _():
        _row_gather_start(dest_cur_ref, y_hbm, buf(0), sem_ref.at[0], TOP_K * tm)

    @pl.when(i + 1 < n)
    def _():
        _row_gather_start(dest_next_ref, y_hbm, buf(1 - slot), sem_ref.at[1 - slot], TOP_K * tm)

    _row_gather_wait(y_hbm, buf(slot), sem_ref.at[slot], TOP_K * tm)
    route = route_ref[...]
    y0 = _load_slabs(ybuf_ref, slot * buf_rows, tm, SLAB, SLAB_PITCH)
    y1 = _load_slabs(ybuf_ref, slot * buf_rows + tm * SLAB_PITCH, tm, SLAB, SLAB_PITCH)
    h2 = h1_ref[...] + (route[:, 2:3] * y0 + route[:, 3:4] * y1)
    xn = _rms(h2, gple_ref[...], d_model).astype(BF16)
    gate = jax.nn.sigmoid(jnp.dot(xn, wpg_ref[...], preferred_element_type=F32) + bpg_ref[...])
    pe = jnp.dot(p_ref[...].astype(BF16), wple_ref[...], preferred_element_type=F32)
    o_ref[...] = h2 + gate * pe


def _ple(dest3, y, h1, route, p2, w_ple, gple, w_pg, b_pg):
    t, d_model = h1.shape
    tm = TM_PLE
    n_tiles = t // tm
    last = n_tiles - 1
    const = lambda i: (0, 0)
    row = lambda i: (i, 0)
    full = lambda a: pl.BlockSpec(a.shape, const)
    return pl.pallas_call(
        _ple_kernel,
        grid=(n_tiles,),
        in_specs=[pl.BlockSpec((1, 1, TOP_K * tm), lambda i: (i, 0, 0), memory_space=pltpu.SMEM),
                  pl.BlockSpec((1, 1, TOP_K * tm), lambda i: (jnp.minimum(i + 1, last), 0, 0), memory_space=pltpu.SMEM),
                  pl.BlockSpec(memory_space=pl.ANY),
                  pl.BlockSpec((tm, d_model), row), pl.BlockSpec((tm, LANES), row), pl.BlockSpec((tm, p2.shape[1]), row),
                  full(w_ple), full(gple), pl.BlockSpec(w_pg.shape, const, pipeline_mode=pl.Buffered(1)), full(b_pg)],
        out_specs=pl.BlockSpec((tm, d_model), row),
        out_shape=jax.ShapeDtypeStruct((t, d_model), F32),
        scratch_shapes=[pltpu.VMEM((2 * TOP_K * tm * SLAB_PITCH, LANES), F32), pltpu.SemaphoreType.DMA((2,))],
        compiler_params=pltpu.CompilerParams(dimension_semantics=("arbitrary",), vmem_limit_bytes=VMEM_LIMIT),
        name="ple",
    )(dest3, dest3, y, h1, route, p2, w_ple, gple, w_pg, b_pg)


def _dispatch_plan(route):
    t = route.shape[0]
    tm = TM_MOE
    total = TOP_K * t
    n_tiles = total // tm + N_EXPERTS
    expert = route[:, :TOP_K].astype(jnp.int32).reshape(-1)
    pos = jnp.arange(total, dtype=jnp.int32)
    order = jnp.sort(expert * total + pos) % total
    rank = jnp.sort(order * total + pos) % total
    experts = jnp.arange(N_EXPERTS, dtype=jnp.int32)
    onehot = (expert[:, None] == experts[None, :]).astype(jnp.int32)
    counts = jnp.sum(onehot, axis=0)
    group_start = jnp.cumsum(counts) - counts
    tile_end = jnp.cumsum((counts + tm - 1) // tm)
    tile_start = tile_end - (counts + tm - 1) // tm
    n_active = tile_end[-1]
    shift = tile_start * tm - group_start
    dest = rank + jnp.sum(onehot * shift[None, :], axis=1)
    tiles = jnp.minimum(jnp.arange(n_tiles, dtype=jnp.int32), n_active - 1)
    tile_expert = jnp.sum((tile_end[None, :] <= tiles[:, None]).astype(jnp.int32), axis=1)
    j = jnp.arange(tm, dtype=jnp.int32)
    fill_row = (tile_start * tm + counts)[:, None] + j[None, :]
    fill_row = jnp.where(fill_row < (tile_end * tm)[:, None], fill_row, n_tiles * tm)
    fill_tok = (experts[:, None] * tm + j[None, :]) % t
    keys = jnp.concatenate([dest * t + pos // TOP_K, (fill_row * t + fill_tok).reshape(-1)])
    row_token3 = (jnp.sort(keys) % t).reshape(n_tiles, 1, tm)
    dest3 = dest.reshape(t // TM_PLE, TM_PLE, TOP_K).transpose(0, 2, 1).reshape(t // TM_PLE, 1, TOP_K * TM_PLE)
    prev_expert = jnp.concatenate([jnp.full((1,), -1, jnp.int32), tile_expert[:-1]])
    first = (tile_expert != prev_expert).astype(jnp.int32)
    nonempty = counts > 0
    wslot = (jnp.sum((nonempty[None, :] & (experts[None, :] < tile_expert[:, None])).astype(jnp.int32), axis=1)) % 2
    later = nonempty[None, :] & (experts[None, :] > tile_expert[:, None])
    next_expert = jnp.min(jnp.where(later, experts[None, :], N_EXPERTS), axis=1)
    next_expert = jnp.where(next_expert < N_EXPERTS, next_expert, -1)
    plan = (tile_expert, n_active.reshape(1), first, wslot, next_expert, row_token3)
    return plan, dest3


def _rope_swap_pad(w):
    half = QK_ROPE // 2
    x1 = w[..., QK_NOPE:QK_NOPE + half]
    x2 = w[..., QK_NOPE + half:]
    return jnp.concatenate([w[..., :QK_NOPE], x1, x2, x2, x1], axis=-1)


def kernel(x, p, positions, norm_mix, w_in, g_q_lat, g_kv_lat, w_uq, w_ukv, g_q_head, g_k_head, w_conv, g_out_attn, g_out_conv, w_out, norm_moe, w_group, b_group, w_router, b_router, w1, w3, w2, norm_ple, w_ple, w_ple_gate, b_ple_gate):
    batch, seq, d_model = x.shape
    depth = w_in.shape[0]
    t = batch * seq
    q_rank = g_q_lat.shape[1]
    kv_rank = g_kv_lat.shape[1]
    half = QK_ROPE // 2
    rope_lo = q_rank + kv_rank

    pos2 = positions.reshape(t, 1)
    inv_freq = ROPE_BASE ** (-jnp.arange(0, QK_ROPE, 2, dtype=F32) / QK_ROPE)
    freq = jnp.tile(inv_freq, LANES // half).reshape(1, LANES)

    h = x.reshape(t, d_model)
    for i in range(depth):
        w_in_t = jnp.swapaxes(w_in[i], 0, 1).astype(BF16)
        w_rope = jnp.concatenate([w_in_t[rope_lo:rope_lo + QK_ROPE], w_in_t[rope_lo + half:rope_lo + QK_ROPE],
                                  w_in_t[rope_lo:rope_lo + half]], axis=0)
        wuq_p = _rope_swap_pad(w_uq[i]).reshape(q_rank, MLA_HEADS * HEAD_PAD).astype(BF16)
        wukv_p = jnp.concatenate([w_ukv[i][:, :, :QK_NOPE].reshape(kv_rank, -1),
                                  w_ukv[i][:, :, QK_NOPE:].reshape(kv_rank, -1)], axis=1).astype(BF16)
        gq_p = _rope_swap_pad(g_q_head[i]).reshape(1, HEAD_PAD)
        gk_p = _rope_swap_pad(g_k_head[i]).reshape(1, HEAD_PAD)
        q, k, v, yc = _front(h, pos2, freq, norm_mix[i].reshape(1, -1), w_in_t, w_rope, g_q_lat[i].reshape(1, -1),
                             g_kv_lat[i].reshape(1, -1), wuq_p, wukv_p, gq_p, gk_p, w_conv[i],
                             g_out_conv[i].reshape(1, -1), seq=seq)
        ya = _attention(q, k, v, batch=batch, seq=seq)

        pad = LANES - N_GROUPS - N_EXPERTS
        w_gr32 = jnp.concatenate([w_group[i], w_router[i], jnp.zeros((d_model, pad), F32)], axis=1)
        w_gr_hi = w_gr32.astype(BF16)
        w_gr = jnp.concatenate([w_gr_hi, (w_gr32 - w_gr_hi.astype(F32)).astype(BF16)], axis=1)
        b_gr = jnp.concatenate([b_group[i], b_router[i], jnp.zeros((pad,), F32)]).reshape(1, LANES)
        h1, xm, route = _outproj(ya, yc, h, g_out_attn[i].reshape(1, -1), w_out[i].astype(BF16),
                                 norm_moe[i].reshape(1, -1), w_gr, b_gr)

        plan, dest3 = _dispatch_plan(route)
        y = _moe(plan, xm, w1[i], w3[i], w2[i])
        h = _ple(dest3, y, h1, route, p[i].reshape(t, -1), w_ple[i].astype(BF16), norm_ple[i].reshape(1, -1),
                 w_ple_gate[i].astype(BF16), b_ple_gate[i].reshape(1, -1))
    return h.reshape(batch, seq, d_model)
```

```python
import functools

import jax
import jax.numpy as jnp
from jax import lax
from jax.experimental import pallas as pl
from jax.experimental.pallas import tpu as pltpu

F32 = jnp.float32
BF16 = jnp.bfloat16

CHUNK = 64
EPS = 1e-6
MASK_VALUE = -1e30
ROPE_BASE = 10000.0

MLA_HEADS = 8
QK_NOPE = 128
QK_ROPE = 64
QK_HEAD = QK_NOPE + QK_ROPE
V_HEAD = 128
HEAD_PAD = 256
Q_SCALE = QK_HEAD ** -0.5 * 1.4426950408889634
N_GROUPS = 4
EXPERTS_PER_GROUP = 8
N_EXPERTS = N_GROUPS * EXPERTS_PER_GROUP
TOP_K = 2
LANES = 128
SLAB = 16
SLAB_PITCH = 20
DMA_PRIORITIES = 2

TM_FRONT = 512
TQ_ATTN = 1024
TM_OUT = 512
TM_MOE = 256
TM_PLE = 256
VMEM_LIMIT = 56 * 1024 * 1024


def _rms(x, g, n):
    ms = jnp.sum(x * x, axis=-1, keepdims=True) * (1.0 / n)
    return x * lax.rsqrt(ms + EPS) * g


def _store_slabs(ref, val, pitch, mask=None):
    n, width = val.shape
    for c in range(width // LANES):
        piece = val[:, c * LANES:(c + 1) * LANES]
        rows = pl.ds(c, n, stride=pitch)
        if mask is not None:
            piece = jnp.where(mask, piece, ref[rows, :])
        ref[rows, :] = piece


def _load_slabs(ref, start, n, slab, pitch):
    return jnp.concatenate([ref[pl.ds(start + c, n, stride=pitch), :] for c in range(slab)], axis=1)


def _front_kernel(x_ref, pos_ref, freq_ref, gmix_ref, win_ref, wrope_ref, gql_ref, gkvl_ref,
                  wuq_ref, wukv_ref, gq_ref, gk_ref, wconv_ref, gconv_ref,
                  q_ref, k_ref, v_ref, yc_ref, prev_ref, *, tiles_per_seq, q_rank, kv_rank, d_conv):
    i = pl.program_id(0)
    is_start = (i % tiles_per_seq) == 0
    tm = x_ref.shape[0]
    d_model = x_ref.shape[1]
    rope_lo = q_rank + kv_rank
    nt = (((1,), (1,)), ((), ()))

    @pl.when(is_start)
    def _():
        prev_ref[...] = jnp.zeros_like(prev_ref)

    xn = _rms(x_ref[...], gmix_ref[...], d_model).astype(BF16)
    lat = lax.dot_general(xn, win_ref[:rope_lo, :], nt, preferred_element_type=F32)
    q_lat = lat[:, :q_rank]
    kv_lat = lat[:, q_rank:]
    kr = lax.dot_general(xn, wrope_ref[...], nt, preferred_element_type=F32)

    qn = _rms(q_lat, gql_ref[...], q_rank).astype(BF16)
    kvn = _rms(kv_lat, gkvl_ref[...], kv_rank).astype(BF16)
    q_all = jnp.dot(qn, wuq_ref[...], preferred_element_type=F32)
    kv_all = jnp.dot(kvn, wukv_ref[...], preferred_element_type=F32)

    lane = lax.broadcasted_iota(jnp.int32, (tm, LANES), 1)
    ang = pos_ref[...].astype(F32) * freq_ref[...]
    cos = jnp.cos(ang)
    sin = jnp.sin(ang)
    rope_c = jnp.where(lane < QK_ROPE, cos, 0.0)
    rope_s = jnp.where(lane < QK_ROPE // 2, -sin, jnp.where(lane < QK_ROPE, sin, 0.0))
    rope_lanes = lane < QK_ROPE

    def rope(t):
        return t * rope_c + pltpu.roll(t, QK_ROPE, 1) * rope_s

    gq = gq_ref[...]
    gk = gk_ref[...]
    inv_n = 1.0 / QK_HEAD
    kr_ssq = jnp.sum(jnp.where(rope_lanes, kr * kr, 0.0), axis=-1, keepdims=True)
    kr_rot = rope(kr * gk[:, QK_NOPE:])
    for h in range(MLA_HEADS):
        qa = q_all[:, h * HEAD_PAD:h * HEAD_PAD + QK_NOPE]
        qb = q_all[:, h * HEAD_PAD + QK_NOPE:(h + 1) * HEAD_PAD]
        ssq = (jnp.sum(qa * qa, axis=-1, keepdims=True)
               + jnp.sum(jnp.where(rope_lanes, qb * qb, 0.0), axis=-1, keepdims=True))
        inv = lax.rsqrt(ssq * inv_n + EPS)
        q_ref[:, h * HEAD_PAD:h * HEAD_PAD + QK_NOPE] = (qa * inv * gq[:, :QK_NOPE] * Q_SCALE).astype(BF16)
        q_ref[:, h * HEAD_PAD + QK_NOPE:(h + 1) * HEAD_PAD] = (rope(qb * inv * gq[:, QK_NOPE:]) * Q_SCALE).astype(BF16)

        ka = kv_all[:, h * QK_NOPE:(h + 1) * QK_NOPE]
        kinv = lax.rsqrt((jnp.sum(ka * ka, axis=-1, keepdims=True) + kr_ssq) * inv_n + EPS)
        k_ref[:, h * HEAD_PAD:h * HEAD_PAD + QK_NOPE] = (ka * kinv * gk[:, :QK_NOPE]).astype(BF16)
        k_ref[:, h * HEAD_PAD + QK_NOPE:(h + 1) * HEAD_PAD] = (kr_rot * kinv).astype(BF16)
        v_lo = MLA_HEADS * QK_NOPE + h * V_HEAD
        v_ref[:, h * HEAD_PAD:h * HEAD_PAD + V_HEAD] = kv_all[:, v_lo:v_lo + V_HEAD].astype(BF16)
        v_ref[:, h * HEAD_PAD + V_HEAD:(h + 1) * HEAD_PAD] = jnp.ones((tm, HEAD_PAD - V_HEAD), BF16)

    conv_lo = rope_lo + QK_ROPE
    half_c = d_conv // 2
    row = lax.broadcasted_iota(jnp.int32, (tm, half_c), 0)
    wc = wconv_ref[...]
    ys = []
    ssq = jnp.zeros((tm, 1), F32)
    for cs in (slice(0, half_c), slice(half_c, d_conv)):
        b_gate, c_gate, h_conv = (lax.dot_general(xn, win_ref[conv_lo + k * d_conv + cs.start:conv_lo + k * d_conv + cs.stop, :],
                                                  nt, preferred_element_type=F32) for k in range(3))
        u = c_gate * h_conv
        prev = prev_ref[:, cs]
        hm1 = prev[7:8, :]
        hm2 = prev[6:7, :]
        u1 = jnp.where(row == 0, hm1, pltpu.roll(u, 1, 0))
        u2 = jnp.where(row == 0, hm2, jnp.where(row == 1, hm1, pltpu.roll(u, 2, 0)))
        prev_ref[:, cs] = u[tm - 8:, :]
        y = b_gate * (u2 * wc[0:1, cs] + u1 * wc[1:2, cs] + u * wc[2:3, cs])
        ssq = ssq + jnp.sum(y * y, axis=-1, keepdims=True)
        ys.append(y)
    inv = lax.rsqrt(ssq * (1.0 / d_conv) + EPS)
    yc_ref[...] = (jnp.concatenate(ys, axis=1) * inv * gconv_ref[...]).astype(BF16)


def _front(x2, pos2, freq, gmix, w_in_t, w_rope, gql, gkvl, wuq_p, wukv_p, gq_p, gk_p, wconv, gconv, *, seq):
    t, d_model = x2.shape
    tm = TM_FRONT
    q_rank = gql.shape[1]
    kv_rank = gkvl.shape[1]
    d_conv = gconv.shape[1]
    const = lambda i: (0, 0)
    row = lambda i: (i, 0)
    full = lambda a: pl.BlockSpec(a.shape, const)
    kern = functools.partial(_front_kernel, tiles_per_seq=seq // tm, q_rank=q_rank, kv_rank=kv_rank, d_conv=d_conv)
    return pl.pallas_call(
        kern,
        grid=(t // tm,),
        in_specs=[pl.BlockSpec((tm, d_model), row), pl.BlockSpec((tm, 1), row), full(freq), full(gmix),
                  pl.BlockSpec(w_in_t.shape, const, pipeline_mode=pl.Buffered(1)), full(w_rope),
                  full(gql), full(gkvl), full(wuq_p), full(wukv_p), full(gq_p), full(gk_p), full(wconv), full(gconv)],
        out_specs=[pl.BlockSpec((tm, MLA_HEADS * HEAD_PAD), row), pl.BlockSpec((tm, MLA_HEADS * HEAD_PAD), row),
                   pl.BlockSpec((tm, MLA_HEADS * HEAD_PAD), row), pl.BlockSpec((tm, d_conv), row)],
        out_shape=[jax.ShapeDtypeStruct((t, MLA_HEADS * HEAD_PAD), BF16), jax.ShapeDtypeStruct((t, MLA_HEADS * HEAD_PAD), BF16),
                   jax.ShapeDtypeStruct((t, MLA_HEADS * HEAD_PAD), BF16), jax.ShapeDtypeStruct((t, d_conv), BF16)],
        scratch_shapes=[pltpu.VMEM((8, d_conv), F32)],
        compiler_params=pltpu.CompilerParams(dimension_semantics=("arbitrary",), vmem_limit_bytes=VMEM_LIMIT),
        name="front",
    )(x2, pos2, freq, gmix, w_in_t, w_rope, gql, gkvl, wuq_p, wukv_p, gq_p, gk_p, wconv, gconv)


def _attn_kernel(q_ref, k_ref, v_ref, *rest):
    n_side = (len(rest) - 1) // 2
    o_ref = rest[n_side]

    def round_side_weights():
        for w_ref, wb_ref in zip(rest[:n_side], rest[n_side + 1:]):
            wb_ref[...] = w_ref[...].astype(BF16)

    i = pl.program_id(2)
    tq = q_ref.shape[0]
    half = tq // 2
    q_halves = (q_ref[:half, :], q_ref[half:, :])

    def step(start, n_keys, masks, carry):
        kvs = [(k_ref[pl.ds(start, n), :], v_ref[pl.ds(start, n), :]) for n in n_keys]
        scores = [lax.dot_general(qh, kb, (((1,), (1,)), ((), ())), preferred_element_type=F32)
                  for qh, (kb, _) in zip(q_halves, kvs)]
        scores = [s if mask is None else jnp.where(mask, s, MASK_VALUE) for s, mask in zip(scores, masks)]
        m_new = [jnp.maximum(m, jnp.max(s, axis=-1, keepdims=True)) for s, (m, _) in zip(scores, carry)]
        probs = [jnp.exp2(s - mn).astype(BF16) for s, mn in zip(scores, m_new)]
        return tuple((mn, jnp.exp2(m - mn) * acc + jnp.dot(p, vb, preferred_element_type=F32))
                     for p, mn, (m, acc), (_, vb) in zip(probs, m_new, carry, kvs))

    def chunk_mask(n, first_row):
        r = (lax.broadcasted_iota(jnp.int32, (half, n), 0) + first_row) // CHUNK
        c = lax.broadcasted_iota(jnp.int32, (half, n), 1) // CHUNK
        return c <= r

    for qi in range(k_ref.shape[0] // tq):
        @pl.when(i == qi)
        def _(qi=qi):
            round_side_weights()
            carry = tuple((jnp.full((half, 1), MASK_VALUE, F32), jnp.zeros((half, HEAD_PAD), F32)) for _ in range(2))
            for j in range(qi):
                carry = step(j * tq, (tq, tq), (None, None), carry)
            (_, acc0), (_, acc1) = step(qi * tq, (half, tq), (chunk_mask(half, 0), chunk_mask(tq, half)), carry)
            o_ref[:half, :] = acc0[:, :V_HEAD] / acc0[:, V_HEAD:]
            o_ref[half:, :] = acc1[:, :V_HEAD] / acc1[:, V_HEAD:]


def _attention(q, k, v, side_weights, *, batch, seq):
    tq = TQ_ATTN
    nq = seq // tq
    step = lambda b, h, i: ((b * MLA_HEADS + h) * nq + i, 0, 0)
    side_specs = [pl.BlockSpec((1,) + w.shape[1:], step) for w in side_weights]
    outs = pl.pallas_call(
        _attn_kernel,
        grid=(batch, MLA_HEADS, nq),
        in_specs=[pl.BlockSpec((tq, HEAD_PAD), lambda b, h, i: (b * nq + i, h)),
                  pl.BlockSpec((seq, HEAD_PAD), lambda b, h, i: (b, h)),
                  pl.BlockSpec((seq, HEAD_PAD), lambda b, h, i: (b, h))] + side_specs,
        out_specs=[pl.BlockSpec((tq, V_HEAD), lambda b, h, i: (b * nq + i, h))] + side_specs,
        out_shape=[jax.ShapeDtypeStruct((batch * seq, MLA_HEADS * V_HEAD), F32)]
                  + [jax.ShapeDtypeStruct(w.shape, BF16) for w in side_weights],
        compiler_params=pltpu.CompilerParams(dimension_semantics=("arbitrary", "arbitrary", "arbitrary"),
                                             vmem_limit_bytes=VMEM_LIMIT),
        name="attn",
    )(q, k, v, *side_weights)
    return outs[0], outs[1:]


def _outproj_kernel(ya_ref, yc_ref, x_ref, gattn_ref, wout_ref, gmoe_ref, wgr_ref, bgr_ref,
                    h1_ref, xm_ref, route_ref):
    tm, d_attn = ya_ref.shape
    d_model = x_ref.shape[1]
    ya = _rms(ya_ref[...], gattn_ref[...], d_attn).astype(BF16)
    h1 = (x_ref[...]
          + jnp.dot(ya, wout_ref[:d_attn, :], preferred_element_type=F32)
          + jnp.dot(yc_ref[...], wout_ref[d_attn:, :], preferred_element_type=F32))
    h1_ref[...] = h1
    xm = _rms(h1, gmoe_ref[...], d_model)
    _store_slabs(xm_ref, xm, SLAB_PITCH)
    for c in range(SLAB, SLAB_PITCH):
        xm_ref[pl.ds(c, tm, stride=SLAB_PITCH), :] = jnp.zeros((tm, LANES), F32)

    x_hi = xm.astype(BF16)
    x_lo = (xm - x_hi.astype(F32)).astype(BF16)
    hi_part = jnp.dot(x_hi, wgr_ref[...], preferred_element_type=F32)
    lo_part = jnp.dot(x_lo, wgr_ref[:, :LANES], preferred_element_type=F32)
    logits = hi_part[:, :LANES] + hi_part[:, LANES:] + lo_part + bgr_ref[...]
    lane = lax.broadcasted_iota(jnp.int32, (tm, LANES), 1).astype(F32)
    neg = -jnp.inf
    big = 1e9
    gl = jnp.where(lane < N_GROUPS, logits, neg)
    gmax = jnp.max(gl, axis=-1, keepdims=True)
    gsel = jnp.min(jnp.where(gl == gmax, lane, big), axis=-1, keepdims=True)
    p_g = 1.0 / jnp.sum(jnp.exp(gl - gmax), axis=-1, keepdims=True)
    lo = N_GROUPS + EXPERTS_PER_GROUP * gsel
    el = jnp.where((lane >= lo) & (lane < lo + EXPERTS_PER_GROUP), logits, neg)
    v1 = jnp.max(el, axis=-1, keepdims=True)
    i1 = jnp.min(jnp.where(el == v1, lane, big), axis=-1, keepdims=True)
    el2 = jnp.where(lane == i1, neg, el)
    v2 = jnp.max(el2, axis=-1, keepdims=True)
    i2 = jnp.min(jnp.where(el2 == v2, lane, big), axis=-1, keepdims=True)
    e2 = jnp.exp(v2 - v1)
    den = 1.0 + e2
    w1 = p_g / den
    w2 = p_g * e2 / den
    route_ref[...] = jnp.where(lane == 0, i1 - N_GROUPS,
                               jnp.where(lane == 1, i2 - N_GROUPS,
                                         jnp.where(lane == 2, w1, jnp.where(lane == 3, w2, 0.0))))


def _outproj(ya, yc, x2, gattn, w_out, gmoe, w_gr, b_gr):
    t, d_model = x2.shape
    d_attn = ya.shape[1]
    tm = TM_OUT
    const = lambda i: (0, 0)
    row = lambda i: (i, 0)
    full = lambda a: pl.BlockSpec(a.shape, const)
    return pl.pallas_call(
        _outproj_kernel,
        grid=(t // tm,),
        in_specs=[pl.BlockSpec((tm, d_attn), row), pl.BlockSpec((tm, yc.shape[1]), row), pl.BlockSpec((tm, d_model), row),
                  full(gattn), pl.BlockSpec(w_out.shape, const, pipeline_mode=pl.Buffered(1)), full(gmoe), full(w_gr), full(b_gr)],
        out_specs=[pl.BlockSpec((tm, d_model), row), pl.BlockSpec((tm * SLAB_PITCH, LANES), row), pl.BlockSpec((tm, LANES), row)],
        out_shape=[jax.ShapeDtypeStruct((t, d_model), F32), jax.ShapeDtypeStruct((t * SLAB_PITCH, LANES), F32),
                   jax.ShapeDtypeStruct((t, LANES), F32)],
        compiler_params=pltpu.CompilerParams(dimension_semantics=("arbitrary",), vmem_limit_bytes=VMEM_LIMIT),
        name="outproj",
    )(ya, yc, x2, gattn, w_out, gmoe, w_gr, b_gr)


def _row_gather_start(idx_ref, src_hbm, dst_ref, sem, n_rows, src_pitch=SLAB):
    def body(pair, carry):
        for prio in range(DMA_PRIORITIES):
            r = pair * DMA_PRIORITIES + prio
            src = src_hbm.at[pl.ds(idx_ref[0, 0, r] * src_pitch, SLAB), :]
            pltpu.make_async_copy(src, dst_ref.at[pl.ds(r * SLAB_PITCH, SLAB), :], sem).start(priority=prio)
        return carry
    lax.fori_loop(0, n_rows // DMA_PRIORITIES, body, 0, unroll=True)


def _row_gather_wait(src_hbm, dst_ref, sem, n_rows):
    pltpu.make_async_copy(src_hbm.at[pl.ds(0, n_rows * SLAB), :], dst_ref.at[pl.ds(0, n_rows * SLAB), :], sem).wait()


def _moe_kernel(tile_expert_ref, n_active_ref, first_ref, wslot_ref, next_expert_ref, tok_cur_ref, tok_next_ref,
                xm_hbm, w1_hbm, w3_hbm, w2_hbm, y_ref, xbuf_ref, w1_buf, w3_buf, w2_buf, sem_ref, wsem_ref, *, tm):
    i = pl.program_id(0)
    n_active = n_active_ref[0]
    slot = i % 2
    buf_rows = tm * SLAB_PITCH
    wslot = wslot_ref[i]

    def buf(s):
        return xbuf_ref.at[pl.ds(s * buf_rows, buf_rows), :]

    def weight_copies(e, s):
        return (pltpu.make_async_copy(w1_hbm.at[e], w1_buf.at[s], wsem_ref.at[0, s]),
                pltpu.make_async_copy(w3_hbm.at[e], w3_buf.at[s], wsem_ref.at[1, s]),
                pltpu.make_async_copy(w2_hbm.at[e], w2_buf.at[s], wsem_ref.at[2, s]))

    @pl.when(i == 0)
    def _():
        for cp in weight_copies(tile_expert_ref[0], 0):
            cp.start()
        _row_gather_start(tok_cur_ref, xm_hbm, buf(0), sem_ref.at[0], tm, src_pitch=SLAB_PITCH)

    active = i < n_active
    new_expert = active & (first_ref[i] == 1)

    @pl.when(new_expert & (next_expert_ref[i] >= 0))
    def _():
        for cp in weight_copies(next_expert_ref[i], 1 - wslot):
            cp.start()

    @pl.when(i + 1 < n_active)
    def _():
        _row_gather_start(tok_next_ref, xm_hbm, buf(1 - slot), sem_ref.at[1 - slot], tm, src_pitch=SLAB_PITCH)

    @pl.when(new_expert)
    def _():
        for cp in weight_copies(tile_expert_ref[i], wslot):
            cp.wait()

    @pl.when(active)
    def _():
        _row_gather_wait(xm_hbm, buf(slot), sem_ref.at[slot], tm)
        x = _load_slabs(xbuf_ref, slot * buf_rows, tm, SLAB, SLAB_PITCH).astype(BF16)
        a = jnp.dot(x, w1_buf[wslot], preferred_element_type=F32)
        g = jnp.dot(x, w3_buf[wslot], preferred_element_type=F32)
        hid = (a * jax.nn.sigmoid(a) * g).astype(BF16)
        _store_slabs(y_ref, jnp.dot(hid, w2_buf[wslot], preferred_element_type=F32), SLAB)

    @pl.when(i >= n_active)
    def _():
        y_ref[...] = jnp.zeros_like(y_ref)


def _moe(plan, xm, w1, w3, w2):
    tile_expert, n_active, first, wslot, next_expert, row_token3 = plan
    n_tiles = row_token3.shape[0]
    tm = TM_MOE
    _, d_model, d_exp = w1.shape
    last = n_tiles - 1
    grid_spec = pltpu.PrefetchScalarGridSpec(
        num_scalar_prefetch=5,
        grid=(n_tiles,),
        in_specs=[pl.BlockSpec((1, 1, tm), lambda i, *_: (i, 0, 0), memory_space=pltpu.SMEM),
                  pl.BlockSpec((1, 1, tm), lambda i, *_: (jnp.minimum(i + 1, last), 0, 0), memory_space=pltpu.SMEM),
                  pl.BlockSpec(memory_space=pl.ANY), pl.BlockSpec(memory_space=pl.ANY),
                  pl.BlockSpec(memory_space=pl.ANY), pl.BlockSpec(memory_space=pl.ANY)],
        out_specs=pl.BlockSpec((tm * SLAB, LANES), lambda i, *_: (i, 0)),
        scratch_shapes=[pltpu.VMEM((2 * tm * SLAB_PITCH, LANES), F32),
                        pltpu.VMEM((2, d_model, d_exp), BF16), pltpu.VMEM((2, d_model, d_exp), BF16),
                        pltpu.VMEM((2, d_exp, d_model), BF16),
                        pltpu.SemaphoreType.DMA((2,)), pltpu.SemaphoreType.DMA((3, 2))],
    )
    return pl.pallas_call(
        functools.partial(_moe_kernel, tm=tm),
        grid_spec=grid_spec,
        out_shape=jax.ShapeDtypeStruct((n_tiles * tm * SLAB, LANES), F32),
        compiler_params=pltpu.CompilerParams(dimension_semantics=("arbitrary",), vmem_limit_bytes=VMEM_LIMIT),
        name="moe",
    )(tile_expert, n_active, first, wslot, next_expert, row_token3, row_token3, xm, w1, w3, w2)


def _ple_kernel(dest_cur_ref, dest_next_ref, y_hbm, h1_ref, route_ref, p_ref, wple_ref, gple_ref, wpg_ref, bpg_ref,
                o_ref, ybuf_ref, sem_ref):
    i = pl.program_id(0)
    n = pl.num_programs(0)
    tm, d_model = h1_ref.shape
    slot = i % 2
    buf_rows = TOP_K * tm * SLAB_PITCH

    def buf(s):
        return ybuf_ref.at[pl.ds(s * buf_rows, buf_rows), :]

    @pl.when(i == 0)
    def _():
        _row_gather_start(dest_cur_ref, y_hbm, buf(0), sem_ref.at[0], TOP_K * tm)

    @pl.when(i + 1 < n)
    def _():
        _row_gather_start(dest_next_ref, y_hbm, buf(1 - slot), sem_ref.at[1 - slot], TOP_K * tm)

    _row_gather_wait(y_hbm, buf(slot), sem_ref.at[slot], TOP_K * tm)
    route = route_ref[...]
    y0 = _load_slabs(ybuf_ref, slot * buf_rows, tm, SLAB, SLAB_PITCH)
    y1 = _load_slabs(ybuf_ref, slot * buf_rows + tm * SLAB_PITCH, tm, SLAB, SLAB_PITCH)
    h2 = h1_ref[...] + (route[:, 2:3] * y0 + route[:, 3:4] * y1)
    xn = _rms(h2, gple_ref[...], d_model).astype(BF16)
    gate = jax.nn.sigmoid(jnp.dot(xn, wpg_ref[...], preferred_element_type=F32) + bpg_ref[...])
    pe = jnp.dot(p_ref[...].astype(BF16), wple_ref[...], preferred_element_type=F32)
    o_ref[...] = h2 + gate * pe


def _ple(dest3, y, h1, route, p2, w_ple, gple, w_pg, b_pg):
    t, d_model = h1.shape
    tm = TM_PLE
    n_tiles = t // tm
    last = n_tiles - 1
    const = lambda i: (0, 0)
    row = lambda i: (i, 0)
    full = lambda a: pl.BlockSpec(a.shape, const)
    return pl.pallas_call(
        _ple_kernel,
        grid=(n_tiles,),
        in_specs=[pl.BlockSpec((1, 1, TOP_K * tm), lambda i: (i, 0, 0), memory_space=pltpu.SMEM),
                  pl.BlockSpec((1, 1, TOP_K * tm), lambda i: (jnp.minimum(i + 1, last), 0, 0), memory_space=pltpu.SMEM),
                  pl.BlockSpec(memory_space=pl.ANY),
                  pl.BlockSpec((tm, d_model), row), pl.BlockSpec((tm, LANES), row), pl.BlockSpec((tm, p2.shape[1]), row),
                  full(w_ple), full(gple), pl.BlockSpec(w_pg.shape, const, pipeline_mode=pl.Buffered(1)), full(b_pg)],
        out_specs=pl.BlockSpec((tm, d_model), row),
        out_shape=jax.ShapeDtypeStruct((t, d_model), F32),
        scratch_shapes=[pltpu.VMEM((2 * TOP_K * tm * SLAB_PITCH, LANES), F32), pltpu.SemaphoreType.DMA((2,))],
        compiler_params=pltpu.CompilerParams(dimension_semantics=("arbitrary",), vmem_limit_bytes=VMEM_LIMIT),
        name="ple",
    )(dest3, dest3, y, h1, route, p2, w_ple, gple, w_pg, b_pg)


def _dispatch_plan(route):
    t = route.shape[0]
    tm = TM_MOE
    total = TOP_K * t
    n_tiles = total // tm + N_EXPERTS
    expert = route[:, :TOP_K].astype(jnp.int32).reshape(-1)
    pos = jnp.arange(total, dtype=jnp.int32)
    order = jnp.sort(expert * total + pos) % total
    rank = jnp.sort(order * total + pos) % total
    experts = jnp.arange(N_EXPERTS, dtype=jnp.int32)
    onehot = (expert[:, None] == experts[None, :]).astype(jnp.int32)
    counts = jnp.sum(onehot, axis=0)
    group_start = jnp.cumsum(counts) - counts
    tile_end = jnp.cumsum((counts + tm - 1) // tm)
    tile_start = tile_end - (counts + tm - 1) // tm
    n_active = tile_end[-1]
    shift = tile_start * tm - group_start
    dest = rank + jnp.sum(onehot * shift[None, :], axis=1)
    tiles = jnp.minimum(jnp.arange(n_tiles, dtype=jnp.int32), n_active - 1)
    tile_expert = jnp.sum((tile_end[None, :] <= tiles[:, None]).astype(jnp.int32), axis=1)
    j = jnp.arange(tm, dtype=jnp.int32)
    fill_row = (tile_start * tm + counts)[:, None] + j[None, :]
    fill_row = jnp.where(fill_row < (tile_end * tm)[:, None], fill_row, n_tiles * tm)
    fill_tok = (experts[:, None] * tm + j[None, :]) % t
    keys = jnp.concatenate([dest * t + pos // TOP_K, (fill_row * t + fill_tok).reshape(-1)])
    row_token3 = (jnp.sort(keys) % t).reshape(n_tiles, 1, tm)
    dest3 = dest.reshape(t // TM_PLE, TM_PLE, TOP_K).transpose(0, 2, 1).reshape(t // TM_PLE, 1, TOP_K * TM_PLE)
    prev_expert = jnp.concatenate([jnp.full((1,), -1, jnp.int32), tile_expert[:-1]])
    first = (tile_expert != prev_expert).astype(jnp.int32)
    nonempty = counts > 0
    wslot = (jnp.sum((nonempty[None, :] & (experts[None, :] < tile_expert[:, None])).astype(jnp.int32), axis=1)) % 2
    later = nonempty[None, :] & (experts[None, :] > tile_expert[:, None])
    next_expert = jnp.min(jnp.where(later, experts[None, :], N_EXPERTS), axis=1)
    next_expert = jnp.where(next_expert < N_EXPERTS, next_expert, -1)
    plan = (tile_expert, n_active.reshape(1), first, wslot, next_expert, row_token3)
    return plan, dest3


def _rope_swap_pad(w):
    half = QK_ROPE // 2
    x1 = w[..., QK_NOPE:QK_NOPE + half]
    x2 = w[..., QK_NOPE + half:]
    return jnp.concatenate([w[..., :QK_NOPE], x1, x2, x2, x1], axis=-1)


def kernel(x, p, positions, norm_mix, w_in, g_q_lat, g_kv_lat, w_uq, w_ukv, g_q_head, g_k_head, w_conv, g_out_attn, g_out_conv, w_out, norm_moe, w_group, b_group, w_router, b_router, w1, w3, w2, norm_ple, w_ple, w_ple_gate, b_ple_gate):
    batch, seq, d_model = x.shape
    depth = w_in.shape[0]
    t = batch * seq
    q_rank = g_q_lat.shape[1]
    kv_rank = g_kv_lat.shape[1]
    half = QK_ROPE // 2
    rope_lo = q_rank + kv_rank

    pos2 = positions.reshape(t, 1)
    inv_freq = ROPE_BASE ** (-jnp.arange(0, QK_ROPE, 2, dtype=F32) / QK_ROPE)
    freq = jnp.tile(inv_freq, LANES // half).reshape(1, LANES)

    h = x.reshape(t, d_model)
    for i in range(depth):
        w_in_t = jnp.swapaxes(w_in[i], 0, 1).astype(BF16)
        w_rope = jnp.concatenate([w_in_t[rope_lo:rope_lo + QK_ROPE], w_in_t[rope_lo + half:rope_lo + QK_ROPE],
                                  w_in_t[rope_lo:rope_lo + half]], axis=0)
        wuq_p = _rope_swap_pad(w_uq[i]).reshape(q_rank, MLA_HEADS * HEAD_PAD).astype(BF16)
        wukv_p = jnp.concatenate([w_ukv[i][:, :, :QK_NOPE].reshape(kv_rank, -1),
                                  w_ukv[i][:, :, QK_NOPE:].reshape(kv_rank, -1)], axis=1).astype(BF16)
        gq_p = _rope_swap_pad(g_q_head[i]).reshape(1, HEAD_PAD)
        gk_p = _rope_swap_pad(g_k_head[i]).reshape(1, HEAD_PAD)
        q, k, v, yc = _front(h, pos2, freq, norm_mix[i].reshape(1, -1), w_in_t, w_rope, g_q_lat[i].reshape(1, -1),
                             g_kv_lat[i].reshape(1, -1), wuq_p, wukv_p, gq_p, gk_p, w_conv[i],
                             g_out_conv[i].reshape(1, -1), seq=seq)
        n_steps = batch * MLA_HEADS * (seq // TQ_ATTN)
        expert_w = (w1[i], w3[i], w2[i])
        ya, expert_wb = _attention(q, k, v, [w.reshape(n_steps, -1, w.shape[-1]) for w in expert_w], batch=batch, seq=seq)
        w1b, w3b, w2b = (wb.reshape(w.shape) for wb, w in zip(expert_wb, expert_w))

        pad = LANES - N_GROUPS - N_EXPERTS
        w_gr32 = jnp.concatenate([w_group[i], w_router[i], jnp.zeros((d_model, pad), F32)], axis=1)
        w_gr_hi = w_gr32.astype(BF16)
        w_gr = jnp.concatenate([w_gr_hi, (w_gr32 - w_gr_hi.astype(F32)).astype(BF16)], axis=1)
        b_gr = jnp.concatenate([b_group[i], b_router[i], jnp.zeros((pad,), F32)]).reshape(1, LANES)
        h1, xm, route = _outproj(ya, yc, h, g_out_attn[i].reshape(1, -1), w_out[i].astype(BF16),
                                 norm_moe[i].reshape(1, -1), w_gr, b_gr)

        plan, dest3 = _dispatch_plan(route)
        y = _moe(plan, xm, w1b, w3b, w2b)
        h = _ple(dest3, y, h1, route, p[i].reshape(t, -1), w_ple[i].astype(BF16), norm_ple[i].reshape(1, -1),
                 w_ple_gate[i].astype(BF16), b_ple_gate[i].reshape(1, -1))
    return h.reshape(batch, seq, d_model)
```

```python
import functools

import jax
import jax.numpy as jnp
from jax import lax
from jax.experimental import pallas as pl
from jax.experimental.pallas import tpu as pltpu

F32 = jnp.float32
BF16 = jnp.bfloat16

CHUNK = 64
EPS = 1e-6
MASK_VALUE = -1e30
ROPE_BASE = 10000.0

MLA_HEADS = 8
QK_NOPE = 128
QK_ROPE = 64
QK_HEAD = QK_NOPE + QK_ROPE
V_HEAD = 128
HEAD_PAD = 256
Q_SCALE = QK_HEAD ** -0.5 * 1.4426950408889634
N_GROUPS = 4
EXPERTS_PER_GROUP = 8
N_EXPERTS = N_GROUPS * EXPERTS_PER_GROUP
TOP_K = 2
LANES = 128
SLAB = 16
SLAB_PITCH = 20
DMA_PRIORITIES = 2

TM_FRONT = 256
TQ_ATTN = 1024
TM_OUT = 512
TM_MOE = 256
TM_PLE = 256
VMEM_LIMIT = 56 * 1024 * 1024


def _rms(x, g, n):
    ms = jnp.sum(x * x, axis=-1, keepdims=True) * (1.0 / n)
    return x * lax.rsqrt(ms + EPS) * g


def _store_slabs(ref, val, pitch, mask=None):
    n, width = val.shape
    for c in range(width // LANES):
        piece = val[:, c * LANES:(c + 1) * LANES]
        rows = pl.ds(c, n, stride=pitch)
        if mask is not None:
            piece = jnp.where(mask, piece, ref[rows, :])
        ref[rows, :] = piece


def _load_slabs(ref, start, n, slab, pitch):
    return jnp.concatenate([ref[pl.ds(start + c, n, stride=pitch), :] for c in range(slab)], axis=1)


def _front_kernel(x_ref, pos_ref, freq_ref, gmix_ref, win_ref, wrope_ref, gql_ref, gkvl_ref,
                  wuq_ref, wukv_ref, gq_ref, gk_ref, wconv_ref, gconv_ref, side_ref,
                  q_ref, k_ref, v_ref, yc_ref, side_out_ref, prev_ref, *, tiles_per_seq, q_rank, kv_rank, d_conv):
    i = pl.program_id(0)
    is_start = (i % tiles_per_seq) == 0
    tm = x_ref.shape[0]
    d_model = x_ref.shape[1]
    rope_lo = q_rank + kv_rank
    nt = (((1,), (1,)), ((), ()))

    @pl.when(is_start)
    def _():
        prev_ref[...] = jnp.zeros_like(prev_ref)

    xn = _rms(x_ref[...], gmix_ref[...], d_model).astype(BF16)
    lat = lax.dot_general(xn, win_ref[:rope_lo, :], nt, preferred_element_type=F32)
    q_lat = lat[:, :q_rank]
    kv_lat = lat[:, q_rank:]
    kr = lax.dot_general(xn, wrope_ref[...], nt, preferred_element_type=F32)

    qn = _rms(q_lat, gql_ref[...], q_rank).astype(BF16)
    kvn = _rms(kv_lat, gkvl_ref[...], kv_rank).astype(BF16)
    q_all = jnp.dot(qn, wuq_ref[...], preferred_element_type=F32)
    kv_all = jnp.dot(kvn, wukv_ref[...], preferred_element_type=F32)

    lane = lax.broadcasted_iota(jnp.int32, (tm, LANES), 1)
    ang = pos_ref[...].astype(F32) * freq_ref[...]
    cos = jnp.cos(ang)
    sin = jnp.sin(ang)
    rope_c = jnp.where(lane < QK_ROPE, cos, 0.0)
    rope_s = jnp.where(lane < QK_ROPE // 2, -sin, jnp.where(lane < QK_ROPE, sin, 0.0))
    rope_lanes = lane < QK_ROPE

    def rope(t):
        return t * rope_c + pltpu.roll(t, QK_ROPE, 1) * rope_s

    gq = gq_ref[...]
    gk = gk_ref[...]
    inv_n = 1.0 / QK_HEAD
    kr_ssq = jnp.sum(jnp.where(rope_lanes, kr * kr, 0.0), axis=-1, keepdims=True)
    kr_rot = rope(kr * gk[:, QK_NOPE:])
    for h in range(MLA_HEADS):
        qa = q_all[:, h * HEAD_PAD:h * HEAD_PAD + QK_NOPE]
        qb = q_all[:, h * HEAD_PAD + QK_NOPE:(h + 1) * HEAD_PAD]
        ssq = (jnp.sum(qa * qa, axis=-1, keepdims=True)
               + jnp.sum(jnp.where(rope_lanes, qb * qb, 0.0), axis=-1, keepdims=True))
        inv = lax.rsqrt(ssq * inv_n + EPS)
        q_ref[:, h * HEAD_PAD:h * HEAD_PAD + QK_NOPE] = (qa * inv * gq[:, :QK_NOPE] * Q_SCALE).astype(BF16)
        q_ref[:, h * HEAD_PAD + QK_NOPE:(h + 1) * HEAD_PAD] = (rope(qb * inv * gq[:, QK_NOPE:]) * Q_SCALE).astype(BF16)

        ka = kv_all[:, h * QK_NOPE:(h + 1) * QK_NOPE]
        kinv = lax.rsqrt((jnp.sum(ka * ka, axis=-1, keepdims=True) + kr_ssq) * inv_n + EPS)
        k_ref[:, h * HEAD_PAD:h * HEAD_PAD + QK_NOPE] = (ka * kinv * gk[:, :QK_NOPE]).astype(BF16)
        k_ref[:, h * HEAD_PAD + QK_NOPE:(h + 1) * HEAD_PAD] = (kr_rot * kinv).astype(BF16)
        v_lo = MLA_HEADS * QK_NOPE + h * V_HEAD
        v_ref[:, h * HEAD_PAD:h * HEAD_PAD + V_HEAD] = kv_all[:, v_lo:v_lo + V_HEAD].astype(BF16)
        v_ref[:, h * HEAD_PAD + V_HEAD:(h + 1) * HEAD_PAD] = jnp.ones((tm, HEAD_PAD - V_HEAD), BF16)

    conv_lo = rope_lo + QK_ROPE
    half_c = d_conv // 2
    row = lax.broadcasted_iota(jnp.int32, (tm, half_c), 0)
    wc = wconv_ref[...]
    ys = []
    ssq = jnp.zeros((tm, 1), F32)
    for cs in (slice(0, half_c), slice(half_c, d_conv)):
        b_gate, c_gate, h_conv = (lax.dot_general(xn, win_ref[conv_lo + k * d_conv + cs.start:conv_lo + k * d_conv + cs.stop, :],
                                                  nt, preferred_element_type=F32) for k in range(3))
        u = c_gate * h_conv
        prev = prev_ref[:, cs]
        hm1 = prev[7:8, :]
        hm2 = prev[6:7, :]
        u1 = jnp.where(row == 0, hm1, pltpu.roll(u, 1, 0))
        u2 = jnp.where(row == 0, hm2, jnp.where(row == 1, hm1, pltpu.roll(u, 2, 0)))
        prev_ref[:, cs] = u[tm - 8:, :]
        y = b_gate * (u2 * wc[0:1, cs] + u1 * wc[1:2, cs] + u * wc[2:3, cs])
        ssq = ssq + jnp.sum(y * y, axis=-1, keepdims=True)
        ys.append(y)
    inv = lax.rsqrt(ssq * (1.0 / d_conv) + EPS)
    yc_ref[...] = (jnp.concatenate(ys, axis=1) * inv * gconv_ref[...]).astype(BF16)
    side_out_ref[...] = side_ref[...].astype(BF16)


def _front(x2, pos2, freq, gmix, w_in_t, w_rope, gql, gkvl, wuq_p, wukv_p, gq_p, gk_p, wconv, gconv, side, *, seq):
    t, d_model = x2.shape
    tm = TM_FRONT
    q_rank = gql.shape[1]
    kv_rank = gkvl.shape[1]
    d_conv = gconv.shape[1]
    const = lambda i: (0, 0)
    row = lambda i: (i, 0)
    full = lambda a: pl.BlockSpec(a.shape, const)
    side_spec = pl.BlockSpec((1,) + side.shape[1:], lambda i: (i, 0, 0))
    kern = functools.partial(_front_kernel, tiles_per_seq=seq // tm, q_rank=q_rank, kv_rank=kv_rank, d_conv=d_conv)
    return pl.pallas_call(
        kern,
        grid=(t // tm,),
        in_specs=[pl.BlockSpec((tm, d_model), row), pl.BlockSpec((tm, 1), row), full(freq), full(gmix),
                  pl.BlockSpec(w_in_t.shape, const, pipeline_mode=pl.Buffered(1)), full(w_rope),
                  full(gql), full(gkvl), full(wuq_p), full(wukv_p), full(gq_p), full(gk_p), full(wconv), full(gconv), side_spec],
        out_specs=[pl.BlockSpec((tm, MLA_HEADS * HEAD_PAD), row), pl.BlockSpec((tm, MLA_HEADS * HEAD_PAD), row),
                   pl.BlockSpec((tm, MLA_HEADS * HEAD_PAD), row), pl.BlockSpec((tm, d_conv), row), side_spec],
        out_shape=[jax.ShapeDtypeStruct((t, MLA_HEADS * HEAD_PAD), BF16), jax.ShapeDtypeStruct((t, MLA_HEADS * HEAD_PAD), BF16),
                   jax.ShapeDtypeStruct((t, MLA_HEADS * HEAD_PAD), BF16), jax.ShapeDtypeStruct((t, d_conv), BF16),
                   jax.ShapeDtypeStruct(side.shape, BF16)],
        scratch_shapes=[pltpu.VMEM((8, d_conv), F32)],
        compiler_params=pltpu.CompilerParams(dimension_semantics=("arbitrary",), vmem_limit_bytes=VMEM_LIMIT),
        name="front",
    )(x2, pos2, freq, gmix, w_in_t, w_rope, gql, gkvl, wuq_p, wukv_p, gq_p, gk_p, wconv, gconv, side)


def _attn_kernel(q_ref, k_ref, v_ref, *rest):
    n_side = (len(rest) - 1) // 2
    o_ref = rest[n_side]

    def round_side_weights():
        for w_ref, wb_ref in zip(rest[:n_side], rest[n_side + 1:]):
            wb_ref[...] = w_ref[...].astype(BF16)

    i = pl.program_id(2)
    tq = q_ref.shape[0]
    half = tq // 2
    q_halves = (q_ref[:half, :], q_ref[half:, :])

    def step(start, n_keys, masks, carry):
        kvs = [(k_ref[pl.ds(start, n), :], v_ref[pl.ds(start, n), :]) for n in n_keys]
        scores = [lax.dot_general(qh, kb, (((1,), (1,)), ((), ())), preferred_element_type=F32)
                  for qh, (kb, _) in zip(q_halves, kvs)]
        scores = [s if mask is None else jnp.where(mask, s, MASK_VALUE) for s, mask in zip(scores, masks)]
        m_new = [jnp.maximum(m, jnp.max(s, axis=-1, keepdims=True)) for s, (m, _) in zip(scores, carry)]
        probs = [jnp.exp2(s - mn).astype(BF16) for s, mn in zip(scores, m_new)]
        return tuple((mn, jnp.exp2(m - mn) * acc + jnp.dot(p, vb, preferred_element_type=F32))
                     for p, mn, (m, acc), (_, vb) in zip(probs, m_new, carry, kvs))

    def chunk_mask(n, first_row):
        r = (lax.broadcasted_iota(jnp.int32, (half, n), 0) + first_row) // CHUNK
        c = lax.broadcasted_iota(jnp.int32, (half, n), 1) // CHUNK
        return c <= r

    for qi in range(k_ref.shape[0] // tq):
        @pl.when(i == qi)
        def _(qi=qi):
            round_side_weights()
            carry = tuple((jnp.full((half, 1), MASK_VALUE, F32), jnp.zeros((half, HEAD_PAD), F32)) for _ in range(2))
            for j in range(qi):
                carry = step(j * tq, (tq, tq), (None, None), carry)
            (_, acc0), (_, acc1) = step(qi * tq, (half, tq), (chunk_mask(half, 0), chunk_mask(tq, half)), carry)
            o_ref[:half, :] = acc0[:, :V_HEAD] / acc0[:, V_HEAD:]
            o_ref[half:, :] = acc1[:, :V_HEAD] / acc1[:, V_HEAD:]


def _attention(q, k, v, side_weights, *, batch, seq):
    tq = TQ_ATTN
    nq = seq // tq
    step = lambda b, h, i: ((b * MLA_HEADS + h) * nq + i, 0, 0)
    side_specs = [pl.BlockSpec((1,) + w.shape[1:], step) for w in side_weights]
    outs = pl.pallas_call(
        _attn_kernel,
        grid=(batch, MLA_HEADS, nq),
        in_specs=[pl.BlockSpec((tq, HEAD_PAD), lambda b, h, i: (b * nq + i, h)),
                  pl.BlockSpec((seq, HEAD_PAD), lambda b, h, i: (b, h)),
                  pl.BlockSpec((seq, HEAD_PAD), lambda b, h, i: (b, h))] + side_specs,
        out_specs=[pl.BlockSpec((tq, V_HEAD), lambda b, h, i: (b * nq + i, h))] + side_specs,
        out_shape=[jax.ShapeDtypeStruct((batch * seq, MLA_HEADS * V_HEAD), F32)]
                  + [jax.ShapeDtypeStruct(w.shape, BF16) for w in side_weights],
        compiler_params=pltpu.CompilerParams(dimension_semantics=("arbitrary", "arbitrary", "arbitrary"),
                                             vmem_limit_bytes=VMEM_LIMIT),
        name="attn",
    )(q, k, v, *side_weights)
    return outs[0], outs[1:]


def _outproj_kernel(ya_ref, yc_ref, x_ref, gattn_ref, wout_ref, gmoe_ref, wgr_ref, bgr_ref,
                    h1_ref, xm_ref, route_ref):
    tm, d_attn = ya_ref.shape
    d_model = x_ref.shape[1]
    ya = _rms(ya_ref[...], gattn_ref[...], d_attn).astype(BF16)
    h1 = (x_ref[...]
          + jnp.dot(ya, wout_ref[:d_attn, :], preferred_element_type=F32)
          + jnp.dot(yc_ref[...], wout_ref[d_attn:, :], preferred_element_type=F32))
    h1_ref[...] = h1
    xm = _rms(h1, gmoe_ref[...], d_model)
    _store_slabs(xm_ref, xm, SLAB_PITCH)
    for c in range(SLAB, SLAB_PITCH):
        xm_ref[pl.ds(c, tm, stride=SLAB_PITCH), :] = jnp.zeros((tm, LANES), F32)

    x_hi = xm.astype(BF16)
    x_lo = (xm - x_hi.astype(F32)).astype(BF16)
    hi_part = jnp.dot(x_hi, wgr_ref[...], preferred_element_type=F32)
    lo_part = jnp.dot(x_lo, wgr_ref[:, :LANES], preferred_element_type=F32)
    logits = hi_part[:, :LANES] + hi_part[:, LANES:] + lo_part + bgr_ref[...]
    lane = lax.broadcasted_iota(jnp.int32, (tm, LANES), 1).astype(F32)
    neg = -jnp.inf
    big = 1e9
    gl = jnp.where(lane < N_GROUPS, logits, neg)
    gmax = jnp.max(gl, axis=-1, keepdims=True)
    gsel = jnp.min(jnp.where(gl == gmax, lane, big), axis=-1, keepdims=True)
    p_g = 1.0 / jnp.sum(jnp.exp(gl - gmax), axis=-1, keepdims=True)
    lo = N_GROUPS + EXPERTS_PER_GROUP * gsel
    el = jnp.where((lane >= lo) & (lane < lo + EXPERTS_PER_GROUP), logits, neg)
    v1 = jnp.max(el, axis=-1, keepdims=True)
    i1 = jnp.min(jnp.where(el == v1, lane, big), axis=-1, keepdims=True)
    el2 = jnp.where(lane == i1, neg, el)
    v2 = jnp.max(el2, axis=-1, keepdims=True)
    i2 = jnp.min(jnp.where(el2 == v2, lane, big), axis=-1, keepdims=True)
    e2 = jnp.exp(v2 - v1)
    den = 1.0 + e2
    w1 = p_g / den
    w2 = p_g * e2 / den
    route_ref[...] = jnp.where(lane == 0, i1 - N_GROUPS,
                               jnp.where(lane == 1, i2 - N_GROUPS,
                                         jnp.where(lane == 2, w1, jnp.where(lane == 3, w2, 0.0))))


def _outproj(ya, yc, x2, gattn, w_out, gmoe, w_gr, b_gr):
    t, d_model = x2.shape
    d_attn = ya.shape[1]
    tm = TM_OUT
    const = lambda i: (0, 0)
    row = lambda i: (i, 0)
    full = lambda a: pl.BlockSpec(a.shape, const)
    return pl.pallas_call(
        _outproj_kernel,
        grid=(t // tm,),
        in_specs=[pl.BlockSpec((tm, d_attn), row), pl.BlockSpec((tm, yc.shape[1]), row), pl.BlockSpec((tm, d_model), row),
                  full(gattn), pl.BlockSpec(w_out.shape, const, pipeline_mode=pl.Buffered(1)), full(gmoe), full(w_gr), full(b_gr)],
        out_specs=[pl.BlockSpec((tm, d_model), row), pl.BlockSpec((tm * SLAB_PITCH, LANES), row), pl.BlockSpec((tm, LANES), row)],
        out_shape=[jax.ShapeDtypeStruct((t, d_model), F32), jax.ShapeDtypeStruct((t * SLAB_PITCH, LANES), F32),
                   jax.ShapeDtypeStruct((t, LANES), F32)],
        compiler_params=pltpu.CompilerParams(dimension_semantics=("arbitrary",), vmem_limit_bytes=VMEM_LIMIT),
        name="outproj",
    )(ya, yc, x2, gattn, w_out, gmoe, w_gr, b_gr)


def _row_gather_start(idx_ref, src_hbm, dst_ref, sem, n_rows, src_pitch=SLAB):
    def body(pair, carry):
        for prio in range(DMA_PRIORITIES):
            r = pair * DMA_PRIORITIES + prio
            src = src_hbm.at[pl.ds(idx_ref[0, 0, r] * src_pitch, SLAB), :]
            pltpu.make_async_copy(src, dst_ref.at[pl.ds(r * SLAB_PITCH, SLAB), :], sem).start(priority=prio)
        return carry
    lax.fori_loop(0, n_rows // DMA_PRIORITIES, body, 0, unroll=True)


def _row_gather_wait(src_hbm, dst_ref, sem, n_rows):
    pltpu.make_async_copy(src_hbm.at[pl.ds(0, n_rows * SLAB), :], dst_ref.at[pl.ds(0, n_rows * SLAB), :], sem).wait()


def _moe_kernel(tile_expert_ref, n_active_ref, first_ref, wslot_ref, next_expert_ref, tok_cur_ref, tok_next_ref,
                xm_hbm, w1_hbm, w3_hbm, w2_hbm, y_ref, xbuf_ref, w1_buf, w3_buf, w2_buf, sem_ref, wsem_ref, *, tm):
    i = pl.program_id(0)
    n_active = n_active_ref[0]
    slot = i % 2
    buf_rows = tm * SLAB_PITCH
    wslot = wslot_ref[i]

    def buf(s):
        return xbuf_ref.at[pl.ds(s * buf_rows, buf_rows), :]

    def weight_copies(e, s):
        return (pltpu.make_async_copy(w1_hbm.at[e], w1_buf.at[s], wsem_ref.at[0, s]),
                pltpu.make_async_copy(w3_hbm.at[e], w3_buf.at[s], wsem_ref.at[1, s]),
                pltpu.make_async_copy(w2_hbm.at[e], w2_buf.at[s], wsem_ref.at[2, s]))

    @pl.when(i == 0)
    def _():
        for cp in weight_copies(tile_expert_ref[0], 0):
            cp.start()
        _row_gather_start(tok_cur_ref, xm_hbm, buf(0), sem_ref.at[0], tm, src_pitch=SLAB_PITCH)

    active = i < n_active
    new_expert = active & (first_ref[i] == 1)

    @pl.when(new_expert & (next_expert_ref[i] >= 0))
    def _():
        for cp in weight_copies(next_expert_ref[i], 1 - wslot):
            cp.start()

    @pl.when(i + 1 < n_active)
    def _():
        _row_gather_start(tok_next_ref, xm_hbm, buf(1 - slot), sem_ref.at[1 - slot], tm, src_pitch=SLAB_PITCH)

    @pl.when(new_expert)
    def _():
        for cp in weight_copies(tile_expert_ref[i], wslot):
            cp.wait()

    @pl.when(active)
    def _():
        _row_gather_wait(xm_hbm, buf(slot), sem_ref.at[slot], tm)
        x = _load_slabs(xbuf_ref, slot * buf_rows, tm, SLAB, SLAB_PITCH).astype(BF16)
        a = jnp.dot(x, w1_buf[wslot], preferred_element_type=F32)
        g = jnp.dot(x, w3_buf[wslot], preferred_element_type=F32)
        hid = (a * jax.nn.sigmoid(a) * g).astype(BF16)
        _store_slabs(y_ref, jnp.dot(hid, w2_buf[wslot], preferred_element_type=F32), SLAB)

    @pl.when(i >= n_active)
    def _():
        y_ref[...] = jnp.zeros_like(y_ref)


def _moe(plan, xm, w1, w3, w2):
    tile_expert, n_active, first, wslot, next_expert, row_token3 = plan
    n_tiles = row_token3.shape[0]
    tm = TM_MOE
    _, d_model, d_exp = w1.shape
    last = n_tiles - 1
    grid_spec = pltpu.PrefetchScalarGridSpec(
        num_scalar_prefetch=5,
        grid=(n_tiles,),
        in_specs=[pl.BlockSpec((1, 1, tm), lambda i, *_: (i, 0, 0), memory_space=pltpu.SMEM),
                  pl.BlockSpec((1, 1, tm), lambda i, *_: (jnp.minimum(i + 1, last), 0, 0), memory_space=pltpu.SMEM),
                  pl.BlockSpec(memory_space=pl.ANY), pl.BlockSpec(memory_space=pl.ANY),
                  pl.BlockSpec(memory_space=pl.ANY), pl.BlockSpec(memory_space=pl.ANY)],
        out_specs=pl.BlockSpec((tm * SLAB, LANES), lambda i, *_: (i, 0)),
        scratch_shapes=[pltpu.VMEM((2 * tm * SLAB_PITCH, LANES), F32),
                        pltpu.VMEM((2, d_model, d_exp), BF16), pltpu.VMEM((2, d_model, d_exp), BF16),
                        pltpu.VMEM((2, d_exp, d_model), BF16),
                        pltpu.SemaphoreType.DMA((2,)), pltpu.SemaphoreType.DMA((3, 2))],
    )
    return pl.pallas_call(
        functools.partial(_moe_kernel, tm=tm),
        grid_spec=grid_spec,
        out_shape=jax.ShapeDtypeStruct((n_tiles * tm * SLAB, LANES), F32),
        compiler_params=pltpu.CompilerParams(dimension_semantics=("arbitrary",), vmem_limit_bytes=VMEM_LIMIT),
        name="moe",
    )(tile_expert, n_active, first, wslot, next_expert, row_token3, row_token3, xm, w1, w3, w2)


def _ple_kernel(dest_cur_ref, dest_next_ref, y_hbm, h1_ref, route_ref, p_ref, wple_ref, gple_ref, wpg_ref, bpg_ref,
                o_ref, ybuf_ref, sem_ref):
    i = pl.program_id(0)
    n = pl.num_programs(0)
    tm, d_model = h1_ref.shape
    slot = i % 2
    buf_rows = TOP_K * tm * SLAB_PITCH

    def buf(s):
        return ybuf_ref.at[pl.ds(s * buf_rows, buf_rows), :]

    @pl.when(i == 0)
    def _():
        _row_gather_start(dest_cur_ref, y_hbm, buf(0), sem_ref.at[0], TOP_K * tm)

    @pl.when(i + 1 < n)
    def _():
        _row_gather_start(dest_next_ref, y_hbm, buf(1 - slot), sem_ref.at[1 - slot], TOP_K * tm)

    _row_gather_wait(y_hbm, buf(slot), sem_ref.at[slot], TOP_K * tm)
    route = route_ref[...]
    y0 = _load_slabs(ybuf_ref, slot * buf_rows, tm, SLAB, SLAB_PITCH)
    y1 = _load_slabs(ybuf_ref, slot * buf_rows + tm * SLAB_PITCH, tm, SLAB, SLAB_PITCH)
    h2 = h1_ref[...] + (route[:, 2:3] * y0 + route[:, 3:4] * y1)
    xn = _rms(h2, gple_ref[...], d_model).astype(BF16)
    gate = jax.nn.sigmoid(jnp.dot(xn, wpg_ref[...], preferred_element_type=F32) + bpg_ref[...])
    pe = jnp.dot(p_ref[...].astype(BF16), wple_ref[...], preferred_element_type=F32)
    o_ref[...] = h2 + gate * pe


def _ple(dest3, y, h1, route, p2, w_ple, gple, w_pg, b_pg):
    t, d_model = h1.shape
    tm = TM_PLE
    n_tiles = t // tm
    last = n_tiles - 1
    const = lambda i: (0, 0)
    row = lambda i: (i, 0)
    full = lambda a: pl.BlockSpec(a.shape, const)
    return pl.pallas_call(
        _ple_kernel,
        grid=(n_tiles,),
        in_specs=[pl.BlockSpec((1, 1, TOP_K * tm), lambda i: (i, 0, 0), memory_space=pltpu.SMEM),
                  pl.BlockSpec((1, 1, TOP_K * tm), lambda i: (jnp.minimum(i + 1, last), 0, 0), memory_space=pltpu.SMEM),
                  pl.BlockSpec(memory_space=pl.ANY),
                  pl.BlockSpec((tm, d_model), row), pl.BlockSpec((tm, LANES), row), pl.BlockSpec((tm, p2.shape[1]), row),
                  full(w_ple), full(gple), pl.BlockSpec(w_pg.shape, const, pipeline_mode=pl.Buffered(1)), full(b_pg)],
        out_specs=pl.BlockSpec((tm, d_model), row),
        out_shape=jax.ShapeDtypeStruct((t, d_model), F32),
        scratch_shapes=[pltpu.VMEM((2 * TOP_K * tm * SLAB_PITCH, LANES), F32), pltpu.SemaphoreType.DMA((2,))],
        compiler_params=pltpu.CompilerParams(dimension_semantics=("arbitrary",), vmem_limit_bytes=VMEM_LIMIT),
        name="ple",
    )(dest3, dest3, y, h1, route, p2, w_ple, gple, w_pg, b_pg)


def _dispatch_plan(route):
    t = route.shape[0]
    tm = TM_MOE
    total = TOP_K * t
    n_tiles = total // tm + N_EXPERTS
    expert = route[:, :TOP_K].astype(jnp.int32).reshape(-1)
    pos = jnp.arange(total, dtype=jnp.int32)
    order = jnp.sort(expert * total + pos) % total
    rank = jnp.sort(order * total + pos) % total
    experts = jnp.arange(N_EXPERTS, dtype=jnp.int32)
    onehot = (expert[:, None] == experts[None, :]).astype(jnp.int32)
    counts = jnp.sum(onehot, axis=0)
    group_start = jnp.cumsum(counts) - counts
    tile_end = jnp.cumsum((counts + tm - 1) // tm)
    tile_start = tile_end - (counts + tm - 1) // tm
    n_active = tile_end[-1]
    shift = tile_start * tm - group_start
    dest = rank + jnp.sum(onehot * shift[None, :], axis=1)
    tiles = jnp.minimum(jnp.arange(n_tiles, dtype=jnp.int32), n_active - 1)
    tile_expert = jnp.sum((tile_end[None, :] <= tiles[:, None]).astype(jnp.int32), axis=1)
    j = jnp.arange(tm, dtype=jnp.int32)
    fill_row = (tile_start * tm + counts)[:, None] + j[None, :]
    fill_row = jnp.where(fill_row < (tile_end * tm)[:, None], fill_row, n_tiles * tm)
    fill_tok = (experts[:, None] * tm + j[None, :]) % t
    keys = jnp.concatenate([dest * t + pos // TOP_K, (fill_row * t + fill_tok).reshape(-1)])
    row_token3 = (jnp.sort(keys) % t).reshape(n_tiles, 1, tm)
    dest3 = dest.reshape(t // TM_PLE, TM_PLE, TOP_K).transpose(0, 2, 1).reshape(t // TM_PLE, 1, TOP_K * TM_PLE)
    prev_expert = jnp.concatenate([jnp.full((1,), -1, jnp.int32), tile_expert[:-1]])
    first = (tile_expert != prev_expert).astype(jnp.int32)
    nonempty = counts > 0
    wslot = (jnp.sum((nonempty[None, :] & (experts[None, :] < tile_expert[:, None])).astype(jnp.int32), axis=1)) % 2
    later = nonempty[None, :] & (experts[None, :] > tile_expert[:, None])
    next_expert = jnp.min(jnp.where(later, experts[None, :], N_EXPERTS), axis=1)
    next_expert = jnp.where(next_expert < N_EXPERTS, next_expert, -1)
    plan = (tile_expert, n_active.reshape(1), first, wslot, next_expert, row_token3)
    return plan, dest3


def _rope_swap_pad(w):
    half = QK_ROPE // 2
    x1 = w[..., QK_NOPE:QK_NOPE + half]
    x2 = w[..., QK_NOPE + half:]
    return jnp.concatenate([w[..., :QK_NOPE], x1, x2, x2, x1], axis=-1)


def kernel(x, p, positions, norm_mix, w_in, g_q_lat, g_kv_lat, w_uq, w_ukv, g_q_head, g_k_head, w_conv, g_out_attn, g_out_conv, w_out, norm_moe, w_group, b_group, w_router, b_router, w1, w3, w2, norm_ple, w_ple, w_ple_gate, b_ple_gate):
    batch, seq, d_model = x.shape
    depth = w_in.shape[0]
    t = batch * seq
    q_rank = g_q_lat.shape[1]
    kv_rank = g_kv_lat.shape[1]
    half = QK_ROPE // 2
    rope_lo = q_rank + kv_rank

    pos2 = positions.reshape(t, 1)
    inv_freq = ROPE_BASE ** (-jnp.arange(0, QK_ROPE, 2, dtype=F32) / QK_ROPE)
    freq = jnp.tile(inv_freq, LANES // half).reshape(1, LANES)

    h = x.reshape(t, d_model)
    for i in range(depth):
        w_in_t = jnp.swapaxes(w_in[i], 0, 1).astype(BF16)
        w_rope = jnp.concatenate([w_in_t[rope_lo:rope_lo + QK_ROPE], w_in_t[rope_lo + half:rope_lo + QK_ROPE],
                                  w_in_t[rope_lo:rope_lo + half]], axis=0)
        wuq_p = _rope_swap_pad(w_uq[i]).reshape(q_rank, MLA_HEADS * HEAD_PAD).astype(BF16)
        wukv_p = jnp.concatenate([w_ukv[i][:, :, :QK_NOPE].reshape(kv_rank, -1),
                                  w_ukv[i][:, :, QK_NOPE:].reshape(kv_rank, -1)], axis=1).astype(BF16)
        gq_p = _rope_swap_pad(g_q_head[i]).reshape(1, HEAD_PAD)
        gk_p = _rope_swap_pad(g_k_head[i]).reshape(1, HEAD_PAD)
        as_steps = lambda w, n: w.reshape(n, -1, w.shape[-1])
        q, k, v, yc, w1b = _front(h, pos2, freq, norm_mix[i].reshape(1, -1), w_in_t, w_rope, g_q_lat[i].reshape(1, -1),
                                  g_kv_lat[i].reshape(1, -1), wuq_p, wukv_p, gq_p, gk_p, w_conv[i],
                                  g_out_conv[i].reshape(1, -1), as_steps(w1[i], t // TM_FRONT), seq=seq)
        n_steps = batch * MLA_HEADS * (seq // TQ_ATTN)
        ya, (w3b, w2b) = _attention(q, k, v, [as_steps(w3[i], n_steps), as_steps(w2[i], n_steps)], batch=batch, seq=seq)
        w1b, w3b, w2b = w1b.reshape(w1[i].shape), w3b.reshape(w3[i].shape), w2b.reshape(w2[i].shape)

        pad = LANES - N_GROUPS - N_EXPERTS
        w_gr32 = jnp.concatenate([w_group[i], w_router[i], jnp.zeros((d_model, pad), F32)], axis=1)
        w_gr_hi = w_gr32.astype(BF16)
        w_gr = jnp.concatenate([w_gr_hi, (w_gr32 - w_gr_hi.astype(F32)).astype(BF16)], axis=1)
        b_gr = jnp.concatenate([b_group[i], b_router[i], jnp.zeros((pad,), F32)]).reshape(1, LANES)
        h1, xm, route = _outproj(ya, yc, h, g_out_attn[i].reshape(1, -1), w_out[i].astype(BF16),
                                 norm_moe[i].reshape(1, -1), w_gr, b_gr)

        plan, dest3 = _dispatch_plan(route)
        y = _moe(plan, xm, w1b, w3b, w2b)
        h = _ple(dest3, y, h1, route, p[i].reshape(t, -1), w_ple[i].astype(BF16), norm_ple[i].reshape(1, -1),
                 w_ple_gate[i].astype(BF16), b_ple_gate[i].reshape(1, -1))
    return h.reshape(batch, seq, d_model)
```

```python
import functools

import jax
import jax.numpy as jnp
from jax import lax
from jax.experimental import pallas as pl
from jax.experimental.pallas import tpu as pltpu

F32 = jnp.float32
BF16 = jnp.bfloat16

CHUNK = 64
EPS = 1e-6
MASK_VALUE = -1e30
ROPE_BASE = 10000.0

MLA_HEADS = 8
QK_NOPE = 128
QK_ROPE = 64
QK_HEAD = QK_NOPE + QK_ROPE
V_HEAD = 128
HEAD_PAD = 256
Q_SCALE = QK_HEAD ** -0.5 * 1.4426950408889634
N_GROUPS = 4
EXPERTS_PER_GROUP = 8
N_EXPERTS = N_GROUPS * EXPERTS_PER_GROUP
TOP_K = 2
LANES = 128
SLAB = 16
SLAB_PITCH = 20
DMA_PRIORITIES = 2

TM_FRONT = 256
TQ_ATTN = 1024
TM_OUT = 512
TM_MOE = 256
TM_PLE = 256
VMEM_LIMIT = 56 * 1024 * 1024


def _rms(x, g, n):
    ms = jnp.sum(x * x, axis=-1, keepdims=True) * (1.0 / n)
    return x * lax.rsqrt(ms + EPS) * g


def _store_slabs(ref, val, pitch, mask=None):
    n, width = val.shape
    for c in range(width // LANES):
        piece = val[:, c * LANES:(c + 1) * LANES]
        rows = pl.ds(c, n, stride=pitch)
        if mask is not None:
            piece = jnp.where(mask, piece, ref[rows, :])
        ref[rows, :] = piece


def _load_slabs(ref, start, n, slab, pitch):
    return jnp.concatenate([ref[pl.ds(start + c, n, stride=pitch), :] for c in range(slab)], axis=1)


def _front_kernel(x_ref, pos_ref, freq_ref, gmix_ref, win_ref, wrope_ref, gql_ref, gkvl_ref,
                  wuq_ref, wukv_ref, gq_ref, gk_ref, wconv_ref, gconv_ref, side_ref,
                  q_ref, k_ref, v_ref, yc_ref, side_out_ref, prev_ref, *, tiles_per_seq, q_rank, kv_rank, d_conv):
    i = pl.program_id(0)
    is_start = (i % tiles_per_seq) == 0
    tm = x_ref.shape[0]
    d_model = x_ref.shape[1]
    rope_lo = q_rank + kv_rank
    nt = (((1,), (1,)), ((), ()))

    @pl.when(is_start)
    def _():
        prev_ref[...] = jnp.zeros_like(prev_ref)

    xn = _rms(x_ref[...], gmix_ref[...], d_model).astype(BF16)
    lat = lax.dot_general(xn, win_ref[:rope_lo, :], nt, preferred_element_type=F32)
    q_lat = lat[:, :q_rank]
    kv_lat = lat[:, q_rank:]
    kr = lax.dot_general(xn, wrope_ref[...], nt, preferred_element_type=F32)

    qn = _rms(q_lat, gql_ref[...], q_rank).astype(BF16)
    kvn = _rms(kv_lat, gkvl_ref[...], kv_rank).astype(BF16)
    q_all = jnp.dot(qn, wuq_ref[...], preferred_element_type=F32)
    kv_all = jnp.dot(kvn, wukv_ref[...], preferred_element_type=F32)

    lane = lax.broadcasted_iota(jnp.int32, (tm, LANES), 1)
    ang = pos_ref[...].astype(F32) * freq_ref[...]
    cos = jnp.cos(ang)
    sin = jnp.sin(ang)
    rope_c = jnp.where(lane < QK_ROPE, cos, 0.0)
    rope_s = jnp.where(lane < QK_ROPE // 2, -sin, jnp.where(lane < QK_ROPE, sin, 0.0))
    rope_lanes = lane < QK_ROPE

    def rope(t):
        return t * rope_c + pltpu.roll(t, QK_ROPE, 1) * rope_s

    gq = gq_ref[...]
    gk = gk_ref[...]
    inv_n = 1.0 / QK_HEAD
    kr_ssq = jnp.sum(jnp.where(rope_lanes, kr * kr, 0.0), axis=-1, keepdims=True)
    kr_rot = rope(kr * gk[:, QK_NOPE:])
    for h in range(MLA_HEADS):
        qa = q_all[:, h * HEAD_PAD:h * HEAD_PAD + QK_NOPE]
        qb = q_all[:, h * HEAD_PAD + QK_NOPE:(h + 1) * HEAD_PAD]
        ssq = (jnp.sum(qa * qa, axis=-1, keepdims=True)
               + jnp.sum(jnp.where(rope_lanes, qb * qb, 0.0), axis=-1, keepdims=True))
        inv = lax.rsqrt(ssq * inv_n + EPS)
        q_ref[:, h * HEAD_PAD:h * HEAD_PAD + QK_NOPE] = (qa * inv * gq[:, :QK_NOPE] * Q_SCALE).astype(BF16)
        q_ref[:, h * HEAD_PAD + QK_NOPE:(h + 1) * HEAD_PAD] = (rope(qb * inv * gq[:, QK_NOPE:]) * Q_SCALE).astype(BF16)

        ka = kv_all[:, h * QK_NOPE:(h + 1) * QK_NOPE]
        kinv = lax.rsqrt((jnp.sum(ka * ka, axis=-1, keepdims=True) + kr_ssq) * inv_n + EPS)
        k_ref[:, h * HEAD_PAD:h * HEAD_PAD + QK_NOPE] = (ka * kinv * gk[:, :QK_NOPE]).astype(BF16)
        k_ref[:, h * HEAD_PAD + QK_NOPE:(h + 1) * HEAD_PAD] = (kr_rot * kinv).astype(BF16)
        v_lo = MLA_HEADS * QK_NOPE + h * V_HEAD
        v_ref[:, h * HEAD_PAD:h * HEAD_PAD + V_HEAD] = kv_all[:, v_lo:v_lo + V_HEAD].astype(BF16)
        v_ref[:, h * HEAD_PAD + V_HEAD:(h + 1) * HEAD_PAD] = jnp.ones((tm, HEAD_PAD - V_HEAD), BF16)

    conv_lo = rope_lo + QK_ROPE
    half_c = d_conv // 2
    row = lax.broadcasted_iota(jnp.int32, (tm, half_c), 0)
    wc = wconv_ref[...]
    ys = []
    ssq = jnp.zeros((tm, 1), F32)
    for cs in (slice(0, half_c), slice(half_c, d_conv)):
        b_gate, c_gate, h_conv = (lax.dot_general(xn, win_ref[conv_lo + k * d_conv + cs.start:conv_lo + k * d_conv + cs.stop, :],
                                                  nt, preferred_element_type=F32) for k in range(3))
        u = c_gate * h_conv
        prev = prev_ref[:, cs]
        hm1 = prev[7:8, :]
        hm2 = prev[6:7, :]
        u1 = jnp.where(row == 0, hm1, pltpu.roll(u, 1, 0))
        u2 = jnp.where(row == 0, hm2, jnp.where(row == 1, hm1, pltpu.roll(u, 2, 0)))
        prev_ref[:, cs] = u[tm - 8:, :]
        y = b_gate * (u2 * wc[0:1, cs] + u1 * wc[1:2, cs] + u * wc[2:3, cs])
        ssq = ssq + jnp.sum(y * y, axis=-1, keepdims=True)
        ys.append(y)
    inv = lax.rsqrt(ssq * (1.0 / d_conv) + EPS)
    yc_ref[...] = (jnp.concatenate(ys, axis=1) * inv * gconv_ref[...]).astype(BF16)
    side_out_ref[...] = side_ref[...].astype(BF16)


def _front(x2, pos2, freq, gmix, w_in_t, w_rope, gql, gkvl, wuq_p, wukv_p, gq_p, gk_p, wconv, gconv, side, *, seq):
    t, d_model = x2.shape
    tm = TM_FRONT
    q_rank = gql.shape[1]
    kv_rank = gkvl.shape[1]
    d_conv = gconv.shape[1]
    const = lambda i: (0, 0)
    row = lambda i: (i, 0)
    full = lambda a: pl.BlockSpec(a.shape, const)
    side_spec = pl.BlockSpec((1,) + side.shape[1:], lambda i: (i, 0, 0))
    kern = functools.partial(_front_kernel, tiles_per_seq=seq // tm, q_rank=q_rank, kv_rank=kv_rank, d_conv=d_conv)
    return pl.pallas_call(
        kern,
        grid=(t // tm,),
        in_specs=[pl.BlockSpec((tm, d_model), row), pl.BlockSpec((tm, 1), row), full(freq), full(gmix),
                  pl.BlockSpec(w_in_t.shape, const, pipeline_mode=pl.Buffered(1)), full(w_rope),
                  full(gql), full(gkvl), full(wuq_p), full(wukv_p), full(gq_p), full(gk_p), full(wconv), full(gconv), side_spec],
        out_specs=[pl.BlockSpec((tm, MLA_HEADS * HEAD_PAD), row), pl.BlockSpec((tm, MLA_HEADS * HEAD_PAD), row),
                   pl.BlockSpec((tm, MLA_HEADS * HEAD_PAD), row), pl.BlockSpec((tm, d_conv), row), side_spec],
        out_shape=[jax.ShapeDtypeStruct((t, MLA_HEADS * HEAD_PAD), BF16), jax.ShapeDtypeStruct((t, MLA_HEADS * HEAD_PAD), BF16),
                   jax.ShapeDtypeStruct((t, MLA_HEADS * HEAD_PAD), BF16), jax.ShapeDtypeStruct((t, d_conv), BF16),
                   jax.ShapeDtypeStruct(side.shape, BF16)],
        scratch_shapes=[pltpu.VMEM((8, d_conv), F32)],
        compiler_params=pltpu.CompilerParams(dimension_semantics=("arbitrary",), vmem_limit_bytes=VMEM_LIMIT),
        name="front",
    )(x2, pos2, freq, gmix, w_in_t, w_rope, gql, gkvl, wuq_p, wukv_p, gq_p, gk_p, wconv, gconv, side)


def _attn_kernel(q_ref, k_ref, v_ref, *rest):
    n_side = (len(rest) - 1) // 2
    o_ref = rest[n_side]

    def round_side_weights():
        for w_ref, wb_ref in zip(rest[:n_side], rest[n_side + 1:]):
            wb_ref[...] = w_ref[...].astype(BF16)

    i = pl.program_id(2)
    tq = q_ref.shape[0]
    half = tq // 2
    q_halves = (q_ref[:half, :], q_ref[half:, :])

    def step(start, n_keys, masks, carry):
        kvs = [(k_ref[pl.ds(start, n), :], v_ref[pl.ds(start, n), :]) for n in n_keys]
        scores = [lax.dot_general(qh, kb, (((1,), (1,)), ((), ())), preferred_element_type=F32)
                  for qh, (kb, _) in zip(q_halves, kvs)]
        scores = [s if mask is None else jnp.where(mask, s, MASK_VALUE) for s, mask in zip(scores, masks)]
        m_new = [jnp.maximum(m, jnp.max(s, axis=-1, keepdims=True)) for s, (m, _) in zip(scores, carry)]
        probs = [jnp.exp2(s - mn).astype(BF16) for s, mn in zip(scores, m_new)]
        return tuple((mn, jnp.exp2(m - mn) * acc + jnp.dot(p, vb, preferred_element_type=F32))
                     for p, mn, (m, acc), (_, vb) in zip(probs, m_new, carry, kvs))

    def chunk_mask(n, first_row):
        r = (lax.broadcasted_iota(jnp.int32, (half, n), 0) + first_row) // CHUNK
        c = lax.broadcasted_iota(jnp.int32, (half, n), 1) // CHUNK
        return c <= r

    for qi in range(k_ref.shape[0] // tq):
        @pl.when(i == qi)
        def _(qi=qi):
            round_side_weights()
            carry = tuple((jnp.full((half, 1), MASK_VALUE, F32), jnp.zeros((half, HEAD_PAD), F32)) for _ in range(2))
            for j in range(qi):
                carry = step(j * tq, (tq, tq), (None, None), carry)
            (_, acc0), (_, acc1) = step(qi * tq, (half, tq), (chunk_mask(half, 0), chunk_mask(tq, half)), carry)
            o_ref[:half, :] = acc0[:, :V_HEAD] / acc0[:, V_HEAD:]
            o_ref[half:, :] = acc1[:, :V_HEAD] / acc1[:, V_HEAD:]


def _attention(q, k, v, side_weights, *, batch, seq):
    tq = TQ_ATTN
    nq = seq // tq
    step = lambda b, h, i: ((b * MLA_HEADS + h) * nq + i, 0, 0)
    side_specs = [pl.BlockSpec((1,) + w.shape[1:], step) for w in side_weights]
    outs = pl.pallas_call(
        _attn_kernel,
        grid=(batch, MLA_HEADS, nq),
        in_specs=[pl.BlockSpec((tq, HEAD_PAD), lambda b, h, i: (b * nq + i, h)),
                  pl.BlockSpec((seq, HEAD_PAD), lambda b, h, i: (b, h)),
                  pl.BlockSpec((seq, HEAD_PAD), lambda b, h, i: (b, h))] + side_specs,
        out_specs=[pl.BlockSpec((tq, V_HEAD), lambda b, h, i: (b * nq + i, h))] + side_specs,
        out_shape=[jax.ShapeDtypeStruct((batch * seq, MLA_HEADS * V_HEAD), F32)]
                  + [jax.ShapeDtypeStruct(w.shape, BF16) for w in side_weights],
        compiler_params=pltpu.CompilerParams(dimension_semantics=("arbitrary", "arbitrary", "arbitrary"),
                                             vmem_limit_bytes=VMEM_LIMIT),
        name="attn",
    )(q, k, v, *side_weights)
    return outs[0], outs[1:]


def _outproj_kernel(ya_ref, yc_ref, x_ref, gattn_ref, wout_ref, gmoe_ref, wgr_ref, bgr_ref,
                    h1_ref, xm_ref, route_ref):
    tm, d_attn = ya_ref.shape
    d_model = x_ref.shape[1]
    ya = _rms(ya_ref[...], gattn_ref[...], d_attn).astype(BF16)
    h1 = (x_ref[...]
          + jnp.dot(ya, wout_ref[:d_attn, :], preferred_element_type=F32)
          + jnp.dot(yc_ref[...], wout_ref[d_attn:, :], preferred_element_type=F32))
    h1_ref[...] = h1
    xm = _rms(h1, gmoe_ref[...], d_model)
    _store_slabs(xm_ref, xm, SLAB_PITCH)
    for c in range(SLAB, SLAB_PITCH):
        xm_ref[pl.ds(c, tm, stride=SLAB_PITCH), :] = jnp.zeros((tm, LANES), F32)

    x_hi = xm.astype(BF16)
    x_lo = (xm - x_hi.astype(F32)).astype(BF16)
    hi_part = jnp.dot(x_hi, wgr_ref[...], preferred_element_type=F32)
    lo_part = jnp.dot(x_lo, wgr_ref[:, :LANES], preferred_element_type=F32)
    logits = hi_part[:, :LANES] + hi_part[:, LANES:] + lo_part + bgr_ref[...]
    lane = lax.broadcasted_iota(jnp.int32, (tm, LANES), 1).astype(F32)
    neg = -jnp.inf
    big = 1e9
    gl = jnp.where(lane < N_GROUPS, logits, neg)
    gmax = jnp.max(gl, axis=-1, keepdims=True)
    gsel = jnp.min(jnp.where(gl == gmax, lane, big), axis=-1, keepdims=True)
    p_g = 1.0 / jnp.sum(jnp.exp(gl - gmax), axis=-1, keepdims=True)
    lo = N_GROUPS + EXPERTS_PER_GROUP * gsel
    el = jnp.where((lane >= lo) & (lane < lo + EXPERTS_PER_GROUP), logits, neg)
    v1 = jnp.max(el, axis=-1, keepdims=True)
    i1 = jnp.min(jnp.where(el == v1, lane, big), axis=-1, keepdims=True)
    el2 = jnp.where(lane == i1, neg, el)
    v2 = jnp.max(el2, axis=-1, keepdims=True)
    i2 = jnp.min(jnp.where(el2 == v2, lane, big), axis=-1, keepdims=True)
    e2 = jnp.exp(v2 - v1)
    den = 1.0 + e2
    w1 = p_g / den
    w2 = p_g * e2 / den
    route_ref[...] = jnp.where(lane == 0, i1 - N_GROUPS,
                               jnp.where(lane == 1, i2 - N_GROUPS,
                                         jnp.where(lane == 2, w1, jnp.where(lane == 3, w2, 0.0))))


def _outproj(ya, yc, x2, gattn, w_out, gmoe, w_gr, b_gr):
    t, d_model = x2.shape
    d_attn = ya.shape[1]
    tm = TM_OUT
    const = lambda i: (0, 0)
    row = lambda i: (i, 0)
    full = lambda a: pl.BlockSpec(a.shape, const)
    return pl.pallas_call(
        _outproj_kernel,
        grid=(t // tm,),
        in_specs=[pl.BlockSpec((tm, d_attn), row), pl.BlockSpec((tm, yc.shape[1]), row), pl.BlockSpec((tm, d_model), row),
                  full(gattn), pl.BlockSpec(w_out.shape, const, pipeline_mode=pl.Buffered(1)), full(gmoe), full(w_gr), full(b_gr)],
        out_specs=[pl.BlockSpec((tm, d_model), row), pl.BlockSpec((tm * SLAB_PITCH, LANES), row), pl.BlockSpec((tm, LANES), row)],
        out_shape=[jax.ShapeDtypeStruct((t, d_model), F32), jax.ShapeDtypeStruct((t * SLAB_PITCH, LANES), F32),
                   jax.ShapeDtypeStruct((t, LANES), F32)],
        compiler_params=pltpu.CompilerParams(dimension_semantics=("arbitrary",), vmem_limit_bytes=VMEM_LIMIT),
        name="outproj",
    )(ya, yc, x2, gattn, w_out, gmoe, w_gr, b_gr)


def _row_gather_start(idx_ref, src_hbm, dst_ref, sem, n_rows, src_pitch=SLAB):
    def body(pair, carry):
        for prio in range(DMA_PRIORITIES):
            r = pair * DMA_PRIORITIES + prio
            src = src_hbm.at[pl.ds(idx_ref[0, 0, r] * src_pitch, SLAB), :]
            pltpu.make_async_copy(src, dst_ref.at[pl.ds(r * SLAB_PITCH, SLAB), :], sem).start(priority=prio)
        return carry
    lax.fori_loop(0, n_rows // DMA_PRIORITIES, body, 0, unroll=True)


def _row_gather_wait(src_hbm, dst_ref, sem, n_rows):
    pltpu.make_async_copy(src_hbm.at[pl.ds(0, n_rows * SLAB), :], dst_ref.at[pl.ds(0, n_rows * SLAB), :], sem).wait()


def _moe_kernel(tile_expert_ref, n_active_ref, first_ref, wslot_ref, next_expert_ref, tok_cur_ref, tok_next_ref,
                xm_hbm, w1_hbm, w3_hbm, w2_hbm, y_ref, xbuf_ref, w1_buf, w3_buf, w2_buf, sem_ref, wsem_ref, *, tm):
    i = pl.program_id(0)
    n_active = n_active_ref[0]
    slot = i % 2
    buf_rows = tm * SLAB_PITCH
    wslot = wslot_ref[i]

    def buf(s):
        return xbuf_ref.at[pl.ds(s * buf_rows, buf_rows), :]

    def weight_copies(e, s):
        return (pltpu.make_async_copy(w1_hbm.at[e], w1_buf.at[s], wsem_ref.at[0, s]),
                pltpu.make_async_copy(w3_hbm.at[e], w3_buf.at[s], wsem_ref.at[1, s]),
                pltpu.make_async_copy(w2_hbm.at[e], w2_buf.at[s], wsem_ref.at[2, s]))

    @pl.when(i == 0)
    def _():
        for cp in weight_copies(tile_expert_ref[0], 0):
            cp.start()
        _row_gather_start(tok_cur_ref, xm_hbm, buf(0), sem_ref.at[0], tm, src_pitch=SLAB_PITCH)

    active = i < n_active
    new_expert = active & (first_ref[i] == 1)

    @pl.when(new_expert & (next_expert_ref[i] >= 0))
    def _():
        for cp in weight_copies(next_expert_ref[i], 1 - wslot):
            cp.start()

    @pl.when(i + 1 < n_active)
    def _():
        _row_gather_start(tok_next_ref, xm_hbm, buf(1 - slot), sem_ref.at[1 - slot], tm, src_pitch=SLAB_PITCH)

    @pl.when(new_expert)
    def _():
        for cp in weight_copies(tile_expert_ref[i], wslot):
            cp.wait()

    @pl.when(active)
    def _():
        _row_gather_wait(xm_hbm, buf(slot), sem_ref.at[slot], tm)
        x = _load_slabs(xbuf_ref, slot * buf_rows, tm, SLAB, SLAB_PITCH).astype(BF16)
        a = jnp.dot(x, w1_buf[wslot].astype(BF16), preferred_element_type=F32)
        g = jnp.dot(x, w3_buf[wslot].astype(BF16), preferred_element_type=F32)
        hid = (a * jax.nn.sigmoid(a) * g).astype(BF16)
        _store_slabs(y_ref, jnp.dot(hid, w2_buf[wslot].astype(BF16), preferred_element_type=F32), SLAB)

    @pl.when(i >= n_active)
    def _():
        y_ref[...] = jnp.zeros_like(y_ref)


def _moe(plan, xm, w1, w3, w2):
    tile_expert, n_active, first, wslot, next_expert, row_token3 = plan
    n_tiles = row_token3.shape[0]
    tm = TM_MOE
    _, d_model, d_exp = w1.shape
    last = n_tiles - 1
    grid_spec = pltpu.PrefetchScalarGridSpec(
        num_scalar_prefetch=5,
        grid=(n_tiles,),
        in_specs=[pl.BlockSpec((1, 1, tm), lambda i, *_: (i, 0, 0), memory_space=pltpu.SMEM),
                  pl.BlockSpec((1, 1, tm), lambda i, *_: (jnp.minimum(i + 1, last), 0, 0), memory_space=pltpu.SMEM),
                  pl.BlockSpec(memory_space=pl.ANY), pl.BlockSpec(memory_space=pl.ANY),
                  pl.BlockSpec(memory_space=pl.ANY), pl.BlockSpec(memory_space=pl.ANY)],
        out_specs=pl.BlockSpec((tm * SLAB, LANES), lambda i, *_: (i, 0)),
        scratch_shapes=[pltpu.VMEM((2 * tm * SLAB_PITCH, LANES), F32),
                        pltpu.VMEM((2, d_model, d_exp), w1.dtype), pltpu.VMEM((2, d_model, d_exp), w3.dtype),
                        pltpu.VMEM((2, d_exp, d_model), w2.dtype),
                        pltpu.SemaphoreType.DMA((2,)), pltpu.SemaphoreType.DMA((3, 2))],
    )
    return pl.pallas_call(
        functools.partial(_moe_kernel, tm=tm),
        grid_spec=grid_spec,
        out_shape=jax.ShapeDtypeStruct((n_tiles * tm * SLAB, LANES), F32),
        compiler_params=pltpu.CompilerParams(dimension_semantics=("arbitrary",), vmem_limit_bytes=VMEM_LIMIT),
        name="moe",
    )(tile_expert, n_active, first, wslot, next_expert, row_token3, row_token3, xm, w1, w3, w2)


def _ple_kernel(dest_cur_ref, dest_next_ref, y_hbm, h1_ref, route_ref, p_ref, wple_ref, gple_ref, wpg_ref, bpg_ref,
                o_ref, ybuf_ref, sem_ref):
    i = pl.program_id(0)
    n = pl.num_programs(0)
    tm, d_model = h1_ref.shape
    slot = i % 2
    buf_rows = TOP_K * tm * SLAB_PITCH

    def buf(s):
        return ybuf_ref.at[pl.ds(s * buf_rows, buf_rows), :]

    @pl.when(i == 0)
    def _():
        _row_gather_start(dest_cur_ref, y_hbm, buf(0), sem_ref.at[0], TOP_K * tm)

    @pl.when(i + 1 < n)
    def _():
        _row_gather_start(dest_next_ref, y_hbm, buf(1 - slot), sem_ref.at[1 - slot], TOP_K * tm)

    _row_gather_wait(y_hbm, buf(slot), sem_ref.at[slot], TOP_K * tm)
    route = route_ref[...]
    y0 = _load_slabs(ybuf_ref, slot * buf_rows, tm, SLAB, SLAB_PITCH)
    y1 = _load_slabs(ybuf_ref, slot * buf_rows + tm * SLAB_PITCH, tm, SLAB, SLAB_PITCH)
    h2 = h1_ref[...] + (route[:, 2:3] * y0 + route[:, 3:4] * y1)
    xn = _rms(h2, gple_ref[...], d_model).astype(BF16)
    gate = jax.nn.sigmoid(jnp.dot(xn, wpg_ref[...], preferred_element_type=F32) + bpg_ref[...])
    pe = jnp.dot(p_ref[...].astype(BF16), wple_ref[...], preferred_element_type=F32)
    o_ref[...] = h2 + gate * pe


def _ple(dest3, y, h1, route, p2, w_ple, gple, w_pg, b_pg):
    t, d_model = h1.shape
    tm = TM_PLE
    n_tiles = t // tm
    last = n_tiles - 1
    const = lambda i: (0, 0)
    row = lambda i: (i, 0)
    full = lambda a: pl.BlockSpec(a.shape, const)
    return pl.pallas_call(
        _ple_kernel,
        grid=(n_tiles,),
        in_specs=[pl.BlockSpec((1, 1, TOP_K * tm), lambda i: (i, 0, 0), memory_space=pltpu.SMEM),
                  pl.BlockSpec((1, 1, TOP_K * tm), lambda i: (jnp.minimum(i + 1, last), 0, 0), memory_space=pltpu.SMEM),
                  pl.BlockSpec(memory_space=pl.ANY),
                  pl.BlockSpec((tm, d_model), row), pl.BlockSpec((tm, LANES), row), pl.BlockSpec((tm, p2.shape[1]), row),
                  full(w_ple), full(gple), pl.BlockSpec(w_pg.shape, const, pipeline_mode=pl.Buffered(1)), full(b_pg)],
        out_specs=pl.BlockSpec((tm, d_model), row),
        out_shape=jax.ShapeDtypeStruct((t, d_model), F32),
        scratch_shapes=[pltpu.VMEM((2 * TOP_K * tm * SLAB_PITCH, LANES), F32), pltpu.SemaphoreType.DMA((2,))],
        compiler_params=pltpu.CompilerParams(dimension_semantics=("arbitrary",), vmem_limit_bytes=VMEM_LIMIT),
        name="ple",
    )(dest3, dest3, y, h1, route, p2, w_ple, gple, w_pg, b_pg)


def _dispatch_plan(route):
    t = route.shape[0]
    tm = TM_MOE
    total = TOP_K * t
    n_tiles = total // tm + N_EXPERTS
    expert = route[:, :TOP_K].astype(jnp.int32).reshape(-1)
    pos = jnp.arange(total, dtype=jnp.int32)
    order = jnp.sort(expert * total + pos) % total
    rank = jnp.sort(order * total + pos) % total
    experts = jnp.arange(N_EXPERTS, dtype=jnp.int32)
    onehot = (expert[:, None] == experts[None, :]).astype(jnp.int32)
    counts = jnp.sum(onehot, axis=0)
    group_start = jnp.cumsum(counts) - counts
    tile_end = jnp.cumsum((counts + tm - 1) // tm)
    tile_start = tile_end - (counts + tm - 1) // tm
    n_active = tile_end[-1]
    shift = tile_start * tm - group_start
    dest = rank + jnp.sum(onehot * shift[None, :], axis=1)
    tiles = jnp.minimum(jnp.arange(n_tiles, dtype=jnp.int32), n_active - 1)
    tile_expert = jnp.sum((tile_end[None, :] <= tiles[:, None]).astype(jnp.int32), axis=1)
    j = jnp.arange(tm, dtype=jnp.int32)
    fill_row = (tile_start * tm + counts)[:, None] + j[None, :]
    fill_row = jnp.where(fill_row < (tile_end * tm)[:, None], fill_row, n_tiles * tm)
    fill_tok = (experts[:, None] * tm + j[None, :]) % t
    keys = jnp.concatenate([dest * t + pos // TOP_K, (fill_row * t + fill_tok).reshape(-1)])
    row_token3 = (jnp.sort(keys) % t).reshape(n_tiles, 1, tm)
    dest3 = dest.reshape(t // TM_PLE, TM_PLE, TOP_K).transpose(0, 2, 1).reshape(t // TM_PLE, 1, TOP_K * TM_PLE)
    prev_expert = jnp.concatenate([jnp.full((1,), -1, jnp.int32), tile_expert[:-1]])
    first = (tile_expert != prev_expert).astype(jnp.int32)
    nonempty = counts > 0
    wslot = (jnp.sum((nonempty[None, :] & (experts[None, :] < tile_expert[:, None])).astype(jnp.int32), axis=1)) % 2
    later = nonempty[None, :] & (experts[None, :] > tile_expert[:, None])
    next_expert = jnp.min(jnp.where(later, experts[None, :], N_EXPERTS), axis=1)
    next_expert = jnp.where(next_expert < N_EXPERTS, next_expert, -1)
    plan = (tile_expert, n_active.reshape(1), first, wslot, next_expert, row_token3)
    return plan, dest3


def _rope_swap_pad(w):
    half = QK_ROPE // 2
    x1 = w[..., QK_NOPE:QK_NOPE + half]
    x2 = w[..., QK_NOPE + half:]
    return jnp.concatenate([w[..., :QK_NOPE], x1, x2, x2, x1], axis=-1)


def kernel(x, p, positions, norm_mix, w_in, g_q_lat, g_kv_lat, w_uq, w_ukv, g_q_head, g_k_head, w_conv, g_out_attn, g_out_conv, w_out, norm_moe, w_group, b_group, w_router, b_router, w1, w3, w2, norm_ple, w_ple, w_ple_gate, b_ple_gate):
    batch, seq, d_model = x.shape
    depth = w_in.shape[0]
    t = batch * seq
    q_rank = g_q_lat.shape[1]
    kv_rank = g_kv_lat.shape[1]
    half = QK_ROPE // 2
    rope_lo = q_rank + kv_rank

    pos2 = positions.reshape(t, 1)
    inv_freq = ROPE_BASE ** (-jnp.arange(0, QK_ROPE, 2, dtype=F32) / QK_ROPE)
    freq = jnp.tile(inv_freq, LANES // half).reshape(1, LANES)

    h = x.reshape(t, d_model)
    for i in range(depth):
        w_in_t = jnp.swapaxes(w_in[i], 0, 1).astype(BF16)
        w_rope = jnp.concatenate([w_in_t[rope_lo:rope_lo + QK_ROPE], w_in_t[rope_lo + half:rope_lo + QK_ROPE],
                                  w_in_t[rope_lo:rope_lo + half]], axis=0)
        wuq_p = _rope_swap_pad(w_uq[i]).reshape(q_rank, MLA_HEADS * HEAD_PAD).astype(BF16)
        wukv_p = jnp.concatenate([w_ukv[i][:, :, :QK_NOPE].reshape(kv_rank, -1),
                                  w_ukv[i][:, :, QK_NOPE:].reshape(kv_rank, -1)], axis=1).astype(BF16)
        gq_p = _rope_swap_pad(g_q_head[i]).reshape(1, HEAD_PAD)
        gk_p = _rope_swap_pad(g_k_head[i]).reshape(1, HEAD_PAD)
        as_steps = lambda w, n: w.reshape(n, -1, w.shape[-1])
        q, k, v, yc, w1b = _front(h, pos2, freq, norm_mix[i].reshape(1, -1), w_in_t, w_rope, g_q_lat[i].reshape(1, -1),
                                  g_kv_lat[i].reshape(1, -1), wuq_p, wukv_p, gq_p, gk_p, w_conv[i],
                                  g_out_conv[i].reshape(1, -1), as_steps(w1[i], t // TM_FRONT), seq=seq)
        n_steps = batch * MLA_HEADS * (seq // TQ_ATTN)
        ya, (w3b,) = _attention(q, k, v, [as_steps(w3[i], n_steps)], batch=batch, seq=seq)
        w1b, w3b = w1b.reshape(w1[i].shape), w3b.reshape(w3[i].shape)

        pad = LANES - N_GROUPS - N_EXPERTS
        w_gr32 = jnp.concatenate([w_group[i], w_router[i], jnp.zeros((d_model, pad), F32)], axis=1)
        w_gr_hi = w_gr32.astype(BF16)
        w_gr = jnp.concatenate([w_gr_hi, (w_gr32 - w_gr_hi.astype(F32)).astype(BF16)], axis=1)
        b_gr = jnp.concatenate([b_group[i], b_router[i], jnp.zeros((pad,), F32)]).reshape(1, LANES)
        h1, xm, route = _outproj(ya, yc, h, g_out_attn[i].reshape(1, -1), w_out[i].astype(BF16),
                                 norm_moe[i].reshape(1, -1), w_gr, b_gr)

        plan, dest3 = _dispatch_plan(route)
        y = _moe(plan, xm, w1b, w3b, w2[i])
        h = _ple(dest3, y, h1, route, p[i].reshape(t, -1), w_ple[i].astype(BF16), norm_ple[i].reshape(1, -1),
                 w_ple_gate[i].astype(BF16), b_ple_gate[i].reshape(1, -1))
    return h.reshape(batch, seq, d_model)
```

```python
import functools

import jax
import jax.numpy as jnp
from jax import lax
from jax.experimental import pallas as pl
from jax.experimental.pallas import tpu as pltpu

F32 = jnp.float32
BF16 = jnp.bfloat16

CHUNK = 64
EPS = 1e-6
MASK_VALUE = -1e30
ROPE_BASE = 10000.0

MLA_HEADS = 8
QK_NOPE = 128
QK_ROPE = 64
QK_HEAD = QK_NOPE + QK_ROPE
V_HEAD = 128
HEAD_PAD = 256
Q_SCALE = QK_HEAD ** -0.5 * 1.4426950408889634
N_GROUPS = 4
EXPERTS_PER_GROUP = 8
N_EXPERTS = N_GROUPS * EXPERTS_PER_GROUP
TOP_K = 2
LANES = 128
SLAB = 16
SLAB_PITCH = 20
DMA_PRIORITIES = 2

TM_FRONT = 256
TQ_ATTN = 1024
TM_OUT = 512
TM_MOE = 256
TM_PLE = 256
VMEM_LIMIT = 56 * 1024 * 1024


def _rms(x, g, n):
    ms = jnp.sum(x * x, axis=-1, keepdims=True) * (1.0 / n)
    return x * lax.rsqrt(ms + EPS) * g


def _store_slabs(ref, val, pitch, mask=None):
    n, width = val.shape
    for c in range(width // LANES):
        piece = val[:, c * LANES:(c + 1) * LANES]
        rows = pl.ds(c, n, stride=pitch)
        if mask is not None:
            piece = jnp.where(mask, piece, ref[rows, :])
        ref[rows, :] = piece


def _load_slabs(ref, start, n, slab, pitch):
    return jnp.concatenate([ref[pl.ds(start + c, n, stride=pitch), :] for c in range(slab)], axis=1)


def _front_kernel(x_ref, pos_ref, freq_ref, gmix_ref, win_ref, wrope_ref, gql_ref, gkvl_ref,
                  wuq_ref, wukv_ref, gq_ref, gk_ref, wconv_ref, gconv_ref, side_ref,
                  q_ref, k_ref, v_ref, yc_ref, side_out_ref, prev_ref, *, tiles_per_seq, q_rank, kv_rank, d_conv):
    i = pl.program_id(0)
    is_start = (i % tiles_per_seq) == 0
    tm = x_ref.shape[0]
    d_model = x_ref.shape[1]
    rope_lo = q_rank + kv_rank
    nt = (((1,), (1,)), ((), ()))

    @pl.when(is_start)
    def _():
        prev_ref[...] = jnp.zeros_like(prev_ref)

    xn = _rms(x_ref[...], gmix_ref[...], d_model).astype(BF16)
    lat = lax.dot_general(xn, win_ref[:rope_lo, :], nt, preferred_element_type=F32)
    q_lat = lat[:, :q_rank]
    kv_lat = lat[:, q_rank:]
    kr = lax.dot_general(xn, wrope_ref[...], nt, preferred_element_type=F32)

    qn = _rms(q_lat, gql_ref[...], q_rank).astype(BF16)
    kvn = _rms(kv_lat, gkvl_ref[...], kv_rank).astype(BF16)
    q_all = jnp.dot(qn, wuq_ref[...], preferred_element_type=F32)
    kv_all = jnp.dot(kvn, wukv_ref[...], preferred_element_type=F32)

    lane = lax.broadcasted_iota(jnp.int32, (tm, LANES), 1)
    ang = pos_ref[...].astype(F32) * freq_ref[...]
    cos = jnp.cos(ang)
    sin = jnp.sin(ang)
    rope_c = jnp.where(lane < QK_ROPE, cos, 0.0)
    rope_s = jnp.where(lane < QK_ROPE // 2, -sin, jnp.where(lane < QK_ROPE, sin, 0.0))
    rope_lanes = lane < QK_ROPE

    def rope(t):
        return t * rope_c + pltpu.roll(t, QK_ROPE, 1) * rope_s

    gq = gq_ref[...]
    gk = gk_ref[...]
    inv_n = 1.0 / QK_HEAD
    kr_ssq = jnp.sum(jnp.where(rope_lanes, kr * kr, 0.0), axis=-1, keepdims=True)
    kr_rot = rope(kr * gk[:, QK_NOPE:])
    for h in range(MLA_HEADS):
        qa = q_all[:, h * HEAD_PAD:h * HEAD_PAD + QK_NOPE]
        qb = q_all[:, h * HEAD_PAD + QK_NOPE:(h + 1) * HEAD_PAD]
        ssq = (jnp.sum(qa * qa, axis=-1, keepdims=True)
               + jnp.sum(jnp.where(rope_lanes, qb * qb, 0.0), axis=-1, keepdims=True))
        inv = lax.rsqrt(ssq * inv_n + EPS)
        q_ref[:, h * HEAD_PAD:h * HEAD_PAD + QK_NOPE] = (qa * inv * gq[:, :QK_NOPE] * Q_SCALE).astype(BF16)
        q_ref[:, h * HEAD_PAD + QK_NOPE:(h + 1) * HEAD_PAD] = (rope(qb * inv * gq[:, QK_NOPE:]) * Q_SCALE).astype(BF16)

        ka = kv_all[:, h * QK_NOPE:(h + 1) * QK_NOPE]
        kinv = lax.rsqrt((jnp.sum(ka * ka, axis=-1, keepdims=True) + kr_ssq) * inv_n + EPS)
        k_ref[:, h * HEAD_PAD:h * HEAD_PAD + QK_NOPE] = (ka * kinv * gk[:, :QK_NOPE]).astype(BF16)
        k_ref[:, h * HEAD_PAD + QK_NOPE:(h + 1) * HEAD_PAD] = (kr_rot * kinv).astype(BF16)
    v_ref[...] = kv_all[:, MLA_HEADS * QK_NOPE:].astype(BF16)

    conv_lo = rope_lo + QK_ROPE
    half_c = d_conv // 2
    row = lax.broadcasted_iota(jnp.int32, (tm, half_c), 0)
    wc = wconv_ref[...]
    ys = []
    ssq = jnp.zeros((tm, 1), F32)
    for cs in (slice(0, half_c), slice(half_c, d_conv)):
        b_gate, c_gate, h_conv = (lax.dot_general(xn, win_ref[conv_lo + k * d_conv + cs.start:conv_lo + k * d_conv + cs.stop, :],
                                                  nt, preferred_element_type=F32) for k in range(3))
        u = c_gate * h_conv
        prev = prev_ref[:, cs]
        hm1 = prev[7:8, :]
        hm2 = prev[6:7, :]
        u1 = jnp.where(row == 0, hm1, pltpu.roll(u, 1, 0))
        u2 = jnp.where(row == 0, hm2, jnp.where(row == 1, hm1, pltpu.roll(u, 2, 0)))
        prev_ref[:, cs] = u[tm - 8:, :]
        y = b_gate * (u2 * wc[0:1, cs] + u1 * wc[1:2, cs] + u * wc[2:3, cs])
        ssq = ssq + jnp.sum(y * y, axis=-1, keepdims=True)
        ys.append(y)
    inv = lax.rsqrt(ssq * (1.0 / d_conv) + EPS)
    yc_ref[...] = (jnp.concatenate(ys, axis=1) * inv * gconv_ref[...]).astype(BF16)
    side_out_ref[...] = side_ref[...].astype(BF16)


def _front(x2, pos2, freq, gmix, w_in_t, w_rope, gql, gkvl, wuq_p, wukv_p, gq_p, gk_p, wconv, gconv, side, *, seq):
    t, d_model = x2.shape
    tm = TM_FRONT
    q_rank = gql.shape[1]
    kv_rank = gkvl.shape[1]
    d_conv = gconv.shape[1]
    const = lambda i: (0, 0)
    row = lambda i: (i, 0)
    full = lambda a: pl.BlockSpec(a.shape, const)
    side_spec = pl.BlockSpec((1,) + side.shape[1:], lambda i: (i, 0, 0))
    kern = functools.partial(_front_kernel, tiles_per_seq=seq // tm, q_rank=q_rank, kv_rank=kv_rank, d_conv=d_conv)
    return pl.pallas_call(
        kern,
        grid=(t // tm,),
        in_specs=[pl.BlockSpec((tm, d_model), row), pl.BlockSpec((tm, 1), row), full(freq), full(gmix),
                  pl.BlockSpec(w_in_t.shape, const, pipeline_mode=pl.Buffered(1)), full(w_rope),
                  full(gql), full(gkvl), full(wuq_p), full(wukv_p), full(gq_p), full(gk_p), full(wconv), full(gconv), side_spec],
        out_specs=[pl.BlockSpec((tm, MLA_HEADS * HEAD_PAD), row), pl.BlockSpec((tm, MLA_HEADS * HEAD_PAD), row),
                   pl.BlockSpec((tm, MLA_HEADS * V_HEAD), row), pl.BlockSpec((tm, d_conv), row), side_spec],
        out_shape=[jax.ShapeDtypeStruct((t, MLA_HEADS * HEAD_PAD), BF16), jax.ShapeDtypeStruct((t, MLA_HEADS * HEAD_PAD), BF16),
                   jax.ShapeDtypeStruct((t, MLA_HEADS * V_HEAD), BF16), jax.ShapeDtypeStruct((t, d_conv), BF16),
                   jax.ShapeDtypeStruct(side.shape, BF16)],
        scratch_shapes=[pltpu.VMEM((8, d_conv), F32)],
        compiler_params=pltpu.CompilerParams(dimension_semantics=("arbitrary",), vmem_limit_bytes=VMEM_LIMIT),
        name="front",
    )(x2, pos2, freq, gmix, w_in_t, w_rope, gql, gkvl, wuq_p, wukv_p, gq_p, gk_p, wconv, gconv, side)


def _attn_kernel(q_ref, k_ref, v_ref, *rest):
    n_side = (len(rest) - 1) // 2
    o_ref = rest[n_side]

    def round_side_weights():
        for w_ref, wb_ref in zip(rest[:n_side], rest[n_side + 1:]):
            wb_ref[...] = w_ref[...].astype(BF16)

    i = pl.program_id(2)
    tq = q_ref.shape[0]
    half = tq // 2
    q_halves = (q_ref[:half, :], q_ref[half:, :])

    def step(start, n_keys, masks, carry):
        kvs = [(k_ref[pl.ds(start, n), :],
                jnp.concatenate([v_ref[pl.ds(start, n), :], jnp.ones((n, HEAD_PAD - V_HEAD), BF16)], axis=1)) for n in n_keys]
        scores = [lax.dot_general(qh, kb, (((1,), (1,)), ((), ())), preferred_element_type=F32)
                  for qh, (kb, _) in zip(q_halves, kvs)]
        scores = [s if mask is None else jnp.where(mask, s, MASK_VALUE) for s, mask in zip(scores, masks)]
        m_new = [jnp.maximum(m, jnp.max(s, axis=-1, keepdims=True)) for s, (m, _) in zip(scores, carry)]
        probs = [jnp.exp2(s - mn).astype(BF16) for s, mn in zip(scores, m_new)]
        return tuple((mn, jnp.exp2(m - mn) * acc + jnp.dot(p, vb, preferred_element_type=F32))
                     for p, mn, (m, acc), (_, vb) in zip(probs, m_new, carry, kvs))

    def chunk_mask(n, first_row):
        r = (lax.broadcasted_iota(jnp.int32, (half, n), 0) + first_row) // CHUNK
        c = lax.broadcasted_iota(jnp.int32, (half, n), 1) // CHUNK
        return c <= r

    for qi in range(k_ref.shape[0] // tq):
        @pl.when(i == qi)
        def _(qi=qi):
            round_side_weights()
            carry = tuple((jnp.full((half, 1), MASK_VALUE, F32), jnp.zeros((half, HEAD_PAD), F32)) for _ in range(2))
            for j in range(qi):
                carry = step(j * tq, (tq, tq), (None, None), carry)
            (_, acc0), (_, acc1) = step(qi * tq, (half, tq), (chunk_mask(half, 0), chunk_mask(tq, half)), carry)
            o_ref[:half, :] = (acc0[:, :V_HEAD] / acc0[:, V_HEAD:]).astype(o_ref.dtype)
            o_ref[half:, :] = (acc1[:, :V_HEAD] / acc1[:, V_HEAD:]).astype(o_ref.dtype)


def _attention(q, k, v, side_weights, *, batch, seq):
    tq = TQ_ATTN
    nq = seq // tq
    step = lambda b, h, i: ((b * MLA_HEADS + h) * nq + i, 0, 0)
    side_specs = [pl.BlockSpec((1,) + w.shape[1:], step) for w in side_weights]
    outs = pl.pallas_call(
        _attn_kernel,
        grid=(batch, MLA_HEADS, nq),
        in_specs=[pl.BlockSpec((tq, HEAD_PAD), lambda b, h, i: (b * nq + i, h)),
                  pl.BlockSpec((seq, HEAD_PAD), lambda b, h, i: (b, h)),
                  pl.BlockSpec((seq, V_HEAD), lambda b, h, i: (b, h))] + side_specs,
        out_specs=[pl.BlockSpec((tq, V_HEAD), lambda b, h, i: (b * nq + i, h))] + side_specs,
        out_shape=[jax.ShapeDtypeStruct((batch * seq, MLA_HEADS * V_HEAD), BF16)]
                  + [jax.ShapeDtypeStruct(w.shape, BF16) for w in side_weights],
        compiler_params=pltpu.CompilerParams(dimension_semantics=("arbitrary", "arbitrary", "arbitrary"),
                                             vmem_limit_bytes=VMEM_LIMIT),
        name="attn",
    )(q, k, v, *side_weights)
    return outs[0], outs[1:]


def _outproj_kernel(ya_ref, yc_ref, x_ref, gattn_ref, wout_ref, gmoe_ref, wgr_ref, bgr_ref,
                    h1_ref, xm_ref, route_ref):
    tm, d_attn = ya_ref.shape
    d_model = x_ref.shape[1]
    ya = _rms(ya_ref[...].astype(F32), gattn_ref[...], d_attn).astype(BF16)
    h1 = (x_ref[...]
          + jnp.dot(ya, wout_ref[:d_attn, :], preferred_element_type=F32)
          + jnp.dot(yc_ref[...], wout_ref[d_attn:, :], preferred_element_type=F32))
    h1_ref[...] = h1
    xm = _rms(h1, gmoe_ref[...], d_model)
    _store_slabs(xm_ref, xm, SLAB_PITCH)
    for c in range(SLAB, SLAB_PITCH):
        xm_ref[pl.ds(c, tm, stride=SLAB_PITCH), :] = jnp.zeros((tm, LANES), F32)

    x_hi = xm.astype(BF16)
    x_lo = (xm - x_hi.astype(F32)).astype(BF16)
    hi_part = jnp.dot(x_hi, wgr_ref[...], preferred_element_type=F32)
    lo_part = jnp.dot(x_lo, wgr_ref[:, :LANES], preferred_element_type=F32)
    logits = hi_part[:, :LANES] + hi_part[:, LANES:] + lo_part + bgr_ref[...]
    lane = lax.broadcasted_iota(jnp.int32, (tm, LANES), 1).astype(F32)
    neg = -jnp.inf
    big = 1e9
    gl = jnp.where(lane < N_GROUPS, logits, neg)
    gmax = jnp.max(gl, axis=-1, keepdims=True)
    gsel = jnp.min(jnp.where(gl == gmax, lane, big), axis=-1, keepdims=True)
    p_g = 1.0 / jnp.sum(jnp.exp(gl - gmax), axis=-1, keepdims=True)
    lo = N_GROUPS + EXPERTS_PER_GROUP * gsel
    el = jnp.where((lane >= lo) & (lane < lo + EXPERTS_PER_GROUP), logits, neg)
    v1 = jnp.max(el, axis=-1, keepdims=True)
    i1 = jnp.min(jnp.where(el == v1, lane, big), axis=-1, keepdims=True)
    el2 = jnp.where(lane == i1, neg, el)
    v2 = jnp.max(el2, axis=-1, keepdims=True)
    i2 = jnp.min(jnp.where(el2 == v2, lane, big), axis=-1, keepdims=True)
    e2 = jnp.exp(v2 - v1)
    den = 1.0 + e2
    w1 = p_g / den
    w2 = p_g * e2 / den
    route_ref[...] = jnp.where(lane == 0, i1 - N_GROUPS,
                               jnp.where(lane == 1, i2 - N_GROUPS,
                                         jnp.where(lane == 2, w1, jnp.where(lane == 3, w2, 0.0))))


def _outproj(ya, yc, x2, gattn, w_out, gmoe, w_gr, b_gr):
    t, d_model = x2.shape
    d_attn = ya.shape[1]
    tm = TM_OUT
    const = lambda i: (0, 0)
    row = lambda i: (i, 0)
    full = lambda a: pl.BlockSpec(a.shape, const)
    return pl.pallas_call(
        _outproj_kernel,
        grid=(t // tm,),
        in_specs=[pl.BlockSpec((tm, d_attn), row), pl.BlockSpec((tm, yc.shape[1]), row), pl.BlockSpec((tm, d_model), row),
                  full(gattn), pl.BlockSpec(w_out.shape, const, pipeline_mode=pl.Buffered(1)), full(gmoe), full(w_gr), full(b_gr)],
        out_specs=[pl.BlockSpec((tm, d_model), row), pl.BlockSpec((tm * SLAB_PITCH, LANES), row), pl.BlockSpec((tm, LANES), row)],
        out_shape=[jax.ShapeDtypeStruct((t, d_model), F32), jax.ShapeDtypeStruct((t * SLAB_PITCH, LANES), F32),
                   jax.ShapeDtypeStruct((t, LANES), F32)],
        compiler_params=pltpu.CompilerParams(dimension_semantics=("arbitrary",), vmem_limit_bytes=VMEM_LIMIT),
        name="outproj",
    )(ya, yc, x2, gattn, w_out, gmoe, w_gr, b_gr)


def _row_gather_start(idx_ref, src_hbm, dst_ref, sem, n_rows, src_pitch=SLAB):
    def body(pair, carry):
        for prio in range(DMA_PRIORITIES):
            r = pair * DMA_PRIORITIES + prio
            src = src_hbm.at[pl.ds(idx_ref[0, 0, r] * src_pitch, SLAB), :]
            pltpu.make_async_copy(src, dst_ref.at[pl.ds(r * SLAB_PITCH, SLAB), :], sem).start(priority=prio)
        return carry
    lax.fori_loop(0, n_rows // DMA_PRIORITIES, body, 0, unroll=True)


def _row_gather_wait(src_hbm, dst_ref, sem, n_rows):
    pltpu.make_async_copy(src_hbm.at[pl.ds(0, n_rows * SLAB), :], dst_ref.at[pl.ds(0, n_rows * SLAB), :], sem).wait()


def _moe_kernel(tile_expert_ref, n_active_ref, first_ref, wslot_ref, next_expert_ref, tok_cur_ref, tok_next_ref,
                xm_hbm, w1_hbm, w3_hbm, w2_hbm, y_ref, xbuf_ref, w1_buf, w3_buf, w2_buf, sem_ref, wsem_ref, *, tm):
    i = pl.program_id(0)
    n_active = n_active_ref[0]
    slot = i % 2
    buf_rows = tm * SLAB_PITCH
    wslot = wslot_ref[i]

    def buf(s):
        return xbuf_ref.at[pl.ds(s * buf_rows, buf_rows), :]

    def weight_copies(e, s):
        return (pltpu.make_async_copy(w1_hbm.at[e], w1_buf.at[s], wsem_ref.at[0, s]),
                pltpu.make_async_copy(w3_hbm.at[e], w3_buf.at[s], wsem_ref.at[1, s]),
                pltpu.make_async_copy(w2_hbm.at[e], w2_buf.at[s], wsem_ref.at[2, s]))

    @pl.when(i == 0)
    def _():
        for cp in weight_copies(tile_expert_ref[0], 0):
            cp.start()
        _row_gather_start(tok_cur_ref, xm_hbm, buf(0), sem_ref.at[0], tm, src_pitch=SLAB_PITCH)

    active = i < n_active
    new_expert = active & (first_ref[i] == 1)

    @pl.when(new_expert & (next_expert_ref[i] >= 0))
    def _():
        for cp in weight_copies(next_expert_ref[i], 1 - wslot):
            cp.start()

    @pl.when(i + 1 < n_active)
    def _():
        _row_gather_start(tok_next_ref, xm_hbm, buf(1 - slot), sem_ref.at[1 - slot], tm, src_pitch=SLAB_PITCH)

    @pl.when(new_expert)
    def _():
        for cp in weight_copies(tile_expert_ref[i], wslot):
            cp.wait()

    @pl.when(active)
    def _():
        _row_gather_wait(xm_hbm, buf(slot), sem_ref.at[slot], tm)
        x = _load_slabs(xbuf_ref, slot * buf_rows, tm, SLAB, SLAB_PITCH).astype(BF16)
        a = jnp.dot(x, w1_buf[wslot], preferred_element_type=F32)
        g = jnp.dot(x, w3_buf[wslot], preferred_element_type=F32)
        hid = (a * jax.nn.sigmoid(a) * g).astype(BF16)
        _store_slabs(y_ref, jnp.dot(hid, w2_buf[wslot], preferred_element_type=F32), SLAB)

    @pl.when(i >= n_active)
    def _():
        y_ref[...] = jnp.zeros_like(y_ref)


def _moe(plan, xm, w1, w3, w2):
    tile_expert, n_active, first, wslot, next_expert, row_token3 = plan
    n_tiles = row_token3.shape[0]
    tm = TM_MOE
    _, d_model, d_exp = w1.shape
    last = n_tiles - 1
    grid_spec = pltpu.PrefetchScalarGridSpec(
        num_scalar_prefetch=5,
        grid=(n_tiles,),
        in_specs=[pl.BlockSpec((1, 1, tm), lambda i, *_: (i, 0, 0), memory_space=pltpu.SMEM),
                  pl.BlockSpec((1, 1, tm), lambda i, *_: (jnp.minimum(i + 1, last), 0, 0), memory_space=pltpu.SMEM),
                  pl.BlockSpec(memory_space=pl.ANY), pl.BlockSpec(memory_space=pl.ANY),
                  pl.BlockSpec(memory_space=pl.ANY), pl.BlockSpec(memory_space=pl.ANY)],
        out_specs=pl.BlockSpec((tm * SLAB, LANES), lambda i, *_: (i, 0)),
        scratch_shapes=[pltpu.VMEM((2 * tm * SLAB_PITCH, LANES), F32),
                        pltpu.VMEM((2, d_model, d_exp), BF16), pltpu.VMEM((2, d_model, d_exp), BF16),
                        pltpu.VMEM((2, d_exp, d_model), BF16),
                        pltpu.SemaphoreType.DMA((2,)), pltpu.SemaphoreType.DMA((3, 2))],
    )
    return pl.pallas_call(
        functools.partial(_moe_kernel, tm=tm),
        grid_spec=grid_spec,
        out_shape=jax.ShapeDtypeStruct((n_tiles * tm * SLAB, LANES), F32),
        compiler_params=pltpu.CompilerParams(dimension_semantics=("arbitrary",), vmem_limit_bytes=VMEM_LIMIT),
        name="moe",
    )(tile_expert, n_active, first, wslot, next_expert, row_token3, row_token3, xm, w1, w3, w2)


def _ple_kernel(dest_cur_ref, dest_next_ref, y_hbm, h1_ref, route_ref, p_ref, wple_ref, gple_ref, wpg_ref, bpg_ref,
                o_ref, ybuf_ref, sem_ref):
    i = pl.program_id(0)
    n = pl.num_programs(0)
    tm, d_model = h1_ref.shape
    slot = i % 2
    buf_rows = TOP_K * tm * SLAB_PITCH

    def buf(s):
        return ybuf_ref.at[pl.ds(s * buf_rows, buf_rows), :]

    @pl.when(i == 0)
    def _():
        _row_gather_start(dest_cur_ref, y_hbm, buf(0), sem_ref.at[0], TOP_K * tm)

    @pl.when(i + 1 < n)
    def _():
        _row_gather_start(dest_next_ref, y_hbm, buf(1 - slot), sem_ref.at[1 - slot], TOP_K * tm)

    _row_gather_wait(y_hbm, buf(slot), sem_ref.at[slot], TOP_K * tm)
    route = route_ref[...]
    y0 = _load_slabs(ybuf_ref, slot * buf_rows, tm, SLAB, SLAB_PITCH)
    y1 = _load_slabs(ybuf_ref, slot * buf_rows + tm * SLAB_PITCH, tm, SLAB, SLAB_PITCH)
    h2 = h1_ref[...] + (route[:, 2:3] * y0 + route[:, 3:4] * y1)
    xn = _rms(h2, gple_ref[...], d_model).astype(BF16)
    gate = jax.nn.sigmoid(jnp.dot(xn, wpg_ref[...], preferred_element_type=F32) + bpg_ref[...])
    pe = jnp.dot(p_ref[...].astype(BF16), wple_ref[...], preferred_element_type=F32)
    o_ref[...] = h2 + gate * pe


def _ple(dest3, y, h1, route, p2, w_ple, gple, w_pg, b_pg):
    t, d_model = h1.shape
    tm = TM_PLE
    n_tiles = t // tm
    last = n_tiles - 1
    const = lambda i: (0, 0)
    row = lambda i: (i, 0)
    full = lambda a: pl.BlockSpec(a.shape, const)
    return pl.pallas_call(
        _ple_kernel,
        grid=(n_tiles,),
        in_specs=[pl.BlockSpec((1, 1, TOP_K * tm), lambda i: (i, 0, 0), memory_space=pltpu.SMEM),
                  pl.BlockSpec((1, 1, TOP_K * tm), lambda i: (jnp.minimum(i + 1, last), 0, 0), memory_space=pltpu.SMEM),
                  pl.BlockSpec(memory_space=pl.ANY),
                  pl.BlockSpec((tm, d_model), row), pl.BlockSpec((tm, LANES), row), pl.BlockSpec((tm, p2.shape[1]), row),
                  full(w_ple), full(gple), pl.BlockSpec(w_pg.shape, const, pipeline_mode=pl.Buffered(1)), full(b_pg)],
        out_specs=pl.BlockSpec((tm, d_model), row),
        out_shape=jax.ShapeDtypeStruct((t, d_model), F32),
        scratch_shapes=[pltpu.VMEM((2 * TOP_K * tm * SLAB_PITCH, LANES), F32), pltpu.SemaphoreType.DMA((2,))],
        compiler_params=pltpu.CompilerParams(dimension_semantics=("arbitrary",), vmem_limit_bytes=VMEM_LIMIT),
        name="ple",
    )(dest3, dest3, y, h1, route, p2, w_ple, gple, w_pg, b_pg)


def _dispatch_plan(route):
    t = route.shape[0]
    tm = TM_MOE
    total = TOP_K * t
    n_tiles = total // tm + N_EXPERTS
    expert = route[:, :TOP_K].astype(jnp.int32).reshape(-1)
    pos = jnp.arange(total, dtype=jnp.int32)
    order = jnp.sort(expert * total + pos) % total
    rank = jnp.sort(order * total + pos) % total
    experts = jnp.arange(N_EXPERTS, dtype=jnp.int32)
    onehot = (expert[:, None] == experts[None, :]).astype(jnp.int32)
    counts = jnp.sum(onehot, axis=0)
    group_start = jnp.cumsum(counts) - counts
    tile_end = jnp.cumsum((counts + tm - 1) // tm)
    tile_start = tile_end - (counts + tm - 1) // tm
    n_active = tile_end[-1]
    shift = tile_start * tm - group_start
    dest = rank + jnp.sum(onehot * shift[None, :], axis=1)
    tiles = jnp.minimum(jnp.arange(n_tiles, dtype=jnp.int32), n_active - 1)
    tile_expert = jnp.sum((tile_end[None, :] <= tiles[:, None]).astype(jnp.int32), axis=1)
    j = jnp.arange(tm, dtype=jnp.int32)
    fill_row = (tile_start * tm + counts)[:, None] + j[None, :]
    fill_row = jnp.where(fill_row < (tile_end * tm)[:, None], fill_row, n_tiles * tm)
    fill_tok = (experts[:, None] * tm + j[None, :]) % t
    keys = jnp.concatenate([dest * t + pos // TOP_K, (fill_row * t + fill_tok).reshape(-1)])
    row_token3 = (jnp.sort(keys) % t).reshape(n_tiles, 1, tm)
    dest3 = dest.reshape(t // TM_PLE, TM_PLE, TOP_K).transpose(0, 2, 1).reshape(t // TM_PLE, 1, TOP_K * TM_PLE)
    prev_expert = jnp.concatenate([jnp.full((1,), -1, jnp.int32), tile_expert[:-1]])
    first = (tile_expert != prev_expert).astype(jnp.int32)
    nonempty = counts > 0
    wslot = (jnp.sum((nonempty[None, :] & (experts[None, :] < tile_expert[:, None])).astype(jnp.int32), axis=1)) % 2
    later = nonempty[None, :] & (experts[None, :] > tile_expert[:, None])
    next_expert = jnp.min(jnp.where(later, experts[None, :], N_EXPERTS), axis=1)
    next_expert = jnp.where(next_expert < N_EXPERTS, next_expert, -1)
    plan = (tile_expert, n_active.reshape(1), first, wslot, next_expert, row_token3)
    return plan, dest3


def _rope_swap_pad(w):
    half = QK_ROPE // 2
    x1 = w[..., QK_NOPE:QK_NOPE + half]
    x2 = w[..., QK_NOPE + half:]
    return jnp.concatenate([w[..., :QK_NOPE], x1, x2, x2, x1], axis=-1)


def kernel(x, p, positions, norm_mix, w_in, g_q_lat, g_kv_lat, w_uq, w_ukv, g_q_head, g_k_head, w_conv, g_out_attn, g_out_conv, w_out, norm_moe, w_group, b_group, w_router, b_router, w1, w3, w2, norm_ple, w_ple, w_ple_gate, b_ple_gate):
    batch, seq, d_model = x.shape
    depth = w_in.shape[0]
    t = batch * seq
    q_rank = g_q_lat.shape[1]
    kv_rank = g_kv_lat.shape[1]
    half = QK_ROPE // 2
    rope_lo = q_rank + kv_rank

    pos2 = positions.reshape(t, 1)
    inv_freq = ROPE_BASE ** (-jnp.arange(0, QK_ROPE, 2, dtype=F32) / QK_ROPE)
    freq = jnp.tile(inv_freq, LANES // half).reshape(1, LANES)

    h = x.reshape(t, d_model)
    for i in range(depth):
        w_in_t = jnp.swapaxes(w_in[i], 0, 1).astype(BF16)
        w_rope = jnp.concatenate([w_in_t[rope_lo:rope_lo + QK_ROPE], w_in_t[rope_lo + half:rope_lo + QK_ROPE],
                                  w_in_t[rope_lo:rope_lo + half]], axis=0)
        wuq_p = _rope_swap_pad(w_uq[i]).reshape(q_rank, MLA_HEADS * HEAD_PAD).astype(BF16)
        wukv_p = jnp.concatenate([w_ukv[i][:, :, :QK_NOPE].reshape(kv_rank, -1),
                                  w_ukv[i][:, :, QK_NOPE:].reshape(kv_rank, -1)], axis=1).astype(BF16)
        gq_p = _rope_swap_pad(g_q_head[i]).reshape(1, HEAD_PAD)
        gk_p = _rope_swap_pad(g_k_head[i]).reshape(1, HEAD_PAD)
        as_steps = lambda w, n: w.reshape(n, -1, w.shape[-1])
        q, k, v, yc, w1b = _front(h, pos2, freq, norm_mix[i].reshape(1, -1), w_in_t, w_rope, g_q_lat[i].reshape(1, -1),
                                  g_kv_lat[i].reshape(1, -1), wuq_p, wukv_p, gq_p, gk_p, w_conv[i],
                                  g_out_conv[i].reshape(1, -1), as_steps(w1[i], t // TM_FRONT), seq=seq)
        n_steps = batch * MLA_HEADS * (seq // TQ_ATTN)
        ya, (w3b, w2b) = _attention(q, k, v, [as_steps(w3[i], n_steps), as_steps(w2[i], n_steps)], batch=batch, seq=seq)
        w1b, w3b, w2b = w1b.reshape(w1[i].shape), w3b.reshape(w3[i].shape), w2b.reshape(w2[i].shape)

        pad = LANES - N_GROUPS - N_EXPERTS
        w_gr32 = jnp.concatenate([w_group[i], w_router[i], jnp.zeros((d_model, pad), F32)], axis=1)
        w_gr_hi = w_gr32.astype(BF16)
        w_gr = jnp.concatenate([w_gr_hi, (w_gr32 - w_gr_hi.astype(F32)).astype(BF16)], axis=1)
        b_gr = jnp.concatenate([b_group[i], b_router[i], jnp.zeros((pad,), F32)]).reshape(1, LANES)
        h1, xm, route = _outproj(ya, yc, h, g_out_attn[i].reshape(1, -1), w_out[i].astype(BF16),
                                 norm_moe[i].reshape(1, -1), w_gr, b_gr)

        plan, dest3 = _dispatch_plan(route)
        y = _moe(plan, xm, w1b, w3b, w2b)
        h = _ple(dest3, y, h1, route, p[i].reshape(t, -1), w_ple[i].astype(BF16), norm_ple[i].reshape(1, -1),
                 w_ple_gate[i].astype(BF16), b_ple_gate[i].reshape(1, -1))
    return h.reshape(batch, seq, d_model)
```

```python
import functools

import jax
import jax.numpy as jnp
from jax import lax
from jax.experimental import pallas as pl
from jax.experimental.pallas import tpu as pltpu

F32 = jnp.float32
BF16 = jnp.bfloat16

CHUNK = 64
EPS = 1e-6
MASK_VALUE = -1e30
ROPE_BASE = 10000.0

MLA_HEADS = 8
QK_NOPE = 128
QK_ROPE = 64
QK_HEAD = QK_NOPE + QK_ROPE
V_HEAD = 128
HEAD_PAD = 256
Q_SCALE = QK_HEAD ** -0.5 * 1.4426950408889634
N_GROUPS = 4
EXPERTS_PER_GROUP = 8
N_EXPERTS = N_GROUPS * EXPERTS_PER_GROUP
TOP_K = 2
LANES = 128
SLAB = 16
SLAB_PITCH = 20
DMA_PRIORITIES = 2

TM_FRONT = 256
TQ_ATTN = 1024
TM_OUT = 512
TM_MOE = 256
TM_PLE = 256
VMEM_LIMIT = 56 * 1024 * 1024


def _rms(x, g, n):
    ms = jnp.sum(x * x, axis=-1, keepdims=True) * (1.0 / n)
    return x * lax.rsqrt(ms + EPS) * g


def _store_slabs(ref, val, pitch, mask=None):
    n, width = val.shape
    for c in range(width // LANES):
        piece = val[:, c * LANES:(c + 1) * LANES]
        rows = pl.ds(c, n, stride=pitch)
        if mask is not None:
            piece = jnp.where(mask, piece, ref[rows, :])
        ref[rows, :] = piece


def _load_slabs(ref, start, n, slab, pitch):
    return jnp.concatenate([ref[pl.ds(start + c, n, stride=pitch), :] for c in range(slab)], axis=1)


def _front_kernel(x_ref, pos_ref, freq_ref, gmix_ref, win_ref, wrope_ref, gql_ref, gkvl_ref,
                  wuq_ref, wukv_ref, gq_ref, gk_ref, wconv_ref, gconv_ref, side_ref,
                  q_ref, k_ref, v_ref, yc_ref, side_out_ref, prev_ref, *, tiles_per_seq, q_rank, kv_rank, d_conv):
    i = pl.program_id(0)
    is_start = (i % tiles_per_seq) == 0
    tm = x_ref.shape[0]
    d_model = x_ref.shape[1]
    rope_lo = q_rank + kv_rank
    nt = (((1,), (1,)), ((), ()))

    @pl.when(is_start)
    def _():
        prev_ref[...] = jnp.zeros_like(prev_ref)

    xn = _rms(x_ref[...], gmix_ref[...], d_model).astype(BF16)
    lat = lax.dot_general(xn, win_ref[:rope_lo, :], nt, preferred_element_type=F32)
    q_lat = lat[:, :q_rank]
    kv_lat = lat[:, q_rank:]
    kr = lax.dot_general(xn, wrope_ref[...], nt, preferred_element_type=F32)

    qn = _rms(q_lat, gql_ref[...], q_rank).astype(BF16)
    kvn = _rms(kv_lat, gkvl_ref[...], kv_rank).astype(BF16)
    q_all = jnp.dot(qn, wuq_ref[...], preferred_element_type=F32)
    kv_all = jnp.dot(kvn, wukv_ref[...], preferred_element_type=F32)

    lane = lax.broadcasted_iota(jnp.int32, (tm, LANES), 1)
    ang = pos_ref[...].astype(F32) * freq_ref[...]
    cos = jnp.cos(ang)
    sin = jnp.sin(ang)
    rope_c = jnp.where(lane < QK_ROPE, cos, 0.0)
    rope_s = jnp.where(lane < QK_ROPE // 2, -sin, jnp.where(lane < QK_ROPE, sin, 0.0))
    rope_lanes = lane < QK_ROPE

    def rope(t):
        return t * rope_c + pltpu.roll(t, QK_ROPE, 1) * rope_s

    gq = gq_ref[...]
    gk = gk_ref[...]
    inv_n = 1.0 / QK_HEAD
    kr_ssq = jnp.sum(jnp.where(rope_lanes, kr * kr, 0.0), axis=-1, keepdims=True)
    kr_rot = rope(kr * gk[:, QK_NOPE:])
    for h in range(MLA_HEADS):
        qa = q_all[:, h * HEAD_PAD:h * HEAD_PAD + QK_NOPE]
        qb = q_all[:, h * HEAD_PAD + QK_NOPE:(h + 1) * HEAD_PAD]
        ssq = (jnp.sum(qa * qa, axis=-1, keepdims=True)
               + jnp.sum(jnp.where(rope_lanes, qb * qb, 0.0), axis=-1, keepdims=True))
        inv = lax.rsqrt(ssq * inv_n + EPS)
        q_ref[:, h * HEAD_PAD:h * HEAD_PAD + QK_NOPE] = (qa * inv * gq[:, :QK_NOPE] * Q_SCALE).astype(BF16)
        q_ref[:, h * HEAD_PAD + QK_NOPE:(h + 1) * HEAD_PAD] = (rope(qb * inv * gq[:, QK_NOPE:]) * Q_SCALE).astype(BF16)

        ka = kv_all[:, h * QK_NOPE:(h + 1) * QK_NOPE]
        kinv = lax.rsqrt((jnp.sum(ka * ka, axis=-1, keepdims=True) + kr_ssq) * inv_n + EPS)
        k_ref[:, h * HEAD_PAD:h * HEAD_PAD + QK_NOPE] = (ka * kinv * gk[:, :QK_NOPE]).astype(BF16)
        k_ref[:, h * HEAD_PAD + QK_NOPE:(h + 1) * HEAD_PAD] = (kr_rot * kinv).astype(BF16)
    v_ref[...] = kv_all[:, MLA_HEADS * QK_NOPE:].astype(BF16)

    conv_lo = rope_lo + QK_ROPE
    half_c = d_conv // 2
    row = lax.broadcasted_iota(jnp.int32, (tm, half_c), 0)
    wc = wconv_ref[...]
    ys = []
    ssq = jnp.zeros((tm, 1), F32)
    for cs in (slice(0, half_c), slice(half_c, d_conv)):
        b_gate, c_gate, h_conv = (lax.dot_general(xn, win_ref[conv_lo + k * d_conv + cs.start:conv_lo + k * d_conv + cs.stop, :],
                                                  nt, preferred_element_type=F32) for k in range(3))
        u = c_gate * h_conv
        prev = prev_ref[:, cs]
        hm1 = prev[7:8, :]
        hm2 = prev[6:7, :]
        u1 = jnp.where(row == 0, hm1, pltpu.roll(u, 1, 0))
        u2 = jnp.where(row == 0, hm2, jnp.where(row == 1, hm1, pltpu.roll(u, 2, 0)))
        prev_ref[:, cs] = u[tm - 8:, :]
        y = b_gate * (u2 * wc[0:1, cs] + u1 * wc[1:2, cs] + u * wc[2:3, cs])
        ssq = ssq + jnp.sum(y * y, axis=-1, keepdims=True)
        ys.append(y)
    inv = lax.rsqrt(ssq * (1.0 / d_conv) + EPS)
    yc_ref[...] = (jnp.concatenate(ys, axis=1) * inv * gconv_ref[...]).astype(BF16)
    side_out_ref[...] = side_ref[...].astype(BF16)


def _front(x2, pos2, freq, gmix, w_in_t, w_rope, gql, gkvl, wuq_p, wukv_p, gq_p, gk_p, wconv, gconv, side, *, seq):
    t, d_model = x2.shape
    tm = TM_FRONT
    q_rank = gql.shape[1]
    kv_rank = gkvl.shape[1]
    d_conv = gconv.shape[1]
    const = lambda i: (0, 0)
    row = lambda i: (i, 0)
    full = lambda a: pl.BlockSpec(a.shape, const)
    side_spec = pl.BlockSpec((1,) + side.shape[1:], lambda i: (i, 0, 0))
    kern = functools.partial(_front_kernel, tiles_per_seq=seq // tm, q_rank=q_rank, kv_rank=kv_rank, d_conv=d_conv)
    return pl.pallas_call(
        kern,
        grid=(t // tm,),
        in_specs=[pl.BlockSpec((tm, d_model), row), pl.BlockSpec((tm, 1), row), full(freq), full(gmix),
                  pl.BlockSpec(w_in_t.shape, const, pipeline_mode=pl.Buffered(1)), full(w_rope),
                  full(gql), full(gkvl), full(wuq_p), full(wukv_p), full(gq_p), full(gk_p), full(wconv), full(gconv), side_spec],
        out_specs=[pl.BlockSpec((tm, MLA_HEADS * HEAD_PAD), row), pl.BlockSpec((tm, MLA_HEADS * HEAD_PAD), row),
                   pl.BlockSpec((tm, MLA_HEADS * V_HEAD), row), pl.BlockSpec((tm, d_conv), row), side_spec],
        out_shape=[jax.ShapeDtypeStruct((t, MLA_HEADS * HEAD_PAD), BF16), jax.ShapeDtypeStruct((t, MLA_HEADS * HEAD_PAD), BF16),
                   jax.ShapeDtypeStruct((t, MLA_HEADS * V_HEAD), BF16), jax.ShapeDtypeStruct((t, d_conv), BF16),
                   jax.ShapeDtypeStruct(side.shape, BF16)],
        scratch_shapes=[pltpu.VMEM((8, d_conv), F32)],
        compiler_params=pltpu.CompilerParams(dimension_semantics=("arbitrary",), vmem_limit_bytes=VMEM_LIMIT),
        name="front",
    )(x2, pos2, freq, gmix, w_in_t, w_rope, gql, gkvl, wuq_p, wukv_p, gq_p, gk_p, wconv, gconv, side)


def _attn_kernel(q_ref, k_ref, v_ref, *rest):
    n_side = (len(rest) - 1) // 2
    o_ref = rest[n_side]

    def round_side_weights():
        for w_ref, wb_ref in zip(rest[:n_side], rest[n_side + 1:]):
            wb_ref[...] = w_ref[...].astype(BF16)

    i = pl.program_id(2)
    tq = q_ref.shape[0]
    half = tq // 2
    q_halves = (q_ref[:half, :], q_ref[half:, :])

    def step(start, n_keys, masks, carry):
        kvs = [(k_ref[pl.ds(start, n), :],
                jnp.concatenate([v_ref[pl.ds(start, n), :], jnp.ones((n, HEAD_PAD - V_HEAD), BF16)], axis=1)) for n in n_keys]
        scores = [lax.dot_general(qh, kb, (((1,), (1,)), ((), ())), preferred_element_type=F32)
                  for qh, (kb, _) in zip(q_halves, kvs)]
        scores = [s if mask is None else jnp.where(mask, s, MASK_VALUE) for s, mask in zip(scores, masks)]
        m_new = [jnp.maximum(m, jnp.max(s, axis=-1, keepdims=True)) for s, (m, _) in zip(scores, carry)]
        probs = [jnp.exp2(s - mn).astype(BF16) for s, mn in zip(scores, m_new)]
        return tuple((mn, jnp.exp2(m - mn) * acc + jnp.dot(p, vb, preferred_element_type=F32))
                     for p, mn, (m, acc), (_, vb) in zip(probs, m_new, carry, kvs))

    def chunk_mask(n, first_row):
        r = (lax.broadcasted_iota(jnp.int32, (half, n), 0) + first_row) // CHUNK
        c = lax.broadcasted_iota(jnp.int32, (half, n), 1) // CHUNK
        return c <= r

    for qi in range(k_ref.shape[0] // tq):
        @pl.when(i == qi)
        def _(qi=qi):
            round_side_weights()
            carry = tuple((jnp.full((half, 1), MASK_VALUE, F32), jnp.zeros((half, HEAD_PAD), F32)) for _ in range(2))
            for j in range(qi):
                carry = step(j * tq, (tq, tq), (None, None), carry)
            (_, acc0), (_, acc1) = step(qi * tq, (half, tq), (chunk_mask(half, 0), chunk_mask(tq, half)), carry)
            o_ref[:half, :] = (acc0[:, :V_HEAD] / acc0[:, V_HEAD:]).astype(o_ref.dtype)
            o_ref[half:, :] = (acc1[:, :V_HEAD] / acc1[:, V_HEAD:]).astype(o_ref.dtype)


def _attention(q, k, v, side_weights, *, batch, seq):
    tq = TQ_ATTN
    nq = seq // tq
    step = lambda b, h, i: ((b * MLA_HEADS + h) * nq + i, 0, 0)
    side_specs = [pl.BlockSpec((1,) + w.shape[1:], step) for w in side_weights]
    outs = pl.pallas_call(
        _attn_kernel,
        grid=(batch, MLA_HEADS, nq),
        in_specs=[pl.BlockSpec((tq, HEAD_PAD), lambda b, h, i: (b * nq + i, h)),
                  pl.BlockSpec((seq, HEAD_PAD), lambda b, h, i: (b, h)),
                  pl.BlockSpec((seq, V_HEAD), lambda b, h, i: (b, h))] + side_specs,
        out_specs=[pl.BlockSpec((tq, V_HEAD), lambda b, h, i: (b * nq + i, h))] + side_specs,
        out_shape=[jax.ShapeDtypeStruct((batch * seq, MLA_HEADS * V_HEAD), BF16)]
                  + [jax.ShapeDtypeStruct(w.shape, BF16) for w in side_weights],
        compiler_params=pltpu.CompilerParams(dimension_semantics=("arbitrary", "arbitrary", "arbitrary"),
                                             vmem_limit_bytes=VMEM_LIMIT),
        name="attn",
    )(q, k, v, *side_weights)
    return outs[0], outs[1:]


def _outproj_kernel(ya_ref, yc_ref, x_ref, gattn_ref, wout_ref, gmoe_ref, wgr_ref, bgr_ref,
                    h1_ref, route_ref):
    tm, d_attn = ya_ref.shape
    d_model = x_ref.shape[1]
    ya = _rms(ya_ref[...].astype(F32), gattn_ref[...], d_attn).astype(BF16)
    h1 = (x_ref[...]
          + jnp.dot(ya, wout_ref[:d_attn, :], preferred_element_type=F32)
          + jnp.dot(yc_ref[...], wout_ref[d_attn:, :], preferred_element_type=F32))
    _store_slabs(h1_ref, h1, SLAB_PITCH)
    for c in range(SLAB, SLAB_PITCH):
        h1_ref[pl.ds(c, tm, stride=SLAB_PITCH), :] = jnp.zeros((tm, LANES), F32)
    xm = _rms(h1, gmoe_ref[...], d_model)

    x_hi = xm.astype(BF16)
    x_lo = (xm - x_hi.astype(F32)).astype(BF16)
    hi_part = jnp.dot(x_hi, wgr_ref[...], preferred_element_type=F32)
    lo_part = jnp.dot(x_lo, wgr_ref[:, :LANES], preferred_element_type=F32)
    logits = hi_part[:, :LANES] + hi_part[:, LANES:] + lo_part + bgr_ref[...]
    lane = lax.broadcasted_iota(jnp.int32, (tm, LANES), 1).astype(F32)
    neg = -jnp.inf
    big = 1e9
    gl = jnp.where(lane < N_GROUPS, logits, neg)
    gmax = jnp.max(gl, axis=-1, keepdims=True)
    gsel = jnp.min(jnp.where(gl == gmax, lane, big), axis=-1, keepdims=True)
    p_g = 1.0 / jnp.sum(jnp.exp(gl - gmax), axis=-1, keepdims=True)
    lo = N_GROUPS + EXPERTS_PER_GROUP * gsel
    el = jnp.where((lane >= lo) & (lane < lo + EXPERTS_PER_GROUP), logits, neg)
    v1 = jnp.max(el, axis=-1, keepdims=True)
    i1 = jnp.min(jnp.where(el == v1, lane, big), axis=-1, keepdims=True)
    el2 = jnp.where(lane == i1, neg, el)
    v2 = jnp.max(el2, axis=-1, keepdims=True)
    i2 = jnp.min(jnp.where(el2 == v2, lane, big), axis=-1, keepdims=True)
    e2 = jnp.exp(v2 - v1)
    den = 1.0 + e2
    w1 = p_g / den
    w2 = p_g * e2 / den
    route_ref[...] = jnp.where(lane == 0, i1 - N_GROUPS,
                               jnp.where(lane == 1, i2 - N_GROUPS,
                                         jnp.where(lane == 2, w1, jnp.where(lane == 3, w2, 0.0))))


def _outproj(ya, yc, x2, gattn, w_out, gmoe, w_gr, b_gr):
    t, d_model = x2.shape
    d_attn = ya.shape[1]
    tm = TM_OUT
    const = lambda i: (0, 0)
    row = lambda i: (i, 0)
    full = lambda a: pl.BlockSpec(a.shape, const)
    return pl.pallas_call(
        _outproj_kernel,
        grid=(t // tm,),
        in_specs=[pl.BlockSpec((tm, d_attn), row), pl.BlockSpec((tm, yc.shape[1]), row), pl.BlockSpec((tm, d_model), row),
                  full(gattn), pl.BlockSpec(w_out.shape, const, pipeline_mode=pl.Buffered(1)), full(gmoe), full(w_gr), full(b_gr)],
        out_specs=[pl.BlockSpec((tm * SLAB_PITCH, LANES), row), pl.BlockSpec((tm, LANES), row)],
        out_shape=[jax.ShapeDtypeStruct((t * SLAB_PITCH, LANES), F32), jax.ShapeDtypeStruct((t, LANES), F32)],
        compiler_params=pltpu.CompilerParams(dimension_semantics=("arbitrary",), vmem_limit_bytes=VMEM_LIMIT),
        name="outproj",
    )(ya, yc, x2, gattn, w_out, gmoe, w_gr, b_gr)


def _row_gather_start(idx_ref, src_hbm, dst_ref, sem, n_rows, src_pitch=SLAB):
    def body(pair, carry):
        for prio in range(DMA_PRIORITIES):
            r = pair * DMA_PRIORITIES + prio
            src = src_hbm.at[pl.ds(idx_ref[0, 0, r] * src_pitch, SLAB), :]
            pltpu.make_async_copy(src, dst_ref.at[pl.ds(r * SLAB_PITCH, SLAB), :], sem).start(priority=prio)
        return carry
    lax.fori_loop(0, n_rows // DMA_PRIORITIES, body, 0, unroll=True)


def _row_gather_wait(src_hbm, dst_ref, sem, n_rows):
    pltpu.make_async_copy(src_hbm.at[pl.ds(0, n_rows * SLAB), :], dst_ref.at[pl.ds(0, n_rows * SLAB), :], sem).wait()


def _moe_kernel(tile_expert_ref, n_active_ref, first_ref, wslot_ref, next_expert_ref, tok_cur_ref, tok_next_ref,
                h1_hbm, gmoe_ref, w1_hbm, w3_hbm, w2_hbm, y_ref, xbuf_ref, w1_buf, w3_buf, w2_buf, sem_ref, wsem_ref, *, tm):
    i = pl.program_id(0)
    n_active = n_active_ref[0]
    slot = i % 2
    buf_rows = tm * SLAB_PITCH
    wslot = wslot_ref[i]

    def buf(s):
        return xbuf_ref.at[pl.ds(s * buf_rows, buf_rows), :]

    def weight_copies(e, s):
        return (pltpu.make_async_copy(w1_hbm.at[e], w1_buf.at[s], wsem_ref.at[0, s]),
                pltpu.make_async_copy(w3_hbm.at[e], w3_buf.at[s], wsem_ref.at[1, s]),
                pltpu.make_async_copy(w2_hbm.at[e], w2_buf.at[s], wsem_ref.at[2, s]))

    @pl.when(i == 0)
    def _():
        for cp in weight_copies(tile_expert_ref[0], 0):
            cp.start()
        _row_gather_start(tok_cur_ref, h1_hbm, buf(0), sem_ref.at[0], tm, src_pitch=SLAB_PITCH)

    active = i < n_active
    new_expert = active & (first_ref[i] == 1)

    @pl.when(new_expert & (next_expert_ref[i] >= 0))
    def _():
        for cp in weight_copies(next_expert_ref[i], 1 - wslot):
            cp.start()

    @pl.when(i + 1 < n_active)
    def _():
        _row_gather_start(tok_next_ref, h1_hbm, buf(1 - slot), sem_ref.at[1 - slot], tm, src_pitch=SLAB_PITCH)

    @pl.when(new_expert)
    def _():
        for cp in weight_copies(tile_expert_ref[i], wslot):
            cp.wait()

    @pl.when(active)
    def _():
        _row_gather_wait(h1_hbm, buf(slot), sem_ref.at[slot], tm)
        h1 = _load_slabs(xbuf_ref, slot * buf_rows, tm, SLAB, SLAB_PITCH)
        x = _rms(h1, gmoe_ref[...], h1.shape[1]).astype(BF16)
        a = jnp.dot(x, w1_buf[wslot], preferred_element_type=F32)
        g = jnp.dot(x, w3_buf[wslot], preferred_element_type=F32)
        hid = (a * jax.nn.sigmoid(a) * g).astype(BF16)
        _store_slabs(y_ref, jnp.dot(hid, w2_buf[wslot], preferred_element_type=F32), SLAB)

    @pl.when(i >= n_active)
    def _():
        y_ref[...] = jnp.zeros_like(y_ref)


def _moe(plan, h1_slabs, gmoe, w1, w3, w2):
    tile_expert, n_active, first, wslot, next_expert, row_token3 = plan
    n_tiles = row_token3.shape[0]
    tm = TM_MOE
    _, d_model, d_exp = w1.shape
    last = n_tiles - 1
    grid_spec = pltpu.PrefetchScalarGridSpec(
        num_scalar_prefetch=5,
        grid=(n_tiles,),
        in_specs=[pl.BlockSpec((1, 1, tm), lambda i, *_: (i, 0, 0), memory_space=pltpu.SMEM),
                  pl.BlockSpec((1, 1, tm), lambda i, *_: (jnp.minimum(i + 1, last), 0, 0), memory_space=pltpu.SMEM),
                  pl.BlockSpec(memory_space=pl.ANY), pl.BlockSpec(gmoe.shape, lambda i, *_: (0, 0)),
                  pl.BlockSpec(memory_space=pl.ANY), pl.BlockSpec(memory_space=pl.ANY), pl.BlockSpec(memory_space=pl.ANY)],
        out_specs=pl.BlockSpec((tm * SLAB, LANES), lambda i, *_: (i, 0)),
        scratch_shapes=[pltpu.VMEM((2 * tm * SLAB_PITCH, LANES), F32),
                        pltpu.VMEM((2, d_model, d_exp), BF16), pltpu.VMEM((2, d_model, d_exp), BF16),
                        pltpu.VMEM((2, d_exp, d_model), BF16),
                        pltpu.SemaphoreType.DMA((2,)), pltpu.SemaphoreType.DMA((3, 2))],
    )
    return pl.pallas_call(
        functools.partial(_moe_kernel, tm=tm),
        grid_spec=grid_spec,
        out_shape=jax.ShapeDtypeStruct((n_tiles * tm * SLAB, LANES), F32),
        compiler_params=pltpu.CompilerParams(dimension_semantics=("arbitrary",), vmem_limit_bytes=VMEM_LIMIT),
        name="moe",
    )(tile_expert, n_active, first, wslot, next_expert, row_token3, row_token3, h1_slabs, gmoe, w1, w3, w2)


def _ple_kernel(dest_cur_ref, dest_next_ref, y_hbm, h1_ref, route_ref, p_ref, wple_ref, gple_ref, wpg_ref, bpg_ref,
                o_ref, ybuf_ref, sem_ref):
    i = pl.program_id(0)
    n = pl.num_programs(0)
    tm, d_model = o_ref.shape
    slot = i % 2
    buf_rows = TOP_K * tm * SLAB_PITCH

    def buf(s):
        return ybuf_ref.at[pl.ds(s * buf_rows, buf_rows), :]

    @pl.when(i == 0)
    def _():
        _row_gather_start(dest_cur_ref, y_hbm, buf(0), sem_ref.at[0], TOP_K * tm)

    @pl.when(i + 1 < n)
    def _():
        _row_gather_start(dest_next_ref, y_hbm, buf(1 - slot), sem_ref.at[1 - slot], TOP_K * tm)

    _row_gather_wait(y_hbm, buf(slot), sem_ref.at[slot], TOP_K * tm)
    route = route_ref[...]
    y0 = _load_slabs(ybuf_ref, slot * buf_rows, tm, SLAB, SLAB_PITCH)
    y1 = _load_slabs(ybuf_ref, slot * buf_rows + tm * SLAB_PITCH, tm, SLAB, SLAB_PITCH)
    h2 = _load_slabs(h1_ref, 0, tm, SLAB, SLAB_PITCH) + (route[:, 2:3] * y0 + route[:, 3:4] * y1)
    xn = _rms(h2, gple_ref[...], d_model).astype(BF16)
    gate = jax.nn.sigmoid(jnp.dot(xn, wpg_ref[...], preferred_element_type=F32) + bpg_ref[...])
    pe = jnp.dot(p_ref[...].astype(BF16), wple_ref[...], preferred_element_type=F32)
    o_ref[...] = h2 + gate * pe


def _ple(dest3, y, h1_slabs, route, p2, w_ple, gple, w_pg, b_pg):
    t = route.shape[0]
    d_model = w_pg.shape[0]
    tm = TM_PLE
    n_tiles = t // tm
    last = n_tiles - 1
    const = lambda i: (0, 0)
    row = lambda i: (i, 0)
    full = lambda a: pl.BlockSpec(a.shape, const)
    return pl.pallas_call(
        _ple_kernel,
        grid=(n_tiles,),
        in_specs=[pl.BlockSpec((1, 1, TOP_K * tm), lambda i: (i, 0, 0), memory_space=pltpu.SMEM),
                  pl.BlockSpec((1, 1, TOP_K * tm), lambda i: (jnp.minimum(i + 1, last), 0, 0), memory_space=pltpu.SMEM),
                  pl.BlockSpec(memory_space=pl.ANY),
                  pl.BlockSpec((tm * SLAB_PITCH, LANES), row), pl.BlockSpec((tm, LANES), row), pl.BlockSpec((tm, p2.shape[1]), row),
                  full(w_ple), full(gple), pl.BlockSpec(w_pg.shape, const, pipeline_mode=pl.Buffered(1)), full(b_pg)],
        out_specs=pl.BlockSpec((tm, d_model), row),
        out_shape=jax.ShapeDtypeStruct((t, d_model), F32),
        scratch_shapes=[pltpu.VMEM((2 * TOP_K * tm * SLAB_PITCH, LANES), F32), pltpu.SemaphoreType.DMA((2,))],
        compiler_params=pltpu.CompilerParams(dimension_semantics=("arbitrary",), vmem_limit_bytes=VMEM_LIMIT),
        name="ple",
    )(dest3, dest3, y, h1_slabs, route, p2, w_ple, gple, w_pg, b_pg)


def _dispatch_plan(route):
    t = route.shape[0]
    tm = TM_MOE
    total = TOP_K * t
    n_tiles = total // tm + N_EXPERTS
    expert = route[:, :TOP_K].astype(jnp.int32).reshape(-1)
    pos = jnp.arange(total, dtype=jnp.int32)
    order = jnp.sort(expert * total + pos) % total
    rank = jnp.sort(order * total + pos) % total
    experts = jnp.arange(N_EXPERTS, dtype=jnp.int32)
    onehot = (expert[:, None] == experts[None, :]).astype(jnp.int32)
    counts = jnp.sum(onehot, axis=0)
    group_start = jnp.cumsum(counts) - counts
    tile_end = jnp.cumsum((counts + tm - 1) // tm)
    tile_start = tile_end - (counts + tm - 1) // tm
    n_active = tile_end[-1]
    shift = tile_start * tm - group_start
    dest = rank + jnp.sum(onehot * shift[None, :], axis=1)
    tiles = jnp.minimum(jnp.arange(n_tiles, dtype=jnp.int32), n_active - 1)
    tile_expert = jnp.sum((tile_end[None, :] <= tiles[:, None]).astype(jnp.int32), axis=1)
    j = jnp.arange(tm, dtype=jnp.int32)
    fill_row = (tile_start * tm + counts)[:, None] + j[None, :]
    fill_row = jnp.where(fill_row < (tile_end * tm)[:, None], fill_row, n_tiles * tm)
    fill_tok = (experts[:, None] * tm + j[None, :]) % t
    keys = jnp.concatenate([dest * t + pos // TOP_K, (fill_row * t + fill_tok).reshape(-1)])
    row_token3 = (jnp.sort(keys) % t).reshape(n_tiles, 1, tm)
    dest3 = dest.reshape(t // TM_PLE, TM_PLE, TOP_K).transpose(0, 2, 1).reshape(t // TM_PLE, 1, TOP_K * TM_PLE)
    prev_expert = jnp.concatenate([jnp.full((1,), -1, jnp.int32), tile_expert[:-1]])
    first = (tile_expert != prev_expert).astype(jnp.int32)
    nonempty = counts > 0
    wslot = (jnp.sum((nonempty[None, :] & (experts[None, :] < tile_expert[:, None])).astype(jnp.int32), axis=1)) % 2
    later = nonempty[None, :] & (experts[None, :] > tile_expert[:, None])
    next_expert = jnp.min(jnp.where(later, experts[None, :], N_EXPERTS), axis=1)
    next_expert = jnp.where(next_expert < N_EXPERTS, next_expert, -1)
    plan = (tile_expert, n_active.reshape(1), first, wslot, next_expert, row_token3)
    return plan, dest3


def _rope_swap_pad(w):
    half = QK_ROPE // 2
    x1 = w[..., QK_NOPE:QK_NOPE + half]
    x2 = w[..., QK_NOPE + half:]
    return jnp.concatenate([w[..., :QK_NOPE], x1, x2, x2, x1], axis=-1)


def kernel(x, p, positions, norm_mix, w_in, g_q_lat, g_kv_lat, w_uq, w_ukv, g_q_head, g_k_head, w_conv, g_out_attn, g_out_conv, w_out, norm_moe, w_group, b_group, w_router, b_router, w1, w3, w2, norm_ple, w_ple, w_ple_gate, b_ple_gate):
    batch, seq, d_model = x.shape
    depth = w_in.shape[0]
    t = batch * seq
    q_rank = g_q_lat.shape[1]
    kv_rank = g_kv_lat.shape[1]
    half = QK_ROPE // 2
    rope_lo = q_rank + kv_rank

    pos2 = positions.reshape(t, 1)
    inv_freq = ROPE_BASE ** (-jnp.arange(0, QK_ROPE, 2, dtype=F32) / QK_ROPE)
    freq = jnp.tile(inv_freq, LANES // half).reshape(1, LANES)

    h = x.reshape(t, d_model)
    for i in range(depth):
        w_in_t = jnp.swapaxes(w_in[i], 0, 1).astype(BF16)
        w_rope = jnp.concatenate([w_in_t[rope_lo:rope_lo + QK_ROPE], w_in_t[rope_lo + half:rope_lo + QK_ROPE],
                                  w_in_t[rope_lo:rope_lo + half]], axis=0)
        wuq_p = _rope_swap_pad(w_uq[i]).reshape(q_rank, MLA_HEADS * HEAD_PAD).astype(BF16)
        wukv_p = jnp.concatenate([w_ukv[i][:, :, :QK_NOPE].reshape(kv_rank, -1),
                                  w_ukv[i][:, :, QK_NOPE:].reshape(kv_rank, -1)], axis=1).astype(BF16)
        gq_p = _rope_swap_pad(g_q_head[i]).reshape(1, HEAD_PAD)
        gk_p = _rope_swap_pad(g_k_head[i]).reshape(1, HEAD_PAD)
        as_steps = lambda w, n: w.reshape(n, -1, w.shape[-1])
        q, k, v, yc, w1b = _front(h, pos2, freq, norm_mix[i].reshape(1, -1), w_in_t, w_rope, g_q_lat[i].reshape(1, -1),
                                  g_kv_lat[i].reshape(1, -1), wuq_p, wukv_p, gq_p, gk_p, w_conv[i],
                                  g_out_conv[i].reshape(1, -1), as_steps(w1[i], t // TM_FRONT), seq=seq)
        n_steps = batch * MLA_HEADS * (seq // TQ_ATTN)
        ya, (w3b, w2b) = _attention(q, k, v, [as_steps(w3[i], n_steps), as_steps(w2[i], n_steps)], batch=batch, seq=seq)
        w1b, w3b, w2b = w1b.reshape(w1[i].shape), w3b.reshape(w3[i].shape), w2b.reshape(w2[i].shape)

        pad = LANES - N_GROUPS - N_EXPERTS
        w_gr32 = jnp.concatenate([w_group[i], w_router[i], jnp.zeros((d_model, pad), F32)], axis=1)
        w_gr_hi = w_gr32.astype(BF16)
        w_gr = jnp.concatenate([w_gr_hi, (w_gr32 - w_gr_hi.astype(F32)).astype(BF16)], axis=1)
        b_gr = jnp.concatenate([b_group[i], b_router[i], jnp.zeros((pad,), F32)]).reshape(1, LANES)
        h1_slabs, route = _outproj(ya, yc, h, g_out_attn[i].reshape(1, -1), w_out[i].astype(BF16),
                                   norm_moe[i].reshape(1, -1), w_gr, b_gr)

        plan, dest3 = _dispatch_plan(route)
        y = _moe(plan, h1_slabs, norm_moe[i].reshape(1, -1), w1b, w3b, w2b)
        h = _ple(dest3, y, h1_slabs, route, p[i].reshape(t, -1), w_ple[i].astype(BF16), norm_ple[i].reshape(1, -1),
                 w_ple_gate[i].astype(BF16), b_ple_gate[i].reshape(1, -1))
    return h.reshape(batch, seq, d_model)
```

```python
import functools

import jax
import jax.numpy as jnp
from jax import lax
from jax.experimental import pallas as pl
from jax.experimental.pallas import tpu as pltpu

F32 = jnp.float32
BF16 = jnp.bfloat16

CHUNK = 64
EPS = 1e-6
MASK_VALUE = -1e30
ROPE_BASE = 10000.0

MLA_HEADS = 8
QK_NOPE = 128
QK_ROPE = 64
QK_HEAD = QK_NOPE + QK_ROPE
V_HEAD = 128
HEAD_PAD = 256
Q_SCALE = QK_HEAD ** -0.5 * 1.4426950408889634
N_GROUPS = 4
EXPERTS_PER_GROUP = 8
N_EXPERTS = N_GROUPS * EXPERTS_PER_GROUP
TOP_K = 2
LANES = 128
SLAB = 16
SLAB_PITCH = 20
DMA_PRIORITIES = 2

TM_FRONT = 256
TQ_ATTN = 1024
TM_OUT = 512
TM_MOE = 256
TM_PLE = 256
VMEM_LIMIT = 56 * 1024 * 1024


def _rms(x, g, n):
    ms = jnp.sum(x * x, axis=-1, keepdims=True) * (1.0 / n)
    return x * lax.rsqrt(ms + EPS) * g


def _store_slabs(ref, val, pitch, mask=None):
    n, width = val.shape
    for c in range(width // LANES):
        piece = val[:, c * LANES:(c + 1) * LANES]
        rows = pl.ds(c, n, stride=pitch)
        if mask is not None:
            piece = jnp.where(mask, piece, ref[rows, :])
        ref[rows, :] = piece


def _load_slabs(ref, start, n, slab, pitch):
    return jnp.concatenate([ref[pl.ds(start + c, n, stride=pitch), :] for c in range(slab)], axis=1)


def _front_kernel(x_ref, pos_ref, freq_ref, gmix_ref, win_ref, wrope_ref, gql_ref, gkvl_ref,
                  wuq_ref, wukv_ref, gq_ref, gk_ref, wconv_ref, gconv_ref, *rest,
                  n_side, tiles_per_seq, q_rank, kv_rank, d_conv):
    side_refs = rest[:n_side]
    q_ref, k_ref, v_ref, yc_ref = rest[n_side:n_side + 4]
    side_out_refs = rest[n_side + 4:2 * n_side + 4]
    prev_ref = rest[-1]
    i = pl.program_id(0)
    is_start = (i % tiles_per_seq) == 0
    tm = x_ref.shape[0]
    d_model = x_ref.shape[1]
    rope_lo = q_rank + kv_rank
    nt = (((1,), (1,)), ((), ()))

    @pl.when(is_start)
    def _():
        prev_ref[...] = jnp.zeros_like(prev_ref)

    xn = _rms(x_ref[...], gmix_ref[...], d_model).astype(BF16)
    lat = lax.dot_general(xn, win_ref[:rope_lo, :], nt, preferred_element_type=F32)
    q_lat = lat[:, :q_rank]
    kv_lat = lat[:, q_rank:]
    kr = lax.dot_general(xn, wrope_ref[...], nt, preferred_element_type=F32)

    qn = _rms(q_lat, gql_ref[...], q_rank).astype(BF16)
    kvn = _rms(kv_lat, gkvl_ref[...], kv_rank).astype(BF16)
    q_all = jnp.dot(qn, wuq_ref[...], preferred_element_type=F32)
    kv_all = jnp.dot(kvn, wukv_ref[...], preferred_element_type=F32)

    lane = lax.broadcasted_iota(jnp.int32, (tm, LANES), 1)
    ang = pos_ref[...].astype(F32) * freq_ref[...]
    cos = jnp.cos(ang)
    sin = jnp.sin(ang)
    rope_c = jnp.where(lane < QK_ROPE, cos, 0.0)
    rope_s = jnp.where(lane < QK_ROPE // 2, -sin, jnp.where(lane < QK_ROPE, sin, 0.0))
    rope_lanes = lane < QK_ROPE

    def rope(t):
        return t * rope_c + pltpu.roll(t, QK_ROPE, 1) * rope_s

    gq = gq_ref[...]
    gk = gk_ref[...]
    inv_n = 1.0 / QK_HEAD
    kr_ssq = jnp.sum(jnp.where(rope_lanes, kr * kr, 0.0), axis=-1, keepdims=True)
    kr_rot = rope(kr * gk[:, QK_NOPE:])
    for h in range(MLA_HEADS):
        qa = q_all[:, h * HEAD_PAD:h * HEAD_PAD + QK_NOPE]
        qb = q_all[:, h * HEAD_PAD + QK_NOPE:(h + 1) * HEAD_PAD]
        ssq = (jnp.sum(qa * qa, axis=-1, keepdims=True)
               + jnp.sum(jnp.where(rope_lanes, qb * qb, 0.0), axis=-1, keepdims=True))
        inv = lax.rsqrt(ssq * inv_n + EPS)
        q_ref[:, h * HEAD_PAD:h * HEAD_PAD + QK_NOPE] = (qa * inv * gq[:, :QK_NOPE] * Q_SCALE).astype(BF16)
        q_ref[:, h * HEAD_PAD + QK_NOPE:(h + 1) * HEAD_PAD] = (rope(qb * inv * gq[:, QK_NOPE:]) * Q_SCALE).astype(BF16)

        ka = kv_all[:, h * QK_NOPE:(h + 1) * QK_NOPE]
        kinv = lax.rsqrt((jnp.sum(ka * ka, axis=-1, keepdims=True) + kr_ssq) * inv_n + EPS)
        k_ref[:, h * HEAD_PAD:h * HEAD_PAD + QK_NOPE] = (ka * kinv * gk[:, :QK_NOPE]).astype(BF16)
        k_ref[:, h * HEAD_PAD + QK_NOPE:(h + 1) * HEAD_PAD] = (kr_rot * kinv).astype(BF16)
    v_ref[...] = kv_all[:, MLA_HEADS * QK_NOPE:].astype(BF16)

    conv_lo = rope_lo + QK_ROPE
    half_c = d_conv // 2
    row = lax.broadcasted_iota(jnp.int32, (tm, half_c), 0)
    wc = wconv_ref[...]
    ys = []
    ssq = jnp.zeros((tm, 1), F32)
    for cs in (slice(0, half_c), slice(half_c, d_conv)):
        b_gate, c_gate, h_conv = (lax.dot_general(xn, win_ref[conv_lo + k * d_conv + cs.start:conv_lo + k * d_conv + cs.stop, :],
                                                  nt, preferred_element_type=F32) for k in range(3))
        u = c_gate * h_conv
        prev = prev_ref[:, cs]
        hm1 = prev[7:8, :]
        hm2 = prev[6:7, :]
        u1 = jnp.where(row == 0, hm1, pltpu.roll(u, 1, 0))
        u2 = jnp.where(row == 0, hm2, jnp.where(row == 1, hm1, pltpu.roll(u, 2, 0)))
        prev_ref[:, cs] = u[tm - 8:, :]
        y = b_gate * (u2 * wc[0:1, cs] + u1 * wc[1:2, cs] + u * wc[2:3, cs])
        ssq = ssq + jnp.sum(y * y, axis=-1, keepdims=True)
        ys.append(y)
    inv = lax.rsqrt(ssq * (1.0 / d_conv) + EPS)
    yc_ref[...] = (jnp.concatenate(ys, axis=1) * inv * gconv_ref[...]).astype(BF16)
    for w_ref, wb_ref in zip(side_refs, side_out_refs):
        wb_ref[...] = w_ref[...].astype(BF16)


def _front(x2, pos2, freq, gmix, w_in_t, w_rope, gql, gkvl, wuq_p, wukv_p, gq_p, gk_p, wconv, gconv, sides, *, seq):
    t, d_model = x2.shape
    tm = TM_FRONT
    q_rank = gql.shape[1]
    kv_rank = gkvl.shape[1]
    d_conv = gconv.shape[1]
    const = lambda i: (0, 0)
    row = lambda i: (i, 0)
    full = lambda a: pl.BlockSpec(a.shape, const)
    side_specs = [pl.BlockSpec((1,) + w.shape[1:], lambda i: (i, 0, 0)) for w in sides]
    kern = functools.partial(_front_kernel, n_side=len(sides), tiles_per_seq=seq // tm, q_rank=q_rank, kv_rank=kv_rank,
                             d_conv=d_conv)
    outs = pl.pallas_call(
        kern,
        grid=(t // tm,),
        in_specs=[pl.BlockSpec((tm, d_model), row), pl.BlockSpec((tm, 1), row), full(freq), full(gmix),
                  pl.BlockSpec(w_in_t.shape, const, pipeline_mode=pl.Buffered(1)), full(w_rope),
                  full(gql), full(gkvl), full(wuq_p), full(wukv_p), full(gq_p), full(gk_p), full(wconv), full(gconv)] + side_specs,
        out_specs=[pl.BlockSpec((tm, MLA_HEADS * HEAD_PAD), row), pl.BlockSpec((tm, MLA_HEADS * HEAD_PAD), row),
                   pl.BlockSpec((tm, MLA_HEADS * V_HEAD), row), pl.BlockSpec((tm, d_conv), row)] + side_specs,
        out_shape=[jax.ShapeDtypeStruct((t, MLA_HEADS * HEAD_PAD), BF16), jax.ShapeDtypeStruct((t, MLA_HEADS * HEAD_PAD), BF16),
                   jax.ShapeDtypeStruct((t, MLA_HEADS * V_HEAD), BF16), jax.ShapeDtypeStruct((t, d_conv), BF16)]
                  + [jax.ShapeDtypeStruct(w.shape, BF16) for w in sides],
        scratch_shapes=[pltpu.VMEM((8, d_conv), F32)],
        compiler_params=pltpu.CompilerParams(dimension_semantics=("arbitrary",), vmem_limit_bytes=VMEM_LIMIT),
        name="front",
    )(x2, pos2, freq, gmix, w_in_t, w_rope, gql, gkvl, wuq_p, wukv_p, gq_p, gk_p, wconv, gconv, *sides)
    return outs[:4], outs[4:]


def _attn_kernel(q_ref, k_ref, v_ref, *rest):
    n_side = (len(rest) - 1) // 2
    o_ref = rest[n_side]

    def round_side_weights():
        for w_ref, wb_ref in zip(rest[:n_side], rest[n_side + 1:]):
            wb_ref[...] = w_ref[...].astype(BF16)

    i = pl.program_id(2)
    tq = q_ref.shape[0]
    half = tq // 2
    q_halves = (q_ref[:half, :], q_ref[half:, :])

    def step(start, n_keys, masks, carry):
        kvs = [(k_ref[pl.ds(start, n), :],
                jnp.concatenate([v_ref[pl.ds(start, n), :], jnp.ones((n, HEAD_PAD - V_HEAD), BF16)], axis=1)) for n in n_keys]
        scores = [lax.dot_general(qh, kb, (((1,), (1,)), ((), ())), preferred_element_type=F32)
                  for qh, (kb, _) in zip(q_halves, kvs)]
        scores = [s if mask is None else jnp.where(mask, s, MASK_VALUE) for s, mask in zip(scores, masks)]
        m_new = [jnp.maximum(m, jnp.max(s, axis=-1, keepdims=True)) for s, (m, _) in zip(scores, carry)]
        probs = [jnp.exp2(s - mn).astype(BF16) for s, mn in zip(scores, m_new)]
        return tuple((mn, jnp.exp2(m - mn) * acc + jnp.dot(p, vb, preferred_element_type=F32))
                     for p, mn, (m, acc), (_, vb) in zip(probs, m_new, carry, kvs))

    def chunk_mask(n, first_row):
        r = (lax.broadcasted_iota(jnp.int32, (half, n), 0) + first_row) // CHUNK
        c = lax.broadcasted_iota(jnp.int32, (half, n), 1) // CHUNK
        return c <= r

    for qi in range(k_ref.shape[0] // tq):
        @pl.when(i == qi)
        def _(qi=qi):
            round_side_weights()
            carry = tuple((jnp.full((half, 1), MASK_VALUE, F32), jnp.zeros((half, HEAD_PAD), F32)) for _ in range(2))
            for j in range(qi):
                carry = step(j * tq, (tq, tq), (None, None), carry)
            (_, acc0), (_, acc1) = step(qi * tq, (half, tq), (chunk_mask(half, 0), chunk_mask(tq, half)), carry)
            o_ref[:half, :] = (acc0[:, :V_HEAD] / acc0[:, V_HEAD:]).astype(o_ref.dtype)
            o_ref[half:, :] = (acc1[:, :V_HEAD] / acc1[:, V_HEAD:]).astype(o_ref.dtype)


def _attention(q, k, v, side_weights, *, batch, seq):
    tq = TQ_ATTN
    nq = seq // tq
    step = lambda b, h, i: ((b * MLA_HEADS + h) * nq + i, 0, 0)
    side_specs = [pl.BlockSpec((1,) + w.shape[1:], step) for w in side_weights]
    outs = pl.pallas_call(
        _attn_kernel,
        grid=(batch, MLA_HEADS, nq),
        in_specs=[pl.BlockSpec((tq, HEAD_PAD), lambda b, h, i: (b * nq + i, h)),
                  pl.BlockSpec((seq, HEAD_PAD), lambda b, h, i: (b, h)),
                  pl.BlockSpec((seq, V_HEAD), lambda b, h, i: (b, h))] + side_specs,
        out_specs=[pl.BlockSpec((tq, V_HEAD), lambda b, h, i: (b * nq + i, h))] + side_specs,
        out_shape=[jax.ShapeDtypeStruct((batch * seq, MLA_HEADS * V_HEAD), BF16)]
                  + [jax.ShapeDtypeStruct(w.shape, BF16) for w in side_weights],
        compiler_params=pltpu.CompilerParams(dimension_semantics=("arbitrary", "arbitrary", "arbitrary"),
                                             vmem_limit_bytes=VMEM_LIMIT),
        name="attn",
    )(q, k, v, *side_weights)
    return outs[0], outs[1:]


def _outproj_kernel(ya_ref, yc_ref, x_ref, gattn_ref, wout_ref, gmoe_ref, wgr_ref, bgr_ref,
                    h1_ref, xm_ref, route_ref):
    tm, d_attn = ya_ref.shape
    d_model = x_ref.shape[1]
    ya = _rms(ya_ref[...].astype(F32), gattn_ref[...], d_attn).astype(BF16)
    h1 = (x_ref[...]
          + jnp.dot(ya, wout_ref[:d_attn, :], preferred_element_type=F32)
          + jnp.dot(yc_ref[...], wout_ref[d_attn:, :], preferred_element_type=F32))
    h1_ref[...] = h1
    xm = _rms(h1, gmoe_ref[...], d_model)
    _store_slabs(xm_ref, xm, SLAB_PITCH)
    for c in range(SLAB, SLAB_PITCH):
        xm_ref[pl.ds(c, tm, stride=SLAB_PITCH), :] = jnp.zeros((tm, LANES), F32)

    x_hi = xm.astype(BF16)
    x_lo = (xm - x_hi.astype(F32)).astype(BF16)
    hi_part = jnp.dot(x_hi, wgr_ref[...], preferred_element_type=F32)
    lo_part = jnp.dot(x_lo, wgr_ref[:, :LANES], preferred_element_type=F32)
    logits = hi_part[:, :LANES] + hi_part[:, LANES:] + lo_part + bgr_ref[...]
    lane = lax.broadcasted_iota(jnp.int32, (tm, LANES), 1).astype(F32)
    neg = -jnp.inf
    big = 1e9
    gl = jnp.where(lane < N_GROUPS, logits, neg)
    gmax = jnp.max(gl, axis=-1, keepdims=True)
    gsel = jnp.min(jnp.where(gl == gmax, lane, big), axis=-1, keepdims=True)
    p_g = 1.0 / jnp.sum(jnp.exp(gl - gmax), axis=-1, keepdims=True)
    lo = N_GROUPS + EXPERTS_PER_GROUP * gsel
    el = jnp.where((lane >= lo) & (lane < lo + EXPERTS_PER_GROUP), logits, neg)
    v1 = jnp.max(el, axis=-1, keepdims=True)
    i1 = jnp.min(jnp.where(el == v1, lane, big), axis=-1, keepdims=True)
    el2 = jnp.where(lane == i1, neg, el)
    v2 = jnp.max(el2, axis=-1, keepdims=True)
    i2 = jnp.min(jnp.where(el2 == v2, lane, big), axis=-1, keepdims=True)
    e2 = jnp.exp(v2 - v1)
    den = 1.0 + e2
    w1 = p_g / den
    w2 = p_g * e2 / den
    route_ref[...] = jnp.where(lane == 0, i1 - N_GROUPS,
                               jnp.where(lane == 1, i2 - N_GROUPS,
                                         jnp.where(lane == 2, w1, jnp.where(lane == 3, w2, 0.0))))


def _outproj(ya, yc, x2, gattn, w_out, gmoe, w_gr, b_gr):
    t, d_model = x2.shape
    d_attn = ya.shape[1]
    tm = TM_OUT
    const = lambda i: (0, 0)
    row = lambda i: (i, 0)
    full = lambda a: pl.BlockSpec(a.shape, const)
    return pl.pallas_call(
        _outproj_kernel,
        grid=(t // tm,),
        in_specs=[pl.BlockSpec((tm, d_attn), row), pl.BlockSpec((tm, yc.shape[1]), row), pl.BlockSpec((tm, d_model), row),
                  full(gattn), pl.BlockSpec(w_out.shape, const, pipeline_mode=pl.Buffered(1)), full(gmoe), full(w_gr), full(b_gr)],
        out_specs=[pl.BlockSpec((tm, d_model), row), pl.BlockSpec((tm * SLAB_PITCH, LANES), row), pl.BlockSpec((tm, LANES), row)],
        out_shape=[jax.ShapeDtypeStruct((t, d_model), F32), jax.ShapeDtypeStruct((t * SLAB_PITCH, LANES), F32),
                   jax.ShapeDtypeStruct((t, LANES), F32)],
        compiler_params=pltpu.CompilerParams(dimension_semantics=("arbitrary",), vmem_limit_bytes=VMEM_LIMIT),
        name="outproj",
    )(ya, yc, x2, gattn, w_out, gmoe, w_gr, b_gr)


def _row_gather_start(idx_ref, src_hbm, dst_ref, sem, n_rows, src_pitch=SLAB):
    def body(pair, carry):
        for prio in range(DMA_PRIORITIES):
            r = pair * DMA_PRIORITIES + prio
            src = src_hbm.at[pl.ds(idx_ref[0, 0, r] * src_pitch, SLAB), :]
            pltpu.make_async_copy(src, dst_ref.at[pl.ds(r * SLAB_PITCH, SLAB), :], sem).start(priority=prio)
        return carry
    lax.fori_loop(0, n_rows // DMA_PRIORITIES, body, 0, unroll=True)


def _row_gather_wait(src_hbm, dst_ref, sem, n_rows):
    pltpu.make_async_copy(src_hbm.at[pl.ds(0, n_rows * SLAB), :], dst_ref.at[pl.ds(0, n_rows * SLAB), :], sem).wait()


def _moe_kernel(tile_expert_ref, n_active_ref, first_ref, wslot_ref, next_expert_ref, tok_cur_ref, tok_next_ref,
                xm_hbm, w1_hbm, w3_hbm, w2_hbm, y_ref, xbuf_ref, w1_buf, w3_buf, w2_buf, sem_ref, wsem_ref, *, tm):
    i = pl.program_id(0)
    n_active = n_active_ref[0]
    slot = i % 2
    buf_rows = tm * SLAB_PITCH
    wslot = wslot_ref[i]

    def buf(s):
        return xbuf_ref.at[pl.ds(s * buf_rows, buf_rows), :]

    def weight_copies(e, s):
        return (pltpu.make_async_copy(w1_hbm.at[e], w1_buf.at[s], wsem_ref.at[0, s]),
                pltpu.make_async_copy(w3_hbm.at[e], w3_buf.at[s], wsem_ref.at[1, s]),
                pltpu.make_async_copy(w2_hbm.at[e], w2_buf.at[s], wsem_ref.at[2, s]))

    @pl.when(i == 0)
    def _():
        for cp in weight_copies(tile_expert_ref[0], 0):
            cp.start()
        _row_gather_start(tok_cur_ref, xm_hbm, buf(0), sem_ref.at[0], tm, src_pitch=SLAB_PITCH)

    active = i < n_active
    new_expert = active & (first_ref[i] == 1)

    @pl.when(new_expert & (next_expert_ref[i] >= 0))
    def _():
        for cp in weight_copies(next_expert_ref[i], 1 - wslot):
            cp.start()

    @pl.when(i + 1 < n_active)
    def _():
        _row_gather_start(tok_next_ref, xm_hbm, buf(1 - slot), sem_ref.at[1 - slot], tm, src_pitch=SLAB_PITCH)

    @pl.when(new_expert)
    def _():
        for cp in weight_copies(tile_expert_ref[i], wslot):
            cp.wait()

    @pl.when(active)
    def _():
        _row_gather_wait(xm_hbm, buf(slot), sem_ref.at[slot], tm)
        x = _load_slabs(xbuf_ref, slot * buf_rows, tm, SLAB, SLAB_PITCH).astype(BF16)
        a = jnp.dot(x, w1_buf[wslot], preferred_element_type=F32)
        g = jnp.dot(x, w3_buf[wslot], preferred_element_type=F32)
        hid = (a * jax.nn.sigmoid(a) * g).astype(BF16)
        _store_slabs(y_ref, jnp.dot(hid, w2_buf[wslot], preferred_element_type=F32), SLAB)

    @pl.when(i >= n_active)
    def _():
        y_ref[...] = jnp.zeros_like(y_ref)


def _moe(plan, xm, w1, w3, w2):
    tile_expert, n_active, first, wslot, next_expert, row_token3 = plan
    n_tiles = row_token3.shape[0]
    tm = TM_MOE
    _, d_model, d_exp = w1.shape
    last = n_tiles - 1
    grid_spec = pltpu.PrefetchScalarGridSpec(
        num_scalar_prefetch=5,
        grid=(n_tiles,),
        in_specs=[pl.BlockSpec((1, 1, tm), lambda i, *_: (i, 0, 0), memory_space=pltpu.SMEM),
                  pl.BlockSpec((1, 1, tm), lambda i, *_: (jnp.minimum(i + 1, last), 0, 0), memory_space=pltpu.SMEM),
                  pl.BlockSpec(memory_space=pl.ANY), pl.BlockSpec(memory_space=pl.ANY),
                  pl.BlockSpec(memory_space=pl.ANY), pl.BlockSpec(memory_space=pl.ANY)],
        out_specs=pl.BlockSpec((tm * SLAB, LANES), lambda i, *_: (i, 0)),
        scratch_shapes=[pltpu.VMEM((2 * tm * SLAB_PITCH, LANES), F32),
                        pltpu.VMEM((2, d_model, d_exp), BF16), pltpu.VMEM((2, d_model, d_exp), BF16),
                        pltpu.VMEM((2, d_exp, d_model), BF16),
                        pltpu.SemaphoreType.DMA((2,)), pltpu.SemaphoreType.DMA((3, 2))],
    )
    return pl.pallas_call(
        functools.partial(_moe_kernel, tm=tm),
        grid_spec=grid_spec,
        out_shape=jax.ShapeDtypeStruct((n_tiles * tm * SLAB, LANES), F32),
        compiler_params=pltpu.CompilerParams(dimension_semantics=("arbitrary",), vmem_limit_bytes=VMEM_LIMIT),
        name="moe",
    )(tile_expert, n_active, first, wslot, next_expert, row_token3, row_token3, xm, w1, w3, w2)


def _ple_kernel(dest_cur_ref, dest_next_ref, y_hbm, h1_ref, route_ref, p_ref, wple_ref, gple_ref, wpg_ref, bpg_ref,
                o_ref, ybuf_ref, sem_ref):
    i = pl.program_id(0)
    n = pl.num_programs(0)
    tm, d_model = h1_ref.shape
    slot = i % 2
    buf_rows = TOP_K * tm * SLAB_PITCH

    def buf(s):
        return ybuf_ref.at[pl.ds(s * buf_rows, buf_rows), :]

    @pl.when(i == 0)
    def _():
        _row_gather_start(dest_cur_ref, y_hbm, buf(0), sem_ref.at[0], TOP_K * tm)

    @pl.when(i + 1 < n)
    def _():
        _row_gather_start(dest_next_ref, y_hbm, buf(1 - slot), sem_ref.at[1 - slot], TOP_K * tm)

    _row_gather_wait(y_hbm, buf(slot), sem_ref.at[slot], TOP_K * tm)
    route = route_ref[...]
    y0 = _load_slabs(ybuf_ref, slot * buf_rows, tm, SLAB, SLAB_PITCH)
    y1 = _load_slabs(ybuf_ref, slot * buf_rows + tm * SLAB_PITCH, tm, SLAB, SLAB_PITCH)
    h2 = h1_ref[...] + (route[:, 2:3] * y0 + route[:, 3:4] * y1)
    xn = _rms(h2, gple_ref[...], d_model).astype(BF16)
    gate = jax.nn.sigmoid(jnp.dot(xn, wpg_ref[...], preferred_element_type=F32) + bpg_ref[...])
    pe = jnp.dot(p_ref[...].astype(BF16), wple_ref[...], preferred_element_type=F32)
    o_ref[...] = h2 + gate * pe


def _ple(dest3, y, h1, route, p2, w_ple, gple, w_pg, b_pg):
    t, d_model = h1.shape
    tm = TM_PLE
    n_tiles = t // tm
    last = n_tiles - 1
    const = lambda i: (0, 0)
    row = lambda i: (i, 0)
    full = lambda a: pl.BlockSpec(a.shape, const)
    return pl.pallas_call(
        _ple_kernel,
        grid=(n_tiles,),
        in_specs=[pl.BlockSpec((1, 1, TOP_K * tm), lambda i: (i, 0, 0), memory_space=pltpu.SMEM),
                  pl.BlockSpec((1, 1, TOP_K * tm), lambda i: (jnp.minimum(i + 1, last), 0, 0), memory_space=pltpu.SMEM),
                  pl.BlockSpec(memory_space=pl.ANY),
                  pl.BlockSpec((tm, d_model), row), pl.BlockSpec((tm, LANES), row), pl.BlockSpec((tm, p2.shape[1]), row),
                  full(w_ple), full(gple), pl.BlockSpec(w_pg.shape, const, pipeline_mode=pl.Buffered(1)), full(b_pg)],
        out_specs=pl.BlockSpec((tm, d_model), row),
        out_shape=jax.ShapeDtypeStruct((t, d_model), F32),
        scratch_shapes=[pltpu.VMEM((2 * TOP_K * tm * SLAB_PITCH, LANES), F32), pltpu.SemaphoreType.DMA((2,))],
        compiler_params=pltpu.CompilerParams(dimension_semantics=("arbitrary",), vmem_limit_bytes=VMEM_LIMIT),
        name="ple",
    )(dest3, dest3, y, h1, route, p2, w_ple, gple, w_pg, b_pg)


def _dispatch_plan(route):
    t = route.shape[0]
    tm = TM_MOE
    total = TOP_K * t
    n_tiles = total // tm + N_EXPERTS
    expert = route[:, :TOP_K].astype(jnp.int32).reshape(-1)
    pos = jnp.arange(total, dtype=jnp.int32)
    order = jnp.sort(expert * total + pos) % total
    rank = jnp.sort(order * total + pos) % total
    experts = jnp.arange(N_EXPERTS, dtype=jnp.int32)
    onehot = (expert[:, None] == experts[None, :]).astype(jnp.int32)
    counts = jnp.sum(onehot, axis=0)
    group_start = jnp.cumsum(counts) - counts
    tile_end = jnp.cumsum((counts + tm - 1) // tm)
    tile_start = tile_end - (counts + tm - 1) // tm
    n_active = tile_end[-1]
    shift = tile_start * tm - group_start
    dest = rank + jnp.sum(onehot * shift[None, :], axis=1)
    tiles = jnp.minimum(jnp.arange(n_tiles, dtype=jnp.int32), n_active - 1)
    tile_expert = jnp.sum((tile_end[None, :] <= tiles[:, None]).astype(jnp.int32), axis=1)
    j = jnp.arange(tm, dtype=jnp.int32)
    fill_row = (tile_start * tm + counts)[:, None] + j[None, :]
    fill_row = jnp.where(fill_row < (tile_end * tm)[:, None], fill_row, n_tiles * tm)
    fill_tok = (experts[:, None] * tm + j[None, :]) % t
    keys = jnp.concatenate([dest * t + pos // TOP_K, (fill_row * t + fill_tok).reshape(-1)])
    row_token3 = (jnp.sort(keys) % t).reshape(n_tiles, 1, tm)
    dest3 = dest.reshape(t // TM_PLE, TM_PLE, TOP_K).transpose(0, 2, 1).reshape(t // TM_PLE, 1, TOP_K * TM_PLE)
    prev_expert = jnp.concatenate([jnp.full((1,), -1, jnp.int32), tile_expert[:-1]])
    first = (tile_expert != prev_expert).astype(jnp.int32)
    nonempty = counts > 0
    wslot = (jnp.sum((nonempty[None, :] & (experts[None, :] < tile_expert[:, None])).astype(jnp.int32), axis=1)) % 2
    later = nonempty[None, :] & (experts[None, :] > tile_expert[:, None])
    next_expert = jnp.min(jnp.where(later, experts[None, :], N_EXPERTS), axis=1)
    next_expert = jnp.where(next_expert < N_EXPERTS, next_expert, -1)
    plan = (tile_expert, n_active.reshape(1), first, wslot, next_expert, row_token3)
    return plan, dest3


def _rope_swap_pad(w):
    half = QK_ROPE // 2
    x1 = w[..., QK_NOPE:QK_NOPE + half]
    x2 = w[..., QK_NOPE + half:]
    return jnp.concatenate([w[..., :QK_NOPE], x1, x2, x2, x1], axis=-1)


def kernel(x, p, positions, norm_mix, w_in, g_q_lat, g_kv_lat, w_uq, w_ukv, g_q_head, g_k_head, w_conv, g_out_attn, g_out_conv, w_out, norm_moe, w_group, b_group, w_router, b_router, w1, w3, w2, norm_ple, w_ple, w_ple_gate, b_ple_gate):
    batch, seq, d_model = x.shape
    depth = w_in.shape[0]
    t = batch * seq
    q_rank = g_q_lat.shape[1]
    kv_rank = g_kv_lat.shape[1]
    half = QK_ROPE // 2
    rope_lo = q_rank + kv_rank

    pos2 = positions.reshape(t, 1)
    inv_freq = ROPE_BASE ** (-jnp.arange(0, QK_ROPE, 2, dtype=F32) / QK_ROPE)
    freq = jnp.tile(inv_freq, LANES // half).reshape(1, LANES)

    h = x.reshape(t, d_model)
    for i in range(depth):
        w_in_t = jnp.swapaxes(w_in[i], 0, 1).astype(BF16)
        w_rope = jnp.concatenate([w_in_t[rope_lo:rope_lo + QK_ROPE], w_in_t[rope_lo + half:rope_lo + QK_ROPE],
                                  w_in_t[rope_lo:rope_lo + half]], axis=0)
        wuq_p = _rope_swap_pad(w_uq[i]).reshape(q_rank, MLA_HEADS * HEAD_PAD).astype(BF16)
        wukv_p = jnp.concatenate([w_ukv[i][:, :, :QK_NOPE].reshape(kv_rank, -1),
                                  w_ukv[i][:, :, QK_NOPE:].reshape(kv_rank, -1)], axis=1).astype(BF16)
        gq_p = _rope_swap_pad(g_q_head[i]).reshape(1, HEAD_PAD)
        gk_p = _rope_swap_pad(g_k_head[i]).reshape(1, HEAD_PAD)
        as_steps = lambda w, n: w.reshape(n, -1, w.shape[-1])
        front_sides = [as_steps(w, t // TM_FRONT) for w in (w1[i], w_out[i], w_ple_gate[i])]
        (q, k, v, yc), (w1b, w_out_b, w_pg_b) = _front(
            h, pos2, freq, norm_mix[i].reshape(1, -1), w_in_t, w_rope, g_q_lat[i].reshape(1, -1),
            g_kv_lat[i].reshape(1, -1), wuq_p, wukv_p, gq_p, gk_p, w_conv[i], g_out_conv[i].reshape(1, -1), front_sides, seq=seq)
        n_steps = batch * MLA_HEADS * (seq // TQ_ATTN)
        ya, (w3b, w2b) = _attention(q, k, v, [as_steps(w3[i], n_steps), as_steps(w2[i], n_steps)], batch=batch, seq=seq)
        w1b, w3b, w2b = w1b.reshape(w1[i].shape), w3b.reshape(w3[i].shape), w2b.reshape(w2[i].shape)
        w_out_b, w_pg_b = w_out_b.reshape(w_out[i].shape), w_pg_b.reshape(w_ple_gate[i].shape)

        pad = LANES - N_GROUPS - N_EXPERTS
        w_gr32 = jnp.concatenate([w_group[i], w_router[i], jnp.zeros((d_model, pad), F32)], axis=1)
        w_gr_hi = w_gr32.astype(BF16)
        w_gr = jnp.concatenate([w_gr_hi, (w_gr32 - w_gr_hi.astype(F32)).astype(BF16)], axis=1)
        b_gr = jnp.concatenate([b_group[i], b_router[i], jnp.zeros((pad,), F32)]).reshape(1, LANES)
        h1, xm, route = _outproj(ya, yc, h, g_out_attn[i].reshape(1, -1), w_out_b,
                                 norm_moe[i].reshape(1, -1), w_gr, b_gr)

        plan, dest3 = _dispatch_plan(route)
        y = _moe(plan, xm, w1b, w3b, w2b)
        h = _ple(dest3, y, h1, route, p[i].reshape(t, -1), w_ple[i].astype(BF16), norm_ple[i].reshape(1, -1),
                 w_pg_b, b_ple_gate[i].reshape(1, -1))
    return h.reshape(batch, seq, d_model)
```

```python
import functools

import jax
import jax.numpy as jnp
from jax import lax
from jax.experimental import pallas as pl
from jax.experimental.pallas import tpu as pltpu

F32 = jnp.float32
BF16 = jnp.bfloat16

CHUNK = 64
EPS = 1e-6
MASK_VALUE = -1e30
ROPE_BASE = 10000.0

MLA_HEADS = 8
QK_NOPE = 128
QK_ROPE = 64
QK_HEAD = QK_NOPE + QK_ROPE
V_HEAD = 128
HEAD_PAD = 256
Q_SCALE = QK_HEAD ** -0.5 * 1.4426950408889634
N_GROUPS = 4
EXPERTS_PER_GROUP = 8
N_EXPERTS = N_GROUPS * EXPERTS_PER_GROUP
TOP_K = 2
LANES = 128
SLAB = 16
SLAB_PITCH = 20
DMA_PRIORITIES = 2

TM_FRONT = 256
TQ_ATTN = 1024
TM_OUT = 512
TM_MOE = 256
TM_PLE = 256
VMEM_LIMIT = 56 * 1024 * 1024


def _rms(x, g, n):
    ms = jnp.sum(x * x, axis=-1, keepdims=True) * (1.0 / n)
    return x * lax.rsqrt(ms + EPS) * g


def _store_slabs(ref, val, pitch, mask=None):
    n, width = val.shape
    for c in range(width // LANES):
        piece = val[:, c * LANES:(c + 1) * LANES]
        rows = pl.ds(c, n, stride=pitch)
        if mask is not None:
            piece = jnp.where(mask, piece, ref[rows, :])
        ref[rows, :] = piece


def _load_slabs(ref, start, n, slab, pitch):
    return jnp.concatenate([ref[pl.ds(start + c, n, stride=pitch), :] for c in range(slab)], axis=1)


def _front_kernel(x_ref, pos_ref, freq_ref, gmix_ref, win_ref, wrope_ref, gql_ref, gkvl_ref,
                  wuq_ref, wukv_ref, gq_ref, gk_ref, wconv_ref, gconv_ref, *rest,
                  n_side, tiles_per_seq, q_rank, kv_rank, d_conv):
    side_refs = rest[:n_side]
    q_ref, k_ref, v_ref, yc_ref = rest[n_side:n_side + 4]
    side_out_refs = rest[n_side + 4:2 * n_side + 4]
    prev_ref = rest[-1]
    i = pl.program_id(0)
    is_start = (i % tiles_per_seq) == 0
    tm = x_ref.shape[0]
    d_model = x_ref.shape[1]
    rope_lo = q_rank + kv_rank
    nt = (((1,), (1,)), ((), ()))

    @pl.when(is_start)
    def _():
        prev_ref[...] = jnp.zeros_like(prev_ref)

    xn = _rms(x_ref[...], gmix_ref[...], d_model).astype(BF16)
    lat = lax.dot_general(xn, win_ref[:rope_lo, :], nt, preferred_element_type=F32)
    q_lat = lat[:, :q_rank]
    kv_lat = lat[:, q_rank:]
    kr = lax.dot_general(xn, wrope_ref[...], nt, preferred_element_type=F32)

    qn = _rms(q_lat, gql_ref[...], q_rank).astype(BF16)
    kvn = _rms(kv_lat, gkvl_ref[...], kv_rank).astype(BF16)
    q_all = jnp.dot(qn, wuq_ref[...], preferred_element_type=F32)
    kv_all = jnp.dot(kvn, wukv_ref[...], preferred_element_type=F32)

    lane = lax.broadcasted_iota(jnp.int32, (tm, LANES), 1)
    ang = pos_ref[...].astype(F32) * freq_ref[...]
    cos = jnp.cos(ang)
    sin = jnp.sin(ang)
    rope_c = jnp.where(lane < QK_ROPE, cos, 0.0)
    rope_s = jnp.where(lane < QK_ROPE // 2, -sin, jnp.where(lane < QK_ROPE, sin, 0.0))
    rope_lanes = lane < QK_ROPE

    def rope(t):
        return t * rope_c + pltpu.roll(t, QK_ROPE, 1) * rope_s

    gq = gq_ref[...]
    gk = gk_ref[...]
    inv_n = 1.0 / QK_HEAD
    kr_ssq = jnp.sum(jnp.where(rope_lanes, kr * kr, 0.0), axis=-1, keepdims=True)
    kr_rot = rope(kr * gk[:, QK_NOPE:])
    for h in range(MLA_HEADS):
        qa = q_all[:, h * HEAD_PAD:h * HEAD_PAD + QK_NOPE]
        qb = q_all[:, h * HEAD_PAD + QK_NOPE:(h + 1) * HEAD_PAD]
        ssq = (jnp.sum(qa * qa, axis=-1, keepdims=True)
               + jnp.sum(jnp.where(rope_lanes, qb * qb, 0.0), axis=-1, keepdims=True))
        inv = lax.rsqrt(ssq * inv_n + EPS)
        q_ref[:, h * HEAD_PAD:h * HEAD_PAD + QK_NOPE] = (qa * inv * gq[:, :QK_NOPE] * Q_SCALE).astype(BF16)
        q_ref[:, h * HEAD_PAD + QK_NOPE:(h + 1) * HEAD_PAD] = (rope(qb * inv * gq[:, QK_NOPE:]) * Q_SCALE).astype(BF16)

        ka = kv_all[:, h * QK_NOPE:(h + 1) * QK_NOPE]
        kinv = lax.rsqrt((jnp.sum(ka * ka, axis=-1, keepdims=True) + kr_ssq) * inv_n + EPS)
        k_ref[:, h * HEAD_PAD:h * HEAD_PAD + QK_NOPE] = (ka * kinv * gk[:, :QK_NOPE]).astype(BF16)
        k_ref[:, h * HEAD_PAD + QK_NOPE:(h + 1) * HEAD_PAD] = (kr_rot * kinv).astype(BF16)
    v_ref[...] = kv_all[:, MLA_HEADS * QK_NOPE:].astype(BF16)

    conv_lo = rope_lo + QK_ROPE
    half_c = d_conv // 2
    row = lax.broadcasted_iota(jnp.int32, (tm, half_c), 0)
    wc = wconv_ref[...]
    ys = []
    ssq = jnp.zeros((tm, 1), F32)
    for cs in (slice(0, half_c), slice(half_c, d_conv)):
        b_gate, c_gate, h_conv = (lax.dot_general(xn, win_ref[conv_lo + k * d_conv + cs.start:conv_lo + k * d_conv + cs.stop, :],
                                                  nt, preferred_element_type=F32) for k in range(3))
        u = c_gate * h_conv
        prev = prev_ref[:, cs]
        hm1 = prev[7:8, :]
        hm2 = prev[6:7, :]
        u1 = jnp.where(row == 0, hm1, pltpu.roll(u, 1, 0))
        u2 = jnp.where(row == 0, hm2, jnp.where(row == 1, hm1, pltpu.roll(u, 2, 0)))
        prev_ref[:, cs] = u[tm - 8:, :]
        y = b_gate * (u2 * wc[0:1, cs] + u1 * wc[1:2, cs] + u * wc[2:3, cs])
        ssq = ssq + jnp.sum(y * y, axis=-1, keepdims=True)
        ys.append(y)
    inv = lax.rsqrt(ssq * (1.0 / d_conv) + EPS)
    yc_ref[...] = (jnp.concatenate(ys, axis=1) * inv * gconv_ref[...]).astype(BF16)
    for w_ref, wb_ref in zip(side_refs, side_out_refs):
        wb_ref[...] = w_ref[...].astype(BF16)


def _front(x2, pos2, freq, gmix, w_in_t, w_rope, gql, gkvl, wuq_p, wukv_p, gq_p, gk_p, wconv, gconv, sides, *, seq):
    t, d_model = x2.shape
    tm = TM_FRONT
    q_rank = gql.shape[1]
    kv_rank = gkvl.shape[1]
    d_conv = gconv.shape[1]
    const = lambda i: (0, 0)
    row = lambda i: (i, 0)
    full = lambda a: pl.BlockSpec(a.shape, const)
    side_in = [pl.BlockSpec((1,) + w.shape[1:], lambda i, off=off: (i + off, 0, 0)) for w, off in sides]
    side_out = [pl.BlockSpec((1,) + w.shape[1:], lambda i: (i, 0, 0)) for w, _ in sides]
    kern = functools.partial(_front_kernel, n_side=len(sides), tiles_per_seq=seq // tm, q_rank=q_rank, kv_rank=kv_rank,
                             d_conv=d_conv)
    outs = pl.pallas_call(
        kern,
        grid=(t // tm,),
        in_specs=[pl.BlockSpec((tm, d_model), row), pl.BlockSpec((tm, 1), row), full(freq), full(gmix),
                  pl.BlockSpec(w_in_t.shape, const, pipeline_mode=pl.Buffered(1)), full(w_rope),
                  full(gql), full(gkvl), full(wuq_p), full(wukv_p), full(gq_p), full(gk_p), full(wconv), full(gconv)] + side_in,
        out_specs=[pl.BlockSpec((tm, MLA_HEADS * HEAD_PAD), row), pl.BlockSpec((tm, MLA_HEADS * HEAD_PAD), row),
                   pl.BlockSpec((tm, MLA_HEADS * V_HEAD), row), pl.BlockSpec((tm, d_conv), row)] + side_out,
        out_shape=[jax.ShapeDtypeStruct((t, MLA_HEADS * HEAD_PAD), BF16), jax.ShapeDtypeStruct((t, MLA_HEADS * HEAD_PAD), BF16),
                   jax.ShapeDtypeStruct((t, MLA_HEADS * V_HEAD), BF16), jax.ShapeDtypeStruct((t, d_conv), BF16)]
                  + [jax.ShapeDtypeStruct((t // tm,) + w.shape[1:], BF16) for w, _ in sides],
        scratch_shapes=[pltpu.VMEM((8, d_conv), F32)],
        compiler_params=pltpu.CompilerParams(dimension_semantics=("arbitrary",), vmem_limit_bytes=VMEM_LIMIT),
        name="front",
    )(x2, pos2, freq, gmix, w_in_t, w_rope, gql, gkvl, wuq_p, wukv_p, gq_p, gk_p, wconv, gconv, *[w for w, _ in sides])
    return outs[:4], outs[4:]


def _attn_kernel(q_ref, k_ref, v_ref, *rest):
    n_side = (len(rest) - 1) // 2
    o_ref = rest[n_side]

    def round_side_weights():
        for w_ref, wb_ref in zip(rest[:n_side], rest[n_side + 1:]):
            wb_ref[...] = w_ref[...].astype(BF16)

    i = pl.program_id(2)
    tq = q_ref.shape[0]
    half = tq // 2
    q_halves = (q_ref[:half, :], q_ref[half:, :])

    def step(start, n_keys, masks, carry):
        kvs = [(k_ref[pl.ds(start, n), :],
                jnp.concatenate([v_ref[pl.ds(start, n), :], jnp.ones((n, HEAD_PAD - V_HEAD), BF16)], axis=1)) for n in n_keys]
        scores = [lax.dot_general(qh, kb, (((1,), (1,)), ((), ())), preferred_element_type=F32)
                  for qh, (kb, _) in zip(q_halves, kvs)]
        scores = [s if mask is None else jnp.where(mask, s, MASK_VALUE) for s, mask in zip(scores, masks)]
        m_new = [jnp.maximum(m, jnp.max(s, axis=-1, keepdims=True)) for s, (m, _) in zip(scores, carry)]
        probs = [jnp.exp2(s - mn).astype(BF16) for s, mn in zip(scores, m_new)]
        return tuple((mn, jnp.exp2(m - mn) * acc + jnp.dot(p, vb, preferred_element_type=F32))
                     for p, mn, (m, acc), (_, vb) in zip(probs, m_new, carry, kvs))

    def chunk_mask(n, first_row):
        r = (lax.broadcasted_iota(jnp.int32, (half, n), 0) + first_row) // CHUNK
        c = lax.broadcasted_iota(jnp.int32, (half, n), 1) // CHUNK
        return c <= r

    for qi in range(k_ref.shape[0] // tq):
        @pl.when(i == qi)
        def _(qi=qi):
            round_side_weights()
            carry = tuple((jnp.full((half, 1), MASK_VALUE, F32), jnp.zeros((half, HEAD_PAD), F32)) for _ in range(2))
            for j in range(qi):
                carry = step(j * tq, (tq, tq), (None, None), carry)
            (_, acc0), (_, acc1) = step(qi * tq, (half, tq), (chunk_mask(half, 0), chunk_mask(tq, half)), carry)
            o_ref[:half, :] = (acc0[:, :V_HEAD] / acc0[:, V_HEAD:]).astype(o_ref.dtype)
            o_ref[half:, :] = (acc1[:, :V_HEAD] / acc1[:, V_HEAD:]).astype(o_ref.dtype)


def _attention(q, k, v, side_weights, *, batch, seq):
    tq = TQ_ATTN
    nq = seq // tq
    n_steps = batch * MLA_HEADS * nq
    step = lambda b, h, i, off=0: ((b * MLA_HEADS + h) * nq + i + off, 0, 0)
    side_in = [pl.BlockSpec((1,) + w.shape[1:], functools.partial(step, off=off)) for w, off in side_weights]
    side_out = [pl.BlockSpec((1,) + w.shape[1:], step) for w, _ in side_weights]
    outs = pl.pallas_call(
        _attn_kernel,
        grid=(batch, MLA_HEADS, nq),
        in_specs=[pl.BlockSpec((tq, HEAD_PAD), lambda b, h, i: (b * nq + i, h)),
                  pl.BlockSpec((seq, HEAD_PAD), lambda b, h, i: (b, h)),
                  pl.BlockSpec((seq, V_HEAD), lambda b, h, i: (b, h))] + side_in,
        out_specs=[pl.BlockSpec((tq, V_HEAD), lambda b, h, i: (b * nq + i, h))] + side_out,
        out_shape=[jax.ShapeDtypeStruct((batch * seq, MLA_HEADS * V_HEAD), BF16)]
                  + [jax.ShapeDtypeStruct((n_steps,) + w.shape[1:], BF16) for w, _ in side_weights],
        compiler_params=pltpu.CompilerParams(dimension_semantics=("arbitrary", "arbitrary", "arbitrary"),
                                             vmem_limit_bytes=VMEM_LIMIT),
        name="attn",
    )(q, k, v, *[w for w, _ in side_weights])
    return outs[0], outs[1:]


def _outproj_kernel(ya_ref, yc_ref, x_ref, gattn_ref, wout_ref, gmoe_ref, wgr_ref, bgr_ref,
                    h1_ref, xm_ref, route_ref):
    tm, d_attn = ya_ref.shape
    d_model = x_ref.shape[1]
    ya = _rms(ya_ref[...].astype(F32), gattn_ref[...], d_attn).astype(BF16)
    h1 = (x_ref[...]
          + jnp.dot(ya, wout_ref[:d_attn, :], preferred_element_type=F32)
          + jnp.dot(yc_ref[...], wout_ref[d_attn:, :], preferred_element_type=F32))
    h1_ref[...] = h1
    xm = _rms(h1, gmoe_ref[...], d_model)
    _store_slabs(xm_ref, xm, SLAB_PITCH)
    for c in range(SLAB, SLAB_PITCH):
        xm_ref[pl.ds(c, tm, stride=SLAB_PITCH), :] = jnp.zeros((tm, LANES), F32)

    x_hi = xm.astype(BF16)
    x_lo = (xm - x_hi.astype(F32)).astype(BF16)
    hi_part = jnp.dot(x_hi, wgr_ref[...], preferred_element_type=F32)
    lo_part = jnp.dot(x_lo, wgr_ref[:, :LANES], preferred_element_type=F32)
    logits = hi_part[:, :LANES] + hi_part[:, LANES:] + lo_part + bgr_ref[...]
    lane = lax.broadcasted_iota(jnp.int32, (tm, LANES), 1).astype(F32)
    neg = -jnp.inf
    big = 1e9
    gl = jnp.where(lane < N_GROUPS, logits, neg)
    gmax = jnp.max(gl, axis=-1, keepdims=True)
    gsel = jnp.min(jnp.where(gl == gmax, lane, big), axis=-1, keepdims=True)
    p_g = 1.0 / jnp.sum(jnp.exp(gl - gmax), axis=-1, keepdims=True)
    lo = N_GROUPS + EXPERTS_PER_GROUP * gsel
    el = jnp.where((lane >= lo) & (lane < lo + EXPERTS_PER_GROUP), logits, neg)
    v1 = jnp.max(el, axis=-1, keepdims=True)
    i1 = jnp.min(jnp.where(el == v1, lane, big), axis=-1, keepdims=True)
    el2 = jnp.where(lane == i1, neg, el)
    v2 = jnp.max(el2, axis=-1, keepdims=True)
    i2 = jnp.min(jnp.where(el2 == v2, lane, big), axis=-1, keepdims=True)
    e2 = jnp.exp(v2 - v1)
    den = 1.0 + e2
    w1 = p_g / den
    w2 = p_g * e2 / den
    route_ref[...] = jnp.where(lane == 0, i1 - N_GROUPS,
                               jnp.where(lane == 1, i2 - N_GROUPS,
                                         jnp.where(lane == 2, w1, jnp.where(lane == 3, w2, 0.0))))


def _outproj(ya, yc, x2, gattn, w_out, gmoe, w_gr, b_gr):
    t, d_model = x2.shape
    d_attn = ya.shape[1]
    tm = TM_OUT
    const = lambda i: (0, 0)
    row = lambda i: (i, 0)
    full = lambda a: pl.BlockSpec(a.shape, const)
    return pl.pallas_call(
        _outproj_kernel,
        grid=(t // tm,),
        in_specs=[pl.BlockSpec((tm, d_attn), row), pl.BlockSpec((tm, yc.shape[1]), row), pl.BlockSpec((tm, d_model), row),
                  full(gattn), pl.BlockSpec(w_out.shape, const, pipeline_mode=pl.Buffered(1)), full(gmoe), full(w_gr), full(b_gr)],
        out_specs=[pl.BlockSpec((tm, d_model), row), pl.BlockSpec((tm * SLAB_PITCH, LANES), row), pl.BlockSpec((tm, LANES), row)],
        out_shape=[jax.ShapeDtypeStruct((t, d_model), F32), jax.ShapeDtypeStruct((t * SLAB_PITCH, LANES), F32),
                   jax.ShapeDtypeStruct((t, LANES), F32)],
        compiler_params=pltpu.CompilerParams(dimension_semantics=("arbitrary",), vmem_limit_bytes=VMEM_LIMIT),
        name="outproj",
    )(ya, yc, x2, gattn, w_out, gmoe, w_gr, b_gr)


def _row_gather_start(idx_ref, src_hbm, dst_ref, sem, n_rows, src_pitch=SLAB):
    def body(pair, carry):
        for prio in range(DMA_PRIORITIES):
            r = pair * DMA_PRIORITIES + prio
            src = src_hbm.at[pl.ds(idx_ref[0, 0, r] * src_pitch, SLAB), :]
            pltpu.make_async_copy(src, dst_ref.at[pl.ds(r * SLAB_PITCH, SLAB), :], sem).start(priority=prio)
        return carry
    lax.fori_loop(0, n_rows // DMA_PRIORITIES, body, 0, unroll=True)


def _row_gather_wait(src_hbm, dst_ref, sem, n_rows):
    pltpu.make_async_copy(src_hbm.at[pl.ds(0, n_rows * SLAB), :], dst_ref.at[pl.ds(0, n_rows * SLAB), :], sem).wait()


def _moe_kernel(tile_expert_ref, n_active_ref, first_ref, wslot_ref, next_expert_ref, tok_cur_ref, tok_next_ref,
                xm_hbm, w1_hbm, w3_hbm, w2lo_hbm, w2hi_hbm, y_ref, xbuf_ref, w1_buf, w3_buf, w2_buf, sem_ref, wsem_ref, *, tm):
    i = pl.program_id(0)
    n_active = n_active_ref[0]
    slot = i % 2
    buf_rows = tm * SLAB_PITCH
    wslot = wslot_ref[i]

    def buf(s):
        return xbuf_ref.at[pl.ds(s * buf_rows, buf_rows), :]

    n_lo = w2lo_hbm.shape[0]

    def start_weights(e, s):
        pltpu.make_async_copy(w1_hbm.at[e], w1_buf.at[s], wsem_ref.at[0, s]).start()
        pltpu.make_async_copy(w3_hbm.at[e], w3_buf.at[s], wsem_ref.at[1, s]).start()

        @pl.when(e < n_lo)
        def _():
            pltpu.make_async_copy(w2lo_hbm.at[e], w2_buf.at[s], wsem_ref.at[2, s]).start()

        @pl.when(e >= n_lo)
        def _():
            pltpu.make_async_copy(w2hi_hbm.at[e - n_lo], w2_buf.at[s], wsem_ref.at[2, s]).start()

    def wait_weights(s):
        pltpu.make_async_copy(w1_hbm.at[0], w1_buf.at[s], wsem_ref.at[0, s]).wait()
        pltpu.make_async_copy(w3_hbm.at[0], w3_buf.at[s], wsem_ref.at[1, s]).wait()
        pltpu.make_async_copy(w2lo_hbm.at[0], w2_buf.at[s], wsem_ref.at[2, s]).wait()

    @pl.when(i == 0)
    def _():
        start_weights(tile_expert_ref[0], 0)
        _row_gather_start(tok_cur_ref, xm_hbm, buf(0), sem_ref.at[0], tm, src_pitch=SLAB_PITCH)

    active = i < n_active
    new_expert = active & (first_ref[i] == 1)

    @pl.when(new_expert & (next_expert_ref[i] >= 0))
    def _():
        start_weights(next_expert_ref[i], 1 - wslot)

    @pl.when(i + 1 < n_active)
    def _():
        _row_gather_start(tok_next_ref, xm_hbm, buf(1 - slot), sem_ref.at[1 - slot], tm, src_pitch=SLAB_PITCH)

    @pl.when(new_expert)
    def _():
        wait_weights(wslot)

    @pl.when(active)
    def _():
        _row_gather_wait(xm_hbm, buf(slot), sem_ref.at[slot], tm)
        x = _load_slabs(xbuf_ref, slot * buf_rows, tm, SLAB, SLAB_PITCH).astype(BF16)
        a = jnp.dot(x, w1_buf[wslot], preferred_element_type=F32)
        g = jnp.dot(x, w3_buf[wslot], preferred_element_type=F32)
        hid = (a * jax.nn.sigmoid(a) * g).astype(BF16)
        _store_slabs(y_ref, jnp.dot(hid, w2_buf[wslot], preferred_element_type=F32), SLAB)

    @pl.when(i >= n_active)
    def _():
        y_ref[...] = jnp.zeros_like(y_ref)


def _moe(plan, xm, w1, w3, w2_lo, w2_hi):
    tile_expert, n_active, first, wslot, next_expert, row_token3 = plan
    n_tiles = row_token3.shape[0]
    tm = TM_MOE
    _, d_model, d_exp = w1.shape
    last = n_tiles - 1
    grid_spec = pltpu.PrefetchScalarGridSpec(
        num_scalar_prefetch=5,
        grid=(n_tiles,),
        in_specs=[pl.BlockSpec((1, 1, tm), lambda i, *_: (i, 0, 0), memory_space=pltpu.SMEM),
                  pl.BlockSpec((1, 1, tm), lambda i, *_: (jnp.minimum(i + 1, last), 0, 0), memory_space=pltpu.SMEM),
                  pl.BlockSpec(memory_space=pl.ANY), pl.BlockSpec(memory_space=pl.ANY), pl.BlockSpec(memory_space=pl.ANY),
                  pl.BlockSpec(memory_space=pl.ANY), pl.BlockSpec(memory_space=pl.ANY)],
        out_specs=pl.BlockSpec((tm * SLAB, LANES), lambda i, *_: (i, 0)),
        scratch_shapes=[pltpu.VMEM((2 * tm * SLAB_PITCH, LANES), F32),
                        pltpu.VMEM((2, d_model, d_exp), BF16), pltpu.VMEM((2, d_model, d_exp), BF16),
                        pltpu.VMEM((2, d_exp, d_model), BF16),
                        pltpu.SemaphoreType.DMA((2,)), pltpu.SemaphoreType.DMA((3, 2))],
    )
    return pl.pallas_call(
        functools.partial(_moe_kernel, tm=tm),
        grid_spec=grid_spec,
        out_shape=jax.ShapeDtypeStruct((n_tiles * tm * SLAB, LANES), F32),
        compiler_params=pltpu.CompilerParams(dimension_semantics=("arbitrary",), vmem_limit_bytes=VMEM_LIMIT),
        name="moe",
    )(tile_expert, n_active, first, wslot, next_expert, row_token3, row_token3, xm, w1, w3, w2_lo, w2_hi)


def _ple_kernel(dest_cur_ref, dest_next_ref, y_hbm, h1_ref, route_ref, p_ref, wple_ref, gple_ref, wpg_ref, bpg_ref,
                o_ref, ybuf_ref, sem_ref):
    i = pl.program_id(0)
    n = pl.num_programs(0)
    tm, d_model = h1_ref.shape
    slot = i % 2
    buf_rows = TOP_K * tm * SLAB_PITCH

    def buf(s):
        return ybuf_ref.at[pl.ds(s * buf_rows, buf_rows), :]

    @pl.when(i == 0)
    def _():
        _row_gather_start(dest_cur_ref, y_hbm, buf(0), sem_ref.at[0], TOP_K * tm)

    @pl.when(i + 1 < n)
    def _():
        _row_gather_start(dest_next_ref, y_hbm, buf(1 - slot), sem_ref.at[1 - slot], TOP_K * tm)

    _row_gather_wait(y_hbm, buf(slot), sem_ref.at[slot], TOP_K * tm)
    route = route_ref[...]
    y0 = _load_slabs(ybuf_ref, slot * buf_rows, tm, SLAB, SLAB_PITCH)
    y1 = _load_slabs(ybuf_ref, slot * buf_rows + tm * SLAB_PITCH, tm, SLAB, SLAB_PITCH)
    h2 = h1_ref[...] + (route[:, 2:3] * y0 + route[:, 3:4] * y1)
    xn = _rms(h2, gple_ref[...], d_model).astype(BF16)
    gate = jax.nn.sigmoid(jnp.dot(xn, wpg_ref[...], preferred_element_type=F32) + bpg_ref[...])
    pe = jnp.dot(p_ref[...].astype(BF16), wple_ref[...], preferred_element_type=F32)
    o_ref[...] = h2 + gate * pe


def _ple(dest3, y, h1, route, p2, w_ple, gple, w_pg, b_pg):
    t, d_model = h1.shape
    tm = TM_PLE
    n_tiles = t // tm
    last = n_tiles - 1
    const = lambda i: (0, 0)
    row = lambda i: (i, 0)
    full = lambda a: pl.BlockSpec(a.shape, const)
    return pl.pallas_call(
        _ple_kernel,
        grid=(n_tiles,),
        in_specs=[pl.BlockSpec((1, 1, TOP_K * tm), lambda i: (i, 0, 0), memory_space=pltpu.SMEM),
                  pl.BlockSpec((1, 1, TOP_K * tm), lambda i: (jnp.minimum(i + 1, last), 0, 0), memory_space=pltpu.SMEM),
                  pl.BlockSpec(memory_space=pl.ANY),
                  pl.BlockSpec((tm, d_model), row), pl.BlockSpec((tm, LANES), row), pl.BlockSpec((tm, p2.shape[1]), row),
                  full(w_ple), full(gple), pl.BlockSpec(w_pg.shape, const, pipeline_mode=pl.Buffered(1)), full(b_pg)],
        out_specs=pl.BlockSpec((tm, d_model), row),
        out_shape=jax.ShapeDtypeStruct((t, d_model), F32),
        scratch_shapes=[pltpu.VMEM((2 * TOP_K * tm * SLAB_PITCH, LANES), F32), pltpu.SemaphoreType.DMA((2,))],
        compiler_params=pltpu.CompilerParams(dimension_semantics=("arbitrary",), vmem_limit_bytes=VMEM_LIMIT),
        name="ple",
    )(dest3, dest3, y, h1, route, p2, w_ple, gple, w_pg, b_pg)


def _dispatch_plan(route):
    t = route.shape[0]
    tm = TM_MOE
    total = TOP_K * t
    n_tiles = total // tm + N_EXPERTS
    expert = route[:, :TOP_K].astype(jnp.int32).reshape(-1)
    pos = jnp.arange(total, dtype=jnp.int32)
    order = jnp.sort(expert * total + pos) % total
    rank = jnp.sort(order * total + pos) % total
    experts = jnp.arange(N_EXPERTS, dtype=jnp.int32)
    onehot = (expert[:, None] == experts[None, :]).astype(jnp.int32)
    counts = jnp.sum(onehot, axis=0)
    group_start = jnp.cumsum(counts) - counts
    tile_end = jnp.cumsum((counts + tm - 1) // tm)
    tile_start = tile_end - (counts + tm - 1) // tm
    n_active = tile_end[-1]
    shift = tile_start * tm - group_start
    dest = rank + jnp.sum(onehot * shift[None, :], axis=1)
    tiles = jnp.minimum(jnp.arange(n_tiles, dtype=jnp.int32), n_active - 1)
    tile_expert = jnp.sum((tile_end[None, :] <= tiles[:, None]).astype(jnp.int32), axis=1)
    j = jnp.arange(tm, dtype=jnp.int32)
    fill_row = (tile_start * tm + counts)[:, None] + j[None, :]
    fill_row = jnp.where(fill_row < (tile_end * tm)[:, None], fill_row, n_tiles * tm)
    fill_tok = (experts[:, None] * tm + j[None, :]) % t
    keys = jnp.concatenate([dest * t + pos // TOP_K, (fill_row * t + fill_tok).reshape(-1)])
    row_token3 = (jnp.sort(keys) % t).reshape(n_tiles, 1, tm)
    dest3 = dest.reshape(t // TM_PLE, TM_PLE, TOP_K).transpose(0, 2, 1).reshape(t // TM_PLE, 1, TOP_K * TM_PLE)
    prev_expert = jnp.concatenate([jnp.full((1,), -1, jnp.int32), tile_expert[:-1]])
    first = (tile_expert != prev_expert).astype(jnp.int32)
    nonempty = counts > 0
    wslot = (jnp.sum((nonempty[None, :] & (experts[None, :] < tile_expert[:, None])).astype(jnp.int32), axis=1)) % 2
    later = nonempty[None, :] & (experts[None, :] > tile_expert[:, None])
    next_expert = jnp.min(jnp.where(later, experts[None, :], N_EXPERTS), axis=1)
    next_expert = jnp.where(next_expert < N_EXPERTS, next_expert, -1)
    plan = (tile_expert, n_active.reshape(1), first, wslot, next_expert, row_token3)
    return plan, dest3


def _rope_swap_pad(w):
    half = QK_ROPE // 2
    x1 = w[..., QK_NOPE:QK_NOPE + half]
    x2 = w[..., QK_NOPE + half:]
    return jnp.concatenate([w[..., :QK_NOPE], x1, x2, x2, x1], axis=-1)


def kernel(x, p, positions, norm_mix, w_in, g_q_lat, g_kv_lat, w_uq, w_ukv, g_q_head, g_k_head, w_conv, g_out_attn, g_out_conv, w_out, norm_moe, w_group, b_group, w_router, b_router, w1, w3, w2, norm_ple, w_ple, w_ple_gate, b_ple_gate):
    batch, seq, d_model = x.shape
    depth = w_in.shape[0]
    t = batch * seq
    q_rank = g_q_lat.shape[1]
    kv_rank = g_kv_lat.shape[1]
    half = QK_ROPE // 2
    rope_lo = q_rank + kv_rank

    pos2 = positions.reshape(t, 1)
    inv_freq = ROPE_BASE ** (-jnp.arange(0, QK_ROPE, 2, dtype=F32) / QK_ROPE)
    freq = jnp.tile(inv_freq, LANES // half).reshape(1, LANES)

    h = x.reshape(t, d_model)
    for i in range(depth):
        w_in_t = jnp.swapaxes(w_in[i], 0, 1).astype(BF16)
        w_rope = jnp.concatenate([w_in_t[rope_lo:rope_lo + QK_ROPE], w_in_t[rope_lo + half:rope_lo + QK_ROPE],
                                  w_in_t[rope_lo:rope_lo + half]], axis=0)
        wuq_p = _rope_swap_pad(w_uq[i]).reshape(q_rank, MLA_HEADS * HEAD_PAD).astype(BF16)
        wukv_p = jnp.concatenate([w_ukv[i][:, :, :QK_NOPE].reshape(kv_rank, -1),
                                  w_ukv[i][:, :, QK_NOPE:].reshape(kv_rank, -1)], axis=1).astype(BF16)
        gq_p = _rope_swap_pad(g_q_head[i]).reshape(1, HEAD_PAD)
        gk_p = _rope_swap_pad(g_k_head[i]).reshape(1, HEAD_PAD)
        as_steps = lambda w, n: w.reshape(n, -1, w.shape[-1])
        nf = t // TM_FRONT
        na = batch * MLA_HEADS * (seq // TQ_ATTN)
        front_sides = [(as_steps(w, nf), 0) for w in (w1[i], w_out[i], w_ple_gate[i])] + [(as_steps(w2[i], 2 * nf), 0)]
        (q, k, v, yc), (w1b, w_out_b, w_pg_b, w2_lo) = _front(
            h, pos2, freq, norm_mix[i].reshape(1, -1), w_in_t, w_rope, g_q_lat[i].reshape(1, -1),
            g_kv_lat[i].reshape(1, -1), wuq_p, wukv_p, gq_p, gk_p, w_conv[i], g_out_conv[i].reshape(1, -1), front_sides, seq=seq)
        ya, (w3b, w2_hi) = _attention(q, k, v, [(as_steps(w3[i], na), 0), (as_steps(w2[i], 2 * na), na)], batch=batch, seq=seq)
        half_e = (N_EXPERTS // 2,) + w2[i].shape[1:]
        w1b, w3b, w2_lo, w2_hi = w1b.reshape(w1[i].shape), w3b.reshape(w3[i].shape), w2_lo.reshape(half_e), w2_hi.reshape(half_e)
        w_out_b, w_pg_b = w_out_b.reshape(w_out[i].shape), w_pg_b.reshape(w_ple_gate[i].shape)

        pad = LANES - N_GROUPS - N_EXPERTS
        w_gr32 = jnp.concatenate([w_group[i], w_router[i], jnp.zeros((d_model, pad), F32)], axis=1)
        w_gr_hi = w_gr32.astype(BF16)
        w_gr = jnp.concatenate([w_gr_hi, (w_gr32 - w_gr_hi.astype(F32)).astype(BF16)], axis=1)
        b_gr = jnp.concatenate([b_group[i], b_router[i], jnp.zeros((pad,), F32)]).reshape(1, LANES)
        h1, xm, route = _outproj(ya, yc, h, g_out_attn[i].reshape(1, -1), w_out_b,
                                 norm_moe[i].reshape(1, -1), w_gr, b_gr)

        plan, dest3 = _dispatch_plan(route)
        y = _moe(plan, xm, w1b, w3b, w2_lo, w2_hi)
        h = _ple(dest3, y, h1, route, p[i].reshape(t, -1), w_ple[i].astype(BF16), norm_ple[i].reshape(1, -1),
                 w_pg_b, b_ple_gate[i].reshape(1, -1))
    return h.reshape(batch, seq, d_model)
```

```python
import functools

import jax
import jax.numpy as jnp
from jax import lax
from jax.experimental import pallas as pl
from jax.experimental.pallas import tpu as pltpu

F32 = jnp.float32
BF16 = jnp.bfloat16

CHUNK = 64
EPS = 1e-6
MASK_VALUE = -1e30
ROPE_BASE = 10000.0

MLA_HEADS = 8
QK_NOPE = 128
QK_ROPE = 64
QK_HEAD = QK_NOPE + QK_ROPE
V_HEAD = 128
HEAD_PAD = 256
Q_SCALE = QK_HEAD ** -0.5 * 1.4426950408889634
N_GROUPS = 4
EXPERTS_PER_GROUP = 8
N_EXPERTS = N_GROUPS * EXPERTS_PER_GROUP
TOP_K = 2
LANES = 128
SLAB = 16
SLAB_PITCH = 20

TM_FRONT = 256
TQ_ATTN = 1024
TM_OUT = 512
TM_MOE = 256
TM_PLE = 256
VMEM_LIMIT = 56 * 1024 * 1024


def _rms(x, g, n):
    ms = jnp.sum(x * x, axis=-1, keepdims=True) * (1.0 / n)
    return x * lax.rsqrt(ms + EPS) * g


def _store_slabs(ref, val, pitch):
    n, width = val.shape
    for c in range(width // LANES):
        ref[pl.ds(c, n, stride=pitch), :] = val[:, c * LANES:(c + 1) * LANES]


def _load_slabs(ref, start, n, slab, pitch):
    return jnp.concatenate([ref[pl.ds(start + c, n, stride=pitch), :] for c in range(slab)], axis=1)


def _front_kernel(x_ref, pos_ref, freq_ref, gmix_ref, win_ref, wrope_ref, gql_ref, gkvl_ref,
                  wuq_ref, wukv_ref, gq_ref, gk_ref, wconv_ref, gconv_ref, *rest,
                  n_side, tiles_per_seq, q_rank, kv_rank, d_conv):
    side_refs = rest[:n_side]
    q_ref, k_ref, v_ref, yc_ref = rest[n_side:n_side + 4]
    side_out_refs = rest[n_side + 4:2 * n_side + 4]
    prev_ref = rest[-1]
    i = pl.program_id(0)
    is_start = (i % tiles_per_seq) == 0
    tm = x_ref.shape[0]
    d_model = x_ref.shape[1]
    rope_lo = q_rank + kv_rank
    nt = (((1,), (1,)), ((), ()))

    @pl.when(is_start)
    def _():
        prev_ref[...] = jnp.zeros_like(prev_ref)

    xn = _rms(x_ref[...], gmix_ref[...], d_model).astype(BF16)
    lat = lax.dot_general(xn, win_ref[:rope_lo, :], nt, preferred_element_type=F32)
    q_lat = lat[:, :q_rank]
    kv_lat = lat[:, q_rank:]
    kr = lax.dot_general(xn, wrope_ref[...], nt, preferred_element_type=F32)

    qn = _rms(q_lat, gql_ref[...], q_rank).astype(BF16)
    kvn = _rms(kv_lat, gkvl_ref[...], kv_rank).astype(BF16)
    q_all = jnp.dot(qn, wuq_ref[...], preferred_element_type=F32)
    kv_all = jnp.dot(kvn, wukv_ref[...], preferred_element_type=F32)

    lane = lax.broadcasted_iota(jnp.int32, (tm, LANES), 1)
    ang = pos_ref[...].astype(F32) * freq_ref[...]
    cos = jnp.cos(ang)
    sin = jnp.sin(ang)
    rope_c = jnp.where(lane < QK_ROPE, cos, 0.0)
    rope_s = jnp.where(lane < QK_ROPE // 2, -sin, jnp.where(lane < QK_ROPE, sin, 0.0))
    rope_lanes = lane < QK_ROPE

    def rope(t):
        return t * rope_c + pltpu.roll(t, QK_ROPE, 1) * rope_s

    gq = gq_ref[...]
    gk = gk_ref[...]
    inv_n = 1.0 / QK_HEAD
    kr_ssq = jnp.sum(jnp.where(rope_lanes, kr * kr, 0.0), axis=-1, keepdims=True)
    kr_rot = rope(kr * gk[:, QK_NOPE:])
    for h in range(MLA_HEADS):
        qa = q_all[:, h * HEAD_PAD:h * HEAD_PAD + QK_NOPE]
        qb = q_all[:, h * HEAD_PAD + QK_NOPE:(h + 1) * HEAD_PAD]
        ssq = (jnp.sum(qa * qa, axis=-1, keepdims=True)
               + jnp.sum(jnp.where(rope_lanes, qb * qb, 0.0), axis=-1, keepdims=True))
        inv = lax.rsqrt(ssq * inv_n + EPS)
        q_ref[:, h * HEAD_PAD:h * HEAD_PAD + QK_NOPE] = (qa * inv * gq[:, :QK_NOPE] * Q_SCALE).astype(BF16)
        q_ref[:, h * HEAD_PAD + QK_NOPE:(h + 1) * HEAD_PAD] = (rope(qb * inv * gq[:, QK_NOPE:]) * Q_SCALE).astype(BF16)

        ka = kv_all[:, h * QK_NOPE:(h + 1) * QK_NOPE]
        kinv = lax.rsqrt((jnp.sum(ka * ka, axis=-1, keepdims=True) + kr_ssq) * inv_n + EPS)
        k_ref[:, h * HEAD_PAD:h * HEAD_PAD + QK_NOPE] = (ka * kinv * gk[:, :QK_NOPE]).astype(BF16)
        k_ref[:, h * HEAD_PAD + QK_NOPE:(h + 1) * HEAD_PAD] = (kr_rot * kinv).astype(BF16)
    v_ref[...] = kv_all[:, MLA_HEADS * QK_NOPE:].astype(BF16)

    conv_lo = rope_lo + QK_ROPE
    half_c = d_conv // 2
    row = lax.broadcasted_iota(jnp.int32, (tm, half_c), 0)
    wc = wconv_ref[...]
    ys = []
    ssq = jnp.zeros((tm, 1), F32)
    for cs in (slice(0, half_c), slice(half_c, d_conv)):
        b_gate, c_gate, h_conv = (lax.dot_general(xn, win_ref[conv_lo + k * d_conv + cs.start:conv_lo + k * d_conv + cs.stop, :],
                                                  nt, preferred_element_type=F32) for k in range(3))
        u = c_gate * h_conv
        prev = prev_ref[:, cs]
        hm1 = prev[7:8, :]
        hm2 = prev[6:7, :]
        u1 = jnp.where(row == 0, hm1, pltpu.roll(u, 1, 0))
        u2 = jnp.where(row == 0, hm2, jnp.where(row == 1, hm1, pltpu.roll(u, 2, 0)))
        prev_ref[:, cs] = u[tm - 8:, :]
        y = b_gate * (u2 * wc[0:1, cs] + u1 * wc[1:2, cs] + u * wc[2:3, cs])
        ssq = ssq + jnp.sum(y * y, axis=-1, keepdims=True)
        ys.append(y)
    inv = lax.rsqrt(ssq * (1.0 / d_conv) + EPS)
    yc_ref[...] = (jnp.concatenate(ys, axis=1) * inv * gconv_ref[...]).astype(BF16)
    for w_ref, wb_ref in zip(side_refs, side_out_refs):
        wb_ref[...] = w_ref[...].astype(BF16)


def _front(x2, pos2, freq, gmix, w_in_t, w_rope, gql, gkvl, wuq_p, wukv_p, gq_p, gk_p, wconv, gconv, sides, *, seq):
    t, d_model = x2.shape
    tm = TM_FRONT
    q_rank = gql.shape[1]
    kv_rank = gkvl.shape[1]
    d_conv = gconv.shape[1]
    const = lambda i: (0, 0)
    row = lambda i: (i, 0)
    full = lambda a: pl.BlockSpec(a.shape, const)
    side_in = [pl.BlockSpec((1,) + w.shape[1:], lambda i, off=off: (i + off, 0, 0)) for w, off in sides]
    side_out = [pl.BlockSpec((1,) + w.shape[1:], lambda i: (i, 0, 0)) for w, _ in sides]
    kern = functools.partial(_front_kernel, n_side=len(sides), tiles_per_seq=seq // tm, q_rank=q_rank, kv_rank=kv_rank,
                             d_conv=d_conv)
    outs = pl.pallas_call(
        kern,
        grid=(t // tm,),
        in_specs=[pl.BlockSpec((tm, d_model), row), pl.BlockSpec((tm, 1), row), full(freq), full(gmix),
                  pl.BlockSpec(w_in_t.shape, const, pipeline_mode=pl.Buffered(1)), full(w_rope),
                  full(gql), full(gkvl), full(wuq_p), full(wukv_p), full(gq_p), full(gk_p), full(wconv), full(gconv)] + side_in,
        out_specs=[pl.BlockSpec((tm, MLA_HEADS * HEAD_PAD), row), pl.BlockSpec((tm, MLA_HEADS * HEAD_PAD), row),
                   pl.BlockSpec((tm, MLA_HEADS * V_HEAD), row), pl.BlockSpec((tm, d_conv), row)] + side_out,
        out_shape=[jax.ShapeDtypeStruct((t, MLA_HEADS * HEAD_PAD), BF16), jax.ShapeDtypeStruct((t, MLA_HEADS * HEAD_PAD), BF16),
                   jax.ShapeDtypeStruct((t, MLA_HEADS * V_HEAD), BF16), jax.ShapeDtypeStruct((t, d_conv), BF16)]
                  + [jax.ShapeDtypeStruct((t // tm,) + w.shape[1:], BF16) for w, _ in sides],
        scratch_shapes=[pltpu.VMEM((8, d_conv), F32)],
        compiler_params=pltpu.CompilerParams(dimension_semantics=("arbitrary",), vmem_limit_bytes=VMEM_LIMIT),
        name="front",
    )(x2, pos2, freq, gmix, w_in_t, w_rope, gql, gkvl, wuq_p, wukv_p, gq_p, gk_p, wconv, gconv, *[w for w, _ in sides])
    return outs[:4], outs[4:]


def _attn_kernel(q_ref, k_ref, v_ref, *rest):
    n_side = (len(rest) - 1) // 2
    o_ref = rest[n_side]

    def round_side_weights():
        for w_ref, wb_ref in zip(rest[:n_side], rest[n_side + 1:]):
            wb_ref[...] = w_ref[...].astype(BF16)

    i = pl.program_id(2)
    tq = q_ref.shape[0]
    half = tq // 2
    q_halves = (q_ref[:half, :], q_ref[half:, :])

    def step(start, n_keys, masks, carry):
        kvs = [(k_ref[pl.ds(start, n), :],
                jnp.concatenate([v_ref[pl.ds(start, n), :], jnp.ones((n, HEAD_PAD - V_HEAD), BF16)], axis=1)) for n in n_keys]
        scores = [lax.dot_general(qh, kb, (((1,), (1,)), ((), ())), preferred_element_type=F32)
                  for qh, (kb, _) in zip(q_halves, kvs)]
        scores = [s if mask is None else jnp.where(mask, s, MASK_VALUE) for s, mask in zip(scores, masks)]
        m_new = [jnp.maximum(m, jnp.max(s, axis=-1, keepdims=True)) for s, (m, _) in zip(scores, carry)]
        probs = [jnp.exp2(s - mn).astype(BF16) for s, mn in zip(scores, m_new)]
        return tuple((mn, jnp.exp2(m - mn) * acc + jnp.dot(p, vb, preferred_element_type=F32))
                     for p, mn, (m, acc), (_, vb) in zip(probs, m_new, carry, kvs))

    def chunk_mask(n, first_row):
        r = (lax.broadcasted_iota(jnp.int32, (half, n), 0) + first_row) // CHUNK
        c = lax.broadcasted_iota(jnp.int32, (half, n), 1) // CHUNK
        return c <= r

    for qi in range(k_ref.shape[0] // tq):
        @pl.when(i == qi)
        def _(qi=qi):
            round_side_weights()
            carry = tuple((jnp.full((half, 1), MASK_VALUE, F32), jnp.zeros((half, HEAD_PAD), F32)) for _ in range(2))
            for j in range(qi):
                carry = step(j * tq, (tq, tq), (None, None), carry)
            (_, acc0), (_, acc1) = step(qi * tq, (half, tq), (chunk_mask(half, 0), chunk_mask(tq, half)), carry)
            o_ref[:half, :] = (acc0[:, :V_HEAD] / acc0[:, V_HEAD:]).astype(o_ref.dtype)
            o_ref[half:, :] = (acc1[:, :V_HEAD] / acc1[:, V_HEAD:]).astype(o_ref.dtype)


def _attention(q, k, v, side_weights, *, batch, seq):
    tq = TQ_ATTN
    nq = seq // tq
    n_steps = batch * MLA_HEADS * nq
    step = lambda b, h, i, off=0: ((b * MLA_HEADS + h) * nq + i + off, 0, 0)
    side_in = [pl.BlockSpec((1,) + w.shape[1:], functools.partial(step, off=off)) for w, off in side_weights]
    side_out = [pl.BlockSpec((1,) + w.shape[1:], step) for w, _ in side_weights]
    outs = pl.pallas_call(
        _attn_kernel,
        grid=(batch, MLA_HEADS, nq),
        in_specs=[pl.BlockSpec((tq, HEAD_PAD), lambda b, h, i: (b * nq + i, h)),
                  pl.BlockSpec((seq, HEAD_PAD), lambda b, h, i: (b, h)),
                  pl.BlockSpec((seq, V_HEAD), lambda b, h, i: (b, h))] + side_in,
        out_specs=[pl.BlockSpec((tq, V_HEAD), lambda b, h, i: (b * nq + i, h))] + side_out,
        out_shape=[jax.ShapeDtypeStruct((batch * seq, MLA_HEADS * V_HEAD), BF16)]
                  + [jax.ShapeDtypeStruct((n_steps,) + w.shape[1:], BF16) for w, _ in side_weights],
        compiler_params=pltpu.CompilerParams(dimension_semantics=("arbitrary", "arbitrary", "arbitrary"),
                                             vmem_limit_bytes=VMEM_LIMIT),
        name="attn",
    )(q, k, v, *[w for w, _ in side_weights])
    return outs[0], outs[1:]


def _outproj_kernel(ya_ref, yc_ref, x_ref, gattn_ref, wout_ref, gmoe_ref, wgr_ref, bgr_ref,
                    h1_ref, xm_ref, route_ref):
    tm, d_attn = ya_ref.shape
    d_model = x_ref.shape[1]
    ya = _rms(ya_ref[...].astype(F32), gattn_ref[...], d_attn).astype(BF16)
    h1 = (x_ref[...]
          + jnp.dot(ya, wout_ref[:d_attn, :], preferred_element_type=F32)
          + jnp.dot(yc_ref[...], wout_ref[d_attn:, :], preferred_element_type=F32))
    h1_ref[...] = h1
    xm = _rms(h1, gmoe_ref[...], d_model)
    _store_slabs(xm_ref, xm, SLAB_PITCH)
    for c in range(SLAB, SLAB_PITCH):
        xm_ref[pl.ds(c, tm, stride=SLAB_PITCH), :] = jnp.zeros((tm, LANES), F32)

    x_hi = xm.astype(BF16)
    x_lo = (xm - x_hi.astype(F32)).astype(BF16)
    hi_part = jnp.dot(x_hi, wgr_ref[...], preferred_element_type=F32)
    lo_part = jnp.dot(x_lo, wgr_ref[:, :LANES], preferred_element_type=F32)
    logits = hi_part[:, :LANES] + hi_part[:, LANES:] + lo_part + bgr_ref[...]
    lane = lax.broadcasted_iota(jnp.int32, (tm, LANES), 1).astype(F32)
    neg = -jnp.inf
    big = 1e9
    gl = jnp.where(lane < N_GROUPS, logits, neg)
    gmax = jnp.max(gl, axis=-1, keepdims=True)
    gsel = jnp.min(jnp.where(gl == gmax, lane, big), axis=-1, keepdims=True)
    p_g = 1.0 / jnp.sum(jnp.exp(gl - gmax), axis=-1, keepdims=True)
    lo = N_GROUPS + EXPERTS_PER_GROUP * gsel
    el = jnp.where((lane >= lo) & (lane < lo + EXPERTS_PER_GROUP), logits, neg)
    v1 = jnp.max(el, axis=-1, keepdims=True)
    i1 = jnp.min(jnp.where(el == v1, lane, big), axis=-1, keepdims=True)
    el2 = jnp.where(lane == i1, neg, el)
    v2 = jnp.max(el2, axis=-1, keepdims=True)
    i2 = jnp.min(jnp.where(el2 == v2, lane, big), axis=-1, keepdims=True)
    e2 = jnp.exp(v2 - v1)
    den = 1.0 + e2
    w1 = p_g / den
    w2 = p_g * e2 / den
    route_ref[...] = jnp.where(lane == 0, i1 - N_GROUPS,
                               jnp.where(lane == 1, i2 - N_GROUPS,
                                         jnp.where(lane == 2, w1, jnp.where(lane == 3, w2, 0.0))))


def _outproj(ya, yc, x2, gattn, w_out, gmoe, w_gr, b_gr):
    t, d_model = x2.shape
    d_attn = ya.shape[1]
    tm = TM_OUT
    const = lambda i: (0, 0)
    row = lambda i: (i, 0)
    full = lambda a: pl.BlockSpec(a.shape, const)
    return pl.pallas_call(
        _outproj_kernel,
        grid=(t // tm,),
        in_specs=[pl.BlockSpec((tm, d_attn), row), pl.BlockSpec((tm, yc.shape[1]), row), pl.BlockSpec((tm, d_model), row),
                  full(gattn), pl.BlockSpec(w_out.shape, const, pipeline_mode=pl.Buffered(1)), full(gmoe), full(w_gr), full(b_gr)],
        out_specs=[pl.BlockSpec((tm, d_model), row), pl.BlockSpec((tm * SLAB_PITCH, LANES), row), pl.BlockSpec((tm, LANES), row)],
        out_shape=[jax.ShapeDtypeStruct((t, d_model), F32), jax.ShapeDtypeStruct((t * SLAB_PITCH, LANES), F32),
                   jax.ShapeDtypeStruct((t, LANES), F32)],
        compiler_params=pltpu.CompilerParams(dimension_semantics=("arbitrary",), vmem_limit_bytes=VMEM_LIMIT),
        name="outproj",
    )(ya, yc, x2, gattn, w_out, gmoe, w_gr, b_gr)


def _row_gather_start(idx_ref, src_hbm, dst_ref, sem, n_rows, src_pitch=SLAB):
    def body(r, carry):
        src = src_hbm.at[pl.ds(idx_ref[0, 0, r] * src_pitch, SLAB), :]
        pltpu.make_async_copy(src, dst_ref.at[pl.ds(r * SLAB_PITCH, SLAB), :], sem).start()
        return carry
    lax.fori_loop(0, n_rows, body, 0, unroll=True)


def _row_gather_wait(src_hbm, dst_ref, sem, n_rows):
    pltpu.make_async_copy(src_hbm.at[pl.ds(0, n_rows * SLAB), :], dst_ref.at[pl.ds(0, n_rows * SLAB), :], sem).wait()


def _moe_kernel(tile_expert_ref, n_active_ref, first_ref, wslot_ref, next_expert_ref, tok_cur_ref, tok_next_ref,
                xm_hbm, w1_hbm, w3_hbm, w2lo_hbm, w2hi_hbm, y_ref, xbuf_ref, w1_buf, w3_buf, w2_buf, sem_ref, wsem_ref, *, tm):
    i = pl.program_id(0)
    n_active = n_active_ref[0]
    slot = i % 2
    buf_rows = tm * SLAB_PITCH
    wslot = wslot_ref[i]

    def buf(s):
        return xbuf_ref.at[pl.ds(s * buf_rows, buf_rows), :]

    n_lo = w2lo_hbm.shape[0]

    def start_weights(e, s):
        pltpu.make_async_copy(w1_hbm.at[e], w1_buf.at[s], wsem_ref.at[0, s]).start()
        pltpu.make_async_copy(w3_hbm.at[e], w3_buf.at[s], wsem_ref.at[1, s]).start()

        @pl.when(e < n_lo)
        def _():
            pltpu.make_async_copy(w2lo_hbm.at[e], w2_buf.at[s], wsem_ref.at[2, s]).start()

        @pl.when(e >= n_lo)
        def _():
            pltpu.make_async_copy(w2hi_hbm.at[e - n_lo], w2_buf.at[s], wsem_ref.at[2, s]).start()

    def wait_weights(s):
        pltpu.make_async_copy(w1_hbm.at[0], w1_buf.at[s], wsem_ref.at[0, s]).wait()
        pltpu.make_async_copy(w3_hbm.at[0], w3_buf.at[s], wsem_ref.at[1, s]).wait()
        pltpu.make_async_copy(w2lo_hbm.at[0], w2_buf.at[s], wsem_ref.at[2, s]).wait()

    @pl.when(i == 0)
    def _():
        start_weights(tile_expert_ref[0], 0)
        _row_gather_start(tok_cur_ref, xm_hbm, buf(0), sem_ref.at[0], tm, src_pitch=SLAB_PITCH)

    active = i < n_active
    new_expert = active & (first_ref[i] == 1)

    @pl.when(new_expert & (next_expert_ref[i] >= 0))
    def _():
        start_weights(next_expert_ref[i], 1 - wslot)

    @pl.when(i + 1 < n_active)
    def _():
        _row_gather_start(tok_next_ref, xm_hbm, buf(1 - slot), sem_ref.at[1 - slot], tm, src_pitch=SLAB_PITCH)

    @pl.when(new_expert)
    def _():
        wait_weights(wslot)

    @pl.when(active)
    def _():
        _row_gather_wait(xm_hbm, buf(slot), sem_ref.at[slot], tm)
        x = _load_slabs(xbuf_ref, slot * buf_rows, tm, SLAB, SLAB_PITCH).astype(BF16)
        a = jnp.dot(x, w1_buf[wslot], preferred_element_type=F32)
        g = jnp.dot(x, w3_buf[wslot], preferred_element_type=F32)
        hid = (a * jax.nn.sigmoid(a) * g).astype(BF16)
        _store_slabs(y_ref, jnp.dot(hid, w2_buf[wslot], preferred_element_type=F32), SLAB)

    @pl.when(i >= n_active)
    def _():
        y_ref[...] = jnp.zeros_like(y_ref)


def _moe(plan, xm, w1, w3, w2_lo, w2_hi):
    tile_expert, n_active, first, wslot, next_expert, row_token3 = plan
    n_tiles = row_token3.shape[0]
    tm = TM_MOE
    _, d_model, d_exp = w1.shape
    last = n_tiles - 1
    grid_spec = pltpu.PrefetchScalarGridSpec(
        num_scalar_prefetch=5,
        grid=(n_tiles,),
        in_specs=[pl.BlockSpec((1, 1, tm), lambda i, *_: (i, 0, 0), memory_space=pltpu.SMEM),
                  pl.BlockSpec((1, 1, tm), lambda i, *_: (jnp.minimum(i + 1, last), 0, 0), memory_space=pltpu.SMEM),
                  pl.BlockSpec(memory_space=pl.ANY), pl.BlockSpec(memory_space=pl.ANY), pl.BlockSpec(memory_space=pl.ANY),
                  pl.BlockSpec(memory_space=pl.ANY), pl.BlockSpec(memory_space=pl.ANY)],
        out_specs=pl.BlockSpec((tm * SLAB, LANES), lambda i, *_: (i, 0)),
        scratch_shapes=[pltpu.VMEM((2 * tm * SLAB_PITCH, LANES), F32),
                        pltpu.VMEM((2, d_model, d_exp), BF16), pltpu.VMEM((2, d_model, d_exp), BF16),
                        pltpu.VMEM((2, d_exp, d_model), BF16),
                        pltpu.SemaphoreType.DMA((2,)), pltpu.SemaphoreType.DMA((3, 2))],
    )
    return pl.pallas_call(
        functools.partial(_moe_kernel, tm=tm),
        grid_spec=grid_spec,
        out_shape=jax.ShapeDtypeStruct((n_tiles * tm * SLAB, LANES), F32),
        compiler_params=pltpu.CompilerParams(dimension_semantics=("arbitrary",), vmem_limit_bytes=VMEM_LIMIT),
        name="moe",
    )(tile_expert, n_active, first, wslot, next_expert, row_token3, row_token3, xm, w1, w3, w2_lo, w2_hi)


def _ple_kernel(dest_cur_ref, dest_next_ref, y_hbm, h1_ref, route_ref, p_ref, wple_ref, gple_ref, wpg_ref, bpg_ref,
                o_ref, ybuf_ref, sem_ref):
    i = pl.program_id(0)
    n = pl.num_programs(0)
    tm, d_model = h1_ref.shape
    slot = i % 2
    buf_rows = TOP_K * tm * SLAB_PITCH

    def buf(s):
        return ybuf_ref.at[pl.ds(s * buf_rows, buf_rows), :]

    @pl.when(i == 0)
    def _():
        _row_gather_start(dest_cur_ref, y_hbm, buf(0), sem_ref.at[0], TOP_K * tm)

    @pl.when(i + 1 < n)
    def _():
        _row_gather_start(dest_next_ref, y_hbm, buf(1 - slot), sem_ref.at[1 - slot], TOP_K * tm)

    _row_gather_wait(y_hbm, buf(slot), sem_ref.at[slot], TOP_K * tm)
    route = route_ref[...]
    y0 = _load_slabs(ybuf_ref, slot * buf_rows, tm, SLAB, SLAB_PITCH)
    y1 = _load_slabs(ybuf_ref, slot * buf_rows + tm * SLAB_PITCH, tm, SLAB, SLAB_PITCH)
    h2 = h1_ref[...] + (route[:, 2:3] * y0 + route[:, 3:4] * y1)
    xn = _rms(h2, gple_ref[...], d_model).astype(BF16)
    gate = jax.nn.sigmoid(jnp.dot(xn, wpg_ref[...], preferred_element_type=F32) + bpg_ref[...])
    pe = jnp.dot(p_ref[...].astype(BF16), wple_ref[...], preferred_element_type=F32)
    o_ref[...] = h2 + gate * pe


def _ple(dest3, y, h1, route, p2, w_ple, gple, w_pg, b_pg):
    t, d_model = h1.shape
    tm = TM_PLE
    n_tiles = t // tm
    last = n_tiles - 1
    const = lambda i: (0, 0)
    row = lambda i: (i, 0)
    full = lambda a: pl.BlockSpec(a.shape, const)
    return pl.pallas_call(
        _ple_kernel,
        grid=(n_tiles,),
        in_specs=[pl.BlockSpec((1, 1, TOP_K * tm), lambda i: (i, 0, 0), memory_space=pltpu.SMEM),
                  pl.BlockSpec((1, 1, TOP_K * tm), lambda i: (jnp.minimum(i + 1, last), 0, 0), memory_space=pltpu.SMEM),
                  pl.BlockSpec(memory_space=pl.ANY),
                  pl.BlockSpec((tm, d_model), row), pl.BlockSpec((tm, LANES), row), pl.BlockSpec((tm, p2.shape[1]), row),
                  full(w_ple), full(gple), pl.BlockSpec(w_pg.shape, const, pipeline_mode=pl.Buffered(1)), full(b_pg)],
        out_specs=pl.BlockSpec((tm, d_model), row),
        out_shape=jax.ShapeDtypeStruct((t, d_model), F32),
        scratch_shapes=[pltpu.VMEM((2 * TOP_K * tm * SLAB_PITCH, LANES), F32), pltpu.SemaphoreType.DMA((2,))],
        compiler_params=pltpu.CompilerParams(dimension_semantics=("arbitrary",), vmem_limit_bytes=VMEM_LIMIT),
        name="ple",
    )(dest3, dest3, y, h1, route, p2, w_ple, gple, w_pg, b_pg)


def _dispatch_plan(route):
    t = route.shape[0]
    tm = TM_MOE
    total = TOP_K * t
    n_tiles = total // tm + N_EXPERTS
    expert = route[:, :TOP_K].astype(jnp.int32).reshape(-1)
    pos = jnp.arange(total, dtype=jnp.int32)
    order = jnp.sort(expert * total + pos) % total
    rank = jnp.sort(order * total + pos) % total
    experts = jnp.arange(N_EXPERTS, dtype=jnp.int32)
    onehot = (expert[:, None] == experts[None, :]).astype(jnp.int32)
    counts = jnp.sum(onehot, axis=0)
    group_start = jnp.cumsum(counts) - counts
    tile_end = jnp.cumsum((counts + tm - 1) // tm)
    tile_start = tile_end - (counts + tm - 1) // tm
    n_active = tile_end[-1]
    shift = tile_start * tm - group_start
    dest = rank + jnp.sum(onehot * shift[None, :], axis=1)
    tiles = jnp.minimum(jnp.arange(n_tiles, dtype=jnp.int32), n_active - 1)
    tile_expert = jnp.sum((tile_end[None, :] <= tiles[:, None]).astype(jnp.int32), axis=1)
    j = jnp.arange(tm, dtype=jnp.int32)
    fill_row = (tile_start * tm + counts)[:, None] + j[None, :]
    fill_row = jnp.where(fill_row < (tile_end * tm)[:, None], fill_row, n_tiles * tm)
    fill_tok = (experts[:, None] * tm + j[None, :]) % t
    keys = jnp.concatenate([dest * t + pos // TOP_K, (fill_row * t + fill_tok).reshape(-1)])
    row_token3 = (jnp.sort(keys) % t).reshape(n_tiles, 1, tm)
    dest3 = dest.reshape(t // TM_PLE, TM_PLE, TOP_K).transpose(0, 2, 1).reshape(t // TM_PLE, 1, TOP_K * TM_PLE)
    prev_expert = jnp.concatenate([jnp.full((1,), -1, jnp.int32), tile_expert[:-1]])
    first = (tile_expert != prev_expert).astype(jnp.int32)
    nonempty = counts > 0
    wslot = (jnp.sum((nonempty[None, :] & (experts[None, :] < tile_expert[:, None])).astype(jnp.int32), axis=1)) % 2
    later = nonempty[None, :] & (experts[None, :] > tile_expert[:, None])
    next_expert = jnp.min(jnp.where(later, experts[None, :], N_EXPERTS), axis=1)
    next_expert = jnp.where(next_expert < N_EXPERTS, next_expert, -1)
    plan = (tile_expert, n_active.reshape(1), first, wslot, next_expert, row_token3)
    return plan, dest3


def _rope_swap_pad(w):
    half = QK_ROPE // 2
    x1 = w[..., QK_NOPE:QK_NOPE + half]
    x2 = w[..., QK_NOPE + half:]
    return jnp.concatenate([w[..., :QK_NOPE], x1, x2, x2, x1], axis=-1)


def kernel(x, p, positions, norm_mix, w_in, g_q_lat, g_kv_lat, w_uq, w_ukv, g_q_head, g_k_head, w_conv, g_out_attn, g_out_conv, w_out, norm_moe, w_group, b_group, w_router, b_router, w1, w3, w2, norm_ple, w_ple, w_ple_gate, b_ple_gate):
    batch, seq, d_model = x.shape
    depth = w_in.shape[0]
    t = batch * seq
    q_rank = g_q_lat.shape[1]
    kv_rank = g_kv_lat.shape[1]
    half = QK_ROPE // 2
    rope_lo = q_rank + kv_rank

    pos2 = positions.reshape(t, 1)
    inv_freq = ROPE_BASE ** (-jnp.arange(0, QK_ROPE, 2, dtype=F32) / QK_ROPE)
    freq = jnp.tile(inv_freq, LANES // half).reshape(1, LANES)

    h = x.reshape(t, d_model)
    for i in range(depth):
        w_in_t = jnp.swapaxes(w_in[i], 0, 1).astype(BF16)
        w_rope = jnp.concatenate([w_in_t[rope_lo:rope_lo + QK_ROPE], w_in_t[rope_lo + half:rope_lo + QK_ROPE],
                                  w_in_t[rope_lo:rope_lo + half]], axis=0)
        wuq_p = _rope_swap_pad(w_uq[i]).reshape(q_rank, MLA_HEADS * HEAD_PAD).astype(BF16)
        wukv_p = jnp.concatenate([w_ukv[i][:, :, :QK_NOPE].reshape(kv_rank, -1),
                                  w_ukv[i][:, :, QK_NOPE:].reshape(kv_rank, -1)], axis=1).astype(BF16)
        gq_p = _rope_swap_pad(g_q_head[i]).reshape(1, HEAD_PAD)
        gk_p = _rope_swap_pad(g_k_head[i]).reshape(1, HEAD_PAD)
        as_steps = lambda w, n: w.reshape(n, -1, w.shape[-1])
        nf = t // TM_FRONT
        na = batch * MLA_HEADS * (seq // TQ_ATTN)
        front_sides = [(as_steps(w, nf), 0) for w in (w1[i], w_out[i], w_ple_gate[i])] + [(as_steps(w2[i], 2 * nf), 0)]
        (q, k, v, yc), (w1b, w_out_b, w_pg_b, w2_lo) = _front(
            h, pos2, freq, norm_mix[i].reshape(1, -1), w_in_t, w_rope, g_q_lat[i].reshape(1, -1),
            g_kv_lat[i].reshape(1, -1), wuq_p, wukv_p, gq_p, gk_p, w_conv[i], g_out_conv[i].reshape(1, -1), front_sides, seq=seq)
        ya, (w3b, w2_hi) = _attention(q, k, v, [(as_steps(w3[i], na), 0), (as_steps(w2[i], 2 * na), na)], batch=batch, seq=seq)
        half_e = (N_EXPERTS // 2,) + w2[i].shape[1:]
        w1b, w3b, w2_lo, w2_hi = w1b.reshape(w1[i].shape), w3b.reshape(w3[i].shape), w2_lo.reshape(half_e), w2_hi.reshape(half_e)
        w_out_b, w_pg_b = w_out_b.reshape(w_out[i].shape), w_pg_b.reshape(w_ple_gate[i].shape)

        pad = LANES - N_GROUPS - N_EXPERTS
        w_gr32 = jnp.concatenate([w_group[i], w_router[i], jnp.zeros((d_model, pad), F32)], axis=1)
        w_gr_hi = w_gr32.astype(BF16)
        w_gr = jnp.concatenate([w_gr_hi, (w_gr32 - w_gr_hi.astype(F32)).astype(BF16)], axis=1)
        b_gr = jnp.concatenate([b_group[i], b_router[i], jnp.zeros((pad,), F32)]).reshape(1, LANES)
        h1, xm, route = _outproj(ya, yc, h, g_out_attn[i].reshape(1, -1), w_out_b,
                                 norm_moe[i].reshape(1, -1), w_gr, b_gr)

        plan, dest3 = _dispatch_plan(route)
        y = _moe(plan, xm, w1b, w3b, w2_lo, w2_hi)
        h = _ple(dest3, y, h1, route, p[i].reshape(t, -1), w_ple[i].astype(BF16), norm_ple[i].reshape(1, -1),
                 w_pg_b, b_ple_gate[i].reshape(1, -1))
    return h.reshape(batch, seq, d_model)
```

```python
import functools

import jax
import jax.numpy as jnp
from jax import lax
from jax.experimental import pallas as pl
from jax.experimental.pallas import tpu as pltpu

F32 = jnp.float32
BF16 = jnp.bfloat16

CHUNK = 64
EPS = 1e-6
MASK_VALUE = -1e30
ROPE_BASE = 10000.0

MLA_HEADS = 8
QK_NOPE = 128
QK_ROPE = 64
QK_HEAD = QK_NOPE + QK_ROPE
V_HEAD = 128
HEAD_PAD = 256
Q_SCALE = QK_HEAD ** -0.5 * 1.4426950408889634
N_GROUPS = 4
EXPERTS_PER_GROUP = 8
N_EXPERTS = N_GROUPS * EXPERTS_PER_GROUP
TOP_K = 2
LANES = 128
SLAB = 16
SLAB_PITCH = 20

TM_FRONT = 256
TQ_ATTN = 1024
TM_OUT = 512
TM_MOE = 256
TM_PLE = 256
VMEM_LIMIT = 56 * 1024 * 1024


def _rms(x, g, n):
    ms = jnp.sum(x * x, axis=-1, keepdims=True) * (1.0 / n)
    return x * lax.rsqrt(ms + EPS) * g


def _store_slabs(ref, val, pitch):
    n, width = val.shape
    for c in range(width // LANES):
        ref[pl.ds(c, n, stride=pitch), :] = val[:, c * LANES:(c + 1) * LANES]


def _load_slabs(ref, start, n, slab, pitch):
    return jnp.concatenate([ref[pl.ds(start + c, n, stride=pitch), :] for c in range(slab)], axis=1)


def _front_kernel(x_ref, pos_ref, freq_ref, gmix_ref, win_ref, wrope_ref, gql_ref, gkvl_ref,
                  wuq_ref, wukv_ref, gq_ref, gk_ref, wconv_ref, gconv_ref, *rest,
                  n_side, tiles_per_seq, q_rank, kv_rank, d_conv):
    side_refs = rest[:n_side]
    q_ref, k_ref, v_ref, yc_ref = rest[n_side:n_side + 4]
    side_out_refs = rest[n_side + 4:2 * n_side + 4]
    prev_ref = rest[-1]
    i = pl.program_id(0)
    is_start = (i % tiles_per_seq) == 0
    tm = x_ref.shape[0]
    d_model = x_ref.shape[1]
    rope_lo = q_rank + kv_rank
    nt = (((1,), (1,)), ((), ()))

    @pl.when(is_start)
    def _():
        prev_ref[...] = jnp.zeros_like(prev_ref)

    xn = _rms(x_ref[...], gmix_ref[...], d_model).astype(BF16)
    lat = lax.dot_general(xn, win_ref[:rope_lo, :], nt, preferred_element_type=F32)
    q_lat = lat[:, :q_rank]
    kv_lat = lat[:, q_rank:]
    kr = lax.dot_general(xn, wrope_ref[...], nt, preferred_element_type=F32)

    qn = _rms(q_lat, gql_ref[...], q_rank).astype(BF16)
    kvn = _rms(kv_lat, gkvl_ref[...], kv_rank).astype(BF16)
    q_all = jnp.dot(qn, wuq_ref[...], preferred_element_type=F32)
    kv_all = jnp.dot(kvn, wukv_ref[...], preferred_element_type=F32)

    lane = lax.broadcasted_iota(jnp.int32, (tm, LANES), 1)
    ang = pos_ref[...].astype(F32) * freq_ref[...]
    cos = jnp.cos(ang)
    sin = jnp.sin(ang)
    rope_c = jnp.where(lane < QK_ROPE, cos, 0.0)
    rope_s = jnp.where(lane < QK_ROPE // 2, -sin, jnp.where(lane < QK_ROPE, sin, 0.0))
    rope_lanes = lane < QK_ROPE

    def rope(t):
        return t * rope_c + pltpu.roll(t, QK_ROPE, 1) * rope_s

    gq = gq_ref[...]
    gk = gk_ref[...]
    inv_n = 1.0 / QK_HEAD
    kr_ssq = jnp.sum(jnp.where(rope_lanes, kr * kr, 0.0), axis=-1, keepdims=True)
    kr_rot = rope(kr * gk[:, QK_NOPE:])
    for h in range(MLA_HEADS):
        qa = q_all[:, h * HEAD_PAD:h * HEAD_PAD + QK_NOPE]
        qb = q_all[:, h * HEAD_PAD + QK_NOPE:(h + 1) * HEAD_PAD]
        ssq = (jnp.sum(qa * qa, axis=-1, keepdims=True)
               + jnp.sum(jnp.where(rope_lanes, qb * qb, 0.0), axis=-1, keepdims=True))
        inv = lax.rsqrt(ssq * inv_n + EPS)
        q_ref[:, h * HEAD_PAD:h * HEAD_PAD + QK_NOPE] = (qa * inv * gq[:, :QK_NOPE] * Q_SCALE).astype(BF16)
        q_ref[:, h * HEAD_PAD + QK_NOPE:(h + 1) * HEAD_PAD] = (rope(qb * inv * gq[:, QK_NOPE:]) * Q_SCALE).astype(BF16)

        ka = kv_all[:, h * QK_NOPE:(h + 1) * QK_NOPE]
        kinv = lax.rsqrt((jnp.sum(ka * ka, axis=-1, keepdims=True) + kr_ssq) * inv_n + EPS)
        k_ref[:, h * HEAD_PAD:h * HEAD_PAD + QK_NOPE] = (ka * kinv * gk[:, :QK_NOPE]).astype(BF16)
        k_ref[:, h * HEAD_PAD + QK_NOPE:(h + 1) * HEAD_PAD] = (kr_rot * kinv).astype(BF16)
    v_ref[...] = kv_all[:, MLA_HEADS * QK_NOPE:].astype(BF16)

    conv_lo = rope_lo + QK_ROPE
    half_c = d_conv // 2
    row = lax.broadcasted_iota(jnp.int32, (tm, half_c), 0)
    wc = wconv_ref[...]
    ys = []
    ssq = jnp.zeros((tm, 1), F32)
    for cs in (slice(0, half_c), slice(half_c, d_conv)):
        b_gate, c_gate, h_conv = (lax.dot_general(xn, win_ref[conv_lo + k * d_conv + cs.start:conv_lo + k * d_conv + cs.stop, :],
                                                  nt, preferred_element_type=F32) for k in range(3))
        u = c_gate * h_conv
        prev = prev_ref[:, cs]
        hm1 = prev[7:8, :]
        hm2 = prev[6:7, :]
        u1 = jnp.where(row == 0, hm1, pltpu.roll(u, 1, 0))
        u2 = jnp.where(row == 0, hm2, jnp.where(row == 1, hm1, pltpu.roll(u, 2, 0)))
        prev_ref[:, cs] = u[tm - 8:, :]
        y = b_gate * (u2 * wc[0:1, cs] + u1 * wc[1:2, cs] + u * wc[2:3, cs])
        ssq = ssq + jnp.sum(y * y, axis=-1, keepdims=True)
        ys.append(y)
    inv = lax.rsqrt(ssq * (1.0 / d_conv) + EPS)
    yc_ref[...] = (jnp.concatenate(ys, axis=1) * inv * gconv_ref[...]).astype(BF16)
    for w_ref, wb_ref in zip(side_refs, side_out_refs):
        wb_ref[...] = w_ref[...].astype(BF16)


def _front(x2, pos2, freq, gmix, w_in_t, w_rope, gql, gkvl, wuq_p, wukv_p, gq_p, gk_p, wconv, gconv, sides, *, seq):
    t, d_model = x2.shape
    tm = TM_FRONT
    q_rank = gql.shape[1]
    kv_rank = gkvl.shape[1]
    d_conv = gconv.shape[1]
    const = lambda i: (0, 0)
    row = lambda i: (i, 0)
    full = lambda a: pl.BlockSpec(a.shape, const)
    side_in = [pl.BlockSpec((1,) + w.shape[1:], lambda i, off=off: (i + off, 0, 0)) for w, off in sides]
    side_out = [pl.BlockSpec((1,) + w.shape[1:], lambda i: (i, 0, 0)) for w, _ in sides]
    kern = functools.partial(_front_kernel, n_side=len(sides), tiles_per_seq=seq // tm, q_rank=q_rank, kv_rank=kv_rank,
                             d_conv=d_conv)
    outs = pl.pallas_call(
        kern,
        grid=(t // tm,),
        in_specs=[pl.BlockSpec((tm, d_model), row), pl.BlockSpec((tm, 1), row), full(freq), full(gmix),
                  pl.BlockSpec(w_in_t.shape, const, pipeline_mode=pl.Buffered(1)), full(w_rope),
                  full(gql), full(gkvl), full(wuq_p), full(wukv_p), full(gq_p), full(gk_p), full(wconv), full(gconv)] + side_in,
        out_specs=[pl.BlockSpec((tm, MLA_HEADS * HEAD_PAD), row), pl.BlockSpec((tm, MLA_HEADS * HEAD_PAD), row),
                   pl.BlockSpec((tm, MLA_HEADS * V_HEAD), row), pl.BlockSpec((tm, d_conv), row)] + side_out,
        out_shape=[jax.ShapeDtypeStruct((t, MLA_HEADS * HEAD_PAD), BF16), jax.ShapeDtypeStruct((t, MLA_HEADS * HEAD_PAD), BF16),
                   jax.ShapeDtypeStruct((t, MLA_HEADS * V_HEAD), BF16), jax.ShapeDtypeStruct((t, d_conv), BF16)]
                  + [jax.ShapeDtypeStruct((t // tm,) + w.shape[1:], BF16) for w, _ in sides],
        scratch_shapes=[pltpu.VMEM((8, d_conv), F32)],
        compiler_params=pltpu.CompilerParams(dimension_semantics=("arbitrary",), vmem_limit_bytes=VMEM_LIMIT),
        name="front",
    )(x2, pos2, freq, gmix, w_in_t, w_rope, gql, gkvl, wuq_p, wukv_p, gq_p, gk_p, wconv, gconv, *[w for w, _ in sides])
    return outs[:4], outs[4:]


def _attn_kernel(q_ref, k_ref, v_ref, *rest):
    n_side = (len(rest) - 1) // 2
    o_ref = rest[n_side]

    def round_side_weights():
        for w_ref, wb_ref in zip(rest[:n_side], rest[n_side + 1:]):
            wb_ref[...] = w_ref[...].astype(BF16)

    i = pl.program_id(2)
    tq = q_ref.shape[0]
    half = tq // 2
    q_halves = (q_ref[:half, :], q_ref[half:, :])

    def step(start, n_keys, masks, carry):
        kvs = [(k_ref[pl.ds(start, n), :],
                jnp.concatenate([v_ref[pl.ds(start, n), :], jnp.ones((n, HEAD_PAD - V_HEAD), BF16)], axis=1)) for n in n_keys]
        scores = [lax.dot_general(qh, kb, (((1,), (1,)), ((), ())), preferred_element_type=F32)
                  for qh, (kb, _) in zip(q_halves, kvs)]
        scores = [s if mask is None else jnp.where(mask, s, MASK_VALUE) for s, mask in zip(scores, masks)]
        m_new = [jnp.maximum(m, jnp.max(s, axis=-1, keepdims=True)) for s, (m, _) in zip(scores, carry)]
        probs = [jnp.exp2(s - mn).astype(BF16) for s, mn in zip(scores, m_new)]
        return tuple((mn, jnp.exp2(m - mn) * acc + jnp.dot(p, vb, preferred_element_type=F32))
                     for p, mn, (m, acc), (_, vb) in zip(probs, m_new, carry, kvs))

    def chunk_mask(n, first_row):
        r = (lax.broadcasted_iota(jnp.int32, (half, n), 0) + first_row) // CHUNK
        c = lax.broadcasted_iota(jnp.int32, (half, n), 1) // CHUNK
        return c <= r

    for qi in range(k_ref.shape[0] // tq):
        @pl.when(i == qi)
        def _(qi=qi):
            carry = tuple((jnp.full((half, 1), MASK_VALUE, F32), jnp.zeros((half, HEAD_PAD), F32)) for _ in range(2))
            for j in range(qi):
                carry = step(j * tq, (tq, tq), (None, None), carry)
            (_, acc0), (_, acc1) = step(qi * tq, (half, tq), (chunk_mask(half, 0), chunk_mask(tq, half)), carry)
            o_ref[:half, :] = (acc0[:, :V_HEAD] / acc0[:, V_HEAD:]).astype(o_ref.dtype)
            o_ref[half:, :] = (acc1[:, :V_HEAD] / acc1[:, V_HEAD:]).astype(o_ref.dtype)
            round_side_weights()


def _attention(q, k, v, side_weights, *, batch, seq):
    tq = TQ_ATTN
    nq = seq // tq
    n_steps = batch * MLA_HEADS * nq
    step = lambda b, h, i, off=0: ((b * MLA_HEADS + h) * nq + i + off, 0, 0)
    side_in = [pl.BlockSpec((1,) + w.shape[1:], functools.partial(step, off=off)) for w, off in side_weights]
    side_out = [pl.BlockSpec((1,) + w.shape[1:], step) for w, _ in side_weights]
    outs = pl.pallas_call(
        _attn_kernel,
        grid=(batch, MLA_HEADS, nq),
        in_specs=[pl.BlockSpec((tq, HEAD_PAD), lambda b, h, i: (b * nq + i, h)),
                  pl.BlockSpec((seq, HEAD_PAD), lambda b, h, i: (b, h)),
                  pl.BlockSpec((seq, V_HEAD), lambda b, h, i: (b, h))] + side_in,
        out_specs=[pl.BlockSpec((tq, V_HEAD), lambda b, h, i: (b * nq + i, h))] + side_out,
        out_shape=[jax.ShapeDtypeStruct((batch * seq, MLA_HEADS * V_HEAD), BF16)]
                  + [jax.ShapeDtypeStruct((n_steps,) + w.shape[1:], BF16) for w, _ in side_weights],
        compiler_params=pltpu.CompilerParams(dimension_semantics=("arbitrary", "arbitrary", "arbitrary"),
                                             vmem_limit_bytes=VMEM_LIMIT),
        name="attn",
    )(q, k, v, *[w for w, _ in side_weights])
    return outs[0], outs[1:]


def _outproj_kernel(ya_ref, yc_ref, x_ref, gattn_ref, wout_ref, gmoe_ref, wgr_ref, bgr_ref,
                    h1_ref, xm_ref, route_ref):
    tm, d_attn = ya_ref.shape
    d_model = x_ref.shape[1]
    ya = _rms(ya_ref[...].astype(F32), gattn_ref[...], d_attn).astype(BF16)
    h1 = (x_ref[...]
          + jnp.dot(ya, wout_ref[:d_attn, :], preferred_element_type=F32)
          + jnp.dot(yc_ref[...], wout_ref[d_attn:, :], preferred_element_type=F32))
    h1_ref[...] = h1
    xm = _rms(h1, gmoe_ref[...], d_model)
    _store_slabs(xm_ref, xm, SLAB_PITCH)
    for c in range(SLAB, SLAB_PITCH):
        xm_ref[pl.ds(c, tm, stride=SLAB_PITCH), :] = jnp.zeros((tm, LANES), F32)

    x_hi = xm.astype(BF16)
    x_lo = (xm - x_hi.astype(F32)).astype(BF16)
    hi_part = jnp.dot(x_hi, wgr_ref[...], preferred_element_type=F32)
    lo_part = jnp.dot(x_lo, wgr_ref[:, :LANES], preferred_element_type=F32)
    logits = hi_part[:, :LANES] + hi_part[:, LANES:] + lo_part + bgr_ref[...]
    lane = lax.broadcasted_iota(jnp.int32, (tm, LANES), 1).astype(F32)
    neg = -jnp.inf
    big = 1e9
    gl = jnp.where(lane < N_GROUPS, logits, neg)
    gmax = jnp.max(gl, axis=-1, keepdims=True)
    gsel = jnp.min(jnp.where(gl == gmax, lane, big), axis=-1, keepdims=True)
    p_g = 1.0 / jnp.sum(jnp.exp(gl - gmax), axis=-1, keepdims=True)
    lo = N_GROUPS + EXPERTS_PER_GROUP * gsel
    el = jnp.where((lane >= lo) & (lane < lo + EXPERTS_PER_GROUP), logits, neg)
    v1 = jnp.max(el, axis=-1, keepdims=True)
    i1 = jnp.min(jnp.where(el == v1, lane, big), axis=-1, keepdims=True)
    el2 = jnp.where(lane == i1, neg, el)
    v2 = jnp.max(el2, axis=-1, keepdims=True)
    i2 = jnp.min(jnp.where(el2 == v2, lane, big), axis=-1, keepdims=True)
    e2 = jnp.exp(v2 - v1)
    den = 1.0 + e2
    w1 = p_g / den
    w2 = p_g * e2 / den
    route_ref[...] = jnp.where(lane == 0, i1 - N_GROUPS,
                               jnp.where(lane == 1, i2 - N_GROUPS,
                                         jnp.where(lane == 2, w1, jnp.where(lane == 3, w2, 0.0))))


def _outproj(ya, yc, x2, gattn, w_out, gmoe, w_gr, b_gr):
    t, d_model = x2.shape
    d_attn = ya.shape[1]
    tm = TM_OUT
    const = lambda i: (0, 0)
    row = lambda i: (i, 0)
    full = lambda a: pl.BlockSpec(a.shape, const)
    return pl.pallas_call(
        _outproj_kernel,
        grid=(t // tm,),
        in_specs=[pl.BlockSpec((tm, d_attn), row), pl.BlockSpec((tm, yc.shape[1]), row), pl.BlockSpec((tm, d_model), row),
                  full(gattn), pl.BlockSpec(w_out.shape, const, pipeline_mode=pl.Buffered(1)), full(gmoe), full(w_gr), full(b_gr)],
        out_specs=[pl.BlockSpec((tm, d_model), row), pl.BlockSpec((tm * SLAB_PITCH, LANES), row), pl.BlockSpec((tm, LANES), row)],
        out_shape=[jax.ShapeDtypeStruct((t, d_model), F32), jax.ShapeDtypeStruct((t * SLAB_PITCH, LANES), F32),
                   jax.ShapeDtypeStruct((t, LANES), F32)],
        compiler_params=pltpu.CompilerParams(dimension_semantics=("arbitrary",), vmem_limit_bytes=VMEM_LIMIT),
        name="outproj",
    )(ya, yc, x2, gattn, w_out, gmoe, w_gr, b_gr)


def _row_gather_start(idx_ref, src_hbm, dst_ref, sem, n_rows, src_pitch=SLAB):
    def body(r, carry):
        src = src_hbm.at[pl.ds(idx_ref[0, 0, r] * src_pitch, SLAB), :]
        pltpu.make_async_copy(src, dst_ref.at[pl.ds(r * SLAB_PITCH, SLAB), :], sem).start()
        return carry
    lax.fori_loop(0, n_rows, body, 0, unroll=True)


def _row_gather_wait(src_hbm, dst_ref, sem, n_rows):
    pltpu.make_async_copy(src_hbm.at[pl.ds(0, n_rows * SLAB), :], dst_ref.at[pl.ds(0, n_rows * SLAB), :], sem).wait()


def _moe_kernel(tile_expert_ref, n_active_ref, first_ref, wslot_ref, next_expert_ref, tok_cur_ref, tok_next_ref,
                xm_hbm, w1_hbm, w3_hbm, w2lo_hbm, w2hi_hbm, y_ref, xbuf_ref, w1_buf, w3_buf, w2_buf, sem_ref, wsem_ref, *, tm):
    i = pl.program_id(0)
    n_active = n_active_ref[0]
    slot = i % 2
    buf_rows = tm * SLAB_PITCH
    wslot = wslot_ref[i]

    def buf(s):
        return xbuf_ref.at[pl.ds(s * buf_rows, buf_rows), :]

    n_lo = w2lo_hbm.shape[0]

    def start_weights(e, s):
        pltpu.make_async_copy(w1_hbm.at[e], w1_buf.at[s], wsem_ref.at[0, s]).start()
        pltpu.make_async_copy(w3_hbm.at[e], w3_buf.at[s], wsem_ref.at[1, s]).start()

        @pl.when(e < n_lo)
        def _():
            pltpu.make_async_copy(w2lo_hbm.at[e], w2_buf.at[s], wsem_ref.at[2, s]).start()

        @pl.when(e >= n_lo)
        def _():
            pltpu.make_async_copy(w2hi_hbm.at[e - n_lo], w2_buf.at[s], wsem_ref.at[2, s]).start()

    def wait_weights(s):
        pltpu.make_async_copy(w1_hbm.at[0], w1_buf.at[s], wsem_ref.at[0, s]).wait()
        pltpu.make_async_copy(w3_hbm.at[0], w3_buf.at[s], wsem_ref.at[1, s]).wait()
        pltpu.make_async_copy(w2lo_hbm.at[0], w2_buf.at[s], wsem_ref.at[2, s]).wait()

    @pl.when(i == 0)
    def _():
        start_weights(tile_expert_ref[0], 0)
        _row_gather_start(tok_cur_ref, xm_hbm, buf(0), sem_ref.at[0], tm, src_pitch=SLAB_PITCH)

    active = i < n_active
    new_expert = active & (first_ref[i] == 1)

    @pl.when(new_expert & (next_expert_ref[i] >= 0))
    def _():
        start_weights(next_expert_ref[i], 1 - wslot)

    @pl.when(i + 1 < n_active)
    def _():
        _row_gather_start(tok_next_ref, xm_hbm, buf(1 - slot), sem_ref.at[1 - slot], tm, src_pitch=SLAB_PITCH)

    @pl.when(new_expert)
    def _():
        wait_weights(wslot)

    @pl.when(active)
    def _():
        _row_gather_wait(xm_hbm, buf(slot), sem_ref.at[slot], tm)
        x = _load_slabs(xbuf_ref, slot * buf_rows, tm, SLAB, SLAB_PITCH).astype(BF16)
        a = jnp.dot(x, w1_buf[wslot], preferred_element_type=F32)
        g = jnp.dot(x, w3_buf[wslot], preferred_element_type=F32)
        hid = (a * jax.nn.sigmoid(a) * g).astype(BF16)
        _store_slabs(y_ref, jnp.dot(hid, w2_buf[wslot], preferred_element_type=F32), SLAB)

    @pl.when(i >= n_active)
    def _():
        y_ref[...] = jnp.zeros_like(y_ref)


def _moe(plan, xm, w1, w3, w2_lo, w2_hi):
    tile_expert, n_active, first, wslot, next_expert, row_token3 = plan
    n_tiles = row_token3.shape[0]
    tm = TM_MOE
    _, d_model, d_exp = w1.shape
    last = n_tiles - 1
    grid_spec = pltpu.PrefetchScalarGridSpec(
        num_scalar_prefetch=5,
        grid=(n_tiles,),
        in_specs=[pl.BlockSpec((1, 1, tm), lambda i, *_: (i, 0, 0), memory_space=pltpu.SMEM),
                  pl.BlockSpec((1, 1, tm), lambda i, *_: (jnp.minimum(i + 1, last), 0, 0), memory_space=pltpu.SMEM),
                  pl.BlockSpec(memory_space=pl.ANY), pl.BlockSpec(memory_space=pl.ANY), pl.BlockSpec(memory_space=pl.ANY),
                  pl.BlockSpec(memory_space=pl.ANY), pl.BlockSpec(memory_space=pl.ANY)],
        out_specs=pl.BlockSpec((tm * SLAB, LANES), lambda i, *_: (i, 0)),
        scratch_shapes=[pltpu.VMEM((2 * tm * SLAB_PITCH, LANES), F32),
                        pltpu.VMEM((2, d_model, d_exp), BF16), pltpu.VMEM((2, d_model, d_exp), BF16),
                        pltpu.VMEM((2, d_exp, d_model), BF16),
                        pltpu.SemaphoreType.DMA((2,)), pltpu.SemaphoreType.DMA((3, 2))],
    )
    return pl.pallas_call(
        functools.partial(_moe_kernel, tm=tm),
        grid_spec=grid_spec,
        out_shape=jax.ShapeDtypeStruct((n_tiles * tm * SLAB, LANES), F32),
        compiler_params=pltpu.CompilerParams(dimension_semantics=("arbitrary",), vmem_limit_bytes=VMEM_LIMIT),
        name="moe",
    )(tile_expert, n_active, first, wslot, next_expert, row_token3, row_token3, xm, w1, w3, w2_lo, w2_hi)


def _ple_kernel(dest_cur_ref, dest_next_ref, y_hbm, h1_ref, route_ref, p_ref, wple_ref, gple_ref, wpg_ref, bpg_ref,
                o_ref, ybuf_ref, sem_ref):
    i = pl.program_id(0)
    n = pl.num_programs(0)
    tm, d_model = h1_ref.shape
    slot = i % 2
    buf_rows = TOP_K * tm * SLAB_PITCH

    def buf(s):
        return ybuf_ref.at[pl.ds(s * buf_rows, buf_rows), :]

    @pl.when(i == 0)
    def _():
        _row_gather_start(dest_cur_ref, y_hbm, buf(0), sem_ref.at[0], TOP_K * tm)

    @pl.when(i + 1 < n)
    def _():
        _row_gather_start(dest_next_ref, y_hbm, buf(1 - slot), sem_ref.at[1 - slot], TOP_K * tm)

    _row_gather_wait(y_hbm, buf(slot), sem_ref.at[slot], TOP_K * tm)
    route = route_ref[...]
    y0 = _load_slabs(ybuf_ref, slot * buf_rows, tm, SLAB, SLAB_PITCH)
    y1 = _load_slabs(ybuf_ref, slot * buf_rows + tm * SLAB_PITCH, tm, SLAB, SLAB_PITCH)
    h2 = h1_ref[...] + (route[:, 2:3] * y0 + route[:, 3:4] * y1)
    xn = _rms(h2, gple_ref[...], d_model).astype(BF16)
    gate = jax.nn.sigmoid(jnp.dot(xn, wpg_ref[...], preferred_element_type=F32) + bpg_ref[...])
    pe = jnp.dot(p_ref[...].astype(BF16), wple_ref[...], preferred_element_type=F32)
    o_ref[...] = h2 + gate * pe


def _ple(dest3, y, h1, route, p2, w_ple, gple, w_pg, b_pg):
    t, d_model = h1.shape
    tm = TM_PLE
    n_tiles = t // tm
    last = n_tiles - 1
    const = lambda i: (0, 0)
    row = lambda i: (i, 0)
    full = lambda a: pl.BlockSpec(a.shape, const)
    return pl.pallas_call(
        _ple_kernel,
        grid=(n_tiles,),
        in_specs=[pl.BlockSpec((1, 1, TOP_K * tm), lambda i: (i, 0, 0), memory_space=pltpu.SMEM),
                  pl.BlockSpec((1, 1, TOP_K * tm), lambda i: (jnp.minimum(i + 1, last), 0, 0), memory_space=pltpu.SMEM),
                  pl.BlockSpec(memory_space=pl.ANY),
                  pl.BlockSpec((tm, d_model), row), pl.BlockSpec((tm, LANES), row), pl.BlockSpec((tm, p2.shape[1]), row),
                  full(w_ple), full(gple), pl.BlockSpec(w_pg.shape, const, pipeline_mode=pl.Buffered(1)), full(b_pg)],
        out_specs=pl.BlockSpec((tm, d_model), row),
        out_shape=jax.ShapeDtypeStruct((t, d_model), F32),
        scratch_shapes=[pltpu.VMEM((2 * TOP_K * tm * SLAB_PITCH, LANES), F32), pltpu.SemaphoreType.DMA((2,))],
        compiler_params=pltpu.CompilerParams(dimension_semantics=("arbitrary",), vmem_limit_bytes=VMEM_LIMIT),
        name="ple",
    )(dest3, dest3, y, h1, route, p2, w_ple, gple, w_pg, b_pg)


def _dispatch_plan(route):
    t = route.shape[0]
    tm = TM_MOE
    total = TOP_K * t
    n_tiles = total // tm + N_EXPERTS
    expert = route[:, :TOP_K].astype(jnp.int32).reshape(-1)
    pos = jnp.arange(total, dtype=jnp.int32)
    order = jnp.sort(expert * total + pos) % total
    rank = jnp.sort(order * total + pos) % total
    experts = jnp.arange(N_EXPERTS, dtype=jnp.int32)
    onehot = (expert[:, None] == experts[None, :]).astype(jnp.int32)
    counts = jnp.sum(onehot, axis=0)
    group_start = jnp.cumsum(counts) - counts
    tile_end = jnp.cumsum((counts + tm - 1) // tm)
    tile_start = tile_end - (counts + tm - 1) // tm
    n_active = tile_end[-1]
    shift = tile_start * tm - group_start
    dest = rank + jnp.sum(onehot * shift[None, :], axis=1)
    tiles = jnp.minimum(jnp.arange(n_tiles, dtype=jnp.int32), n_active - 1)
    tile_expert = jnp.sum((tile_end[None, :] <= tiles[:, None]).astype(jnp.int32), axis=1)
    j = jnp.arange(tm, dtype=jnp.int32)
    fill_row = (tile_start * tm + counts)[:, None] + j[None, :]
    fill_row = jnp.where(fill_row < (tile_end * tm)[:, None], fill_row, n_tiles * tm)
    fill_tok = (experts[:, None] * tm + j[None, :]) % t
    keys = jnp.concatenate([dest * t + pos // TOP_K, (fill_row * t + fill_tok).reshape(-1)])
    row_token3 = (jnp.sort(keys) % t).reshape(n_tiles, 1, tm)
    dest3 = dest.reshape(t // TM_PLE, TM_PLE, TOP_K).transpose(0, 2, 1).reshape(t // TM_PLE, 1, TOP_K * TM_PLE)
    prev_expert = jnp.concatenate([jnp.full((1,), -1, jnp.int32), tile_expert[:-1]])
    first = (tile_expert != prev_expert).astype(jnp.int32)
    nonempty = counts > 0
    wslot = (jnp.sum((nonempty[None, :] & (experts[None, :] < tile_expert[:, None])).astype(jnp.int32), axis=1)) % 2
    later = nonempty[None, :] & (experts[None, :] > tile_expert[:, None])
    next_expert = jnp.min(jnp.where(later, experts[None, :], N_EXPERTS), axis=1)
    next_expert = jnp.where(next_expert < N_EXPERTS, next_expert, -1)
    plan = (tile_expert, n_active.reshape(1), first, wslot, next_expert, row_token3)
    return plan, dest3


def _rope_swap_pad(w):
    half = QK_ROPE // 2
    x1 = w[..., QK_NOPE:QK_NOPE + half]
    x2 = w[..., QK_NOPE + half:]
    return jnp.concatenate([w[..., :QK_NOPE], x1, x2, x2, x1], axis=-1)


def kernel(x, p, positions, norm_mix, w_in, g_q_lat, g_kv_lat, w_uq, w_ukv, g_q_head, g_k_head, w_conv, g_out_attn, g_out_conv, w_out, norm_moe, w_group, b_group, w_router, b_router, w1, w3, w2, norm_ple, w_ple, w_ple_gate, b_ple_gate):
    batch, seq, d_model = x.shape
    depth = w_in.shape[0]
    t = batch * seq
    q_rank = g_q_lat.shape[1]
    kv_rank = g_kv_lat.shape[1]
    half = QK_ROPE // 2
    rope_lo = q_rank + kv_rank

    pos2 = positions.reshape(t, 1)
    inv_freq = ROPE_BASE ** (-jnp.arange(0, QK_ROPE, 2, dtype=F32) / QK_ROPE)
    freq = jnp.tile(inv_freq, LANES // half).reshape(1, LANES)

    h = x.reshape(t, d_model)
    for i in range(depth):
        w_in_t = jnp.swapaxes(w_in[i], 0, 1).astype(BF16)
        w_rope = jnp.concatenate([w_in_t[rope_lo:rope_lo + QK_ROPE], w_in_t[rope_lo + half:rope_lo + QK_ROPE],
                                  w_in_t[rope_lo:rope_lo + half]], axis=0)
        wuq_p = _rope_swap_pad(w_uq[i]).reshape(q_rank, MLA_HEADS * HEAD_PAD).astype(BF16)
        wukv_p = jnp.concatenate([w_ukv[i][:, :, :QK_NOPE].reshape(kv_rank, -1),
                                  w_ukv[i][:, :, QK_NOPE:].reshape(kv_rank, -1)], axis=1).astype(BF16)
        gq_p = _rope_swap_pad(g_q_head[i]).reshape(1, HEAD_PAD)
        gk_p = _rope_swap_pad(g_k_head[i]).reshape(1, HEAD_PAD)
        as_steps = lambda w, n: w.reshape(n, -1, w.shape[-1])
        nf = t // TM_FRONT
        na = batch * MLA_HEADS * (seq // TQ_ATTN)
        front_sides = [(as_steps(w, nf), 0) for w in (w1[i], w_out[i], w_ple_gate[i])] + [(as_steps(w2[i], 2 * nf), 0)]
        (q, k, v, yc), (w1b, w_out_b, w_pg_b, w2_lo) = _front(
            h, pos2, freq, norm_mix[i].reshape(1, -1), w_in_t, w_rope, g_q_lat[i].reshape(1, -1),
            g_kv_lat[i].reshape(1, -1), wuq_p, wukv_p, gq_p, gk_p, w_conv[i], g_out_conv[i].reshape(1, -1), front_sides, seq=seq)
        ya, (w3b, w2_hi) = _attention(q, k, v, [(as_steps(w3[i], na), 0), (as_steps(w2[i], 2 * na), na)], batch=batch, seq=seq)
        half_e = (N_EXPERTS // 2,) + w2[i].shape[1:]
        w1b, w3b, w2_lo, w2_hi = w1b.reshape(w1[i].shape), w3b.reshape(w3[i].shape), w2_lo.reshape(half_e), w2_hi.reshape(half_e)
        w_out_b, w_pg_b = w_out_b.reshape(w_out[i].shape), w_pg_b.reshape(w_ple_gate[i].shape)

        pad = LANES - N_GROUPS - N_EXPERTS
        w_gr32 = jnp.concatenate([w_group[i], w_router[i], jnp.zeros((d_model, pad), F32)], axis=1)
        w_gr_hi = w_gr32.astype(BF16)
        w_gr = jnp.concatenate([w_gr_hi, (w_gr32 - w_gr_hi.astype(F32)).astype(BF16)], axis=1)
        b_gr = jnp.concatenate([b_group[i], b_router[i], jnp.zeros((pad,), F32)]).reshape(1, LANES)
        h1, xm, route = _outproj(ya, yc, h, g_out_attn[i].reshape(1, -1), w_out_b,
                                 norm_moe[i].reshape(1, -1), w_gr, b_gr)

        plan, dest3 = _dispatch_plan(route)
        y = _moe(plan, xm, w1b, w3b, w2_lo, w2_hi)
        h = _ple(dest3, y, h1, route, p[i].reshape(t, -1), w_ple[i].astype(BF16), norm_ple[i].reshape(1, -1),
                 w_pg_b, b_ple_gate[i].reshape(1, -1))
    return h.reshape(batch, seq, d_model)
```

```python
import functools

import jax
import jax.numpy as jnp
from jax import lax
from jax.experimental import pallas as pl
from jax.experimental.pallas import tpu as pltpu

F32 = jnp.float32
BF16 = jnp.bfloat16

CHUNK = 64
EPS = 1e-6
MASK_VALUE = -1e30
ROPE_BASE = 10000.0

MLA_HEADS = 8
QK_NOPE = 128
QK_ROPE = 64
QK_HEAD = QK_NOPE + QK_ROPE
V_HEAD = 128
HEAD_PAD = 256
Q_SCALE = QK_HEAD ** -0.5 * 1.4426950408889634
N_GROUPS = 4
EXPERTS_PER_GROUP = 8
N_EXPERTS = N_GROUPS * EXPERTS_PER_GROUP
TOP_K = 2
LANES = 128
SLAB = 16
SLAB_PITCH = 20

TM_FRONT = 256
TQ_ATTN = 1024
TM_OUT = 512
TM_MOE = 256
TM_PLE = 256
VMEM_LIMIT = 56 * 1024 * 1024


def _rms(x, g, n):
    ms = jnp.sum(x * x, axis=-1, keepdims=True) * (1.0 / n)
    return x * lax.rsqrt(ms + EPS) * g


def _store_slabs(ref, val, pitch):
    n, width = val.shape
    for c in range(width // LANES):
        ref[pl.ds(c, n, stride=pitch), :] = val[:, c * LANES:(c + 1) * LANES]


def _load_slabs(ref, start, n, slab, pitch):
    return jnp.concatenate([ref[pl.ds(start + c, n, stride=pitch), :] for c in range(slab)], axis=1)


def _front_kernel(x_ref, pos_ref, freq_ref, gmix_ref, win_ref, wrope_ref, gql_ref, gkvl_ref,
                  wuq_ref, wukv_ref, gq_ref, gk_ref, wconv_ref, gconv_ref, *rest,
                  n_side, tiles_per_seq, q_rank, kv_rank, d_conv):
    side_refs = rest[:n_side]
    q_ref, k_ref, v_ref, yc_ref = rest[n_side:n_side + 4]
    side_out_refs = rest[n_side + 4:2 * n_side + 4]
    prev_ref = rest[-1]
    i = pl.program_id(0)
    is_start = (i % tiles_per_seq) == 0
    tm = x_ref.shape[0]
    d_model = x_ref.shape[1]
    rope_lo = q_rank + kv_rank
    nt = (((1,), (1,)), ((), ()))

    @pl.when(is_start)
    def _():
        prev_ref[...] = jnp.zeros_like(prev_ref)

    xn = _rms(x_ref[...], gmix_ref[...], d_model).astype(BF16)
    lat = lax.dot_general(xn, win_ref[:rope_lo, :], nt, preferred_element_type=F32)
    q_lat = lat[:, :q_rank]
    kv_lat = lat[:, q_rank:]
    kr = lax.dot_general(xn, wrope_ref[...], nt, preferred_element_type=F32)

    qn = _rms(q_lat, gql_ref[...], q_rank).astype(BF16)
    kvn = _rms(kv_lat, gkvl_ref[...], kv_rank).astype(BF16)
    q_all = jnp.dot(qn, wuq_ref[...], preferred_element_type=F32)
    kv_all = jnp.dot(kvn, wukv_ref[...], preferred_element_type=F32)

    lane = lax.broadcasted_iota(jnp.int32, (tm, LANES), 1)
    ang = pos_ref[...].astype(F32) * freq_ref[...]
    cos = jnp.cos(ang)
    sin = jnp.sin(ang)
    rope_c = jnp.where(lane < QK_ROPE, cos, 0.0)
    rope_s = jnp.where(lane < QK_ROPE // 2, -sin, jnp.where(lane < QK_ROPE, sin, 0.0))
    rope_lanes = lane < QK_ROPE

    def rope(t):
        return t * rope_c + pltpu.roll(t, QK_ROPE, 1) * rope_s

    gq = gq_ref[...]
    gk = gk_ref[...]
    inv_n = 1.0 / QK_HEAD
    kr_ssq = jnp.sum(jnp.where(rope_lanes, kr * kr, 0.0), axis=-1, keepdims=True)
    kr_rot = rope(kr * gk[:, QK_NOPE:])
    for h in range(MLA_HEADS):
        qa = q_all[:, h * HEAD_PAD:h * HEAD_PAD + QK_NOPE]
        qb = q_all[:, h * HEAD_PAD + QK_NOPE:(h + 1) * HEAD_PAD]
        ssq = (jnp.sum(qa * qa, axis=-1, keepdims=True)
               + jnp.sum(jnp.where(rope_lanes, qb * qb, 0.0), axis=-1, keepdims=True))
        inv = lax.rsqrt(ssq * inv_n + EPS)
        q_ref[:, h * HEAD_PAD:h * HEAD_PAD + QK_NOPE] = (qa * inv * gq[:, :QK_NOPE] * Q_SCALE).astype(BF16)
        q_ref[:, h * HEAD_PAD + QK_NOPE:(h + 1) * HEAD_PAD] = (rope(qb * inv * gq[:, QK_NOPE:]) * Q_SCALE).astype(BF16)

        ka = kv_all[:, h * QK_NOPE:(h + 1) * QK_NOPE]
        kinv = lax.rsqrt((jnp.sum(ka * ka, axis=-1, keepdims=True) + kr_ssq) * inv_n + EPS)
        k_ref[:, h * HEAD_PAD:h * HEAD_PAD + QK_NOPE] = (ka * kinv * gk[:, :QK_NOPE]).astype(BF16)
        k_ref[:, h * HEAD_PAD + QK_NOPE:(h + 1) * HEAD_PAD] = (kr_rot * kinv).astype(BF16)
    v_ref[...] = kv_all[:, MLA_HEADS * QK_NOPE:].astype(BF16)

    conv_lo = rope_lo + QK_ROPE
    half_c = d_conv // 2
    row = lax.broadcasted_iota(jnp.int32, (tm, half_c), 0)
    wc = wconv_ref[...]
    ys = []
    ssq = jnp.zeros((tm, 1), F32)
    for cs in (slice(0, half_c), slice(half_c, d_conv)):
        b_gate, c_gate, h_conv = (lax.dot_general(xn, win_ref[conv_lo + k * d_conv + cs.start:conv_lo + k * d_conv + cs.stop, :],
                                                  nt, preferred_element_type=F32) for k in range(3))
        u = c_gate * h_conv
        prev = prev_ref[:, cs]
        hm1 = prev[7:8, :]
        hm2 = prev[6:7, :]
        u1 = jnp.where(row == 0, hm1, pltpu.roll(u, 1, 0))
        u2 = jnp.where(row == 0, hm2, jnp.where(row == 1, hm1, pltpu.roll(u, 2, 0)))
        prev_ref[:, cs] = u[tm - 8:, :]
        y = b_gate * (u2 * wc[0:1, cs] + u1 * wc[1:2, cs] + u * wc[2:3, cs])
        ssq = ssq + jnp.sum(y * y, axis=-1, keepdims=True)
        ys.append(y)
    inv = lax.rsqrt(ssq * (1.0 / d_conv) + EPS)
    yc_ref[...] = (jnp.concatenate(ys, axis=1) * inv * gconv_ref[...]).astype(BF16)
    for w_ref, wb_ref in zip(side_refs, side_out_refs):
        wb_ref[...] = w_ref[...].astype(BF16)


def _front(x2, pos2, freq, gmix, w_in_t, w_rope, gql, gkvl, wuq_p, wukv_p, gq_p, gk_p, wconv, gconv, sides, *, seq):
    t, d_model = x2.shape
    tm = TM_FRONT
    q_rank = gql.shape[1]
    kv_rank = gkvl.shape[1]
    d_conv = gconv.shape[1]
    const = lambda i: (0, 0)
    row = lambda i: (i, 0)
    full = lambda a: pl.BlockSpec(a.shape, const)
    side_in = [pl.BlockSpec((1,) + w.shape[1:], lambda i, off=off: (i + off, 0, 0)) for w, off in sides]
    side_out = [pl.BlockSpec((1,) + w.shape[1:], lambda i: (i, 0, 0)) for w, _ in sides]
    kern = functools.partial(_front_kernel, n_side=len(sides), tiles_per_seq=seq // tm, q_rank=q_rank, kv_rank=kv_rank,
                             d_conv=d_conv)
    outs = pl.pallas_call(
        kern,
        grid=(t // tm,),
        in_specs=[pl.BlockSpec((tm, d_model), row), pl.BlockSpec((tm, 1), row), full(freq), full(gmix),
                  pl.BlockSpec(w_in_t.shape, const, pipeline_mode=pl.Buffered(1)), full(w_rope),
                  full(gql), full(gkvl), full(wuq_p), full(wukv_p), full(gq_p), full(gk_p), full(wconv), full(gconv)] + side_in,
        out_specs=[pl.BlockSpec((tm, MLA_HEADS * HEAD_PAD), row), pl.BlockSpec((tm, MLA_HEADS * HEAD_PAD), row),
                   pl.BlockSpec((tm, MLA_HEADS * V_HEAD), row), pl.BlockSpec((tm, d_conv), row)] + side_out,
        out_shape=[jax.ShapeDtypeStruct((t, MLA_HEADS * HEAD_PAD), BF16), jax.ShapeDtypeStruct((t, MLA_HEADS * HEAD_PAD), BF16),
                   jax.ShapeDtypeStruct((t, MLA_HEADS * V_HEAD), BF16), jax.ShapeDtypeStruct((t, d_conv), BF16)]
                  + [jax.ShapeDtypeStruct((t // tm,) + w.shape[1:], BF16) for w, _ in sides],
        scratch_shapes=[pltpu.VMEM((8, d_conv), F32)],
        compiler_params=pltpu.CompilerParams(dimension_semantics=("arbitrary",), vmem_limit_bytes=VMEM_LIMIT),
        name="front",
    )(x2, pos2, freq, gmix, w_in_t, w_rope, gql, gkvl, wuq_p, wukv_p, gq_p, gk_p, wconv, gconv, *[w for w, _ in sides])
    return outs[:4], outs[4:]


def _attn_kernel(q_ref, k_ref, v_ref, *rest):
    n_side = (len(rest) - 1) // 2
    o_ref = rest[n_side]

    def round_side_weights():
        for w_ref, wb_ref in zip(rest[:n_side], rest[n_side + 1:]):
            wb_ref[...] = w_ref[...].astype(BF16)

    i = pl.program_id(2)
    tq = q_ref.shape[0]
    half = tq // 2
    q_halves = (q_ref[:half, :], q_ref[half:, :])

    def step(start, n_keys, masks, carry):
        kvs = [(k_ref[pl.ds(start, n), :],
                jnp.concatenate([v_ref[pl.ds(start, n), :], jnp.ones((n, HEAD_PAD - V_HEAD), BF16)], axis=1)) for n in n_keys]
        scores = [lax.dot_general(qh, kb, (((1,), (1,)), ((), ())), preferred_element_type=F32)
                  for qh, (kb, _) in zip(q_halves, kvs)]
        scores = [s if mask is None else jnp.where(mask, s, MASK_VALUE) for s, mask in zip(scores, masks)]
        m_new = [jnp.maximum(m, jnp.max(s, axis=-1, keepdims=True)) for s, (m, _) in zip(scores, carry)]
        probs = [jnp.exp2(s - mn).astype(BF16) for s, mn in zip(scores, m_new)]
        return tuple((mn, jnp.exp2(m - mn) * acc + jnp.dot(p, vb, preferred_element_type=F32))
                     for p, mn, (m, acc), (_, vb) in zip(probs, m_new, carry, kvs))

    def chunk_mask(n, first_row):
        r = (lax.broadcasted_iota(jnp.int32, (half, n), 0) + first_row) // CHUNK
        c = lax.broadcasted_iota(jnp.int32, (half, n), 1) // CHUNK
        return c <= r

    for qi in range(k_ref.shape[0] // tq):
        @pl.when(i == qi)
        def _(qi=qi):
            carry = tuple((jnp.full((half, 1), MASK_VALUE, F32), jnp.zeros((half, HEAD_PAD), F32)) for _ in range(2))
            for j in range(qi):
                carry = step(j * tq, (tq, tq), (None, None), carry)
            (_, acc0), (_, acc1) = step(qi * tq, (half, tq), (chunk_mask(half, 0), chunk_mask(tq, half)), carry)
            o_ref[:half, :] = (acc0[:, :V_HEAD] / acc0[:, V_HEAD:]).astype(o_ref.dtype)
            o_ref[half:, :] = (acc1[:, :V_HEAD] / acc1[:, V_HEAD:]).astype(o_ref.dtype)
            round_side_weights()


def _attention(q, k, v, side_weights, *, batch, seq):
    tq = TQ_ATTN
    nq = seq // tq
    n_steps = batch * MLA_HEADS * nq
    step = lambda b, h, i, off=0: ((b * MLA_HEADS + h) * nq + i + off, 0, 0)
    side_in = [pl.BlockSpec((1,) + w.shape[1:], functools.partial(step, off=off)) for w, off in side_weights]
    side_out = [pl.BlockSpec((1,) + w.shape[1:], step) for w, _ in side_weights]
    outs = pl.pallas_call(
        _attn_kernel,
        grid=(batch, MLA_HEADS, nq),
        in_specs=[pl.BlockSpec((tq, HEAD_PAD), lambda b, h, i: (b * nq + i, h)),
                  pl.BlockSpec((seq, HEAD_PAD), lambda b, h, i: (b, h)),
                  pl.BlockSpec((seq, V_HEAD), lambda b, h, i: (b, h))] + side_in,
        out_specs=[pl.BlockSpec((tq, V_HEAD), lambda b, h, i: (b * nq + i, h))] + side_out,
        out_shape=[jax.ShapeDtypeStruct((batch * seq, MLA_HEADS * V_HEAD), BF16)]
                  + [jax.ShapeDtypeStruct((n_steps,) + w.shape[1:], BF16) for w, _ in side_weights],
        compiler_params=pltpu.CompilerParams(dimension_semantics=("arbitrary", "arbitrary", "arbitrary"),
                                             vmem_limit_bytes=VMEM_LIMIT),
        name="attn",
    )(q, k, v, *[w for w, _ in side_weights])
    return outs[0], outs[1:]


def _outproj_kernel(ya_ref, yc_ref, x_ref, gattn_ref, wout_ref, gmoe_ref, wgr_ref, bgr_ref,
                    h1_ref, xm_ref, route_ref):
    tm, d_attn = ya_ref.shape
    d_model = x_ref.shape[1]
    ya = _rms(ya_ref[...].astype(F32), gattn_ref[...], d_attn).astype(BF16)
    h1 = (x_ref[...]
          + jnp.dot(ya, wout_ref[:d_attn, :], preferred_element_type=F32)
          + jnp.dot(yc_ref[...], wout_ref[d_attn:, :], preferred_element_type=F32))
    h1_ref[...] = h1
    xm = _rms(h1, gmoe_ref[...], d_model)
    _store_slabs(xm_ref, xm, SLAB_PITCH)
    for c in range(SLAB, SLAB_PITCH):
        xm_ref[pl.ds(c, tm, stride=SLAB_PITCH), :] = jnp.zeros((tm, LANES), F32)

    x_hi = xm.astype(BF16)
    x_lo = (xm - x_hi.astype(F32)).astype(BF16)
    hi_part = jnp.dot(x_hi, wgr_ref[...], preferred_element_type=F32)
    lo_part = jnp.dot(x_lo, wgr_ref[:, :LANES], preferred_element_type=F32)
    logits = hi_part[:, :LANES] + hi_part[:, LANES:] + lo_part + bgr_ref[...]
    lane = lax.broadcasted_iota(jnp.int32, (tm, LANES), 1).astype(F32)
    neg = -jnp.inf
    big = 1e9
    gl = jnp.where(lane < N_GROUPS, logits, neg)
    gmax = jnp.max(gl, axis=-1, keepdims=True)
    gsel = jnp.min(jnp.where(gl == gmax, lane, big), axis=-1, keepdims=True)
    p_g = 1.0 / jnp.sum(jnp.exp(gl - gmax), axis=-1, keepdims=True)
    lo = N_GROUPS + EXPERTS_PER_GROUP * gsel
    el = jnp.where((lane >= lo) & (lane < lo + EXPERTS_PER_GROUP), logits, neg)
    v1 = jnp.max(el, axis=-1, keepdims=True)
    i1 = jnp.min(jnp.where(el == v1, lane, big), axis=-1, keepdims=True)
    el2 = jnp.where(lane == i1, neg, el)
    v2 = jnp.max(el2, axis=-1, keepdims=True)
    i2 = jnp.min(jnp.where(el2 == v2, lane, big), axis=-1, keepdims=True)
    e2 = jnp.exp(v2 - v1)
    den = 1.0 + e2
    w1 = p_g / den
    w2 = p_g * e2 / den
    route_ref[...] = jnp.where(lane == 0, i1 - N_GROUPS,
                               jnp.where(lane == 1, i2 - N_GROUPS,
                                         jnp.where(lane == 2, w1, jnp.where(lane == 3, w2, 0.0))))


def _outproj(ya, yc, x2, gattn, w_out, gmoe, w_gr, b_gr):
    t, d_model = x2.shape
    d_attn = ya.shape[1]
    tm = TM_OUT
    const = lambda i: (0, 0)
    row = lambda i: (i, 0)
    full = lambda a: pl.BlockSpec(a.shape, const)
    return pl.pallas_call(
        _outproj_kernel,
        grid=(t // tm,),
        in_specs=[pl.BlockSpec((tm, d_attn), row), pl.BlockSpec((tm, yc.shape[1]), row), pl.BlockSpec((tm, d_model), row),
                  full(gattn), pl.BlockSpec(w_out.shape, const, pipeline_mode=pl.Buffered(1)), full(gmoe), full(w_gr), full(b_gr)],
        out_specs=[pl.BlockSpec((tm, d_model), row), pl.BlockSpec((tm * SLAB_PITCH, LANES), row), pl.BlockSpec((tm, LANES), row)],
        out_shape=[jax.ShapeDtypeStruct((t, d_model), F32), jax.ShapeDtypeStruct((t * SLAB_PITCH, LANES), F32),
                   jax.ShapeDtypeStruct((t, LANES), F32)],
        compiler_params=pltpu.CompilerParams(dimension_semantics=("arbitrary",), vmem_limit_bytes=VMEM_LIMIT),
        name="outproj",
    )(ya, yc, x2, gattn, w_out, gmoe, w_gr, b_gr)


def _row_gather_start(idx_ref, src_hbm, dst_ref, sem, n_rows, src_pitch=SLAB):
    def body(r, carry):
        src = src_hbm.at[pl.ds(idx_ref[0, 0, r] * src_pitch, SLAB), :]
        pltpu.make_async_copy(src, dst_ref.at[pl.ds(r * SLAB_PITCH, SLAB), :], sem).start()
        return carry
    lax.fori_loop(0, n_rows, body, 0, unroll=True)


def _row_gather_wait(src_hbm, dst_ref, sem, n_rows):
    pltpu.make_async_copy(src_hbm.at[pl.ds(0, n_rows * SLAB), :], dst_ref.at[pl.ds(0, n_rows * SLAB), :], sem).wait()


def _moe_kernel(tile_expert_ref, n_active_ref, first_ref, wslot_ref, next_expert_ref, tok_cur_ref, tok_next_ref,
                xm_hbm, w1_hbm, w3_hbm, w2lo_hbm, w2hi_hbm, y_ref, xbuf_ref, w1_buf, w3_buf, w2_buf, sem_ref, wsem_ref, *, tm):
    i = pl.program_id(0)
    n_active = n_active_ref[0]
    slot = i % 2
    buf_rows = tm * SLAB_PITCH
    wslot = wslot_ref[i]

    def buf(s):
        return xbuf_ref.at[pl.ds(s * buf_rows, buf_rows), :]

    n_lo = w2lo_hbm.shape[0]

    def start_weights(e, s):
        pltpu.make_async_copy(w1_hbm.at[e], w1_buf.at[s], wsem_ref.at[0, s]).start()
        pltpu.make_async_copy(w3_hbm.at[e], w3_buf.at[s], wsem_ref.at[1, s]).start()

        @pl.when(e < n_lo)
        def _():
            pltpu.make_async_copy(w2lo_hbm.at[e], w2_buf.at[s], wsem_ref.at[2, s]).start()

        @pl.when(e >= n_lo)
        def _():
            pltpu.make_async_copy(w2hi_hbm.at[e - n_lo], w2_buf.at[s], wsem_ref.at[2, s]).start()

    def wait_weights(s):
        pltpu.make_async_copy(w1_hbm.at[0], w1_buf.at[s], wsem_ref.at[0, s]).wait()
        pltpu.make_async_copy(w3_hbm.at[0], w3_buf.at[s], wsem_ref.at[1, s]).wait()
        pltpu.make_async_copy(w2lo_hbm.at[0], w2_buf.at[s], wsem_ref.at[2, s]).wait()

    @pl.when(i == 0)
    def _():
        start_weights(tile_expert_ref[0], 0)
        _row_gather_start(tok_cur_ref, xm_hbm, buf(0), sem_ref.at[0], tm, src_pitch=SLAB_PITCH)

    active = i < n_active
    new_expert = active & (first_ref[i] == 1)

    @pl.when(new_expert & (next_expert_ref[i] >= 0))
    def _():
        start_weights(next_expert_ref[i], 1 - wslot)

    @pl.when(i + 1 < n_active)
    def _():
        _row_gather_start(tok_next_ref, xm_hbm, buf(1 - slot), sem_ref.at[1 - slot], tm, src_pitch=SLAB_PITCH)

    @pl.when(new_expert)
    def _():
        wait_weights(wslot)

    @pl.when(active)
    def _():
        _row_gather_wait(xm_hbm, buf(slot), sem_ref.at[slot], tm)
        x = _load_slabs(xbuf_ref, slot * buf_rows, tm, SLAB, SLAB_PITCH).astype(BF16)
        a = jnp.dot(x, w1_buf[wslot], preferred_element_type=F32)
        g = jnp.dot(x, w3_buf[wslot], preferred_element_type=F32)
        hid = (a * jax.nn.sigmoid(a) * g).astype(BF16)
        _store_slabs(y_ref, jnp.dot(hid, w2_buf[wslot], preferred_element_type=F32), SLAB)

    @pl.when(i >= n_active)
    def _():
        y_ref[...] = jnp.zeros_like(y_ref)


def _moe(plan, xm, w1, w3, w2_lo, w2_hi):
    tile_expert, n_active, first, wslot, next_expert, row_token3 = plan
    n_tiles = row_token3.shape[0]
    tm = TM_MOE
    _, d_model, d_exp = w1.shape
    last = n_tiles - 1
    grid_spec = pltpu.PrefetchScalarGridSpec(
        num_scalar_prefetch=5,
        grid=(n_tiles,),
        in_specs=[pl.BlockSpec((1, 1, tm), lambda i, *_: (i, 0, 0), memory_space=pltpu.SMEM),
                  pl.BlockSpec((1, 1, tm), lambda i, *_: (jnp.minimum(i + 1, last), 0, 0), memory_space=pltpu.SMEM),
                  pl.BlockSpec(memory_space=pl.ANY), pl.BlockSpec(memory_space=pl.ANY), pl.BlockSpec(memory_space=pl.ANY),
                  pl.BlockSpec(memory_space=pl.ANY), pl.BlockSpec(memory_space=pl.ANY)],
        out_specs=pl.BlockSpec((tm * SLAB, LANES), lambda i, *_: (i, 0)),
        scratch_shapes=[pltpu.VMEM((2 * tm * SLAB_PITCH, LANES), F32),
                        pltpu.VMEM((2, d_model, d_exp), BF16), pltpu.VMEM((2, d_model, d_exp), BF16),
                        pltpu.VMEM((2, d_exp, d_model), BF16),
                        pltpu.SemaphoreType.DMA((2,)), pltpu.SemaphoreType.DMA((3, 2))],
    )
    return pl.pallas_call(
        functools.partial(_moe_kernel, tm=tm),
        grid_spec=grid_spec,
        out_shape=jax.ShapeDtypeStruct((n_tiles * tm * SLAB, LANES), F32),
        compiler_params=pltpu.CompilerParams(dimension_semantics=("arbitrary",), vmem_limit_bytes=VMEM_LIMIT),
        name="moe",
    )(tile_expert, n_active, first, wslot, next_expert, row_token3, row_token3, xm, w1, w3, w2_lo, w2_hi)


def _ple_kernel(dest_a_ref, dest_b_ref, dest_next_ref, y_hbm, h1_ref, route_ref, p_ref, wple_ref, gple_ref, wpg_ref,
                bpg_ref, o_ref, ybuf_ref, sem_ref):
    i = pl.program_id(0)
    n = pl.num_programs(0)
    tm = h1_ref.shape[0] // 2
    d_model = h1_ref.shape[1]
    buf_rows = TOP_K * tm * SLAB_PITCH
    bufs = [ybuf_ref.at[pl.ds(s * buf_rows, buf_rows), :] for s in range(2)]

    @pl.when(i == 0)
    def _():
        _row_gather_start(dest_a_ref, y_hbm, bufs[0], sem_ref.at[0], TOP_K * tm)

    _row_gather_start(dest_b_ref, y_hbm, bufs[1], sem_ref.at[1], TOP_K * tm)

    def tile(s):
        rows = slice(s * tm, (s + 1) * tm)
        _row_gather_wait(y_hbm, bufs[s], sem_ref.at[s], TOP_K * tm)
        route = route_ref[rows, :]
        y0 = _load_slabs(ybuf_ref, s * buf_rows, tm, SLAB, SLAB_PITCH)
        y1 = _load_slabs(ybuf_ref, s * buf_rows + tm * SLAB_PITCH, tm, SLAB, SLAB_PITCH)
        h2 = h1_ref[rows, :] + (route[:, 2:3] * y0 + route[:, 3:4] * y1)
        xn = _rms(h2, gple_ref[...], d_model).astype(BF16)
        gate = jax.nn.sigmoid(jnp.dot(xn, wpg_ref[...], preferred_element_type=F32) + bpg_ref[...])
        pe = jnp.dot(p_ref[rows, :].astype(BF16), wple_ref[...], preferred_element_type=F32)
        o_ref[rows, :] = h2 + gate * pe

    tile(0)

    @pl.when(i + 1 < n)
    def _():
        _row_gather_start(dest_next_ref, y_hbm, bufs[0], sem_ref.at[0], TOP_K * tm)

    tile(1)


def _ple(dest3, y, h1, route, p2, w_ple, gple, w_pg, b_pg):
    t, d_model = h1.shape
    tm = TM_PLE
    n_steps = t // (2 * tm)
    last = t // tm - 1
    const = lambda i: (0, 0)
    row = lambda i: (i, 0)
    full = lambda a: pl.BlockSpec(a.shape, const)
    dest_spec = lambda index: pl.BlockSpec((1, 1, TOP_K * tm), lambda i: (index(i), 0, 0), memory_space=pltpu.SMEM)
    return pl.pallas_call(
        _ple_kernel,
        grid=(n_steps,),
        in_specs=[dest_spec(lambda i: 2 * i), dest_spec(lambda i: 2 * i + 1), dest_spec(lambda i: jnp.minimum(2 * i + 2, last)),
                  pl.BlockSpec(memory_space=pl.ANY),
                  pl.BlockSpec((2 * tm, d_model), row), pl.BlockSpec((2 * tm, LANES), row),
                  pl.BlockSpec((2 * tm, p2.shape[1]), row),
                  full(w_ple), full(gple), pl.BlockSpec(w_pg.shape, const, pipeline_mode=pl.Buffered(1)), full(b_pg)],
        out_specs=pl.BlockSpec((2 * tm, d_model), row),
        out_shape=jax.ShapeDtypeStruct((t, d_model), F32),
        scratch_shapes=[pltpu.VMEM((2 * TOP_K * tm * SLAB_PITCH, LANES), F32), pltpu.SemaphoreType.DMA((2,))],
        compiler_params=pltpu.CompilerParams(dimension_semantics=("arbitrary",), vmem_limit_bytes=VMEM_LIMIT),
        name="ple",
    )(dest3, dest3, dest3, y, h1, route, p2, w_ple, gple, w_pg, b_pg)


def _dispatch_plan(route):
    t = route.shape[0]
    tm = TM_MOE
    total = TOP_K * t
    n_tiles = total // tm + N_EXPERTS
    expert = route[:, :TOP_K].astype(jnp.int32).reshape(-1)
    pos = jnp.arange(total, dtype=jnp.int32)
    order = jnp.sort(expert * total + pos) % total
    rank = jnp.sort(order * total + pos) % total
    experts = jnp.arange(N_EXPERTS, dtype=jnp.int32)
    onehot = (expert[:, None] == experts[None, :]).astype(jnp.int32)
    counts = jnp.sum(onehot, axis=0)
    group_start = jnp.cumsum(counts) - counts
    tile_end = jnp.cumsum((counts + tm - 1) // tm)
    tile_start = tile_end - (counts + tm - 1) // tm
    n_active = tile_end[-1]
    shift = tile_start * tm - group_start
    dest = rank + jnp.sum(onehot * shift[None, :], axis=1)
    tiles = jnp.minimum(jnp.arange(n_tiles, dtype=jnp.int32), n_active - 1)
    tile_expert = jnp.sum((tile_end[None, :] <= tiles[:, None]).astype(jnp.int32), axis=1)
    j = jnp.arange(tm, dtype=jnp.int32)
    fill_row = (tile_start * tm + counts)[:, None] + j[None, :]
    fill_row = jnp.where(fill_row < (tile_end * tm)[:, None], fill_row, n_tiles * tm)
    fill_tok = (experts[:, None] * tm + j[None, :]) % t
    keys = jnp.concatenate([dest * t + pos // TOP_K, (fill_row * t + fill_tok).reshape(-1)])
    row_token3 = (jnp.sort(keys) % t).reshape(n_tiles, 1, tm)
    dest3 = dest.reshape(t // TM_PLE, TM_PLE, TOP_K).transpose(0, 2, 1).reshape(t // TM_PLE, 1, TOP_K * TM_PLE)
    prev_expert = jnp.concatenate([jnp.full((1,), -1, jnp.int32), tile_expert[:-1]])
    first = (tile_expert != prev_expert).astype(jnp.int32)
    nonempty = counts > 0
    wslot = (jnp.sum((nonempty[None, :] & (experts[None, :] < tile_expert[:, None])).astype(jnp.int32), axis=1)) % 2
    later = nonempty[None, :] & (experts[None, :] > tile_expert[:, None])
    next_expert = jnp.min(jnp.where(later, experts[None, :], N_EXPERTS), axis=1)
    next_expert = jnp.where(next_expert < N_EXPERTS, next_expert, -1)
    plan = (tile_expert, n_active.reshape(1), first, wslot, next_expert, row_token3)
    return plan, dest3


def _rope_swap_pad(w):
    half = QK_ROPE // 2
    x1 = w[..., QK_NOPE:QK_NOPE + half]
    x2 = w[..., QK_NOPE + half:]
    return jnp.concatenate([w[..., :QK_NOPE], x1, x2, x2, x1], axis=-1)


def kernel(x, p, positions, norm_mix, w_in, g_q_lat, g_kv_lat, w_uq, w_ukv, g_q_head, g_k_head, w_conv, g_out_attn, g_out_conv, w_out, norm_moe, w_group, b_group, w_router, b_router, w1, w3, w2, norm_ple, w_ple, w_ple_gate, b_ple_gate):
    batch, seq, d_model = x.shape
    depth = w_in.shape[0]
    t = batch * seq
    q_rank = g_q_lat.shape[1]
    kv_rank = g_kv_lat.shape[1]
    half = QK_ROPE // 2
    rope_lo = q_rank + kv_rank

    pos2 = positions.reshape(t, 1)
    inv_freq = ROPE_BASE ** (-jnp.arange(0, QK_ROPE, 2, dtype=F32) / QK_ROPE)
    freq = jnp.tile(inv_freq, LANES // half).reshape(1, LANES)

    h = x.reshape(t, d_model)
    for i in range(depth):
        w_in_t = jnp.swapaxes(w_in[i], 0, 1).astype(BF16)
        w_rope = jnp.concatenate([w_in_t[rope_lo:rope_lo + QK_ROPE], w_in_t[rope_lo + half:rope_lo + QK_ROPE],
                                  w_in_t[rope_lo:rope_lo + half]], axis=0)
        wuq_p = _rope_swap_pad(w_uq[i]).reshape(q_rank, MLA_HEADS * HEAD_PAD).astype(BF16)
        wukv_p = jnp.concatenate([w_ukv[i][:, :, :QK_NOPE].reshape(kv_rank, -1),
                                  w_ukv[i][:, :, QK_NOPE:].reshape(kv_rank, -1)], axis=1).astype(BF16)
        gq_p = _rope_swap_pad(g_q_head[i]).reshape(1, HEAD_PAD)
        gk_p = _rope_swap_pad(g_k_head[i]).reshape(1, HEAD_PAD)
        as_steps = lambda w, n: w.reshape(n, -1, w.shape[-1])
        nf = t // TM_FRONT
        na = batch * MLA_HEADS * (seq // TQ_ATTN)
        front_sides = [(as_steps(w, nf), 0) for w in (w1[i], w_out[i], w_ple_gate[i])] + [(as_steps(w2[i], 2 * nf), 0)]
        (q, k, v, yc), (w1b, w_out_b, w_pg_b, w2_lo) = _front(
            h, pos2, freq, norm_mix[i].reshape(1, -1), w_in_t, w_rope, g_q_lat[i].reshape(1, -1),
            g_kv_lat[i].reshape(1, -1), wuq_p, wukv_p, gq_p, gk_p, w_conv[i], g_out_conv[i].reshape(1, -1), front_sides, seq=seq)
        ya, (w3b, w2_hi) = _attention(q, k, v, [(as_steps(w3[i], na), 0), (as_steps(w2[i], 2 * na), na)], batch=batch, seq=seq)
        half_e = (N_EXPERTS // 2,) + w2[i].shape[1:]
        w1b, w3b, w2_lo, w2_hi = w1b.reshape(w1[i].shape), w3b.reshape(w3[i].shape), w2_lo.reshape(half_e), w2_hi.reshape(half_e)
        w_out_b, w_pg_b = w_out_b.reshape(w_out[i].shape), w_pg_b.reshape(w_ple_gate[i].shape)

        pad = LANES - N_GROUPS - N_EXPERTS
        w_gr32 = jnp.concatenate([w_group[i], w_router[i], jnp.zeros((d_model, pad), F32)], axis=1)
        w_gr_hi = w_gr32.astype(BF16)
        w_gr = jnp.concatenate([w_gr_hi, (w_gr32 - w_gr_hi.astype(F32)).astype(BF16)], axis=1)
        b_gr = jnp.concatenate([b_group[i], b_router[i], jnp.zeros((pad,), F32)]).reshape(1, LANES)
        h1, xm, route = _outproj(ya, yc, h, g_out_attn[i].reshape(1, -1), w_out_b,
                                 norm_moe[i].reshape(1, -1), w_gr, b_gr)

        plan, dest3 = _dispatch_plan(route)
        y = _moe(plan, xm, w1b, w3b, w2_lo, w2_hi)
        h = _ple(dest3, y, h1, route, p[i].reshape(t, -1), w_ple[i].astype(BF16), norm_ple[i].reshape(1, -1),
                 w_pg_b, b_ple_gate[i].reshape(1, -1))
    return h.reshape(batch, seq, d_model)
```

```python
import functools

import jax
import jax.numpy as jnp
from jax import lax
from jax.experimental import pallas as pl
from jax.experimental.pallas import tpu as pltpu

F32 = jnp.float32
BF16 = jnp.bfloat16

CHUNK = 64
EPS = 1e-6
MASK_VALUE = -1e30
ROPE_BASE = 10000.0

MLA_HEADS = 8
QK_NOPE = 128
QK_ROPE = 64
QK_HEAD = QK_NOPE + QK_ROPE
V_HEAD = 128
HEAD_PAD = 256
Q_SCALE = QK_HEAD ** -0.5 * 1.4426950408889634
N_GROUPS = 4
EXPERTS_PER_GROUP = 8
N_EXPERTS = N_GROUPS * EXPERTS_PER_GROUP
TOP_K = 2
LANES = 128
SLAB = 16
SLAB_PITCH = 20
DMA_PRIORITIES = 2

TM_FRONT = 256
TQ_ATTN = 1024
TM_OUT = 512
TM_MOE = 256
TM_PLE = 256
VMEM_LIMIT = 56 * 1024 * 1024


def _rms(x, g, n):
    ms = jnp.sum(x * x, axis=-1, keepdims=True) * (1.0 / n)
    return x * lax.rsqrt(ms + EPS) * g


def _store_slabs(ref, val, pitch):
    n, width = val.shape
    for c in range(width // LANES):
        ref[pl.ds(c, n, stride=pitch), :] = val[:, c * LANES:(c + 1) * LANES]


def _load_slabs(ref, start, n, slab, pitch):
    return jnp.concatenate([ref[pl.ds(start + c, n, stride=pitch), :] for c in range(slab)], axis=1)


def _front_kernel(x_ref, pos_ref, freq_ref, gmix_ref, win_ref, wrope_ref, gql_ref, gkvl_ref,
                  wuq_ref, wukv_ref, gq_ref, gk_ref, wconv_ref, gconv_ref, *rest,
                  n_side, tiles_per_seq, q_rank, kv_rank, d_conv):
    side_refs = rest[:n_side]
    q_ref, k_ref, v_ref, yc_ref = rest[n_side:n_side + 4]
    side_out_refs = rest[n_side + 4:2 * n_side + 4]
    prev_ref = rest[-1]
    i = pl.program_id(0)
    is_start = (i % tiles_per_seq) == 0
    tm = x_ref.shape[0]
    d_model = x_ref.shape[1]
    rope_lo = q_rank + kv_rank
    nt = (((1,), (1,)), ((), ()))

    @pl.when(is_start)
    def _():
        prev_ref[...] = jnp.zeros_like(prev_ref)

    xn = _rms(x_ref[...], gmix_ref[...], d_model).astype(BF16)
    lat = lax.dot_general(xn, win_ref[:rope_lo, :], nt, preferred_element_type=F32)
    q_lat = lat[:, :q_rank]
    kv_lat = lat[:, q_rank:]
    kr = lax.dot_general(xn, wrope_ref[...], nt, preferred_element_type=F32)

    qn = _rms(q_lat, gql_ref[...], q_rank).astype(BF16)
    kvn = _rms(kv_lat, gkvl_ref[...], kv_rank).astype(BF16)
    q_all = jnp.dot(qn, wuq_ref[...], preferred_element_type=F32)
    kv_all = jnp.dot(kvn, wukv_ref[...], preferred_element_type=F32)

    lane = lax.broadcasted_iota(jnp.int32, (tm, LANES), 1)
    ang = pos_ref[...].astype(F32) * freq_ref[...]
    cos = jnp.cos(ang)
    sin = jnp.sin(ang)
    rope_c = jnp.where(lane < QK_ROPE, cos, 0.0)
    rope_s = jnp.where(lane < QK_ROPE // 2, -sin, jnp.where(lane < QK_ROPE, sin, 0.0))
    rope_lanes = lane < QK_ROPE

    def rope(t):
        return t * rope_c + pltpu.roll(t, QK_ROPE, 1) * rope_s

    gq = gq_ref[...]
    gk = gk_ref[...]
    inv_n = 1.0 / QK_HEAD
    kr_ssq = jnp.sum(jnp.where(rope_lanes, kr * kr, 0.0), axis=-1, keepdims=True)
    kr_rot = rope(kr * gk[:, QK_NOPE:])
    for h in range(MLA_HEADS):
        qa = q_all[:, h * HEAD_PAD:h * HEAD_PAD + QK_NOPE]
        qb = q_all[:, h * HEAD_PAD + QK_NOPE:(h + 1) * HEAD_PAD]
        ssq = (jnp.sum(qa * qa, axis=-1, keepdims=True)
               + jnp.sum(jnp.where(rope_lanes, qb * qb, 0.0), axis=-1, keepdims=True))
        inv = lax.rsqrt(ssq * inv_n + EPS)
        q_ref[:, h * HEAD_PAD:h * HEAD_PAD + QK_NOPE] = (qa * inv * gq[:, :QK_NOPE] * Q_SCALE).astype(BF16)
        q_ref[:, h * HEAD_PAD + QK_NOPE:(h + 1) * HEAD_PAD] = (rope(qb * inv * gq[:, QK_NOPE:]) * Q_SCALE).astype(BF16)

        ka = kv_all[:, h * QK_NOPE:(h + 1) * QK_NOPE]
        kinv = lax.rsqrt((jnp.sum(ka * ka, axis=-1, keepdims=True) + kr_ssq) * inv_n + EPS)
        k_ref[:, h * HEAD_PAD:h * HEAD_PAD + QK_NOPE] = (ka * kinv * gk[:, :QK_NOPE]).astype(BF16)
        k_ref[:, h * HEAD_PAD + QK_NOPE:(h + 1) * HEAD_PAD] = (kr_rot * kinv).astype(BF16)
    v_ref[...] = kv_all[:, MLA_HEADS * QK_NOPE:].astype(BF16)

    conv_lo = rope_lo + QK_ROPE
    half_c = d_conv // 2
    row = lax.broadcasted_iota(jnp.int32, (tm, half_c), 0)
    wc = wconv_ref[...]
    ys = []
    ssq = jnp.zeros((tm, 1), F32)
    for cs in (slice(0, half_c), slice(half_c, d_conv)):
        b_gate, c_gate, h_conv = (lax.dot_general(xn, win_ref[conv_lo + k * d_conv + cs.start:conv_lo + k * d_conv + cs.stop, :],
                                                  nt, preferred_element_type=F32) for k in range(3))
        u = c_gate * h_conv
        prev = prev_ref[:, cs]
        hm1 = prev[7:8, :]
        hm2 = prev[6:7, :]
        u1 = jnp.where(row == 0, hm1, pltpu.roll(u, 1, 0))
        u2 = jnp.where(row == 0, hm2, jnp.where(row == 1, hm1, pltpu.roll(u, 2, 0)))
        prev_ref[:, cs] = u[tm - 8:, :]
        y = b_gate * (u2 * wc[0:1, cs] + u1 * wc[1:2, cs] + u * wc[2:3, cs])
        ssq = ssq + jnp.sum(y * y, axis=-1, keepdims=True)
        ys.append(y)
    inv = lax.rsqrt(ssq * (1.0 / d_conv) + EPS)
    yc_ref[...] = (jnp.concatenate(ys, axis=1) * inv * gconv_ref[...]).astype(BF16)
    for w_ref, wb_ref in zip(side_refs, side_out_refs):
        wb_ref[...] = w_ref[...].astype(BF16)


def _front(x2, pos2, freq, gmix, w_in_t, w_rope, gql, gkvl, wuq_p, wukv_p, gq_p, gk_p, wconv, gconv, sides, *, seq):
    t, d_model = x2.shape
    tm = TM_FRONT
    q_rank = gql.shape[1]
    kv_rank = gkvl.shape[1]
    d_conv = gconv.shape[1]
    const = lambda i: (0, 0)
    row = lambda i: (i, 0)
    full = lambda a: pl.BlockSpec(a.shape, const)
    side_in = [pl.BlockSpec((1,) + w.shape[1:], lambda i, off=off: (i + off, 0, 0)) for w, off in sides]
    side_out = [pl.BlockSpec((1,) + w.shape[1:], lambda i: (i, 0, 0)) for w, _ in sides]
    kern = functools.partial(_front_kernel, n_side=len(sides), tiles_per_seq=seq // tm, q_rank=q_rank, kv_rank=kv_rank,
                             d_conv=d_conv)
    outs = pl.pallas_call(
        kern,
        grid=(t // tm,),
        in_specs=[pl.BlockSpec((tm, d_model), row), pl.BlockSpec((tm, 1), row), full(freq), full(gmix),
                  pl.BlockSpec(w_in_t.shape, const, pipeline_mode=pl.Buffered(1)), full(w_rope),
                  full(gql), full(gkvl), full(wuq_p), full(wukv_p), full(gq_p), full(gk_p), full(wconv), full(gconv)] + side_in,
        out_specs=[pl.BlockSpec((tm, MLA_HEADS * HEAD_PAD), row), pl.BlockSpec((tm, MLA_HEADS * HEAD_PAD), row),
                   pl.BlockSpec((tm, MLA_HEADS * V_HEAD), row), pl.BlockSpec((tm, d_conv), row)] + side_out,
        out_shape=[jax.ShapeDtypeStruct((t, MLA_HEADS * HEAD_PAD), BF16), jax.ShapeDtypeStruct((t, MLA_HEADS * HEAD_PAD), BF16),
                   jax.ShapeDtypeStruct((t, MLA_HEADS * V_HEAD), BF16), jax.ShapeDtypeStruct((t, d_conv), BF16)]
                  + [jax.ShapeDtypeStruct((t // tm,) + w.shape[1:], BF16) for w, _ in sides],
        scratch_shapes=[pltpu.VMEM((8, d_conv), F32)],
        compiler_params=pltpu.CompilerParams(dimension_semantics=("arbitrary",), vmem_limit_bytes=VMEM_LIMIT),
        name="front",
    )(x2, pos2, freq, gmix, w_in_t, w_rope, gql, gkvl, wuq_p, wukv_p, gq_p, gk_p, wconv, gconv, *[w for w, _ in sides])
    return outs[:4], outs[4:]


def _attn_kernel(q_ref, k_ref, v_ref, *rest):
    n_side = (len(rest) - 1) // 2
    o_ref = rest[n_side]

    def round_side_weights():
        for w_ref, wb_ref in zip(rest[:n_side], rest[n_side + 1:]):
            wb_ref[...] = w_ref[...].astype(BF16)

    i = pl.program_id(2)
    tq = q_ref.shape[0]
    half = tq // 2
    q_halves = (q_ref[:half, :], q_ref[half:, :])

    def step(start, n_keys, masks, carry):
        kvs = [(k_ref[pl.ds(start, n), :],
                jnp.concatenate([v_ref[pl.ds(start, n), :], jnp.ones((n, HEAD_PAD - V_HEAD), BF16)], axis=1)) for n in n_keys]
        scores = [lax.dot_general(qh, kb, (((1,), (1,)), ((), ())), preferred_element_type=F32)
                  for qh, (kb, _) in zip(q_halves, kvs)]
        scores = [s if mask is None else jnp.where(mask, s, MASK_VALUE) for s, mask in zip(scores, masks)]
        m_new = [jnp.maximum(m, jnp.max(s, axis=-1, keepdims=True)) for s, (m, _) in zip(scores, carry)]
        probs = [jnp.exp2(s - mn).astype(BF16) for s, mn in zip(scores, m_new)]
        return tuple((mn, jnp.exp2(m - mn) * acc + jnp.dot(p, vb, preferred_element_type=F32))
                     for p, mn, (m, acc), (_, vb) in zip(probs, m_new, carry, kvs))

    def chunk_mask(n, first_row):
        r = (lax.broadcasted_iota(jnp.int32, (half, n), 0) + first_row) // CHUNK
        c = lax.broadcasted_iota(jnp.int32, (half, n), 1) // CHUNK
        return c <= r

    for qi in range(k_ref.shape[0] // tq):
        @pl.when(i == qi)
        def _(qi=qi):
            carry = tuple((jnp.full((half, 1), MASK_VALUE, F32), jnp.zeros((half, HEAD_PAD), F32)) for _ in range(2))
            for j in range(qi):
                carry = step(j * tq, (tq, tq), (None, None), carry)
            (_, acc0), (_, acc1) = step(qi * tq, (half, tq), (chunk_mask(half, 0), chunk_mask(tq, half)), carry)
            o_ref[:half, :] = (acc0[:, :V_HEAD] / acc0[:, V_HEAD:]).astype(o_ref.dtype)
            o_ref[half:, :] = (acc1[:, :V_HEAD] / acc1[:, V_HEAD:]).astype(o_ref.dtype)
            round_side_weights()


def _attention(q, k, v, side_weights, *, batch, seq):
    tq = TQ_ATTN
    nq = seq // tq
    n_steps = batch * MLA_HEADS * nq
    step = lambda b, h, i, off=0: ((b * MLA_HEADS + h) * nq + i + off, 0, 0)
    side_in = [pl.BlockSpec((1,) + w.shape[1:], functools.partial(step, off=off)) for w, off in side_weights]
    side_out = [pl.BlockSpec((1,) + w.shape[1:], step) for w, _ in side_weights]
    outs = pl.pallas_call(
        _attn_kernel,
        grid=(batch, MLA_HEADS, nq),
        in_specs=[pl.BlockSpec((tq, HEAD_PAD), lambda b, h, i: (b * nq + i, h)),
                  pl.BlockSpec((seq, HEAD_PAD), lambda b, h, i: (b, h)),
                  pl.BlockSpec((seq, V_HEAD), lambda b, h, i: (b, h))] + side_in,
        out_specs=[pl.BlockSpec((tq, V_HEAD), lambda b, h, i: (b * nq + i, h))] + side_out,
        out_shape=[jax.ShapeDtypeStruct((batch * seq, MLA_HEADS * V_HEAD), BF16)]
                  + [jax.ShapeDtypeStruct((n_steps,) + w.shape[1:], BF16) for w, _ in side_weights],
        compiler_params=pltpu.CompilerParams(dimension_semantics=("arbitrary", "arbitrary", "arbitrary"),
                                             vmem_limit_bytes=VMEM_LIMIT),
        name="attn",
    )(q, k, v, *[w for w, _ in side_weights])
    return outs[0], outs[1:]


def _outproj_kernel(ya_ref, yc_ref, x_ref, gattn_ref, wout_ref, gmoe_ref, wgr_ref, bgr_ref,
                    h1_ref, xm_ref, route_ref):
    tm, d_attn = ya_ref.shape
    d_model = x_ref.shape[1]
    ya = _rms(ya_ref[...].astype(F32), gattn_ref[...], d_attn).astype(BF16)
    h1 = (x_ref[...]
          + jnp.dot(ya, wout_ref[:d_attn, :], preferred_element_type=F32)
          + jnp.dot(yc_ref[...], wout_ref[d_attn:, :], preferred_element_type=F32))
    h1_ref[...] = h1
    xm = _rms(h1, gmoe_ref[...], d_model)
    _store_slabs(xm_ref, xm, SLAB_PITCH)
    for c in range(SLAB, SLAB_PITCH):
        xm_ref[pl.ds(c, tm, stride=SLAB_PITCH), :] = jnp.zeros((tm, LANES), F32)

    x_hi = xm.astype(BF16)
    x_lo = (xm - x_hi.astype(F32)).astype(BF16)
    hi_part = jnp.dot(x_hi, wgr_ref[...], preferred_element_type=F32)
    lo_part = jnp.dot(x_lo, wgr_ref[:, :LANES], preferred_element_type=F32)
    logits = hi_part[:, :LANES] + hi_part[:, LANES:] + lo_part + bgr_ref[...]
    lane = lax.broadcasted_iota(jnp.int32, (tm, LANES), 1).astype(F32)
    neg = -jnp.inf
    big = 1e9
    gl = jnp.where(lane < N_GROUPS, logits, neg)
    gmax = jnp.max(gl, axis=-1, keepdims=True)
    gsel = jnp.min(jnp.where(gl == gmax, lane, big), axis=-1, keepdims=True)
    p_g = 1.0 / jnp.sum(jnp.exp(gl - gmax), axis=-1, keepdims=True)
    lo = N_GROUPS + EXPERTS_PER_GROUP * gsel
    el = jnp.where((lane >= lo) & (lane < lo + EXPERTS_PER_GROUP), logits, neg)
    v1 = jnp.max(el, axis=-1, keepdims=True)
    i1 = jnp.min(jnp.where(el == v1, lane, big), axis=-1, keepdims=True)
    el2 = jnp.where(lane == i1, neg, el)
    v2 = jnp.max(el2, axis=-1, keepdims=True)
    i2 = jnp.min(jnp.where(el2 == v2, lane, big), axis=-1, keepdims=True)
    e2 = jnp.exp(v2 - v1)
    den = 1.0 + e2
    w1 = p_g / den
    w2 = p_g * e2 / den
    route_ref[...] = jnp.where(lane == 0, i1 - N_GROUPS,
                               jnp.where(lane == 1, i2 - N_GROUPS,
                                         jnp.where(lane == 2, w1, jnp.where(lane == 3, w2, 0.0))))


def _outproj(ya, yc, x2, gattn, w_out, gmoe, w_gr, b_gr):
    t, d_model = x2.shape
    d_attn = ya.shape[1]
    tm = TM_OUT
    const = lambda i: (0, 0)
    row = lambda i: (i, 0)
    full = lambda a: pl.BlockSpec(a.shape, const)
    return pl.pallas_call(
        _outproj_kernel,
        grid=(t // tm,),
        in_specs=[pl.BlockSpec((tm, d_attn), row), pl.BlockSpec((tm, yc.shape[1]), row), pl.BlockSpec((tm, d_model), row),
                  full(gattn), pl.BlockSpec(w_out.shape, const, pipeline_mode=pl.Buffered(1)), full(gmoe), full(w_gr), full(b_gr)],
        out_specs=[pl.BlockSpec((tm, d_model), row), pl.BlockSpec((tm * SLAB_PITCH, LANES), row), pl.BlockSpec((tm, LANES), row)],
        out_shape=[jax.ShapeDtypeStruct((t, d_model), F32), jax.ShapeDtypeStruct((t * SLAB_PITCH, LANES), F32),
                   jax.ShapeDtypeStruct((t, LANES), F32)],
        compiler_params=pltpu.CompilerParams(dimension_semantics=("arbitrary",), vmem_limit_bytes=VMEM_LIMIT),
        name="outproj",
    )(ya, yc, x2, gattn, w_out, gmoe, w_gr, b_gr)


def _row_gather_start(idx_ref, src_hbm, dst_ref, sem, n_rows, src_pitch=SLAB):
    def body(pair, carry):
        for prio in range(DMA_PRIORITIES):
            r = pair * DMA_PRIORITIES + prio
            src = src_hbm.at[pl.ds(idx_ref[0, 0, r] * src_pitch, SLAB), :]
            pltpu.make_async_copy(src, dst_ref.at[pl.ds(r * SLAB_PITCH, SLAB), :], sem).start(priority=prio)
        return carry
    lax.fori_loop(0, n_rows // DMA_PRIORITIES, body, 0, unroll=True)


def _row_gather_wait(src_hbm, dst_ref, sem, n_rows):
    pltpu.make_async_copy(src_hbm.at[pl.ds(0, n_rows * SLAB), :], dst_ref.at[pl.ds(0, n_rows * SLAB), :], sem).wait()


def _moe_kernel(tile_expert_ref, n_active_ref, first_ref, wslot_ref, next_expert_ref, tok_cur_ref, tok_next_ref,
                xm_hbm, w1_hbm, w3_hbm, w2lo_hbm, w2hi_hbm, y_ref, xbuf_ref, w1_buf, w3_buf, w2_buf, sem_ref, wsem_ref, *, tm):
    i = pl.program_id(0)
    n_active = n_active_ref[0]
    slot = i % 2
    buf_rows = tm * SLAB_PITCH
    wslot = wslot_ref[i]

    def buf(s):
        return xbuf_ref.at[pl.ds(s * buf_rows, buf_rows), :]

    n_lo = w2lo_hbm.shape[0]

    def start_weights(e, s):
        pltpu.make_async_copy(w1_hbm.at[e], w1_buf.at[s], wsem_ref.at[0, s]).start()
        pltpu.make_async_copy(w3_hbm.at[e], w3_buf.at[s], wsem_ref.at[1, s]).start()

        @pl.when(e < n_lo)
        def _():
            pltpu.make_async_copy(w2lo_hbm.at[e], w2_buf.at[s], wsem_ref.at[2, s]).start()

        @pl.when(e >= n_lo)
        def _():
            pltpu.make_async_copy(w2hi_hbm.at[e - n_lo], w2_buf.at[s], wsem_ref.at[2, s]).start()

    def wait_weights(s):
        pltpu.make_async_copy(w1_hbm.at[0], w1_buf.at[s], wsem_ref.at[0, s]).wait()
        pltpu.make_async_copy(w3_hbm.at[0], w3_buf.at[s], wsem_ref.at[1, s]).wait()
        pltpu.make_async_copy(w2lo_hbm.at[0], w2_buf.at[s], wsem_ref.at[2, s]).wait()

    @pl.when(i == 0)
    def _():
        start_weights(tile_expert_ref[0], 0)
        _row_gather_start(tok_cur_ref, xm_hbm, buf(0), sem_ref.at[0], tm, src_pitch=SLAB_PITCH)

    active = i < n_active
    new_expert = active & (first_ref[i] == 1)

    @pl.when(new_expert & (next_expert_ref[i] >= 0))
    def _():
        start_weights(next_expert_ref[i], 1 - wslot)

    @pl.when(i + 1 < n_active)
    def _():
        _row_gather_start(tok_next_ref, xm_hbm, buf(1 - slot), sem_ref.at[1 - slot], tm, src_pitch=SLAB_PITCH)

    @pl.when(new_expert)
    def _():
        wait_weights(wslot)

    @pl.when(active)
    def _():
        _row_gather_wait(xm_hbm, buf(slot), sem_ref.at[slot], tm)
        x = _load_slabs(xbuf_ref, slot * buf_rows, tm, SLAB, SLAB_PITCH).astype(BF16)
        a = jnp.dot(x, w1_buf[wslot], preferred_element_type=F32)
        g = jnp.dot(x, w3_buf[wslot], preferred_element_type=F32)
        hid = (a * jax.nn.sigmoid(a) * g).astype(BF16)
        _store_slabs(y_ref, jnp.dot(hid, w2_buf[wslot], preferred_element_type=F32), SLAB)

    @pl.when(i >= n_active)
    def _():
        y_ref[...] = jnp.zeros_like(y_ref)


def _moe(plan, xm, w1, w3, w2_lo, w2_hi):
    tile_expert, n_active, first, wslot, next_expert, row_token3 = plan
    n_tiles = row_token3.shape[0]
    tm = TM_MOE
    _, d_model, d_exp = w1.shape
    last = n_tiles - 1
    grid_spec = pltpu.PrefetchScalarGridSpec(
        num_scalar_prefetch=5,
        grid=(n_tiles,),
        in_specs=[pl.BlockSpec((1, 1, tm), lambda i, *_: (i, 0, 0), memory_space=pltpu.SMEM),
                  pl.BlockSpec((1, 1, tm), lambda i, *_: (jnp.minimum(i + 1, last), 0, 0), memory_space=pltpu.SMEM),
                  pl.BlockSpec(memory_space=pl.ANY), pl.BlockSpec(memory_space=pl.ANY), pl.BlockSpec(memory_space=pl.ANY),
                  pl.BlockSpec(memory_space=pl.ANY), pl.BlockSpec(memory_space=pl.ANY)],
        out_specs=pl.BlockSpec((tm * SLAB, LANES), lambda i, *_: (i, 0)),
        scratch_shapes=[pltpu.VMEM((2 * tm * SLAB_PITCH, LANES), F32),
                        pltpu.VMEM((2, d_model, d_exp), BF16), pltpu.VMEM((2, d_model, d_exp), BF16),
                        pltpu.VMEM((2, d_exp, d_model), BF16),
                        pltpu.SemaphoreType.DMA((2,)), pltpu.SemaphoreType.DMA((3, 2))],
    )
    return pl.pallas_call(
        functools.partial(_moe_kernel, tm=tm),
        grid_spec=grid_spec,
        out_shape=jax.ShapeDtypeStruct((n_tiles * tm * SLAB, LANES), F32),
        compiler_params=pltpu.CompilerParams(dimension_semantics=("arbitrary",), vmem_limit_bytes=VMEM_LIMIT),
        name="moe",
    )(tile_expert, n_active, first, wslot, next_expert, row_token3, row_token3, xm, w1, w3, w2_lo, w2_hi)


def _ple_kernel(dest_a_ref, dest_b_ref, dest_next_ref, y_hbm, h1_ref, route_ref, p_ref, wple_ref, gple_ref, wpg_ref,
                bpg_ref, o_ref, ybuf_ref, sem_ref):
    i = pl.program_id(0)
    n = pl.num_programs(0)
    tm = h1_ref.shape[0] // 2
    d_model = h1_ref.shape[1]
    buf_rows = TOP_K * tm * SLAB_PITCH
    bufs = [ybuf_ref.at[pl.ds(s * buf_rows, buf_rows), :] for s in range(2)]

    @pl.when(i == 0)
    def _():
        _row_gather_start(dest_a_ref, y_hbm, bufs[0], sem_ref.at[0], TOP_K * tm)

    _row_gather_start(dest_b_ref, y_hbm, bufs[1], sem_ref.at[1], TOP_K * tm)

    def tile(s):
        rows = slice(s * tm, (s + 1) * tm)
        _row_gather_wait(y_hbm, bufs[s], sem_ref.at[s], TOP_K * tm)
        route = route_ref[rows, :]
        y0 = _load_slabs(ybuf_ref, s * buf_rows, tm, SLAB, SLAB_PITCH)
        y1 = _load_slabs(ybuf_ref, s * buf_rows + tm * SLAB_PITCH, tm, SLAB, SLAB_PITCH)
        h2 = h1_ref[rows, :] + (route[:, 2:3] * y0 + route[:, 3:4] * y1)
        xn = _rms(h2, gple_ref[...], d_model).astype(BF16)
        gate = jax.nn.sigmoid(jnp.dot(xn, wpg_ref[...], preferred_element_type=F32) + bpg_ref[...])
        pe = jnp.dot(p_ref[rows, :].astype(BF16), wple_ref[...], preferred_element_type=F32)
        o_ref[rows, :] = h2 + gate * pe

    tile(0)

    @pl.when(i + 1 < n)
    def _():
        _row_gather_start(dest_next_ref, y_hbm, bufs[0], sem_ref.at[0], TOP_K * tm)

    tile(1)


def _ple(dest3, y, h1, route, p2, w_ple, gple, w_pg, b_pg):
    t, d_model = h1.shape
    tm = TM_PLE
    n_steps = t // (2 * tm)
    last = t // tm - 1
    const = lambda i: (0, 0)
    row = lambda i: (i, 0)
    full = lambda a: pl.BlockSpec(a.shape, const)
    dest_spec = lambda index: pl.BlockSpec((1, 1, TOP_K * tm), lambda i: (index(i), 0, 0), memory_space=pltpu.SMEM)
    return pl.pallas_call(
        _ple_kernel,
        grid=(n_steps,),
        in_specs=[dest_spec(lambda i: 2 * i), dest_spec(lambda i: 2 * i + 1), dest_spec(lambda i: jnp.minimum(2 * i + 2, last)),
                  pl.BlockSpec(memory_space=pl.ANY),
                  pl.BlockSpec((2 * tm, d_model), row), pl.BlockSpec((2 * tm, LANES), row),
                  pl.BlockSpec((2 * tm, p2.shape[1]), row),
                  full(w_ple), full(gple), pl.BlockSpec(w_pg.shape, const, pipeline_mode=pl.Buffered(1)), full(b_pg)],
        out_specs=pl.BlockSpec((2 * tm, d_model), row),
        out_shape=jax.ShapeDtypeStruct((t, d_model), F32),
        scratch_shapes=[pltpu.VMEM((2 * TOP_K * tm * SLAB_PITCH, LANES), F32), pltpu.SemaphoreType.DMA((2,))],
        compiler_params=pltpu.CompilerParams(dimension_semantics=("arbitrary",), vmem_limit_bytes=VMEM_LIMIT),
        name="ple",
    )(dest3, dest3, dest3, y, h1, route, p2, w_ple, gple, w_pg, b_pg)


def _dispatch_plan(route):
    t = route.shape[0]
    tm = TM_MOE
    total = TOP_K * t
    n_tiles = total // tm + N_EXPERTS
    expert = route[:, :TOP_K].astype(jnp.int32).reshape(-1)
    pos = jnp.arange(total, dtype=jnp.int32)
    order = jnp.sort(expert * total + pos) % total
    rank = jnp.sort(order * total + pos) % total
    experts = jnp.arange(N_EXPERTS, dtype=jnp.int32)
    onehot = (expert[:, None] == experts[None, :]).astype(jnp.int32)
    counts = jnp.sum(onehot, axis=0)
    group_start = jnp.cumsum(counts) - counts
    tile_end = jnp.cumsum((counts + tm - 1) // tm)
    tile_start = tile_end - (counts + tm - 1) // tm
    n_active = tile_end[-1]
    shift = tile_start * tm - group_start
    dest = rank + jnp.sum(onehot * shift[None, :], axis=1)
    tiles = jnp.minimum(jnp.arange(n_tiles, dtype=jnp.int32), n_active - 1)
    tile_expert = jnp.sum((tile_end[None, :] <= tiles[:, None]).astype(jnp.int32), axis=1)
    j = jnp.arange(tm, dtype=jnp.int32)
    fill_row = (tile_start * tm + counts)[:, None] + j[None, :]
    fill_row = jnp.where(fill_row < (tile_end * tm)[:, None], fill_row, n_tiles * tm)
    fill_tok = (experts[:, None] * tm + j[None, :]) % t
    keys = jnp.concatenate([dest * t + pos // TOP_K, (fill_row * t + fill_tok).reshape(-1)])
    row_token3 = (jnp.sort(keys) % t).reshape(n_tiles, 1, tm)
    dest3 = dest.reshape(t // TM_PLE, TM_PLE, TOP_K).transpose(0, 2, 1).reshape(t // TM_PLE, 1, TOP_K * TM_PLE)
    prev_expert = jnp.concatenate([jnp.full((1,), -1, jnp.int32), tile_expert[:-1]])
    first = (tile_expert != prev_expert).astype(jnp.int32)
    nonempty = counts > 0
    wslot = (jnp.sum((nonempty[None, :] & (experts[None, :] < tile_expert[:, None])).astype(jnp.int32), axis=1)) % 2
    later = nonempty[None, :] & (experts[None, :] > tile_expert[:, None])
    next_expert = jnp.min(jnp.where(later, experts[None, :], N_EXPERTS), axis=1)
    next_expert = jnp.where(next_expert < N_EXPERTS, next_expert, -1)
    plan = (tile_expert, n_active.reshape(1), first, wslot, next_expert, row_token3)
    return plan, dest3


def _rope_swap_pad(w):
    half = QK_ROPE // 2
    x1 = w[..., QK_NOPE:QK_NOPE + half]
    x2 = w[..., QK_NOPE + half:]
    return jnp.concatenate([w[..., :QK_NOPE], x1, x2, x2, x1], axis=-1)


def kernel(x, p, positions, norm_mix, w_in, g_q_lat, g_kv_lat, w_uq, w_ukv, g_q_head, g_k_head, w_conv, g_out_attn, g_out_conv, w_out, norm_moe, w_group, b_group, w_router, b_router, w1, w3, w2, norm_ple, w_ple, w_ple_gate, b_ple_gate):
    batch, seq, d_model = x.shape
    depth = w_in.shape[0]
    t = batch * seq
    q_rank = g_q_lat.shape[1]
    kv_rank = g_kv_lat.shape[1]
    half = QK_ROPE // 2
    rope_lo = q_rank + kv_rank

    pos2 = positions.reshape(t, 1)
    inv_freq = ROPE_BASE ** (-jnp.arange(0, QK_ROPE, 2, dtype=F32) / QK_ROPE)
    freq = jnp.tile(inv_freq, LANES // half).reshape(1, LANES)

    h = x.reshape(t, d_model)
    for i in range(depth):
        w_in_t = jnp.swapaxes(w_in[i], 0, 1).astype(BF16)
        w_rope = jnp.concatenate([w_in_t[rope_lo:rope_lo + QK_ROPE], w_in_t[rope_lo + half:rope_lo + QK_ROPE],
                                  w_in_t[rope_lo:rope_lo + half]], axis=0)
        wuq_p = _rope_swap_pad(w_uq[i]).reshape(q_rank, MLA_HEADS * HEAD_PAD).astype(BF16)
        wukv_p = jnp.concatenate([w_ukv[i][:, :, :QK_NOPE].reshape(kv_rank, -1),
                                  w_ukv[i][:, :, QK_NOPE:].reshape(kv_rank, -1)], axis=1).astype(BF16)
        gq_p = _rope_swap_pad(g_q_head[i]).reshape(1, HEAD_PAD)
        gk_p = _rope_swap_pad(g_k_head[i]).reshape(1, HEAD_PAD)
        as_steps = lambda w, n: w.reshape(n, -1, w.shape[-1])
        nf = t // TM_FRONT
        na = batch * MLA_HEADS * (seq // TQ_ATTN)
        front_sides = [(as_steps(w, nf), 0) for w in (w1[i], w_out[i], w_ple_gate[i])] + [(as_steps(w2[i], 2 * nf), 0)]
        (q, k, v, yc), (w1b, w_out_b, w_pg_b, w2_lo) = _front(
            h, pos2, freq, norm_mix[i].reshape(1, -1), w_in_t, w_rope, g_q_lat[i].reshape(1, -1),
            g_kv_lat[i].reshape(1, -1), wuq_p, wukv_p, gq_p, gk_p, w_conv[i], g_out_conv[i].reshape(1, -1), front_sides, seq=seq)
        ya, (w3b, w2_hi) = _attention(q, k, v, [(as_steps(w3[i], na), 0), (as_steps(w2[i], 2 * na), na)], batch=batch, seq=seq)
        half_e = (N_EXPERTS // 2,) + w2[i].shape[1:]
        w1b, w3b, w2_lo, w2_hi = w1b.reshape(w1[i].shape), w3b.reshape(w3[i].shape), w2_lo.reshape(half_e), w2_hi.reshape(half_e)
        w_out_b, w_pg_b = w_out_b.reshape(w_out[i].shape), w_pg_b.reshape(w_ple_gate[i].shape)

        pad = LANES - N_GROUPS - N_EXPERTS
        w_gr32 = jnp.concatenate([w_group[i], w_router[i], jnp.zeros((d_model, pad), F32)], axis=1)
        w_gr_hi = w_gr32.astype(BF16)
        w_gr = jnp.concatenate([w_gr_hi, (w_gr32 - w_gr_hi.astype(F32)).astype(BF16)], axis=1)
        b_gr = jnp.concatenate([b_group[i], b_router[i], jnp.zeros((pad,), F32)]).reshape(1, LANES)
        h1, xm, route = _outproj(ya, yc, h, g_out_attn[i].reshape(1, -1), w_out_b,
                                 norm_moe[i].reshape(1, -1), w_gr, b_gr)

        plan, dest3 = _dispatch_plan(route)
        y = _moe(plan, xm, w1b, w3b, w2_lo, w2_hi)
        h = _ple(dest3, y, h1, route, p[i].reshape(t, -1), w_ple[i].astype(BF16), norm_ple[i].reshape(1, -1),
                 w_pg_b, b_ple_gate[i].reshape(1, -1))
    return h.reshape(batch, seq, d_model)
```

```python
import functools

import jax
import jax.numpy as jnp
from jax import lax
from jax.experimental import pallas as pl
from jax.experimental.pallas import tpu as pltpu

F32 = jnp.float32
BF16 = jnp.bfloat16

CHUNK = 64
EPS = 1e-6
MASK_VALUE = -1e30
ROPE_BASE = 10000.0

MLA_HEADS = 8
QK_NOPE = 128
QK_ROPE = 64
QK_HEAD = QK_NOPE + QK_ROPE
V_HEAD = 128
HEAD_PAD = 256
Q_SCALE = QK_HEAD ** -0.5 * 1.4426950408889634
N_GROUPS = 4
EXPERTS_PER_GROUP = 8
N_EXPERTS = N_GROUPS * EXPERTS_PER_GROUP
TOP_K = 2
LANES = 128
SLAB = 16
SLAB_PITCH = 20

TM_FRONT = 256
TQ_ATTN = 1024
TM_OUT = 512
TM_MOE = 256
TM_PLE = 256
VMEM_LIMIT = 56 * 1024 * 1024


def _rms(x, g, n):
    ms = jnp.sum(x * x, axis=-1, keepdims=True) * (1.0 / n)
    return x * lax.rsqrt(ms + EPS) * g


def _store_slabs(ref, val, pitch):
    n, width = val.shape
    for c in range(width // LANES):
        ref[pl.ds(c, n, stride=pitch), :] = val[:, c * LANES:(c + 1) * LANES]


def _load_slabs(ref, start, n, slab, pitch):
    return jnp.concatenate([ref[pl.ds(start + c, n, stride=pitch), :] for c in range(slab)], axis=1)


def _front_kernel(x_ref, pos_ref, freq_ref, gmix_ref, win_ref, wrope_ref, gql_ref, gkvl_ref,
                  wuq_ref, wukv_ref, gq_ref, gk_ref, wconv_ref, gconv_ref, *rest,
                  n_side, tiles_per_seq, q_rank, kv_rank, d_conv):
    side_refs = rest[:n_side]
    q_ref, k_ref, v_ref, yc_ref = rest[n_side:n_side + 4]
    side_out_refs = rest[n_side + 4:2 * n_side + 4]
    prev_ref = rest[-1]
    i = pl.program_id(0)
    is_start = (i % tiles_per_seq) == 0
    tm = x_ref.shape[0]
    d_model = x_ref.shape[1]
    rope_lo = q_rank + kv_rank
    nt = (((1,), (1,)), ((), ()))

    @pl.when(is_start)
    def _():
        prev_ref[...] = jnp.zeros_like(prev_ref)

    xn = _rms(x_ref[...], gmix_ref[...], d_model).astype(BF16)
    lat = lax.dot_general(xn, win_ref[:rope_lo, :], nt, preferred_element_type=F32)
    q_lat = lat[:, :q_rank]
    kv_lat = lat[:, q_rank:]
    kr = lax.dot_general(xn, wrope_ref[...], nt, preferred_element_type=F32)

    qn = _rms(q_lat, gql_ref[...], q_rank).astype(BF16)
    kvn = _rms(kv_lat, gkvl_ref[...], kv_rank).astype(BF16)
    q_all = jnp.dot(qn, wuq_ref[...], preferred_element_type=F32)
    kv_all = jnp.dot(kvn, wukv_ref[...], preferred_element_type=F32)

    lane = lax.broadcasted_iota(jnp.int32, (tm, LANES), 1)
    ang = pos_ref[...].astype(F32) * freq_ref[...]
    cos = jnp.cos(ang)
    sin = jnp.sin(ang)
    rope_c = jnp.where(lane < QK_ROPE, cos, 0.0)
    rope_s = jnp.where(lane < QK_ROPE // 2, -sin, jnp.where(lane < QK_ROPE, sin, 0.0))
    rope_lanes = lane < QK_ROPE

    def rope(t):
        return t * rope_c + pltpu.roll(t, QK_ROPE, 1) * rope_s

    gq = gq_ref[...]
    gk = gk_ref[...]
    inv_n = 1.0 / QK_HEAD
    kr_ssq = jnp.sum(jnp.where(rope_lanes, kr * kr, 0.0), axis=-1, keepdims=True)
    kr_rot = rope(kr * gk[:, QK_NOPE:])
    for h in range(MLA_HEADS):
        qa = q_all[:, h * HEAD_PAD:h * HEAD_PAD + QK_NOPE]
        qb = q_all[:, h * HEAD_PAD + QK_NOPE:(h + 1) * HEAD_PAD]
        ssq = (jnp.sum(qa * qa, axis=-1, keepdims=True)
               + jnp.sum(jnp.where(rope_lanes, qb * qb, 0.0), axis=-1, keepdims=True))
        inv = lax.rsqrt(ssq * inv_n + EPS)
        q_ref[:, h * HEAD_PAD:h * HEAD_PAD + QK_NOPE] = (qa * inv * gq[:, :QK_NOPE] * Q_SCALE).astype(BF16)
        q_ref[:, h * HEAD_PAD + QK_NOPE:(h + 1) * HEAD_PAD] = (rope(qb * inv * gq[:, QK_NOPE:]) * Q_SCALE).astype(BF16)

        ka = kv_all[:, h * QK_NOPE:(h + 1) * QK_NOPE]
        kinv = lax.rsqrt((jnp.sum(ka * ka, axis=-1, keepdims=True) + kr_ssq) * inv_n + EPS)
        k_ref[:, h * HEAD_PAD:h * HEAD_PAD + QK_NOPE] = (ka * kinv * gk[:, :QK_NOPE]).astype(BF16)
        k_ref[:, h * HEAD_PAD + QK_NOPE:(h + 1) * HEAD_PAD] = (kr_rot * kinv).astype(BF16)
    v_ref[...] = kv_all[:, MLA_HEADS * QK_NOPE:].astype(BF16)

    conv_lo = rope_lo + QK_ROPE
    half_c = d_conv // 2
    row = lax.broadcasted_iota(jnp.int32, (tm, half_c), 0)
    wc = wconv_ref[...]
    ys = []
    ssq = jnp.zeros((tm, 1), F32)
    for cs in (slice(0, half_c), slice(half_c, d_conv)):
        b_gate, c_gate, h_conv = (lax.dot_general(xn, win_ref[conv_lo + k * d_conv + cs.start:conv_lo + k * d_conv + cs.stop, :],
                                                  nt, preferred_element_type=F32) for k in range(3))
        u = c_gate * h_conv
        prev = prev_ref[:, cs]
        hm1 = prev[7:8, :]
        hm2 = prev[6:7, :]
        u1 = jnp.where(row == 0, hm1, pltpu.roll(u, 1, 0))
        u2 = jnp.where(row == 0, hm2, jnp.where(row == 1, hm1, pltpu.roll(u, 2, 0)))
        prev_ref[:, cs] = u[tm - 8:, :]
        y = b_gate * (u2 * wc[0:1, cs] + u1 * wc[1:2, cs] + u * wc[2:3, cs])
        ssq = ssq + jnp.sum(y * y, axis=-1, keepdims=True)
        ys.append(y)
    inv = lax.rsqrt(ssq * (1.0 / d_conv) + EPS)
    yc_ref[...] = (jnp.concatenate(ys, axis=1) * inv * gconv_ref[...]).astype(BF16)
    for w_ref, wb_ref in zip(side_refs, side_out_refs):
        wb_ref[...] = w_ref[...].astype(BF16)


def _front(x2, pos2, freq, gmix, w_in_t, w_rope, gql, gkvl, wuq_p, wukv_p, gq_p, gk_p, wconv, gconv, sides, *, seq):
    t, d_model = x2.shape
    tm = TM_FRONT
    q_rank = gql.shape[1]
    kv_rank = gkvl.shape[1]
    d_conv = gconv.shape[1]
    const = lambda i: (0, 0)
    row = lambda i: (i, 0)
    full = lambda a: pl.BlockSpec(a.shape, const)
    side_in = [pl.BlockSpec((1,) + w.shape[1:], lambda i, off=off: (i + off, 0, 0)) for w, off in sides]
    side_out = [pl.BlockSpec((1,) + w.shape[1:], lambda i: (i, 0, 0)) for w, _ in sides]
    kern = functools.partial(_front_kernel, n_side=len(sides), tiles_per_seq=seq // tm, q_rank=q_rank, kv_rank=kv_rank,
                             d_conv=d_conv)
    outs = pl.pallas_call(
        kern,
        grid=(t // tm,),
        in_specs=[pl.BlockSpec((tm, d_model), row), pl.BlockSpec((tm, 1), row), full(freq), full(gmix),
                  pl.BlockSpec(w_in_t.shape, const, pipeline_mode=pl.Buffered(1)), full(w_rope),
                  full(gql), full(gkvl), full(wuq_p), full(wukv_p), full(gq_p), full(gk_p), full(wconv), full(gconv)] + side_in,
        out_specs=[pl.BlockSpec((tm, MLA_HEADS * HEAD_PAD), row), pl.BlockSpec((tm, MLA_HEADS * HEAD_PAD), row),
                   pl.BlockSpec((tm, MLA_HEADS * V_HEAD), row), pl.BlockSpec((tm, d_conv), row)] + side_out,
        out_shape=[jax.ShapeDtypeStruct((t, MLA_HEADS * HEAD_PAD), BF16), jax.ShapeDtypeStruct((t, MLA_HEADS * HEAD_PAD), BF16),
                   jax.ShapeDtypeStruct((t, MLA_HEADS * V_HEAD), BF16), jax.ShapeDtypeStruct((t, d_conv), BF16)]
                  + [jax.ShapeDtypeStruct((t // tm,) + w.shape[1:], BF16) for w, _ in sides],
        scratch_shapes=[pltpu.VMEM((8, d_conv), F32)],
        compiler_params=pltpu.CompilerParams(dimension_semantics=("arbitrary",), vmem_limit_bytes=VMEM_LIMIT),
        name="front",
    )(x2, pos2, freq, gmix, w_in_t, w_rope, gql, gkvl, wuq_p, wukv_p, gq_p, gk_p, wconv, gconv, *[w for w, _ in sides])
    return outs[:4], outs[4:]


def _attn_kernel(q_ref, k_ref, v_ref, *rest):
    n_side = (len(rest) - 1) // 2
    o_ref = rest[n_side]

    def round_side_weights():
        for w_ref, wb_ref in zip(rest[:n_side], rest[n_side + 1:]):
            wb_ref[...] = w_ref[...].astype(BF16)

    i = pl.program_id(2)
    tq = q_ref.shape[0]
    half = tq // 2
    q_halves = (q_ref[:half, :], q_ref[half:, :])

    def step(start, n_keys, masks, carry):
        kvs = [(k_ref[pl.ds(start, n), :],
                jnp.concatenate([v_ref[pl.ds(start, n), :], jnp.ones((n, HEAD_PAD - V_HEAD), BF16)], axis=1)) for n in n_keys]
        scores = [lax.dot_general(qh, kb, (((1,), (1,)), ((), ())), preferred_element_type=F32)
                  for qh, (kb, _) in zip(q_halves, kvs)]
        scores = [s if mask is None else jnp.where(mask, s, MASK_VALUE) for s, mask in zip(scores, masks)]
        m_new = [jnp.maximum(m, jnp.max(s, axis=-1, keepdims=True)) for s, (m, _) in zip(scores, carry)]
        probs = [jnp.exp2(s - mn).astype(BF16) for s, mn in zip(scores, m_new)]
        return tuple((mn, jnp.exp2(m - mn) * acc + jnp.dot(p, vb, preferred_element_type=F32))
                     for p, mn, (m, acc), (_, vb) in zip(probs, m_new, carry, kvs))

    def chunk_mask(n, first_row):
        r = (lax.broadcasted_iota(jnp.int32, (half, n), 0) + first_row) // CHUNK
        c = lax.broadcasted_iota(jnp.int32, (half, n), 1) // CHUNK
        return c <= r

    for qi in range(k_ref.shape[0] // tq):
        @pl.when(i == qi)
        def _(qi=qi):
            carry = tuple((jnp.full((half, 1), MASK_VALUE, F32), jnp.zeros((half, HEAD_PAD), F32)) for _ in range(2))
            for j in range(qi):
                carry = step(j * tq, (tq, tq), (None, None), carry)
            (_, acc0), (_, acc1) = step(qi * tq, (half, tq), (chunk_mask(half, 0), chunk_mask(tq, half)), carry)
            o_ref[:half, :] = (acc0[:, :V_HEAD] / acc0[:, V_HEAD:]).astype(o_ref.dtype)
            o_ref[half:, :] = (acc1[:, :V_HEAD] / acc1[:, V_HEAD:]).astype(o_ref.dtype)
            round_side_weights()


def _attention(q, k, v, side_weights, *, batch, seq):
    tq = TQ_ATTN
    nq = seq // tq
    n_steps = batch * MLA_HEADS * nq
    step = lambda b, h, i, off=0: ((b * MLA_HEADS + h) * nq + i + off, 0, 0)
    side_in = [pl.BlockSpec((1,) + w.shape[1:], functools.partial(step, off=off)) for w, off in side_weights]
    side_out = [pl.BlockSpec((1,) + w.shape[1:], step) for w, _ in side_weights]
    outs = pl.pallas_call(
        _attn_kernel,
        grid=(batch, MLA_HEADS, nq),
        in_specs=[pl.BlockSpec((tq, HEAD_PAD), lambda b, h, i: (b * nq + i, h)),
                  pl.BlockSpec((seq, HEAD_PAD), lambda b, h, i: (b, h)),
                  pl.BlockSpec((seq, V_HEAD), lambda b, h, i: (b, h))] + side_in,
        out_specs=[pl.BlockSpec((tq, V_HEAD), lambda b, h, i: (b * nq + i, h))] + side_out,
        out_shape=[jax.ShapeDtypeStruct((batch * seq, MLA_HEADS * V_HEAD), BF16)]
                  + [jax.ShapeDtypeStruct((n_steps,) + w.shape[1:], BF16) for w, _ in side_weights],
        compiler_params=pltpu.CompilerParams(dimension_semantics=("arbitrary", "arbitrary", "arbitrary"),
                                             vmem_limit_bytes=VMEM_LIMIT),
        name="attn",
    )(q, k, v, *[w for w, _ in side_weights])
    return outs[0], outs[1:]


def _outproj_kernel(ya_ref, yc_ref, x_ref, gattn_ref, wout_ref, gmoe_ref, wgr_ref, bgr_ref,
                    h1_ref, xm_ref, route_ref):
    tm, d_attn = ya_ref.shape
    d_model = x_ref.shape[1]
    ya = _rms(ya_ref[...].astype(F32), gattn_ref[...], d_attn).astype(BF16)
    h1 = (x_ref[...]
          + jnp.dot(ya, wout_ref[:d_attn, :], preferred_element_type=F32)
          + jnp.dot(yc_ref[...], wout_ref[d_attn:, :], preferred_element_type=F32))
    h1_ref[...] = h1
    xm = _rms(h1, gmoe_ref[...], d_model)
    _store_slabs(xm_ref, xm, SLAB_PITCH)
    for c in range(SLAB, SLAB_PITCH):
        xm_ref[pl.ds(c, tm, stride=SLAB_PITCH), :] = jnp.zeros((tm, LANES), F32)

    x_hi = xm.astype(BF16)
    x_lo = (xm - x_hi.astype(F32)).astype(BF16)
    hi_part = jnp.dot(x_hi, wgr_ref[...], preferred_element_type=F32)
    lo_part = jnp.dot(x_lo, wgr_ref[:, :LANES], preferred_element_type=F32)
    logits = hi_part[:, :LANES] + hi_part[:, LANES:] + lo_part + bgr_ref[...]
    lane = lax.broadcasted_iota(jnp.int32, (tm, LANES), 1).astype(F32)
    neg = -jnp.inf
    big = 1e9
    gl = jnp.where(lane < N_GROUPS, logits, neg)
    gmax = jnp.max(gl, axis=-1, keepdims=True)
    gsel = jnp.min(jnp.where(gl == gmax, lane, big), axis=-1, keepdims=True)
    p_g = 1.0 / jnp.sum(jnp.exp(gl - gmax), axis=-1, keepdims=True)
    lo = N_GROUPS + EXPERTS_PER_GROUP * gsel
    el = jnp.where((lane >= lo) & (lane < lo + EXPERTS_PER_GROUP), logits, neg)
    v1 = jnp.max(el, axis=-1, keepdims=True)
    i1 = jnp.min(jnp.where(el == v1, lane, big), axis=-1, keepdims=True)
    el2 = jnp.where(lane == i1, neg, el)
    v2 = jnp.max(el2, axis=-1, keepdims=True)
    i2 = jnp.min(jnp.where(el2 == v2, lane, big), axis=-1, keepdims=True)
    e2 = jnp.exp(v2 - v1)
    den = 1.0 + e2
    w1 = p_g / den
    w2 = p_g * e2 / den
    route_ref[...] = jnp.where(lane == 0, i1 - N_GROUPS,
                               jnp.where(lane == 1, i2 - N_GROUPS,
                                         jnp.where(lane == 2, w1, jnp.where(lane == 3, w2, 0.0))))


def _outproj(ya, yc, x2, gattn, w_out, gmoe, w_gr, b_gr):
    t, d_model = x2.shape
    d_attn = ya.shape[1]
    tm = TM_OUT
    const = lambda i: (0, 0)
    row = lambda i: (i, 0)
    full = lambda a: pl.BlockSpec(a.shape, const)
    return pl.pallas_call(
        _outproj_kernel,
        grid=(t // tm,),
        in_specs=[pl.BlockSpec((tm, d_attn), row), pl.BlockSpec((tm, yc.shape[1]), row), pl.BlockSpec((tm, d_model), row),
                  full(gattn), pl.BlockSpec(w_out.shape, const, pipeline_mode=pl.Buffered(1)), full(gmoe), full(w_gr), full(b_gr)],
        out_specs=[pl.BlockSpec((tm, d_model), row), pl.BlockSpec((tm * SLAB_PITCH, LANES), row), pl.BlockSpec((tm, LANES), row)],
        out_shape=[jax.ShapeDtypeStruct((t, d_model), F32), jax.ShapeDtypeStruct((t * SLAB_PITCH, LANES), F32),
                   jax.ShapeDtypeStruct((t, LANES), F32)],
        compiler_params=pltpu.CompilerParams(dimension_semantics=("arbitrary",), vmem_limit_bytes=VMEM_LIMIT),
        name="outproj",
    )(ya, yc, x2, gattn, w_out, gmoe, w_gr, b_gr)


def _row_gather_start(idx_ref, src_hbm, dst_ref, sem, n_rows, src_pitch=SLAB):
    def body(r, carry):
        src = src_hbm.at[pl.ds(idx_ref[0, 0, r] * src_pitch, SLAB), :]
        pltpu.make_async_copy(src, dst_ref.at[pl.ds(r * SLAB_PITCH, SLAB), :], sem).start()
        return carry
    lax.fori_loop(0, n_rows, body, 0, unroll=True)


def _row_gather_wait(src_hbm, dst_ref, sem, n_rows):
    pltpu.make_async_copy(src_hbm.at[pl.ds(0, n_rows * SLAB), :], dst_ref.at[pl.ds(0, n_rows * SLAB), :], sem).wait()


def _moe_kernel(tile_expert_ref, n_active_ref, first_ref, wslot_ref, next_expert_ref, tok_cur_ref, tok_next_ref,
                xm_hbm, w1_hbm, w3_hbm, w2lo_hbm, w2hi_hbm, y_ref, xbuf_ref, w1_buf, w3_buf, w2_buf, sem_ref, wsem_ref, *, tm):
    i = pl.program_id(0)
    n_active = n_active_ref[0]
    slot = i % 2
    buf_rows = tm * SLAB_PITCH
    wslot = wslot_ref[i]

    def buf(s):
        return xbuf_ref.at[pl.ds(s * buf_rows, buf_rows), :]

    n_lo = w2lo_hbm.shape[0]

    def start_weights(e, s):
        pltpu.make_async_copy(w1_hbm.at[e], w1_buf.at[s], wsem_ref.at[0, s]).start()
        pltpu.make_async_copy(w3_hbm.at[e], w3_buf.at[s], wsem_ref.at[1, s]).start()

        @pl.when(e < n_lo)
        def _():
            pltpu.make_async_copy(w2lo_hbm.at[e], w2_buf.at[s], wsem_ref.at[2, s]).start()

        @pl.when(e >= n_lo)
        def _():
            pltpu.make_async_copy(w2hi_hbm.at[e - n_lo], w2_buf.at[s], wsem_ref.at[2, s]).start()

    def wait_weights(s):
        pltpu.make_async_copy(w1_hbm.at[0], w1_buf.at[s], wsem_ref.at[0, s]).wait()
        pltpu.make_async_copy(w3_hbm.at[0], w3_buf.at[s], wsem_ref.at[1, s]).wait()
        pltpu.make_async_copy(w2lo_hbm.at[0], w2_buf.at[s], wsem_ref.at[2, s]).wait()

    @pl.when(i == 0)
    def _():
        start_weights(tile_expert_ref[0], 0)
        _row_gather_start(tok_cur_ref, xm_hbm, buf(0), sem_ref.at[0], tm, src_pitch=SLAB_PITCH)

    active = i < n_active
    new_expert = active & (first_ref[i] == 1)

    @pl.when(new_expert & (next_expert_ref[i] >= 0))
    def _():
        start_weights(next_expert_ref[i], 1 - wslot)

    @pl.when(i + 1 < n_active)
    def _():
        _row_gather_start(tok_next_ref, xm_hbm, buf(1 - slot), sem_ref.at[1 - slot], tm, src_pitch=SLAB_PITCH)

    @pl.when(new_expert)
    def _():
        wait_weights(wslot)

    @pl.when(active)
    def _():
        _row_gather_wait(xm_hbm, buf(slot), sem_ref.at[slot], tm)
        x = _load_slabs(xbuf_ref, slot * buf_rows, tm, SLAB, SLAB_PITCH).astype(BF16)
        a = jnp.dot(x, w1_buf[wslot], preferred_element_type=F32)
        g = jnp.dot(x, w3_buf[wslot], preferred_element_type=F32)
        hid = (a * jax.nn.sigmoid(a) * g).astype(BF16)
        _store_slabs(y_ref, jnp.dot(hid, w2_buf[wslot], preferred_element_type=F32), SLAB)

    @pl.when(i >= n_active)
    def _():
        y_ref[...] = jnp.zeros_like(y_ref)


def _moe(plan, xm, w1, w3, w2_lo, w2_hi):
    tile_expert, n_active, first, wslot, next_expert, row_token3 = plan
    n_tiles = row_token3.shape[0]
    tm = TM_MOE
    _, d_model, d_exp = w1.shape
    last = n_tiles - 1
    grid_spec = pltpu.PrefetchScalarGridSpec(
        num_scalar_prefetch=5,
        grid=(n_tiles,),
        in_specs=[pl.BlockSpec((1, 1, tm), lambda i, *_: (i, 0, 0), memory_space=pltpu.SMEM),
                  pl.BlockSpec((1, 1, tm), lambda i, *_: (jnp.minimum(i + 1, last), 0, 0), memory_space=pltpu.SMEM),
                  pl.BlockSpec(memory_space=pl.ANY), pl.BlockSpec(memory_space=pl.ANY), pl.BlockSpec(memory_space=pl.ANY),
                  pl.BlockSpec(memory_space=pl.ANY), pl.BlockSpec(memory_space=pl.ANY)],
        out_specs=pl.BlockSpec((tm * SLAB, LANES), lambda i, *_: (i, 0)),
        scratch_shapes=[pltpu.VMEM((2 * tm * SLAB_PITCH, LANES), F32),
                        pltpu.VMEM((2, d_model, d_exp), BF16), pltpu.VMEM((2, d_model, d_exp), BF16),
                        pltpu.VMEM((2, d_exp, d_model), BF16),
                        pltpu.SemaphoreType.DMA((2,)), pltpu.SemaphoreType.DMA((3, 2))],
    )
    return pl.pallas_call(
        functools.partial(_moe_kernel, tm=tm),
        grid_spec=grid_spec,
        out_shape=jax.ShapeDtypeStruct((n_tiles * tm * SLAB, LANES), F32),
        compiler_params=pltpu.CompilerParams(dimension_semantics=("arbitrary",), vmem_limit_bytes=VMEM_LIMIT),
        name="moe",
    )(tile_expert, n_active, first, wslot, next_expert, row_token3, row_token3, xm, w1, w3, w2_lo, w2_hi)


def _ple_kernel(dest_a_ref, dest_b_ref, dest_next_ref, y_hbm, h1_ref, route_ref, p_ref, wple_ref, gple_ref, wpg_ref,
                bpg_ref, o_ref, ybuf_ref, sem_ref):
    i = pl.program_id(0)
    n = pl.num_programs(0)
    tm = h1_ref.shape[0] // 2
    d_model = h1_ref.shape[1]
    buf_rows = TOP_K * tm * SLAB_PITCH
    bufs = [ybuf_ref.at[pl.ds(s * buf_rows, buf_rows), :] for s in range(2)]

    @pl.when(i == 0)
    def _():
        _row_gather_start(dest_a_ref, y_hbm, bufs[0], sem_ref.at[0], TOP_K * tm)

    _row_gather_start(dest_b_ref, y_hbm, bufs[1], sem_ref.at[1], TOP_K * tm)

    def tile(s):
        rows = slice(s * tm, (s + 1) * tm)
        _row_gather_wait(y_hbm, bufs[s], sem_ref.at[s], TOP_K * tm)
        route = route_ref[rows, :]
        y0 = _load_slabs(ybuf_ref, s * buf_rows, tm, SLAB, SLAB_PITCH)
        y1 = _load_slabs(ybuf_ref, s * buf_rows + tm * SLAB_PITCH, tm, SLAB, SLAB_PITCH)
        h2 = h1_ref[rows, :] + (route[:, 2:3] * y0 + route[:, 3:4] * y1)
        xn = _rms(h2, gple_ref[...], d_model).astype(BF16)
        gate = jax.nn.sigmoid(jnp.dot(xn, wpg_ref[...], preferred_element_type=F32) + bpg_ref[...])
        pe = jnp.dot(p_ref[rows, :].astype(BF16), wple_ref[...], preferred_element_type=F32)
        o_ref[rows, :] = h2 + gate * pe

    tile(0)

    @pl.when(i + 1 < n)
    def _():
        _row_gather_start(dest_next_ref, y_hbm, bufs[0], sem_ref.at[0], TOP_K * tm)

    tile(1)


def _ple(dest3, y, h1, route, p2, w_ple, gple, w_pg, b_pg):
    t, d_model = h1.shape
    tm = TM_PLE
    n_steps = t // (2 * tm)
    last = t // tm - 1
    const = lambda i: (0, 0)
    row = lambda i: (i, 0)
    full = lambda a: pl.BlockSpec(a.shape, const)
    dest_spec = lambda index: pl.BlockSpec((1, 1, TOP_K * tm), lambda i: (index(i), 0, 0), memory_space=pltpu.SMEM)
    return pl.pallas_call(
        _ple_kernel,
        grid=(n_steps,),
        in_specs=[dest_spec(lambda i: 2 * i), dest_spec(lambda i: 2 * i + 1), dest_spec(lambda i: jnp.minimum(2 * i + 2, last)),
                  pl.BlockSpec(memory_space=pl.ANY),
                  pl.BlockSpec((2 * tm, d_model), row), pl.BlockSpec((2 * tm, LANES), row),
                  pl.BlockSpec((2 * tm, p2.shape[1]), row),
                  full(w_ple), full(gple), pl.BlockSpec(w_pg.shape, const, pipeline_mode=pl.Buffered(1)), full(b_pg)],
        out_specs=pl.BlockSpec((2 * tm, d_model), row),
        out_shape=jax.ShapeDtypeStruct((t, d_model), F32),
        scratch_shapes=[pltpu.VMEM((2 * TOP_K * tm * SLAB_PITCH, LANES), F32), pltpu.SemaphoreType.DMA((2,))],
        compiler_params=pltpu.CompilerParams(dimension_semantics=("arbitrary",), vmem_limit_bytes=VMEM_LIMIT),
        name="ple",
    )(dest3, dest3, dest3, y, h1, route, p2, w_ple, gple, w_pg, b_pg)


def _dispatch_plan(route):
    t = route.shape[0]
    tm = TM_MOE
    total = TOP_K * t
    n_tiles = total // tm + N_EXPERTS
    expert = route[:, :TOP_K].astype(jnp.int32).reshape(-1)
    pos = jnp.arange(total, dtype=jnp.int32)
    experts = jnp.arange(N_EXPERTS, dtype=jnp.int32)
    onehot = (experts[:, None] == expert[None, :]).astype(jnp.int32)
    running = jnp.cumsum(onehot, axis=1)
    counts = running[:, -1]
    tile_end = jnp.cumsum((counts + tm - 1) // tm)
    tile_start = tile_end - (counts + tm - 1) // tm
    n_active = tile_end[-1]
    dest = jnp.sum(onehot * (running - onehot + (tile_start * tm)[:, None]), axis=0)
    tiles = jnp.minimum(jnp.arange(n_tiles, dtype=jnp.int32), n_active - 1)
    tile_expert = jnp.sum((tile_end[None, :] <= tiles[:, None]).astype(jnp.int32), axis=1)
    j = jnp.arange(tm, dtype=jnp.int32)
    fill_row = (tile_start * tm + counts)[:, None] + j[None, :]
    fill_row = jnp.where(fill_row < (tile_end * tm)[:, None], fill_row, n_tiles * tm)
    fill_tok = (experts[:, None] * tm + j[None, :]) % t
    keys = jnp.concatenate([dest * t + pos // TOP_K, (fill_row * t + fill_tok).reshape(-1)])
    row_token3 = (jnp.sort(keys) % t).reshape(n_tiles, 1, tm)
    dest3 = dest.reshape(t // TM_PLE, TM_PLE, TOP_K).transpose(0, 2, 1).reshape(t // TM_PLE, 1, TOP_K * TM_PLE)
    prev_expert = jnp.concatenate([jnp.full((1,), -1, jnp.int32), tile_expert[:-1]])
    first = (tile_expert != prev_expert).astype(jnp.int32)
    nonempty = counts > 0
    wslot = (jnp.sum((nonempty[None, :] & (experts[None, :] < tile_expert[:, None])).astype(jnp.int32), axis=1)) % 2
    later = nonempty[None, :] & (experts[None, :] > tile_expert[:, None])
    next_expert = jnp.min(jnp.where(later, experts[None, :], N_EXPERTS), axis=1)
    next_expert = jnp.where(next_expert < N_EXPERTS, next_expert, -1)
    plan = (tile_expert, n_active.reshape(1), first, wslot, next_expert, row_token3)
    return plan, dest3


def _rope_swap_pad(w):
    half = QK_ROPE // 2
    x1 = w[..., QK_NOPE:QK_NOPE + half]
    x2 = w[..., QK_NOPE + half:]
    return jnp.concatenate([w[..., :QK_NOPE], x1, x2, x2, x1], axis=-1)


def kernel(x, p, positions, norm_mix, w_in, g_q_lat, g_kv_lat, w_uq, w_ukv, g_q_head, g_k_head, w_conv, g_out_attn, g_out_conv, w_out, norm_moe, w_group, b_group, w_router, b_router, w1, w3, w2, norm_ple, w_ple, w_ple_gate, b_ple_gate):
    batch, seq, d_model = x.shape
    depth = w_in.shape[0]
    t = batch * seq
    q_rank = g_q_lat.shape[1]
    kv_rank = g_kv_lat.shape[1]
    half = QK_ROPE // 2
    rope_lo = q_rank + kv_rank

    pos2 = positions.reshape(t, 1)
    inv_freq = ROPE_BASE ** (-jnp.arange(0, QK_ROPE, 2, dtype=F32) / QK_ROPE)
    freq = jnp.tile(inv_freq, LANES // half).reshape(1, LANES)

    h = x.reshape(t, d_model)
    for i in range(depth):
        w_in_t = jnp.swapaxes(w_in[i], 0, 1).astype(BF16)
        w_rope = jnp.concatenate([w_in_t[rope_lo:rope_lo + QK_ROPE], w_in_t[rope_lo + half:rope_lo + QK_ROPE],
                                  w_in_t[rope_lo:rope_lo + half]], axis=0)
        wuq_p = _rope_swap_pad(w_uq[i]).reshape(q_rank, MLA_HEADS * HEAD_PAD).astype(BF16)
        wukv_p = jnp.concatenate([w_ukv[i][:, :, :QK_NOPE].reshape(kv_rank, -1),
                                  w_ukv[i][:, :, QK_NOPE:].reshape(kv_rank, -1)], axis=1).astype(BF16)
        gq_p = _rope_swap_pad(g_q_head[i]).reshape(1, HEAD_PAD)
        gk_p = _rope_swap_pad(g_k_head[i]).reshape(1, HEAD_PAD)
        as_steps = lambda w, n: w.reshape(n, -1, w.shape[-1])
        nf = t // TM_FRONT
        na = batch * MLA_HEADS * (seq // TQ_ATTN)
        front_sides = [(as_steps(w, nf), 0) for w in (w1[i], w_out[i], w_ple_gate[i])] + [(as_steps(w2[i], 2 * nf), 0)]
        (q, k, v, yc), (w1b, w_out_b, w_pg_b, w2_lo) = _front(
            h, pos2, freq, norm_mix[i].reshape(1, -1), w_in_t, w_rope, g_q_lat[i].reshape(1, -1),
            g_kv_lat[i].reshape(1, -1), wuq_p, wukv_p, gq_p, gk_p, w_conv[i], g_out_conv[i].reshape(1, -1), front_sides, seq=seq)
        ya, (w3b, w2_hi) = _attention(q, k, v, [(as_steps(w3[i], na), 0), (as_steps(w2[i], 2 * na), na)], batch=batch, seq=seq)
        half_e = (N_EXPERTS // 2,) + w2[i].shape[1:]
        w1b, w3b, w2_lo, w2_hi = w1b.reshape(w1[i].shape), w3b.reshape(w3[i].shape), w2_lo.reshape(half_e), w2_hi.reshape(half_e)
        w_out_b, w_pg_b = w_out_b.reshape(w_out[i].shape), w_pg_b.reshape(w_ple_gate[i].shape)

        pad = LANES - N_GROUPS - N_EXPERTS
        w_gr32 = jnp.concatenate([w_group[i], w_router[i], jnp.zeros((d_model, pad), F32)], axis=1)
        w_gr_hi = w_gr32.astype(BF16)
        w_gr = jnp.concatenate([w_gr_hi, (w_gr32 - w_gr_hi.astype(F32)).astype(BF16)], axis=1)
        b_gr = jnp.concatenate([b_group[i], b_router[i], jnp.zeros((pad,), F32)]).reshape(1, LANES)
        h1, xm, route = _outproj(ya, yc, h, g_out_attn[i].reshape(1, -1), w_out_b,
                                 norm_moe[i].reshape(1, -1), w_gr, b_gr)

        plan, dest3 = _dispatch_plan(route)
        y = _moe(plan, xm, w1b, w3b, w2_lo, w2_hi)
        h = _ple(dest3, y, h1, route, p[i].reshape(t, -1), w_ple[i].astype(BF16), norm_ple[i].reshape(1, -1),
                 w_pg_b, b_ple_gate[i].reshape(1, -1))
    return h.reshape(batch, seq, d_model)
```

```python
import functools

import jax
import jax.numpy as jnp
from jax import lax
from jax.experimental import pallas as pl
from jax.experimental.pallas import tpu as pltpu

F32 = jnp.float32
BF16 = jnp.bfloat16

CHUNK = 64
EPS = 1e-6
MASK_VALUE = -1e30
ROPE_BASE = 10000.0

MLA_HEADS = 8
QK_NOPE = 128
QK_ROPE = 64
QK_HEAD = QK_NOPE + QK_ROPE
V_HEAD = 128
HEAD_PAD = 256
Q_SCALE = QK_HEAD ** -0.5 * 1.4426950408889634
N_GROUPS = 4
EXPERTS_PER_GROUP = 8
N_EXPERTS = N_GROUPS * EXPERTS_PER_GROUP
TOP_K = 2
LANES = 128
SLAB = 16
SLAB_PITCH = 20

TM_FRONT = 256
TQ_ATTN = 1024
TM_OUT = 512
TM_MOE = 256
TM_PLE = 256
VMEM_LIMIT = 56 * 1024 * 1024


def _rms(x, g, n):
    ms = jnp.sum(x * x, axis=-1, keepdims=True) * (1.0 / n)
    return x * lax.rsqrt(ms + EPS) * g


def _store_slabs(ref, val, pitch):
    n, width = val.shape
    for c in range(width // LANES):
        ref[pl.ds(c, n, stride=pitch), :] = val[:, c * LANES:(c + 1) * LANES]


def _load_slabs(ref, start, n, slab, pitch):
    return jnp.concatenate([ref[pl.ds(start + c, n, stride=pitch), :] for c in range(slab)], axis=1)


def _front_kernel(x_ref, pos_ref, freq_ref, gmix_ref, win_ref, wrope_ref, gql_ref, gkvl_ref,
                  wuq_ref, wukv_ref, gq_ref, gk_ref, wconv_ref, gconv_ref, *rest,
                  n_side, tiles_per_seq, q_rank, kv_rank, d_conv):
    side_refs = rest[:n_side]
    q_ref, k_ref, v_ref, yc_ref = rest[n_side:n_side + 4]
    side_out_refs = rest[n_side + 4:2 * n_side + 4]
    prev_ref = rest[-1]
    i = pl.program_id(0)
    is_start = (i % tiles_per_seq) == 0
    tm = x_ref.shape[0]
    d_model = x_ref.shape[1]
    rope_lo = q_rank + kv_rank
    nt = (((1,), (1,)), ((), ()))

    @pl.when(is_start)
    def _():
        prev_ref[...] = jnp.zeros_like(prev_ref)

    xn = _rms(x_ref[...], gmix_ref[...], d_model).astype(BF16)
    lat = lax.dot_general(xn, win_ref[:rope_lo, :], nt, preferred_element_type=F32)
    q_lat = lat[:, :q_rank]
    kv_lat = lat[:, q_rank:]
    kr = lax.dot_general(xn, wrope_ref[...], nt, preferred_element_type=F32)

    qn = _rms(q_lat, gql_ref[...], q_rank).astype(BF16)
    kvn = _rms(kv_lat, gkvl_ref[...], kv_rank).astype(BF16)
    q_all = jnp.dot(qn, wuq_ref[...], preferred_element_type=F32)
    kv_all = jnp.dot(kvn, wukv_ref[...], preferred_element_type=F32)

    lane = lax.broadcasted_iota(jnp.int32, (tm, LANES), 1)
    ang = pos_ref[...].astype(F32) * freq_ref[...]
    cos = jnp.cos(ang)
    sin = jnp.sin(ang)
    rope_c = jnp.where(lane < QK_ROPE, cos, 0.0)
    rope_s = jnp.where(lane < QK_ROPE // 2, -sin, jnp.where(lane < QK_ROPE, sin, 0.0))
    rope_lanes = lane < QK_ROPE

    def rope(t):
        return t * rope_c + pltpu.roll(t, QK_ROPE, 1) * rope_s

    gq = gq_ref[...]
    gk = gk_ref[...]
    inv_n = 1.0 / QK_HEAD
    kr_ssq = jnp.sum(jnp.where(rope_lanes, kr * kr, 0.0), axis=-1, keepdims=True)
    kr_rot = rope(kr * gk[:, QK_NOPE:])
    for h in range(MLA_HEADS):
        qa = q_all[:, h * HEAD_PAD:h * HEAD_PAD + QK_NOPE]
        qb = q_all[:, h * HEAD_PAD + QK_NOPE:(h + 1) * HEAD_PAD]
        ssq = (jnp.sum(qa * qa, axis=-1, keepdims=True)
               + jnp.sum(jnp.where(rope_lanes, qb * qb, 0.0), axis=-1, keepdims=True))
        inv = lax.rsqrt(ssq * inv_n + EPS)
        q_ref[:, h * HEAD_PAD:h * HEAD_PAD + QK_NOPE] = (qa * inv * gq[:, :QK_NOPE] * Q_SCALE).astype(BF16)
        q_ref[:, h * HEAD_PAD + QK_NOPE:(h + 1) * HEAD_PAD] = (rope(qb * inv * gq[:, QK_NOPE:]) * Q_SCALE).astype(BF16)

        ka = kv_all[:, h * QK_NOPE:(h + 1) * QK_NOPE]
        kinv = lax.rsqrt((jnp.sum(ka * ka, axis=-1, keepdims=True) + kr_ssq) * inv_n + EPS)
        k_ref[:, h * HEAD_PAD:h * HEAD_PAD + QK_NOPE] = (ka * kinv * gk[:, :QK_NOPE]).astype(BF16)
        k_ref[:, h * HEAD_PAD + QK_NOPE:(h + 1) * HEAD_PAD] = (kr_rot * kinv).astype(BF16)
    v_ref[...] = kv_all[:, MLA_HEADS * QK_NOPE:].astype(BF16)

    conv_lo = rope_lo + QK_ROPE
    half_c = d_conv // 2
    row = lax.broadcasted_iota(jnp.int32, (tm, half_c), 0)
    wc = wconv_ref[...]
    ys = []
    ssq = jnp.zeros((tm, 1), F32)
    for cs in (slice(0, half_c), slice(half_c, d_conv)):
        b_gate, c_gate, h_conv = (lax.dot_general(xn, win_ref[conv_lo + k * d_conv + cs.start:conv_lo + k * d_conv + cs.stop, :],
                                                  nt, preferred_element_type=F32) for k in range(3))
        u = c_gate * h_conv
        prev = prev_ref[:, cs]
        hm1 = prev[7:8, :]
        hm2 = prev[6:7, :]
        u1 = jnp.where(row == 0, hm1, pltpu.roll(u, 1, 0))
        u2 = jnp.where(row == 0, hm2, jnp.where(row == 1, hm1, pltpu.roll(u, 2, 0)))
        prev_ref[:, cs] = u[tm - 8:, :]
        y = b_gate * (u2 * wc[0:1, cs] + u1 * wc[1:2, cs] + u * wc[2:3, cs])
        ssq = ssq + jnp.sum(y * y, axis=-1, keepdims=True)
        ys.append(y)
    inv = lax.rsqrt(ssq * (1.0 / d_conv) + EPS)
    yc_ref[...] = (jnp.concatenate(ys, axis=1) * inv * gconv_ref[...]).astype(BF16)
    for w_ref, wb_ref in zip(side_refs, side_out_refs):
        wb_ref[...] = w_ref[...].astype(BF16)


def _front(x2, pos2, freq, gmix, w_in_t, w_rope, gql, gkvl, wuq_p, wukv_p, gq_p, gk_p, wconv, gconv, sides, *, seq):
    t, d_model = x2.shape
    tm = TM_FRONT
    q_rank = gql.shape[1]
    kv_rank = gkvl.shape[1]
    d_conv = gconv.shape[1]
    const = lambda i: (0, 0)
    row = lambda i: (i, 0)
    full = lambda a: pl.BlockSpec(a.shape, const)
    side_in = [pl.BlockSpec((1,) + w.shape[1:], lambda i, off=off: (i + off, 0, 0)) for w, off in sides]
    side_out = [pl.BlockSpec((1,) + w.shape[1:], lambda i: (i, 0, 0)) for w, _ in sides]
    kern = functools.partial(_front_kernel, n_side=len(sides), tiles_per_seq=seq // tm, q_rank=q_rank, kv_rank=kv_rank,
                             d_conv=d_conv)
    outs = pl.pallas_call(
        kern,
        grid=(t // tm,),
        in_specs=[pl.BlockSpec((tm, d_model), row), pl.BlockSpec((tm, 1), row), full(freq), full(gmix),
                  pl.BlockSpec(w_in_t.shape, const, pipeline_mode=pl.Buffered(1)), full(w_rope),
                  full(gql), full(gkvl), full(wuq_p), full(wukv_p), full(gq_p), full(gk_p), full(wconv), full(gconv)] + side_in,
        out_specs=[pl.BlockSpec((tm, MLA_HEADS * HEAD_PAD), row), pl.BlockSpec((tm, MLA_HEADS * HEAD_PAD), row),
                   pl.BlockSpec((tm, MLA_HEADS * V_HEAD), row), pl.BlockSpec((tm, d_conv), row)] + side_out,
        out_shape=[jax.ShapeDtypeStruct((t, MLA_HEADS * HEAD_PAD), BF16), jax.ShapeDtypeStruct((t, MLA_HEADS * HEAD_PAD), BF16),
                   jax.ShapeDtypeStruct((t, MLA_HEADS * V_HEAD), BF16), jax.ShapeDtypeStruct((t, d_conv), BF16)]
                  + [jax.ShapeDtypeStruct((t // tm,) + w.shape[1:], BF16) for w, _ in sides],
        scratch_shapes=[pltpu.VMEM((8, d_conv), F32)],
        compiler_params=pltpu.CompilerParams(dimension_semantics=("arbitrary",), vmem_limit_bytes=VMEM_LIMIT),
        name="front",
    )(x2, pos2, freq, gmix, w_in_t, w_rope, gql, gkvl, wuq_p, wukv_p, gq_p, gk_p, wconv, gconv, *[w for w, _ in sides])
    return outs[:4], outs[4:]


def _attn_kernel(q_ref, k_ref, v_ref, *rest):
    n_side = (len(rest) - 1) // 2
    o_ref = rest[n_side]

    def round_side_weights():
        for w_ref, wb_ref in zip(rest[:n_side], rest[n_side + 1:]):
            wb_ref[...] = w_ref[...].astype(BF16)

    i = pl.program_id(2)
    tq = q_ref.shape[0]
    half = tq // 2
    q_halves = (q_ref[:half, :], q_ref[half:, :])

    def step(start, n_keys, masks, carry):
        kvs = [(k_ref[pl.ds(start, n), :],
                jnp.concatenate([v_ref[pl.ds(start, n), :], jnp.ones((n, HEAD_PAD - V_HEAD), BF16)], axis=1)) for n in n_keys]
        scores = [lax.dot_general(qh, kb, (((1,), (1,)), ((), ())), preferred_element_type=F32)
                  for qh, (kb, _) in zip(q_halves, kvs)]
        scores = [s if mask is None else jnp.where(mask, s, MASK_VALUE) for s, mask in zip(scores, masks)]
        m_new = [jnp.maximum(m, jnp.max(s, axis=-1, keepdims=True)) for s, (m, _) in zip(scores, carry)]
        probs = [jnp.exp2(s - mn).astype(BF16) for s, mn in zip(scores, m_new)]
        return tuple((mn, jnp.exp2(m - mn) * acc + jnp.dot(p, vb, preferred_element_type=F32))
                     for p, mn, (m, acc), (_, vb) in zip(probs, m_new, carry, kvs))

    def chunk_mask(n, first_row):
        r = (lax.broadcasted_iota(jnp.int32, (half, n), 0) + first_row) // CHUNK
        c = lax.broadcasted_iota(jnp.int32, (half, n), 1) // CHUNK
        return c <= r

    for qi in range(k_ref.shape[0] // tq):
        @pl.when(i == qi)
        def _(qi=qi):
            carry = tuple((jnp.full((half, 1), MASK_VALUE, F32), jnp.zeros((half, HEAD_PAD), F32)) for _ in range(2))
            for j in range(qi):
                carry = step(j * tq, (tq, tq), (None, None), carry)
            (_, acc0), (_, acc1) = step(qi * tq, (half, tq), (chunk_mask(half, 0), chunk_mask(tq, half)), carry)
            o_ref[:half, :] = (acc0[:, :V_HEAD] / acc0[:, V_HEAD:]).astype(o_ref.dtype)
            o_ref[half:, :] = (acc1[:, :V_HEAD] / acc1[:, V_HEAD:]).astype(o_ref.dtype)
            round_side_weights()


def _attention(q, k, v, side_weights, *, batch, seq):
    tq = TQ_ATTN
    nq = seq // tq
    n_steps = batch * MLA_HEADS * nq
    step = lambda b, h, i, off=0: ((b * MLA_HEADS + h) * nq + i + off, 0, 0)
    side_in = [pl.BlockSpec((1,) + w.shape[1:], functools.partial(step, off=off)) for w, off in side_weights]
    side_out = [pl.BlockSpec((1,) + w.shape[1:], step) for w, _ in side_weights]
    outs = pl.pallas_call(
        _attn_kernel,
        grid=(batch, MLA_HEADS, nq),
        in_specs=[pl.BlockSpec((tq, HEAD_PAD), lambda b, h, i: (b * nq + i, h)),
                  pl.BlockSpec((seq, HEAD_PAD), lambda b, h, i: (b, h)),
                  pl.BlockSpec((seq, V_HEAD), lambda b, h, i: (b, h))] + side_in,
        out_specs=[pl.BlockSpec((tq, V_HEAD), lambda b, h, i: (b * nq + i, h))] + side_out,
        out_shape=[jax.ShapeDtypeStruct((batch * seq, MLA_HEADS * V_HEAD), BF16)]
                  + [jax.ShapeDtypeStruct((n_steps,) + w.shape[1:], BF16) for w, _ in side_weights],
        compiler_params=pltpu.CompilerParams(dimension_semantics=("arbitrary", "arbitrary", "arbitrary"),
                                             vmem_limit_bytes=VMEM_LIMIT),
        name="attn",
    )(q, k, v, *[w for w, _ in side_weights])
    return outs[0], outs[1:]


def _outproj_kernel(ya_ref, yc_ref, x_ref, gattn_ref, wout_ref, gmoe_ref, wgr_ref, bgr_ref,
                    h1_ref, xm_ref, route_ref):
    tm, d_attn = ya_ref.shape
    d_model = x_ref.shape[1]
    ya = _rms(ya_ref[...].astype(F32), gattn_ref[...], d_attn).astype(BF16)
    h1 = (x_ref[...]
          + jnp.dot(ya, wout_ref[:d_attn, :], preferred_element_type=F32)
          + jnp.dot(yc_ref[...], wout_ref[d_attn:, :], preferred_element_type=F32))
    h1_ref[...] = h1
    xm = _rms(h1, gmoe_ref[...], d_model)
    _store_slabs(xm_ref, xm, SLAB_PITCH)
    for c in range(SLAB, SLAB_PITCH):
        xm_ref[pl.ds(c, tm, stride=SLAB_PITCH), :] = jnp.zeros((tm, LANES), F32)

    x_hi = xm.astype(BF16)
    x_lo = (xm - x_hi.astype(F32)).astype(BF16)
    hi_part = jnp.dot(x_hi, wgr_ref[...], preferred_element_type=F32)
    lo_part = jnp.dot(x_lo, wgr_ref[:, :LANES], preferred_element_type=F32)
    logits = hi_part[:, :LANES] + hi_part[:, LANES:] + lo_part + bgr_ref[...]
    lane = lax.broadcasted_iota(jnp.int32, (tm, LANES), 1).astype(F32)
    neg = -jnp.inf
    big = 1e9
    gl = jnp.where(lane < N_GROUPS, logits, neg)
    gmax = jnp.max(gl, axis=-1, keepdims=True)
    gsel = jnp.min(jnp.where(gl == gmax, lane, big), axis=-1, keepdims=True)
    p_g = 1.0 / jnp.sum(jnp.exp(gl - gmax), axis=-1, keepdims=True)
    lo = N_GROUPS + EXPERTS_PER_GROUP * gsel
    el = jnp.where((lane >= lo) & (lane < lo + EXPERTS_PER_GROUP), logits, neg)
    v1 = jnp.max(el, axis=-1, keepdims=True)
    i1 = jnp.min(jnp.where(el == v1, lane, big), axis=-1, keepdims=True)
    el2 = jnp.where(lane == i1, neg, el)
    v2 = jnp.max(el2, axis=-1, keepdims=True)
    i2 = jnp.min(jnp.where(el2 == v2, lane, big), axis=-1, keepdims=True)
    e2 = jnp.exp(v2 - v1)
    den = 1.0 + e2
    w1 = p_g / den
    w2 = p_g * e2 / den
    route_ref[...] = jnp.where(lane == 0, i1 - N_GROUPS,
                               jnp.where(lane == 1, i2 - N_GROUPS,
                                         jnp.where(lane == 2, w1, jnp.where(lane == 3, w2, 0.0))))


def _outproj(ya, yc, x2, gattn, w_out, gmoe, w_gr, b_gr):
    t, d_model = x2.shape
    d_attn = ya.shape[1]
    tm = TM_OUT
    const = lambda i: (0, 0)
    row = lambda i: (i, 0)
    full = lambda a: pl.BlockSpec(a.shape, const)
    return pl.pallas_call(
        _outproj_kernel,
        grid=(t // tm,),
        in_specs=[pl.BlockSpec((tm, d_attn), row), pl.BlockSpec((tm, yc.shape[1]), row), pl.BlockSpec((tm, d_model), row),
                  full(gattn), pl.BlockSpec(w_out.shape, const, pipeline_mode=pl.Buffered(1)), full(gmoe), full(w_gr), full(b_gr)],
        out_specs=[pl.BlockSpec((tm, d_model), row), pl.BlockSpec((tm * SLAB_PITCH, LANES), row), pl.BlockSpec((tm, LANES), row)],
        out_shape=[jax.ShapeDtypeStruct((t, d_model), F32), jax.ShapeDtypeStruct((t * SLAB_PITCH, LANES), F32),
                   jax.ShapeDtypeStruct((t, LANES), F32)],
        compiler_params=pltpu.CompilerParams(dimension_semantics=("arbitrary",), vmem_limit_bytes=VMEM_LIMIT),
        name="outproj",
    )(ya, yc, x2, gattn, w_out, gmoe, w_gr, b_gr)


def _row_gather_start(idx_ref, src_hbm, dst_ref, sem, n_rows, src_pitch=SLAB):
    def body(r, carry):
        src = src_hbm.at[pl.ds(idx_ref[0, 0, r] * src_pitch, SLAB), :]
        pltpu.make_async_copy(src, dst_ref.at[pl.ds(r * SLAB_PITCH, SLAB), :], sem).start()
        return carry
    lax.fori_loop(0, n_rows, body, 0, unroll=True)


def _row_gather_wait(src_hbm, dst_ref, sem, n_rows):
    pltpu.make_async_copy(src_hbm.at[pl.ds(0, n_rows * SLAB), :], dst_ref.at[pl.ds(0, n_rows * SLAB), :], sem).wait()


def _moe_kernel(tile_expert_ref, n_active_ref, first_ref, wslot_ref, next_expert_ref, tok_cur_ref, tok_next_ref,
                xm_hbm, w1_hbm, w3_hbm, w2lo_hbm, w2hi_hbm, y_ref, xbuf_ref, w1_buf, w3_buf, w2_buf, sem_ref, wsem_ref, *, tm):
    i = pl.program_id(0)
    n_active = n_active_ref[0]
    slot = i % 2
    buf_rows = tm * SLAB_PITCH
    wslot = wslot_ref[i]

    def buf(s):
        return xbuf_ref.at[pl.ds(s * buf_rows, buf_rows), :]

    n_lo = w2lo_hbm.shape[0]

    def start_weights(e, s):
        pltpu.make_async_copy(w1_hbm.at[e], w1_buf.at[s], wsem_ref.at[0, s]).start()
        pltpu.make_async_copy(w3_hbm.at[e], w3_buf.at[s], wsem_ref.at[1, s]).start()

        @pl.when(e < n_lo)
        def _():
            pltpu.make_async_copy(w2lo_hbm.at[e], w2_buf.at[s], wsem_ref.at[2, s]).start()

        @pl.when(e >= n_lo)
        def _():
            pltpu.make_async_copy(w2hi_hbm.at[e - n_lo], w2_buf.at[s], wsem_ref.at[2, s]).start()

    def wait_weights(s):
        pltpu.make_async_copy(w1_hbm.at[0], w1_buf.at[s], wsem_ref.at[0, s]).wait()
        pltpu.make_async_copy(w3_hbm.at[0], w3_buf.at[s], wsem_ref.at[1, s]).wait()
        pltpu.make_async_copy(w2lo_hbm.at[0], w2_buf.at[s], wsem_ref.at[2, s]).wait()

    @pl.when(i == 0)
    def _():
        start_weights(tile_expert_ref[0], 0)
        _row_gather_start(tok_cur_ref, xm_hbm, buf(0), sem_ref.at[0], tm, src_pitch=SLAB_PITCH)

    active = i < n_active
    new_expert = active & (first_ref[i] == 1)

    @pl.when(new_expert & (next_expert_ref[i] >= 0))
    def _():
        start_weights(next_expert_ref[i], 1 - wslot)

    @pl.when(i + 1 < n_active)
    def _():
        _row_gather_start(tok_next_ref, xm_hbm, buf(1 - slot), sem_ref.at[1 - slot], tm, src_pitch=SLAB_PITCH)

    @pl.when(new_expert)
    def _():
        wait_weights(wslot)

    @pl.when(active)
    def _():
        _row_gather_wait(xm_hbm, buf(slot), sem_ref.at[slot], tm)
        x = _load_slabs(xbuf_ref, slot * buf_rows, tm, SLAB, SLAB_PITCH).astype(BF16)
        a = jnp.dot(x, w1_buf[wslot], preferred_element_type=F32)
        g = jnp.dot(x, w3_buf[wslot], preferred_element_type=F32)
        hid = (a * jax.nn.sigmoid(a) * g).astype(BF16)
        _store_slabs(y_ref, jnp.dot(hid, w2_buf[wslot], preferred_element_type=F32), SLAB)

    @pl.when(i >= n_active)
    def _():
        y_ref[...] = jnp.zeros_like(y_ref)


def _moe(plan, xm, w1, w3, w2_lo, w2_hi):
    tile_expert, n_active, first, wslot, next_expert, row_token3 = plan
    n_tiles = row_token3.shape[0]
    tm = TM_MOE
    _, d_model, d_exp = w1.shape
    last = n_tiles - 1
    grid_spec = pltpu.PrefetchScalarGridSpec(
        num_scalar_prefetch=5,
        grid=(n_tiles,),
        in_specs=[pl.BlockSpec((1, 1, tm), lambda i, *_: (i, 0, 0), memory_space=pltpu.SMEM),
                  pl.BlockSpec((1, 1, tm), lambda i, *_: (jnp.minimum(i + 1, last), 0, 0), memory_space=pltpu.SMEM),
                  pl.BlockSpec(memory_space=pl.ANY), pl.BlockSpec(memory_space=pl.ANY), pl.BlockSpec(memory_space=pl.ANY),
                  pl.BlockSpec(memory_space=pl.ANY), pl.BlockSpec(memory_space=pl.ANY)],
        out_specs=pl.BlockSpec((tm * SLAB, LANES), lambda i, *_: (i, 0)),
        scratch_shapes=[pltpu.VMEM((2 * tm * SLAB_PITCH, LANES), F32),
                        pltpu.VMEM((2, d_model, d_exp), BF16), pltpu.VMEM((2, d_model, d_exp), BF16),
                        pltpu.VMEM((2, d_exp, d_model), BF16),
                        pltpu.SemaphoreType.DMA((2,)), pltpu.SemaphoreType.DMA((3, 2))],
    )
    return pl.pallas_call(
        functools.partial(_moe_kernel, tm=tm),
        grid_spec=grid_spec,
        out_shape=jax.ShapeDtypeStruct((n_tiles * tm * SLAB, LANES), F32),
        compiler_params=pltpu.CompilerParams(dimension_semantics=("arbitrary",), vmem_limit_bytes=VMEM_LIMIT),
        name="moe",
    )(tile_expert, n_active, first, wslot, next_expert, row_token3, row_token3, xm, w1, w3, w2_lo, w2_hi)


def _ple_kernel(dest_a_ref, dest_b_ref, dest_next_ref, y_hbm, h1_ref, route_ref, p_ref, wple_ref, gple_ref, wpg_ref,
                bpg_ref, o_ref, ybuf_ref, sem_ref):
    i = pl.program_id(0)
    n = pl.num_programs(0)
    tm = h1_ref.shape[0] // 2
    d_model = h1_ref.shape[1]
    buf_rows = TOP_K * tm * SLAB_PITCH
    bufs = [ybuf_ref.at[pl.ds(s * buf_rows, buf_rows), :] for s in range(2)]

    @pl.when(i == 0)
    def _():
        _row_gather_start(dest_a_ref, y_hbm, bufs[0], sem_ref.at[0], TOP_K * tm)

    _row_gather_start(dest_b_ref, y_hbm, bufs[1], sem_ref.at[1], TOP_K * tm)

    def tile(s):
        rows = slice(s * tm, (s + 1) * tm)
        _row_gather_wait(y_hbm, bufs[s], sem_ref.at[s], TOP_K * tm)
        route = route_ref[rows, :]
        y0 = _load_slabs(ybuf_ref, s * buf_rows, tm, SLAB, SLAB_PITCH)
        y1 = _load_slabs(ybuf_ref, s * buf_rows + tm * SLAB_PITCH, tm, SLAB, SLAB_PITCH)
        h2 = h1_ref[rows, :] + (route[:, 2:3] * y0 + route[:, 3:4] * y1)
        xn = _rms(h2, gple_ref[...], d_model).astype(BF16)
        gate = jax.nn.sigmoid(jnp.dot(xn, wpg_ref[...], preferred_element_type=F32) + bpg_ref[...])
        pe = jnp.dot(p_ref[rows, :].astype(BF16), wple_ref[...], preferred_element_type=F32)
        o_ref[rows, :] = h2 + gate * pe

    tile(0)

    @pl.when(i + 1 < n)
    def _():
        _row_gather_start(dest_next_ref, y_hbm, bufs[0], sem_ref.at[0], TOP_K * tm)

    tile(1)


def _ple(dest3, y, h1, route, p2, w_ple, gple, w_pg, b_pg):
    t, d_model = h1.shape
    tm = TM_PLE
    n_steps = t // (2 * tm)
    last = t // tm - 1
    const = lambda i: (0, 0)
    row = lambda i: (i, 0)
    full = lambda a: pl.BlockSpec(a.shape, const)
    dest_spec = lambda index: pl.BlockSpec((1, 1, TOP_K * tm), lambda i: (index(i), 0, 0), memory_space=pltpu.SMEM)
    return pl.pallas_call(
        _ple_kernel,
        grid=(n_steps,),
        in_specs=[dest_spec(lambda i: 2 * i), dest_spec(lambda i: 2 * i + 1), dest_spec(lambda i: jnp.minimum(2 * i + 2, last)),
                  pl.BlockSpec(memory_space=pl.ANY),
                  pl.BlockSpec((2 * tm, d_model), row), pl.BlockSpec((2 * tm, LANES), row),
                  pl.BlockSpec((2 * tm, p2.shape[1]), row),
                  full(w_ple), full(gple), pl.BlockSpec(w_pg.shape, const, pipeline_mode=pl.Buffered(1)), full(b_pg)],
        out_specs=pl.BlockSpec((2 * tm, d_model), row),
        out_shape=jax.ShapeDtypeStruct((t, d_model), F32),
        scratch_shapes=[pltpu.VMEM((2 * TOP_K * tm * SLAB_PITCH, LANES), F32), pltpu.SemaphoreType.DMA((2,))],
        compiler_params=pltpu.CompilerParams(dimension_semantics=("arbitrary",), vmem_limit_bytes=VMEM_LIMIT),
        name="ple",
    )(dest3, dest3, dest3, y, h1, route, p2, w_ple, gple, w_pg, b_pg)


def _dispatch_plan(route):
    t = route.shape[0]
    tm = TM_MOE
    total = TOP_K * t
    n_tiles = total // tm + N_EXPERTS
    expert = route[:, :TOP_K].astype(jnp.int32).T.reshape(-1)
    pos = jnp.arange(total, dtype=jnp.int32)
    experts = jnp.arange(N_EXPERTS, dtype=jnp.int32)
    onehot = (experts[:, None] == expert[None, :]).astype(jnp.int32)
    running = jnp.cumsum(onehot, axis=1)
    counts = running[:, -1]
    tile_end = jnp.cumsum((counts + tm - 1) // tm)
    tile_start = tile_end - (counts + tm - 1) // tm
    n_active = tile_end[-1]
    dest = jnp.sum(onehot * (running - onehot + (tile_start * tm)[:, None]), axis=0)
    tiles = jnp.minimum(jnp.arange(n_tiles, dtype=jnp.int32), n_active - 1)
    tile_expert = jnp.sum((tile_end[None, :] <= tiles[:, None]).astype(jnp.int32), axis=1)
    j = jnp.arange(tm, dtype=jnp.int32)
    fill_row = (tile_start * tm + counts)[:, None] + j[None, :]
    fill_row = jnp.where(fill_row < (tile_end * tm)[:, None], fill_row, n_tiles * tm)
    fill_tok = (experts[:, None] * tm + j[None, :]) % t
    keys = jnp.concatenate([dest * t + pos % t, (fill_row * t + fill_tok).reshape(-1)])
    row_token3 = (jnp.sort(keys) % t).reshape(n_tiles, 1, tm)
    dest3 = dest.reshape(TOP_K, t // TM_PLE, TM_PLE).transpose(1, 0, 2).reshape(t // TM_PLE, 1, TOP_K * TM_PLE)
    prev_expert = jnp.concatenate([jnp.full((1,), -1, jnp.int32), tile_expert[:-1]])
    first = (tile_expert != prev_expert).astype(jnp.int32)
    nonempty = counts > 0
    wslot = (jnp.sum((nonempty[None, :] & (experts[None, :] < tile_expert[:, None])).astype(jnp.int32), axis=1)) % 2
    later = nonempty[None, :] & (experts[None, :] > tile_expert[:, None])
    next_expert = jnp.min(jnp.where(later, experts[None, :], N_EXPERTS), axis=1)
    next_expert = jnp.where(next_expert < N_EXPERTS, next_expert, -1)
    plan = (tile_expert, n_active.reshape(1), first, wslot, next_expert, row_token3)
    return plan, dest3


def _rope_swap_pad(w):
    half = QK_ROPE // 2
    x1 = w[..., QK_NOPE:QK_NOPE + half]
    x2 = w[..., QK_NOPE + half:]
    return jnp.concatenate([w[..., :QK_NOPE], x1, x2, x2, x1], axis=-1)


def kernel(x, p, positions, norm_mix, w_in, g_q_lat, g_kv_lat, w_uq, w_ukv, g_q_head, g_k_head, w_conv, g_out_attn, g_out_conv, w_out, norm_moe, w_group, b_group, w_router, b_router, w1, w3, w2, norm_ple, w_ple, w_ple_gate, b_ple_gate):
    batch, seq, d_model = x.shape
    depth = w_in.shape[0]
    t = batch * seq
    q_rank = g_q_lat.shape[1]
    kv_rank = g_kv_lat.shape[1]
    half = QK_ROPE // 2
    rope_lo = q_rank + kv_rank

    pos2 = positions.reshape(t, 1)
    inv_freq = ROPE_BASE ** (-jnp.arange(0, QK_ROPE, 2, dtype=F32) / QK_ROPE)
    freq = jnp.tile(inv_freq, LANES // half).reshape(1, LANES)

    h = x.reshape(t, d_model)
    for i in range(depth):
        w_in_t = jnp.swapaxes(w_in[i], 0, 1).astype(BF16)
        w_rope = jnp.concatenate([w_in_t[rope_lo:rope_lo + QK_ROPE], w_in_t[rope_lo + half:rope_lo + QK_ROPE],
                                  w_in_t[rope_lo:rope_lo + half]], axis=0)
        wuq_p = _rope_swap_pad(w_uq[i]).reshape(q_rank, MLA_HEADS * HEAD_PAD).astype(BF16)
        wukv_p = jnp.concatenate([w_ukv[i][:, :, :QK_NOPE].reshape(kv_rank, -1),
                                  w_ukv[i][:, :, QK_NOPE:].reshape(kv_rank, -1)], axis=1).astype(BF16)
        gq_p = _rope_swap_pad(g_q_head[i]).reshape(1, HEAD_PAD)
        gk_p = _rope_swap_pad(g_k_head[i]).reshape(1, HEAD_PAD)
        as_steps = lambda w, n: w.reshape(n, -1, w.shape[-1])
        nf = t // TM_FRONT
        na = batch * MLA_HEADS * (seq // TQ_ATTN)
        front_sides = [(as_steps(w, nf), 0) for w in (w1[i], w_out[i], w_ple_gate[i])] + [(as_steps(w2[i], 2 * nf), 0)]
        (q, k, v, yc), (w1b, w_out_b, w_pg_b, w2_lo) = _front(
            h, pos2, freq, norm_mix[i].reshape(1, -1), w_in_t, w_rope, g_q_lat[i].reshape(1, -1),
            g_kv_lat[i].reshape(1, -1), wuq_p, wukv_p, gq_p, gk_p, w_conv[i], g_out_conv[i].reshape(1, -1), front_sides, seq=seq)
        ya, (w3b, w2_hi) = _attention(q, k, v, [(as_steps(w3[i], na), 0), (as_steps(w2[i], 2 * na), na)], batch=batch, seq=seq)
        half_e = (N_EXPERTS // 2,) + w2[i].shape[1:]
        w1b, w3b, w2_lo, w2_hi = w1b.reshape(w1[i].shape), w3b.reshape(w3[i].shape), w2_lo.reshape(half_e), w2_hi.reshape(half_e)
        w_out_b, w_pg_b = w_out_b.reshape(w_out[i].shape), w_pg_b.reshape(w_ple_gate[i].shape)

        pad = LANES - N_GROUPS - N_EXPERTS
        w_gr32 = jnp.concatenate([w_group[i], w_router[i], jnp.zeros((d_model, pad), F32)], axis=1)
        w_gr_hi = w_gr32.astype(BF16)
        w_gr = jnp.concatenate([w_gr_hi, (w_gr32 - w_gr_hi.astype(F32)).astype(BF16)], axis=1)
        b_gr = jnp.concatenate([b_group[i], b_router[i], jnp.zeros((pad,), F32)]).reshape(1, LANES)
        h1, xm, route = _outproj(ya, yc, h, g_out_attn[i].reshape(1, -1), w_out_b,
                                 norm_moe[i].reshape(1, -1), w_gr, b_gr)

        plan, dest3 = _dispatch_plan(route)
        y = _moe(plan, xm, w1b, w3b, w2_lo, w2_hi)
        h = _ple(dest3, y, h1, route, p[i].reshape(t, -1), w_ple[i].astype(BF16), norm_ple[i].reshape(1, -1),
                 w_pg_b, b_ple_gate[i].reshape(1, -1))
    return h.reshape(batch, seq, d_model)
```

```python
import functools

import jax
import jax.numpy as jnp
from jax import lax
from jax.experimental import pallas as pl
from jax.experimental.pallas import tpu as pltpu

F32 = jnp.float32
BF16 = jnp.bfloat16

CHUNK = 64
EPS = 1e-6
MASK_VALUE = -1e30
ROPE_BASE = 10000.0

MLA_HEADS = 8
QK_NOPE = 128
QK_ROPE = 64
QK_HEAD = QK_NOPE + QK_ROPE
V_HEAD = 128
HEAD_PAD = 256
Q_SCALE = QK_HEAD ** -0.5 * 1.4426950408889634
N_GROUPS = 4
EXPERTS_PER_GROUP = 8
N_EXPERTS = N_GROUPS * EXPERTS_PER_GROUP
TOP_K = 2
LANES = 128
SLAB = 16
SLAB_PITCH = 20

TM_FRONT = 256
TQ_ATTN = 1024
TM_OUT = 512
TM_MOE = 256
TM_PLE = 256
VMEM_LIMIT = 56 * 1024 * 1024


def _rms(x, g, n):
    ms = jnp.sum(x * x, axis=-1, keepdims=True) * (1.0 / n)
    return x * lax.rsqrt(ms + EPS) * g


def _store_slabs(ref, val, pitch):
    n, width = val.shape
    for c in range(width // LANES):
        ref[pl.ds(c, n, stride=pitch), :] = val[:, c * LANES:(c + 1) * LANES]


def _load_slabs(ref, start, n, slab, pitch):
    return jnp.concatenate([ref[pl.ds(start + c, n, stride=pitch), :] for c in range(slab)], axis=1)


def _front_kernel(x_ref, pos_ref, freq_ref, gmix_ref, win_ref, wrope_ref, gql_ref, gkvl_ref,
                  wuq_ref, wukv_ref, gq_ref, gk_ref, wconv_ref, gconv_ref, *rest,
                  n_side, tiles_per_seq, q_rank, kv_rank, d_conv):
    side_refs = rest[:n_side]
    q_ref, k_ref, v_ref, yc_ref = rest[n_side:n_side + 4]
    side_out_refs = rest[n_side + 4:2 * n_side + 4]
    prev_ref = rest[-1]
    i = pl.program_id(0)
    is_start = (i % tiles_per_seq) == 0
    tm = x_ref.shape[0]
    d_model = x_ref.shape[1]
    rope_lo = q_rank + kv_rank
    nt = (((1,), (1,)), ((), ()))

    @pl.when(is_start)
    def _():
        prev_ref[...] = jnp.zeros_like(prev_ref)

    xn = _rms(x_ref[...], gmix_ref[...], d_model).astype(BF16)
    lat = lax.dot_general(xn, win_ref[:rope_lo, :], nt, preferred_element_type=F32)
    q_lat = lat[:, :q_rank]
    kv_lat = lat[:, q_rank:]
    kr = lax.dot_general(xn, wrope_ref[...], nt, preferred_element_type=F32)

    qn = _rms(q_lat, gql_ref[...], q_rank).astype(BF16)
    kvn = _rms(kv_lat, gkvl_ref[...], kv_rank).astype(BF16)
    q_all = jnp.dot(qn, wuq_ref[...], preferred_element_type=F32)
    kv_all = jnp.dot(kvn, wukv_ref[...], preferred_element_type=F32)

    lane = lax.broadcasted_iota(jnp.int32, (tm, LANES), 1)
    ang = pos_ref[...].astype(F32) * freq_ref[...]
    cos = jnp.cos(ang)
    sin = jnp.sin(ang)
    rope_c = jnp.where(lane < QK_ROPE, cos, 0.0)
    rope_s = jnp.where(lane < QK_ROPE // 2, -sin, jnp.where(lane < QK_ROPE, sin, 0.0))
    rope_lanes = lane < QK_ROPE

    def rope(t):
        return t * rope_c + pltpu.roll(t, QK_ROPE, 1) * rope_s

    gq = gq_ref[...]
    gk = gk_ref[...]
    inv_n = 1.0 / QK_HEAD
    kr_ssq = jnp.sum(jnp.where(rope_lanes, kr * kr, 0.0), axis=-1, keepdims=True)
    kr_rot = rope(kr * gk[:, QK_NOPE:])
    for h in range(MLA_HEADS):
        qa = q_all[:, h * HEAD_PAD:h * HEAD_PAD + QK_NOPE]
        qb = q_all[:, h * HEAD_PAD + QK_NOPE:(h + 1) * HEAD_PAD]
        ssq = (jnp.sum(qa * qa, axis=-1, keepdims=True)
               + jnp.sum(jnp.where(rope_lanes, qb * qb, 0.0), axis=-1, keepdims=True))
        inv = lax.rsqrt(ssq * inv_n + EPS)
        q_ref[:, h * HEAD_PAD:h * HEAD_PAD + QK_NOPE] = (qa * inv * gq[:, :QK_NOPE] * Q_SCALE).astype(BF16)
        q_ref[:, h * HEAD_PAD + QK_NOPE:(h + 1) * HEAD_PAD] = (rope(qb * inv * gq[:, QK_NOPE:]) * Q_SCALE).astype(BF16)

        ka = kv_all[:, h * QK_NOPE:(h + 1) * QK_NOPE]
        kinv = lax.rsqrt((jnp.sum(ka * ka, axis=-1, keepdims=True) + kr_ssq) * inv_n + EPS)
        k_ref[:, h * HEAD_PAD:h * HEAD_PAD + QK_NOPE] = (ka * kinv * gk[:, :QK_NOPE]).astype(BF16)
        k_ref[:, h * HEAD_PAD + QK_NOPE:(h + 1) * HEAD_PAD] = (kr_rot * kinv).astype(BF16)
    v_ref[...] = kv_all[:, MLA_HEADS * QK_NOPE:].astype(BF16)

    conv_lo = rope_lo + QK_ROPE
    half_c = d_conv // 2
    row = lax.broadcasted_iota(jnp.int32, (tm, half_c), 0)
    wc = wconv_ref[...]
    ys = []
    ssq = jnp.zeros((tm, 1), F32)
    for cs in (slice(0, half_c), slice(half_c, d_conv)):
        b_gate, c_gate, h_conv = (lax.dot_general(xn, win_ref[conv_lo + k * d_conv + cs.start:conv_lo + k * d_conv + cs.stop, :],
                                                  nt, preferred_element_type=F32) for k in range(3))
        u = c_gate * h_conv
        prev = prev_ref[:, cs]
        hm1 = prev[7:8, :]
        hm2 = prev[6:7, :]
        u1 = jnp.where(row == 0, hm1, pltpu.roll(u, 1, 0))
        u2 = jnp.where(row == 0, hm2, jnp.where(row == 1, hm1, pltpu.roll(u, 2, 0)))
        prev_ref[:, cs] = u[tm - 8:, :]
        y = b_gate * (u2 * wc[0:1, cs] + u1 * wc[1:2, cs] + u * wc[2:3, cs])
        ssq = ssq + jnp.sum(y * y, axis=-1, keepdims=True)
        ys.append(y)
    inv = lax.rsqrt(ssq * (1.0 / d_conv) + EPS)
    yc_ref[...] = (jnp.concatenate(ys, axis=1) * inv * gconv_ref[...]).astype(BF16)
    for w_ref, wb_ref in zip(side_refs, side_out_refs):
        wb_ref[...] = w_ref[...].astype(BF16)


def _front(x2, pos2, freq, gmix, w_in_t, w_rope, gql, gkvl, wuq_p, wukv_p, gq_p, gk_p, wconv, gconv, sides, *, seq):
    t, d_model = x2.shape
    tm = TM_FRONT
    q_rank = gql.shape[1]
    kv_rank = gkvl.shape[1]
    d_conv = gconv.shape[1]
    const = lambda i: (0, 0)
    row = lambda i: (i, 0)
    full = lambda a: pl.BlockSpec(a.shape, const)
    side_in = [pl.BlockSpec((1,) + w.shape[1:], lambda i, off=off: (i + off, 0, 0)) for w, off in sides]
    side_out = [pl.BlockSpec((1,) + w.shape[1:], lambda i: (i, 0, 0)) for w, _ in sides]
    kern = functools.partial(_front_kernel, n_side=len(sides), tiles_per_seq=seq // tm, q_rank=q_rank, kv_rank=kv_rank,
                             d_conv=d_conv)
    outs = pl.pallas_call(
        kern,
        grid=(t // tm,),
        in_specs=[pl.BlockSpec((tm, d_model), row), pl.BlockSpec((tm, 1), row), full(freq), full(gmix),
                  pl.BlockSpec(w_in_t.shape, const, pipeline_mode=pl.Buffered(1)), full(w_rope),
                  full(gql), full(gkvl), full(wuq_p), full(wukv_p), full(gq_p), full(gk_p), full(wconv), full(gconv)] + side_in,
        out_specs=[pl.BlockSpec((tm, MLA_HEADS * HEAD_PAD), row), pl.BlockSpec((tm, MLA_HEADS * HEAD_PAD), row),
                   pl.BlockSpec((tm, MLA_HEADS * V_HEAD), row), pl.BlockSpec((tm, d_conv), row)] + side_out,
        out_shape=[jax.ShapeDtypeStruct((t, MLA_HEADS * HEAD_PAD), BF16), jax.ShapeDtypeStruct((t, MLA_HEADS * HEAD_PAD), BF16),
                   jax.ShapeDtypeStruct((t, MLA_HEADS * V_HEAD), BF16), jax.ShapeDtypeStruct((t, d_conv), BF16)]
                  + [jax.ShapeDtypeStruct((t // tm,) + w.shape[1:], BF16) for w, _ in sides],
        scratch_shapes=[pltpu.VMEM((8, d_conv), F32)],
        compiler_params=pltpu.CompilerParams(dimension_semantics=("arbitrary",), vmem_limit_bytes=VMEM_LIMIT),
        name="front",
    )(x2, pos2, freq, gmix, w_in_t, w_rope, gql, gkvl, wuq_p, wukv_p, gq_p, gk_p, wconv, gconv, *[w for w, _ in sides])
    return outs[:4], outs[4:]


def _attn_kernel(q_ref, k_ref, v_ref, *rest):
    n_side = (len(rest) - 1) // 2
    o_ref = rest[n_side]

    def round_side_weights():
        for w_ref, wb_ref in zip(rest[:n_side], rest[n_side + 1:]):
            wb_ref[...] = w_ref[...].astype(BF16)

    i = pl.program_id(2)
    tq = q_ref.shape[0]
    half = tq // 2
    q_halves = (q_ref[:half, :], q_ref[half:, :])

    def step(start, n_keys, masks, carry):
        kvs = [(k_ref[pl.ds(start, n), :],
                jnp.concatenate([v_ref[pl.ds(start, n), :], jnp.ones((n, HEAD_PAD - V_HEAD), BF16)], axis=1)) for n in n_keys]
        scores = [lax.dot_general(qh, kb, (((1,), (1,)), ((), ())), preferred_element_type=F32)
                  for qh, (kb, _) in zip(q_halves, kvs)]
        scores = [s if mask is None else jnp.where(mask, s, MASK_VALUE) for s, mask in zip(scores, masks)]
        m_new = [jnp.maximum(m, jnp.max(s, axis=-1, keepdims=True)) for s, (m, _) in zip(scores, carry)]
        probs = [jnp.exp2(s - mn).astype(BF16) for s, mn in zip(scores, m_new)]
        return tuple((mn, jnp.exp2(m - mn) * acc + jnp.dot(p, vb, preferred_element_type=F32))
                     for p, mn, (m, acc), (_, vb) in zip(probs, m_new, carry, kvs))

    def chunk_mask(n, first_row):
        r = (lax.broadcasted_iota(jnp.int32, (half, n), 0) + first_row) // CHUNK
        c = lax.broadcasted_iota(jnp.int32, (half, n), 1) // CHUNK
        return c <= r

    for qi in range(k_ref.shape[0] // tq):
        @pl.when(i == qi)
        def _(qi=qi):
            carry = tuple((jnp.full((half, 1), MASK_VALUE, F32), jnp.zeros((half, HEAD_PAD), F32)) for _ in range(2))
            for j in range(qi):
                carry = step(j * tq, (tq, tq), (None, None), carry)
            (_, acc0), (_, acc1) = step(qi * tq, (half, tq), (chunk_mask(half, 0), chunk_mask(tq, half)), carry)
            o_ref[:half, :] = (acc0[:, :V_HEAD] / acc0[:, V_HEAD:]).astype(o_ref.dtype)
            o_ref[half:, :] = (acc1[:, :V_HEAD] / acc1[:, V_HEAD:]).astype(o_ref.dtype)
            round_side_weights()


def _attention(q, k, v, side_weights, *, batch, seq):
    tq = TQ_ATTN
    nq = seq // tq
    n_steps = batch * MLA_HEADS * nq
    step = lambda b, h, i, off=0: ((b * MLA_HEADS + h) * nq + i + off, 0, 0)
    side_in = [pl.BlockSpec((1,) + w.shape[1:], functools.partial(step, off=off)) for w, off in side_weights]
    side_out = [pl.BlockSpec((1,) + w.shape[1:], step) for w, _ in side_weights]
    outs = pl.pallas_call(
        _attn_kernel,
        grid=(batch, MLA_HEADS, nq),
        in_specs=[pl.BlockSpec((tq, HEAD_PAD), lambda b, h, i: (b * nq + i, h)),
                  pl.BlockSpec((seq, HEAD_PAD), lambda b, h, i: (b, h)),
                  pl.BlockSpec((seq, V_HEAD), lambda b, h, i: (b, h))] + side_in,
        out_specs=[pl.BlockSpec((tq, V_HEAD), lambda b, h, i: (b * nq + i, h))] + side_out,
        out_shape=[jax.ShapeDtypeStruct((batch * seq, MLA_HEADS * V_HEAD), BF16)]
                  + [jax.ShapeDtypeStruct((n_steps,) + w.shape[1:], BF16) for w, _ in side_weights],
        compiler_params=pltpu.CompilerParams(dimension_semantics=("arbitrary", "arbitrary", "arbitrary"),
                                             vmem_limit_bytes=VMEM_LIMIT),
        name="attn",
    )(q, k, v, *[w for w, _ in side_weights])
    return outs[0], outs[1:]


def _outproj_kernel(ya_ref, yc_ref, x_ref, gattn_ref, wout_ref, gmoe_ref, wgr_ref, bgr_ref,
                    h1_ref, xm_ref, route_ref):
    tm, d_attn = ya_ref.shape
    d_model = x_ref.shape[1]
    ya = _rms(ya_ref[...].astype(F32), gattn_ref[...], d_attn).astype(BF16)
    h1 = (x_ref[...]
          + jnp.dot(ya, wout_ref[:d_attn, :], preferred_element_type=F32)
          + jnp.dot(yc_ref[...], wout_ref[d_attn:, :], preferred_element_type=F32))
    h1_ref[...] = h1
    xm = _rms(h1, gmoe_ref[...], d_model)
    _store_slabs(xm_ref, xm, SLAB_PITCH)
    for c in range(SLAB, SLAB_PITCH):
        xm_ref[pl.ds(c, tm, stride=SLAB_PITCH), :] = jnp.zeros((tm, LANES), F32)

    x_hi = xm.astype(BF16)
    x_lo = (xm - x_hi.astype(F32)).astype(BF16)
    hi_part = jnp.dot(x_hi, wgr_ref[...], preferred_element_type=F32)
    lo_part = jnp.dot(x_lo, wgr_ref[:, :LANES], preferred_element_type=F32)
    logits = hi_part[:, :LANES] + hi_part[:, LANES:] + lo_part + bgr_ref[...]
    lane = lax.broadcasted_iota(jnp.int32, (tm, LANES), 1).astype(F32)
    neg = -jnp.inf
    big = 1e9
    gl = jnp.where(lane < N_GROUPS, logits, neg)
    gmax = jnp.max(gl, axis=-1, keepdims=True)
    gsel = jnp.min(jnp.where(gl == gmax, lane, big), axis=-1, keepdims=True)
    p_g = 1.0 / jnp.sum(jnp.exp(gl - gmax), axis=-1, keepdims=True)
    lo = N_GROUPS + EXPERTS_PER_GROUP * gsel
    el = jnp.where((lane >= lo) & (lane < lo + EXPERTS_PER_GROUP), logits, neg)
    v1 = jnp.max(el, axis=-1, keepdims=True)
    i1 = jnp.min(jnp.where(el == v1, lane, big), axis=-1, keepdims=True)
    el2 = jnp.where(lane == i1, neg, el)
    v2 = jnp.max(el2, axis=-1, keepdims=True)
    i2 = jnp.min(jnp.where(el2 == v2, lane, big), axis=-1, keepdims=True)
    e2 = jnp.exp(v2 - v1)
    den = 1.0 + e2
    w1 = p_g / den
    w2 = p_g * e2 / den
    route_ref[...] = jnp.where(lane == 0, i1 - N_GROUPS,
                               jnp.where(lane == 1, i2 - N_GROUPS,
                                         jnp.where(lane == 2, w1, jnp.where(lane == 3, w2, 0.0))))


def _outproj(ya, yc, x2, gattn, w_out, gmoe, w_gr, b_gr):
    t, d_model = x2.shape
    d_attn = ya.shape[1]
    tm = TM_OUT
    const = lambda i: (0, 0)
    row = lambda i: (i, 0)
    full = lambda a: pl.BlockSpec(a.shape, const)
    return pl.pallas_call(
        _outproj_kernel,
        grid=(t // tm,),
        in_specs=[pl.BlockSpec((tm, d_attn), row), pl.BlockSpec((tm, yc.shape[1]), row), pl.BlockSpec((tm, d_model), row),
                  full(gattn), pl.BlockSpec(w_out.shape, const, pipeline_mode=pl.Buffered(1)), full(gmoe), full(w_gr), full(b_gr)],
        out_specs=[pl.BlockSpec((tm, d_model), row), pl.BlockSpec((tm * SLAB_PITCH, LANES), row), pl.BlockSpec((tm, LANES), row)],
        out_shape=[jax.ShapeDtypeStruct((t, d_model), F32), jax.ShapeDtypeStruct((t * SLAB_PITCH, LANES), F32),
                   jax.ShapeDtypeStruct((t, LANES), F32)],
        compiler_params=pltpu.CompilerParams(dimension_semantics=("arbitrary",), vmem_limit_bytes=VMEM_LIMIT),
        name="outproj",
    )(ya, yc, x2, gattn, w_out, gmoe, w_gr, b_gr)


def _row_gather_start(idx_ref, src_hbm, dst_ref, sem, n_rows, src_pitch=SLAB):
    def body(r, carry):
        src = src_hbm.at[pl.ds(idx_ref[0, 0, r] * src_pitch, SLAB), :]
        pltpu.make_async_copy(src, dst_ref.at[pl.ds(r * SLAB_PITCH, SLAB), :], sem).start()
        return carry
    lax.fori_loop(0, n_rows, body, 0, unroll=True)


def _row_gather_wait(src_hbm, dst_ref, sem, n_rows):
    pltpu.make_async_copy(src_hbm.at[pl.ds(0, n_rows * SLAB), :], dst_ref.at[pl.ds(0, n_rows * SLAB), :], sem).wait()


def _moe_kernel(tile_expert_ref, n_active_ref, first_ref, wslot_ref, next_expert_ref, tok_cur_ref, tok_next_ref,
                xm_hbm, w1_hbm, w3_hbm, w2lo_hbm, w2hi_hbm, y_ref, xbuf_ref, w1_buf, w3_buf, w2_buf, sem_ref, wsem_ref, *, tm):
    i = pl.program_id(0)
    n_active = n_active_ref[0]
    slot = i % 2
    buf_rows = tm * SLAB_PITCH
    wslot = wslot_ref[i]

    def buf(s):
        return xbuf_ref.at[pl.ds(s * buf_rows, buf_rows), :]

    n_lo = w2lo_hbm.shape[0]

    def start_weights(e, s):
        pltpu.make_async_copy(w1_hbm.at[e], w1_buf.at[s], wsem_ref.at[0, s]).start()
        pltpu.make_async_copy(w3_hbm.at[e], w3_buf.at[s], wsem_ref.at[1, s]).start()

        @pl.when(e < n_lo)
        def _():
            pltpu.make_async_copy(w2lo_hbm.at[e], w2_buf.at[s], wsem_ref.at[2, s]).start()

        @pl.when(e >= n_lo)
        def _():
            pltpu.make_async_copy(w2hi_hbm.at[e - n_lo], w2_buf.at[s], wsem_ref.at[2, s]).start()

    def wait_weights(s):
        pltpu.make_async_copy(w1_hbm.at[0], w1_buf.at[s], wsem_ref.at[0, s]).wait()
        pltpu.make_async_copy(w3_hbm.at[0], w3_buf.at[s], wsem_ref.at[1, s]).wait()
        pltpu.make_async_copy(w2lo_hbm.at[0], w2_buf.at[s], wsem_ref.at[2, s]).wait()

    @pl.when(i == 0)
    def _():
        start_weights(tile_expert_ref[0], 0)
        _row_gather_start(tok_cur_ref, xm_hbm, buf(0), sem_ref.at[0], tm, src_pitch=SLAB_PITCH)

    active = i < n_active
    new_expert = active & (first_ref[i] == 1)

    @pl.when(new_expert & (next_expert_ref[i] >= 0))
    def _():
        start_weights(next_expert_ref[i], 1 - wslot)

    @pl.when(new_expert)
    def _():
        wait_weights(wslot)

    @pl.when(active)
    def _():
        _row_gather_wait(xm_hbm, buf(slot), sem_ref.at[slot], tm)
        x = _load_slabs(xbuf_ref, slot * buf_rows, tm, SLAB, SLAB_PITCH).astype(BF16)
        a = jnp.dot(x, w1_buf[wslot], preferred_element_type=F32)
        _row_gather_start(tok_next_ref, xm_hbm, buf(1 - slot), sem_ref.at[1 - slot], tm, src_pitch=SLAB_PITCH)
        g = jnp.dot(x, w3_buf[wslot], preferred_element_type=F32)
        hid = (a * jax.nn.sigmoid(a) * g).astype(BF16)
        _store_slabs(y_ref, jnp.dot(hid, w2_buf[wslot], preferred_element_type=F32), SLAB)

    @pl.when(i == n_active)
    def _():
        _row_gather_wait(xm_hbm, buf(slot), sem_ref.at[slot], tm)

    @pl.when(i >= n_active)
    def _():
        y_ref[...] = jnp.zeros_like(y_ref)


def _moe(plan, xm, w1, w3, w2_lo, w2_hi):
    tile_expert, n_active, first, wslot, next_expert, row_token3 = plan
    n_tiles = row_token3.shape[0]
    tm = TM_MOE
    _, d_model, d_exp = w1.shape
    last = n_tiles - 1
    grid_spec = pltpu.PrefetchScalarGridSpec(
        num_scalar_prefetch=5,
        grid=(n_tiles,),
        in_specs=[pl.BlockSpec((1, 1, tm), lambda i, *_: (i, 0, 0), memory_space=pltpu.SMEM),
                  pl.BlockSpec((1, 1, tm), lambda i, *_: (jnp.minimum(i + 1, last), 0, 0), memory_space=pltpu.SMEM),
                  pl.BlockSpec(memory_space=pl.ANY), pl.BlockSpec(memory_space=pl.ANY), pl.BlockSpec(memory_space=pl.ANY),
                  pl.BlockSpec(memory_space=pl.ANY), pl.BlockSpec(memory_space=pl.ANY)],
        out_specs=pl.BlockSpec((tm * SLAB, LANES), lambda i, *_: (i, 0)),
        scratch_shapes=[pltpu.VMEM((2 * tm * SLAB_PITCH, LANES), F32),
                        pltpu.VMEM((2, d_model, d_exp), BF16), pltpu.VMEM((2, d_model, d_exp), BF16),
                        pltpu.VMEM((2, d_exp, d_model), BF16),
                        pltpu.SemaphoreType.DMA((2,)), pltpu.SemaphoreType.DMA((3, 2))],
    )
    return pl.pallas_call(
        functools.partial(_moe_kernel, tm=tm),
        grid_spec=grid_spec,
        out_shape=jax.ShapeDtypeStruct((n_tiles * tm * SLAB, LANES), F32),
        compiler_params=pltpu.CompilerParams(dimension_semantics=("arbitrary",), vmem_limit_bytes=VMEM_LIMIT),
        name="moe",
    )(tile_expert, n_active, first, wslot, next_expert, row_token3, row_token3, xm, w1, w3, w2_lo, w2_hi)


def _ple_kernel(dest_a_ref, dest_b_ref, dest_next_ref, y_hbm, h1_ref, route_ref, p_ref, wple_ref, gple_ref, wpg_ref,
                bpg_ref, o_ref, ybuf_ref, sem_ref):
    i = pl.program_id(0)
    n = pl.num_programs(0)
    tm = h1_ref.shape[0] // 2
    d_model = h1_ref.shape[1]
    buf_rows = TOP_K * tm * SLAB_PITCH
    bufs = [ybuf_ref.at[pl.ds(s * buf_rows, buf_rows), :] for s in range(2)]

    @pl.when(i == 0)
    def _():
        _row_gather_start(dest_a_ref, y_hbm, bufs[0], sem_ref.at[0], TOP_K * tm)

    _row_gather_start(dest_b_ref, y_hbm, bufs[1], sem_ref.at[1], TOP_K * tm)

    def tile(s):
        rows = slice(s * tm, (s + 1) * tm)
        _row_gather_wait(y_hbm, bufs[s], sem_ref.at[s], TOP_K * tm)
        route = route_ref[rows, :]
        y0 = _load_slabs(ybuf_ref, s * buf_rows, tm, SLAB, SLAB_PITCH)
        y1 = _load_slabs(ybuf_ref, s * buf_rows + tm * SLAB_PITCH, tm, SLAB, SLAB_PITCH)
        h2 = h1_ref[rows, :] + (route[:, 2:3] * y0 + route[:, 3:4] * y1)
        xn = _rms(h2, gple_ref[...], d_model).astype(BF16)
        gate = jax.nn.sigmoid(jnp.dot(xn, wpg_ref[...], preferred_element_type=F32) + bpg_ref[...])
        pe = jnp.dot(p_ref[rows, :].astype(BF16), wple_ref[...], preferred_element_type=F32)
        o_ref[rows, :] = h2 + gate * pe

    tile(0)

    @pl.when(i + 1 < n)
    def _():
        _row_gather_start(dest_next_ref, y_hbm, bufs[0], sem_ref.at[0], TOP_K * tm)

    tile(1)


def _ple(dest3, y, h1, route, p2, w_ple, gple, w_pg, b_pg):
    t, d_model = h1.shape
    tm = TM_PLE
    n_steps = t // (2 * tm)
    last = t // tm - 1
    const = lambda i: (0, 0)
    row = lambda i: (i, 0)
    full = lambda a: pl.BlockSpec(a.shape, const)
    dest_spec = lambda index: pl.BlockSpec((1, 1, TOP_K * tm), lambda i: (index(i), 0, 0), memory_space=pltpu.SMEM)
    return pl.pallas_call(
        _ple_kernel,
        grid=(n_steps,),
        in_specs=[dest_spec(lambda i: 2 * i), dest_spec(lambda i: 2 * i + 1), dest_spec(lambda i: jnp.minimum(2 * i + 2, last)),
                  pl.BlockSpec(memory_space=pl.ANY),
                  pl.BlockSpec((2 * tm, d_model), row), pl.BlockSpec((2 * tm, LANES), row),
                  pl.BlockSpec((2 * tm, p2.shape[1]), row),
                  full(w_ple), full(gple), pl.BlockSpec(w_pg.shape, const, pipeline_mode=pl.Buffered(1)), full(b_pg)],
        out_specs=pl.BlockSpec((2 * tm, d_model), row),
        out_shape=jax.ShapeDtypeStruct((t, d_model), F32),
        scratch_shapes=[pltpu.VMEM((2 * TOP_K * tm * SLAB_PITCH, LANES), F32), pltpu.SemaphoreType.DMA((2,))],
        compiler_params=pltpu.CompilerParams(dimension_semantics=("arbitrary",), vmem_limit_bytes=VMEM_LIMIT),
        name="ple",
    )(dest3, dest3, dest3, y, h1, route, p2, w_ple, gple, w_pg, b_pg)


def _dispatch_plan(route):
    t = route.shape[0]
    tm = TM_MOE
    total = TOP_K * t
    n_tiles = total // tm + N_EXPERTS
    expert = route[:, :TOP_K].astype(jnp.int32).T.reshape(-1)
    pos = jnp.arange(total, dtype=jnp.int32)
    experts = jnp.arange(N_EXPERTS, dtype=jnp.int32)
    onehot = (experts[:, None] == expert[None, :]).astype(jnp.int32)
    running = jnp.cumsum(onehot, axis=1)
    counts = running[:, -1]
    tile_end = jnp.cumsum((counts + tm - 1) // tm)
    tile_start = tile_end - (counts + tm - 1) // tm
    n_active = tile_end[-1]
    dest = jnp.sum(onehot * (running - onehot + (tile_start * tm)[:, None]), axis=0)
    tiles = jnp.minimum(jnp.arange(n_tiles, dtype=jnp.int32), n_active - 1)
    tile_expert = jnp.sum((tile_end[None, :] <= tiles[:, None]).astype(jnp.int32), axis=1)
    j = jnp.arange(tm, dtype=jnp.int32)
    fill_row = (tile_start * tm + counts)[:, None] + j[None, :]
    fill_row = jnp.where(fill_row < (tile_end * tm)[:, None], fill_row, n_tiles * tm)
    fill_tok = (experts[:, None] * tm + j[None, :]) % t
    keys = jnp.concatenate([dest * t + pos % t, (fill_row * t + fill_tok).reshape(-1)])
    row_token3 = (jnp.sort(keys) % t).reshape(n_tiles, 1, tm)
    dest3 = dest.reshape(TOP_K, t // TM_PLE, TM_PLE).transpose(1, 0, 2).reshape(t // TM_PLE, 1, TOP_K * TM_PLE)
    prev_expert = jnp.concatenate([jnp.full((1,), -1, jnp.int32), tile_expert[:-1]])
    first = (tile_expert != prev_expert).astype(jnp.int32)
    nonempty = counts > 0
    wslot = (jnp.sum((nonempty[None, :] & (experts[None, :] < tile_expert[:, None])).astype(jnp.int32), axis=1)) % 2
    later = nonempty[None, :] & (experts[None, :] > tile_expert[:, None])
    next_expert = jnp.min(jnp.where(later, experts[None, :], N_EXPERTS), axis=1)
    next_expert = jnp.where(next_expert < N_EXPERTS, next_expert, -1)
    plan = (tile_expert, n_active.reshape(1), first, wslot, next_expert, row_token3)
    return plan, dest3


def _rope_swap_pad(w):
    half = QK_ROPE // 2
    x1 = w[..., QK_NOPE:QK_NOPE + half]
    x2 = w[..., QK_NOPE + half:]
    return jnp.concatenate([w[..., :QK_NOPE], x1, x2, x2, x1], axis=-1)


def kernel(x, p, positions, norm_mix, w_in, g_q_lat, g_kv_lat, w_uq, w_ukv, g_q_head, g_k_head, w_conv, g_out_attn, g_out_conv, w_out, norm_moe, w_group, b_group, w_router, b_router, w1, w3, w2, norm_ple, w_ple, w_ple_gate, b_ple_gate):
    batch, seq, d_model = x.shape
    depth = w_in.shape[0]
    t = batch * seq
    q_rank = g_q_lat.shape[1]
    kv_rank = g_kv_lat.shape[1]
    half = QK_ROPE // 2
    rope_lo = q_rank + kv_rank

    pos2 = positions.reshape(t, 1)
    inv_freq = ROPE_BASE ** (-jnp.arange(0, QK_ROPE, 2, dtype=F32) / QK_ROPE)
    freq = jnp.tile(inv_freq, LANES // half).reshape(1, LANES)

    h = x.reshape(t, d_model)
    for i in range(depth):
        w_in_t = jnp.swapaxes(w_in[i], 0, 1).astype(BF16)
        w_rope = jnp.concatenate([w_in_t[rope_lo:rope_lo + QK_ROPE], w_in_t[rope_lo + half:rope_lo + QK_ROPE],
                                  w_in_t[rope_lo:rope_lo + half]], axis=0)
        wuq_p = _rope_swap_pad(w_uq[i]).reshape(q_rank, MLA_HEADS * HEAD_PAD).astype(BF16)
        wukv_p = jnp.concatenate([w_ukv[i][:, :, :QK_NOPE].reshape(kv_rank, -1),
                                  w_ukv[i][:, :, QK_NOPE:].reshape(kv_rank, -1)], axis=1).astype(BF16)
        gq_p = _rope_swap_pad(g_q_head[i]).reshape(1, HEAD_PAD)
        gk_p = _rope_swap_pad(g_k_head[i]).reshape(1, HEAD_PAD)
        as_steps = lambda w, n: w.reshape(n, -1, w.shape[-1])
        nf = t // TM_FRONT
        na = batch * MLA_HEADS * (seq // TQ_ATTN)
        front_sides = [(as_steps(w, nf), 0) for w in (w1[i], w_out[i], w_ple_gate[i])] + [(as_steps(w2[i], 2 * nf), 0)]
        (q, k, v, yc), (w1b, w_out_b, w_pg_b, w2_lo) = _front(
            h, pos2, freq, norm_mix[i].reshape(1, -1), w_in_t, w_rope, g_q_lat[i].reshape(1, -1),
            g_kv_lat[i].reshape(1, -1), wuq_p, wukv_p, gq_p, gk_p, w_conv[i], g_out_conv[i].reshape(1, -1), front_sides, seq=seq)
        ya, (w3b, w2_hi) = _attention(q, k, v, [(as_steps(w3[i], na), 0), (as_steps(w2[i], 2 * na), na)], batch=batch, seq=seq)
        half_e = (N_EXPERTS // 2,) + w2[i].shape[1:]
        w1b, w3b, w2_lo, w2_hi = w1b.reshape(w1[i].shape), w3b.reshape(w3[i].shape), w2_lo.reshape(half_e), w2_hi.reshape(half_e)
        w_out_b, w_pg_b = w_out_b.reshape(w_out[i].shape), w_pg_b.reshape(w_ple_gate[i].shape)

        pad = LANES - N_GROUPS - N_EXPERTS
        w_gr32 = jnp.concatenate([w_group[i], w_router[i], jnp.zeros((d_model, pad), F32)], axis=1)
        w_gr_hi = w_gr32.astype(BF16)
        w_gr = jnp.concatenate([w_gr_hi, (w_gr32 - w_gr_hi.astype(F32)).astype(BF16)], axis=1)
        b_gr = jnp.concatenate([b_group[i], b_router[i], jnp.zeros((pad,), F32)]).reshape(1, LANES)
        h1, xm, route = _outproj(ya, yc, h, g_out_attn[i].reshape(1, -1), w_out_b,
                                 norm_moe[i].reshape(1, -1), w_gr, b_gr)

        plan, dest3 = _dispatch_plan(route)
        y = _moe(plan, xm, w1b, w3b, w2_lo, w2_hi)
        h = _ple(dest3, y, h1, route, p[i].reshape(t, -1), w_ple[i].astype(BF16), norm_ple[i].reshape(1, -1),
                 w_pg_b, b_ple_gate[i].reshape(1, -1))
    return h.reshape(batch, seq, d_model)
```
